```python
import jax, jax.numpy as jnp
from jax import lax
import numpy as np

D_MODEL = 1024
BATCH = 2
SEQ = 16384
DEPTH = 4

HEAD_DIM = 64
N_ATTN_HEADS = 8
N_KV_GROUPS = 2
HEADS_PER_GROUP = N_ATTN_HEADS // N_KV_GROUPS
ATTN_WIDTH = N_ATTN_HEADS * HEAD_DIM
CONV_WIDTH = D_MODEL - ATTN_WIDTH
KV_WIDTH = N_KV_GROUPS * HEAD_DIM
N_BRANCH = 3
CMP_BLOCK = 32
CMP_STRIDE = 16
CMP_HIDDEN = 4 * HEAD_DIM
SEL_BLOCK = 64
SEL_TOPK = 16
WINDOW = 512
Q_BLOCK = 128
CONV_K = 3
D_FF = 4 * D_MODEL
IN_WIDTH = ATTN_WIDTH + 6 * KV_WIDTH + N_BRANCH * N_ATTN_HEADS + 3 * CONV_WIDTH
EPS = 1e-6
NEG = -1e30
FORCE_BONUS = 1e4

kernel_name = "nsa_shortconv_hymba_trunk"


def rms_norm(x, g):
    xf = x.astype(jnp.float32)
    y = xf * lax.rsqrt(jnp.mean(xf * xf, axis=-1, keepdims=True) + EPS)
    return (y * g.astype(jnp.float32)).astype(x.dtype)


def alibi_slopes():
    h = np.arange(1, N_ATTN_HEADS + 1, dtype=np.float32)
    s = np.power(np.float32(2.0), -8.0 * h / N_ATTN_HEADS).astype(np.float32)
    return jnp.asarray(s, dtype=jnp.float32).reshape(N_KV_GROUPS, HEADS_PER_GROUP)


def masked_softmax(s, mask):
    s = jnp.where(mask, s, NEG)
    m = jnp.max(s, axis=-1, keepdims=True)
    p = jnp.where(mask, jnp.exp(s - m), 0.0)
    return p / jnp.maximum(jnp.sum(p, axis=-1, keepdims=True), 1e-30)


def cmp_to_sel_matrix(n_cmp, n_sel):
    s0 = jnp.arange(n_cmp)[:, None] * CMP_STRIDE
    s1 = jnp.arange(n_sel)[None, :] * SEL_BLOCK
    shared = jnp.clip(jnp.minimum(s0 + CMP_BLOCK, s1 + SEL_BLOCK) - jnp.maximum(s0, s1), 0, None)
    return shared.astype(jnp.float32) / CMP_BLOCK


def compress(k_raw, pe, w1, b1, w2, b2):
    B, T = k_raw.shape[0], k_raw.shape[1]
    ch = k_raw.reshape(B, T // CMP_STRIDE, CMP_STRIDE, N_KV_GROUPS, HEAD_DIM)
    blk = jnp.concatenate([ch[:, :-1], ch[:, 1:]], axis=2)
    blk = blk + pe[None, None, :, None, :]
    blk = blk.transpose(0, 3, 1, 2, 4).reshape(B, N_KV_GROUPS, -1, CMP_BLOCK * HEAD_DIM)
    hdn = jax.nn.gelu(blk @ w1 + b1)
    return hdn @ w2 + b2


def nsa_attention(q, k_cmp, v_cmp, k_slc, v_slc, k_win, v_win, gates):
    B, T = q.shape[0], q.shape[1]
    G, R, hd = N_KV_GROUPS, HEADS_PER_GROUP, HEAD_DIM
    f32 = jnp.float32
    n_cmp = k_cmp.shape[2]
    n_sel = T // SEL_BLOCK
    n_top = min(SEL_TOPK, n_sel)
    n_qb = T // Q_BLOCK
    slopes = alibi_slopes()[None, :, :, None, None]
    sel_map = cmp_to_sel_matrix(n_cmp, n_sel)
    cmp_end = jnp.arange(n_cmp, dtype=jnp.int32) * CMP_STRIDE + (CMP_BLOCK - 1)
    sel_id = jnp.arange(n_sel, dtype=jnp.int32)
    sel_off = jnp.arange(SEL_BLOCK, dtype=jnp.int32)
    win_off = jnp.arange(Q_BLOCK + WINDOW, dtype=jnp.int32) - WINDOW
    ks_blk = k_slc.reshape(B, n_sel, SEL_BLOCK, G, hd).transpose(0, 3, 1, 2, 4)
    vs_blk = v_slc.reshape(B, n_sel, SEL_BLOCK, G, hd).transpose(0, 3, 1, 2, 4)
    kw_pad = jnp.pad(k_win, ((0, 0), (WINDOW, 0), (0, 0), (0, 0)))
    vw_pad = jnp.pad(v_win, ((0, 0), (WINDOW, 0), (0, 0), (0, 0)))
    q_blocks = (q * (hd ** -0.5)).reshape(B, n_qb, Q_BLOCK, G, R, hd).transpose(1, 0, 3, 4, 2, 5)
    g_blocks = gates.reshape(B, n_qb, Q_BLOCK, G, R, N_BRANCH).transpose(1, 0, 3, 4, 2, 5)
    b_ix = jnp.arange(B)[:, None, None, None]
    g_ix = jnp.arange(G)[None, :, None, None]

    def one_block(args):
        c, qb, gb = args
        q0 = c * Q_BLOCK
        t = q0 + jnp.arange(Q_BLOCK, dtype=jnp.int32)
        s = jnp.einsum('bgrqd,bgid->bgrqi', qb, k_cmp).astype(f32)
        dist = (t[:, None] - cmp_end[None, :]).astype(f32)
        p_cmp = masked_softmax(s - slopes * dist, dist >= 0)
        o_cmp = jnp.einsum('bgrqi,bgid->bgrqd', p_cmp.astype(v_cmp.dtype), v_cmp)
        imp = jnp.einsum('bgrqi,ij->bgqj', p_cmp, sel_map)
        jt = (t // SEL_BLOCK)[:, None]
        forced = (sel_id == 0) | (sel_id == jt) | (sel_id == jt - 1)
        valid = sel_id * SEL_BLOCK <= t[:, None]
        imp = jnp.where(valid, jnp.where(forced, imp + FORCE_BONUS, imp), NEG)
        _, idx = lax.top_k(imp, n_top)
        kg = ks_blk[b_ix, g_ix, idx].reshape(B, G, Q_BLOCK, n_top * SEL_BLOCK, hd)
        vg = vs_blk[b_ix, g_ix, idx].reshape(B, G, Q_BLOCK, n_top * SEL_BLOCK, hd)
        pos = (idx[..., None] * SEL_BLOCK + sel_off).reshape(B, G, Q_BLOCK, n_top * SEL_BLOCK)
        dist = (t[None, None, :, None] - pos).astype(f32)[:, :, None]
        s = jnp.einsum('bgrqd,bgqmd->bgrqm', qb, kg).astype(f32)
        p_slc = masked_softmax(s - slopes * dist, dist >= 0)
        o_slc = jnp.einsum('bgrqm,bgqmd->bgrqd', p_slc.astype(vg.dtype), vg)
        kw = lax.dynamic_slice_in_dim(kw_pad, q0, Q_BLOCK + WINDOW, axis=1)
        vw = lax.dynamic_slice_in_dim(vw_pad, q0, Q_BLOCK + WINDOW, axis=1)
        spos = q0 + win_off
        dist_i = t[:, None] - spos[None, :]
        mask = (spos[None, :] >= 0) & (dist_i >= 0) & (dist_i < WINDOW)
        s = jnp.einsum('bgrqd,bsgd->bgrqs', qb, kw).astype(f32)
        p_win = masked_softmax(s - slopes * dist_i.astype(f32), mask)
        o_win = jnp.einsum('bgrqs,bsgd->bgrqd', p_win.astype(vw.dtype), vw)
        o = gb[..., 0:1] * o_cmp + gb[..., 1:2] * o_slc + gb[..., 2:3] * o_win
        return o.astype(q.dtype)

    out = lax.map(one_block, (jnp.arange(n_qb, dtype=jnp.int32), q_blocks, g_blocks))
    return out.transpose(1, 0, 4, 2, 3, 5).reshape(B, T, N_ATTN_HEADS * hd)


def short_conv_mixer(h, c_gate, b_gate, conv_w):
    u = c_gate * h
    v = lax.conv_general_dilated(u, conv_w[:, None, :], window_strides=(1,),
                                 padding=((CONV_K - 1, 0),),
                                 dimension_numbers=('NWC', 'WIO', 'NWC'),
                                 feature_group_count=CONV_WIDTH)
    return b_gate * v


def setup_inputs(seed: int = 0) -> dict:
    key = jax.random.key(seed)
    ks = jax.random.split(key, 16)
    f32 = jnp.float32

    def nrm(k, shape, scale):
        return jax.random.normal(k, shape, f32) * scale

    return {
        'x': nrm(ks[0], (BATCH, SEQ, D_MODEL), 1.0),
        'g_mix_norm': 1.0 + nrm(ks[1], (DEPTH, D_MODEL), 0.02),
        'w_in': nrm(ks[2], (DEPTH, D_MODEL, IN_WIDTH), D_MODEL ** -0.5),
        'g_q': 1.0 + nrm(ks[3], (DEPTH, HEAD_DIM), 0.02),
        'g_k': 1.0 + nrm(ks[4], (DEPTH, N_BRANCH, HEAD_DIM), 0.02),
        'pe_cmp': nrm(ks[5], (DEPTH, 2, CMP_BLOCK, HEAD_DIM), 0.1),
        'w_cmp1': nrm(ks[6], (DEPTH, 2, CMP_BLOCK * HEAD_DIM, CMP_HIDDEN), (CMP_BLOCK * HEAD_DIM) ** -0.5),
        'b_cmp1': nrm(ks[7], (DEPTH, 2, CMP_HIDDEN), 0.01),
        'w_cmp2': nrm(ks[8], (DEPTH, 2, CMP_HIDDEN, HEAD_DIM), CMP_HIDDEN ** -0.5),
        'b_cmp2': nrm(ks[9], (DEPTH, 2, HEAD_DIM), 0.01),
        'conv_w': nrm(ks[10], (DEPTH, CONV_K, CONV_WIDTH), CONV_K ** -0.5),
        'g_out': 1.0 + nrm(ks[11], (DEPTH, D_MODEL), 0.02),
        'w_o': nrm(ks[12], (DEPTH, D_MODEL, D_MODEL), (2 * DEPTH * D_MODEL) ** -0.5),
        'g_ffn_norm': 1.0 + nrm(ks[13], (DEPTH, D_MODEL), 0.02),
        'w_up': nrm(ks[14], (DEPTH, D_MODEL, D_FF), D_MODEL ** -0.5),
        'w_down': nrm(ks[15], (DEPTH, D_FF, D_MODEL), (2 * DEPTH * D_FF) ** -0.5),
    }


def reference(x, g_mix_norm, w_in, g_q, g_k, pe_cmp, w_cmp1, b_cmp1, w_cmp2, b_cmp2,
              conv_w, g_out, w_o, g_ffn_norm, w_up, w_down):
    B, T = x.shape[0], x.shape[1]
    sizes = [ATTN_WIDTH] + [KV_WIDTH] * 6 + [N_BRANCH * N_ATTN_HEADS] + [CONV_WIDTH] * 3
    offs = [int(o) for o in np.cumsum(sizes)[:-1]]
    kv_shape = (B, T, N_KV_GROUPS, HEAD_DIM)
    for l in range(DEPTH):
        h = rms_norm(x, g_mix_norm[l])
        z = h @ w_in[l]
        q, kc, vc, ks_, vs_, kw_, vw_, gl, hc, cg, bg = jnp.split(z, offs, axis=-1)
        q = rms_norm(q.reshape(B, T, N_ATTN_HEADS, HEAD_DIM), g_q[l])
        k_cmp = rms_norm(compress(kc.reshape(kv_shape), pe_cmp[l, 0], w_cmp1[l, 0], b_cmp1[l, 0],
                                  w_cmp2[l, 0], b_cmp2[l, 0]), g_k[l, 0])
        v_cmp = compress(vc.reshape(kv_shape), pe_cmp[l, 1], w_cmp1[l, 1], b_cmp1[l, 1],
                         w_cmp2[l, 1], b_cmp2[l, 1])
        k_slc = rms_norm(ks_.reshape(kv_shape), g_k[l, 1])
        k_win = rms_norm(kw_.reshape(kv_shape), g_k[l, 2])
        gates = jax.nn.sigmoid(gl).reshape(B, T, N_ATTN_HEADS, N_BRANCH)
        attn_out = nsa_attention(q, k_cmp, v_cmp, k_slc, vs_.reshape(kv_shape),
                                 k_win, vw_.reshape(kv_shape), gates)
        conv_out = short_conv_mixer(hc, cg, bg, conv_w[l])
        mixed = jnp.concatenate([rms_norm(attn_out, g_out[l, :ATTN_WIDTH]),
                                 rms_norm(conv_out, g_out[l, ATTN_WIDTH:])], axis=-1)
        x = x + mixed @ w_o[l]
        h = rms_norm(x, g_ffn_norm[l])
        x = x + jnp.square(jax.nn.relu(h @ w_up[l])) @ w_down[l]
    return x
```

```python
import functools

import numpy as np
import jax
import jax.numpy as jnp
from jax import lax
from jax.experimental import pallas as pl
from jax.experimental.pallas import tpu as pltpu

f32 = jnp.float32
bf16 = jnp.bfloat16
i32 = jnp.int32

D_MODEL = 1024
HEAD_DIM = 64
N_HEADS = 8
N_GROUPS = 2
HEADS_PER_GROUP = 4
ATTN_WIDTH = 512
CONV_WIDTH = 512
N_BRANCH = 3
CMP_BLOCK = 32
CMP_STRIDE = 16
CMP_HIDDEN = 256
SEL_BLOCK = 64
SEL_TOPK = 16
WINDOW = 512
Q_BLOCK = 128
D_FF = 4096
EPS = 1e-6
NEG = -1e30
FORCE_BONUS = 1e4
MASKED_BELOW = -1e29
LANES = 128
CMP_CHUNK = 128
SEL_PER_CHUNK = CMP_CHUNK * CMP_STRIDE // SEL_BLOCK
IMP_ROWS = SEL_PER_CHUNK + 8
VMEM_LIMIT = 56 * 1024 * 1024

C_Q, C_KC, C_VC, C_KS, C_KW, C_VS, C_VW, C_HC, C_CG, C_BG, C_GL, C_END = (
    0, 512, 640, 768, 1024, 1280, 1408, 1536, 2048, 2560, 3072, 3200)


def _dot(a, b):
    return jnp.dot(a, b, preferred_element_type=f32)


def _dot_nt(a, b):
    return lax.dot_general(a, b, (((1,), (1,)), ((), ())), preferred_element_type=f32)


def _split_bf16(y):
    hi = y.astype(bf16)
    lo = (y - hi.astype(f32)).astype(bf16)
    return hi, lo


def _head_norm(z, e, g):
    hi, lo = _split_bf16(z * z)
    outs = []
    for c in range(z.shape[1] // LANES):
        sl = slice(c * LANES, (c + 1) * LANES)
        ss = _dot(hi[:, sl], e) + _dot(lo[:, sl], e)
        outs.append(z[:, sl] * lax.rsqrt(ss * (1.0 / HEAD_DIM) + EPS))
    y = outs[0] if len(outs) == 1 else jnp.concatenate(outs, axis=1)
    return y * g


def _pos_cols(pos, width):
    rows = pos.shape[0]
    lane = lax.broadcasted_iota(i32, (rows, LANES), 1)
    tile = jnp.where(lane == HEAD_DIM, (pos >> 7).astype(f32),
                     jnp.where(lane == HEAD_DIM + 1, (pos & 127).astype(f32), 0.0))
    return tile if width == LANES else jnp.concatenate([tile] * (width // LANES), axis=1)


def _inproj_body(x_ref, gm_ref, w_ref, gq_ref, gks_ref, gkw_ref, e_ref,
                 q_ref, kc_ref, vc_ref, ks_ref, kw_ref, vs_ref, vw_ref, u_ref, bg_ref, gate_ref,
                 *, tm, seq):
    x = x_ref[...]
    ms = jnp.mean(x * x, axis=-1, keepdims=True)
    h = ((x * lax.rsqrt(ms + EPS)) * gm_ref[...]).astype(bf16)
    e = e_ref[...]

    def proj(c0, c1):
        return _dot(h, w_ref[:, c0:c1])

    q_ref[...] = (_head_norm(proj(C_Q, C_KC), e, gq_ref[...]) * (HEAD_DIM ** -0.5)).astype(bf16)
    kc_ref[...] = proj(C_KC, C_VC).astype(bf16)
    vc_ref[...] = proj(C_VC, C_KS).astype(bf16)
    t0 = (pl.program_id(0) * tm) % seq
    pos = lax.broadcasted_iota(i32, (tm, LANES), 0) + t0
    pc = _pos_cols(pos, 2 * LANES)
    ks_ref[...] = (_head_norm(proj(C_KS, C_KW), e, gks_ref[...]) + pc).astype(bf16)
    kw_ref[...] = (_head_norm(proj(C_KW, C_VS), e, gkw_ref[...]) + pc).astype(bf16)
    vs_ref[...] = proj(C_VS, C_VW).astype(bf16)
    vw_ref[...] = proj(C_VW, C_HC).astype(bf16)
    u_ref[...] = proj(C_HC, C_CG) * proj(C_CG, C_BG)
    bg_ref[...] = proj(C_BG, C_GL)
    gate_ref[...] = 1.0 / (1.0 + jnp.exp(-proj(C_GL, C_END)))


def _inproj(x2, gm, w, gq, gks, gkw, e, *, seq, tm=512):
    n = x2.shape[0]
    row = lambda w_: pl.BlockSpec((tm, w_), lambda i: (i, 0))
    full = lambda a: pl.BlockSpec(a.shape, lambda i: (0,) * a.ndim)
    outs = [(ATTN_WIDTH, bf16), (LANES, bf16), (LANES, bf16), (2 * LANES, bf16), (2 * LANES, bf16),
            (LANES, bf16), (LANES, bf16), (CONV_WIDTH, f32), (CONV_WIDTH, f32), (LANES, f32)]
    return pl.pallas_call(
        functools.partial(_inproj_body, tm=tm, seq=seq),
        grid=(n // tm,),
        in_specs=[row(D_MODEL), full(gm), full(w), full(gq), full(gks), full(gkw), full(e)],
        out_specs=[row(w_) for w_, _ in outs],
        out_shape=[jax.ShapeDtypeStruct((n, w_), dt) for w_, dt in outs],
        compiler_params=pltpu.CompilerParams(dimension_semantics=("arbitrary",),
                                             vmem_limit_bytes=VMEM_LIMIT),
        name="inproj",
    )(x2, gm, w, gq, gks, gkw, e)


def _gelu_tanh(x):
    return 0.5 * x * (1.0 + jnp.tanh(0.7978845608028654 * (x + 0.044715 * (x * x * x))))


def _compress_body(zk_ref, zv_ref, w1_ref, pe_ref, b1_ref, w2k_ref, w2v_ref, b2k_ref, b2v_ref,
                   gk_ref, e_ref, kc_ref, vc_ref, *, tc):
    last = lax.broadcasted_iota(i32, (tc, 1), 0) == tc - 1

    def hidden(z_ref, kind):
        z = z_ref[0]
        a = _dot(z, w1_ref[kind, 0])
        b = _dot(z, w1_ref[kind, 1])
        b = jnp.concatenate([b[1:], jnp.zeros((1, b.shape[1]), f32)], axis=0)
        bias = (_dot(pe_ref[kind, 0], w1_ref[kind, 0]) + _dot(pe_ref[kind, 1], w1_ref[kind, 1]))[0:1]
        return _gelu_tanh(a + b + bias + b1_ref[kind]).astype(bf16)

    k = _dot(hidden(zk_ref, 0), w2k_ref[...]) + b2k_ref[...]
    k = _head_norm(k, e_ref[...], gk_ref[...])
    pos = lax.broadcasted_iota(i32, (tc, LANES), 0) * CMP_STRIDE + (CMP_BLOCK - 1)
    k = k + _pos_cols(pos, 2 * LANES)
    kc_ref[0] = jnp.where(last, 0.0, k).astype(bf16)
    v = _dot(hidden(zv_ref, 1), w2v_ref[...]) + b2v_ref[...]
    vc_ref[0] = jnp.where(last, 0.0, v).astype(bf16)


def _compress(zk, zv, w1, pe, b1, w2k, w2v, b2k, b2v, gk, e):
    b, tc, _ = zk.shape
    blk = lambda a: pl.BlockSpec((1,) + a.shape[1:], lambda i: (i,) + (0,) * (a.ndim - 1))
    full = lambda a: pl.BlockSpec(a.shape, lambda i: (0,) * a.ndim)
    return pl.pallas_call(
        functools.partial(_compress_body, tc=tc),
        grid=(b,),
        in_specs=[blk(zk), blk(zv)] + [full(a) for a in (w1, pe, b1, w2k, w2v, b2k, b2v, gk, e)],
        out_specs=[pl.BlockSpec((1, tc, 2 * LANES), lambda i: (i, 0, 0)),
                   pl.BlockSpec((1, tc, LANES), lambda i: (i, 0, 0))],
        out_shape=[jax.ShapeDtypeStruct((b, tc, 2 * LANES), bf16),
                   jax.ShapeDtypeStruct((b, tc, LANES), bf16)],
        compiler_params=pltpu.CompilerParams(dimension_semantics=("arbitrary",),
                                             vmem_limit_bytes=VMEM_LIMIT),
        name="compress",
    )(zk, zv, w1, pe, b1, w2k, w2v, b2k, b2v, gk, e)


def _query_tile(q_ref, tail_ref):
    qt = q_ref[0].astype(f32).T
    qt = jnp.concatenate([qt[r * HEAD_DIM:(r + 1) * HEAD_DIM] for r in range(HEADS_PER_GROUP)], axis=1)
    return jnp.concatenate([qt.astype(bf16), tail_ref[0]], axis=0)


def _query_pos(c):
    lane = lax.broadcasted_iota(i32, (1, HEADS_PER_GROUP * Q_BLOCK), 1)
    return c * Q_BLOCK + (lane & (Q_BLOCK - 1))


def _cmp_body(q_ref, tail_ref, kc_ref, vct_ref, mct_ref,
              o_ref, mask_ref, lst_ref, cnt_ref, s_scr, imp_scr, *, nsel):
    c = pl.program_id(2)
    qta = _query_tile(q_ref, tail_ref)
    tq = _query_pos(c)
    nch = (c * (Q_BLOCK // CMP_STRIDE) + (Q_BLOCK - CMP_BLOCK) // CMP_STRIDE) // CMP_CHUNK + 1
    sub = lax.broadcasted_iota(i32, (CMP_CHUNK, 1), 0)
    imp_scr[...] = jnp.zeros(imp_scr.shape, f32)

    def pass1(k, m):
        r0 = pl.multiple_of(k * CMP_CHUNK, CMP_CHUNK)
        s = _dot(kc_ref[0, pl.ds(r0, CMP_CHUNK), :], qta)
        pend = (r0 + sub) * CMP_STRIDE + (CMP_BLOCK - 1)
        s = jnp.where(pend <= tq, s, NEG)
        s_scr[pl.ds(r0, CMP_CHUNK), :] = s
        return jnp.maximum(m, jnp.max(s, axis=0, keepdims=True))

    width = HEADS_PER_GROUP * Q_BLOCK
    m = lax.fori_loop(0, nch, pass1, jnp.full((1, width), NEG, f32))

    def pass2(k, carry):
        l, o = carry
        r0 = pl.multiple_of(k * CMP_CHUNK, CMP_CHUNK)
        s = s_scr[pl.ds(r0, CMP_CHUNK), :]
        p = jnp.where(s > MASKED_BELOW, jnp.exp(s - m), 0.0)
        l = l + jnp.sum(p, axis=0, keepdims=True)
        hi, lo = _split_bf16(p)
        o = o + _dot(vct_ref[0, 0, :, pl.ds(r0, CMP_CHUNK)], hi)
        j0 = pl.multiple_of(k * SEL_PER_CHUNK, SEL_PER_CHUNK)
        imp_scr[pl.ds(j0, IMP_ROWS), :] += _dot(mct_ref[...], hi) + _dot(mct_ref[...], lo)
        return l, o

    l, o = lax.fori_loop(0, nch, pass2, (jnp.zeros((1, width), f32), jnp.zeros((HEAD_DIM, width), f32)))
    inv = 1.0 / jnp.maximum(l, 1e-30)
    o_ref[0, 0, 0] = o * inv

    imp4 = imp_scr[0:nsel, :] * inv
    imp = imp4[:, 0:Q_BLOCK]
    for r in range(1, HEADS_PER_GROUP):
        imp = imp + imp4[:, r * Q_BLOCK:(r + 1) * Q_BLOCK]

    j = lax.broadcasted_iota(i32, (nsel, Q_BLOCK), 0)
    t1 = tq[:, 0:Q_BLOCK]
    jt = t1 >> 6
    bonus = imp + FORCE_BONUS
    v = jnp.where(j == 0, bonus, jnp.where(j == jt, bonus, jnp.where(j == jt - 1, bonus, imp)))
    valid = j * SEL_BLOCK <= t1
    v = jnp.where(valid, v, NEG)
    jf = j.astype(f32)
    sel = jnp.zeros((nsel, Q_BLOCK), f32)
    for _ in range(SEL_TOPK):
        mx = jnp.max(v, axis=0, keepdims=True)
        idx = jnp.min(jnp.where(v == mx, jf, float(nsel)), axis=0, keepdims=True)
        chosen = jf == idx
        sel = jnp.where(chosen, 1.0, sel)
        v = jnp.where(chosen, -3e38, v)
    sel = jnp.where(valid, sel, 0.0)
    mask_ref[0, 0, 0] = sel

    ones8 = jnp.ones((8, Q_BLOCK), bf16)
    flag = jnp.where(_dot_nt(ones8, sel.astype(bf16))[0:1] > 0.0, 1.0, 0.0)
    flag8 = jnp.broadcast_to(flag, (8, nsel)).astype(bf16)
    rr = lax.broadcasted_iota(i32, (nsel, nsel), 0)
    cc = lax.broadcasted_iota(i32, (nsel, nsel), 1)
    prefix = _dot(flag8, jnp.where(rr < cc, 1.0, 0.0).astype(bf16))[0:1]
    place = jnp.where(prefix == rr.astype(f32), flag, 0.0).astype(bf16)
    jrow = jnp.broadcast_to(lax.broadcasted_iota(i32, (1, nsel), 1).astype(f32), (8, nsel)).astype(bf16)
    lst_ref[0] = _dot_nt(jrow, place).astype(i32)
    cnt_ref[0] = _dot(flag8, jnp.ones((nsel, LANES), bf16)).astype(i32)


def _cmp_attention(q, tail, kc, vct, mct):
    b, t, _ = q.shape
    g = N_GROUPS
    nqb, nsel, tc = t // Q_BLOCK, t // SEL_BLOCK, t // CMP_STRIDE
    width = HEADS_PER_GROUP * Q_BLOCK
    flat = lambda bi, gi, ci: (bi * g + gi) * nqb + ci
    return pl.pallas_call(
        functools.partial(_cmp_body, nsel=nsel),
        grid=(b, g, nqb),
        in_specs=[pl.BlockSpec((1, Q_BLOCK, 2 * LANES), lambda bi, gi, ci: (bi, ci, gi)),
                  pl.BlockSpec((1, HEAD_DIM, width), lambda bi, gi, ci: (gi, 0, 0)),
                  pl.BlockSpec((1, tc, LANES), lambda bi, gi, ci: (bi, 0, gi)),
                  pl.BlockSpec((1, 1, HEAD_DIM, tc), lambda bi, gi, ci: (bi, gi, 0, 0)),
                  pl.BlockSpec(mct.shape, lambda bi, gi, ci: (0, 0))],
        out_specs=[pl.BlockSpec((1, 1, 1, HEAD_DIM, width), lambda bi, gi, ci: (bi, gi, ci, 0, 0)),
                   pl.BlockSpec((1, 1, 1, nsel, Q_BLOCK), lambda bi, gi, ci: (bi, gi, ci, 0, 0)),
                   pl.BlockSpec((1, 8, nsel), lambda bi, gi, ci: (flat(bi, gi, ci), 0, 0)),
                   pl.BlockSpec((1, 8, LANES), lambda bi, gi, ci: (flat(bi, gi, ci), 0, 0))],
        out_shape=[jax.ShapeDtypeStruct((b, g, nqb, HEAD_DIM, width), f32),
                   jax.ShapeDtypeStruct((b, g, nqb, nsel, Q_BLOCK), f32),
                   jax.ShapeDtypeStruct((b * g * nqb, 8, nsel), i32),
                   jax.ShapeDtypeStruct((b * g * nqb, 8, LANES), i32)],
        scratch_shapes=[pltpu.VMEM((tc, width), f32), pltpu.VMEM((nsel + 8, width), f32)],
        compiler_params=pltpu.CompilerParams(dimension_semantics=("arbitrary",) * 3,
                                             vmem_limit_bytes=VMEM_LIMIT),
        name="cmp_attention",
    )(q, tail, kc, vct, mct)


def _slc_win_body(lst_ref, cnt_ref, q_ref, tail_ref, ks_ref, vst_ref, kw_ref, vwt_ref,
                  mask_ref, ocmp_ref, gate_ref, out_ref, s_scr):
    c = pl.program_id(2)
    qta = _query_tile(q_ref, tail_ref)
    tq = _query_pos(c)
    width = HEADS_PER_GROUP * Q_BLOCK
    nwin = (WINDOW + Q_BLOCK) // LANES

    ws = jnp.maximum(c - WINDOW // Q_BLOCK, 0) * Q_BLOCK
    sub = lax.broadcasted_iota(i32, (LANES, 1), 0)
    m = jnp.full((1, width), NEG, f32)
    for k in range(nwin):
        r0 = pl.multiple_of(ws + k * LANES, LANES)
        s = _dot(kw_ref[0, pl.ds(r0, LANES), :], qta)
        dist = tq - (r0 + sub)
        s = jnp.where(dist >= 0, jnp.where(dist < WINDOW, s, NEG), NEG)
        s_scr[k * LANES:(k + 1) * LANES, :] = s
        m = jnp.maximum(m, jnp.max(s, axis=0, keepdims=True))
    l = jnp.zeros((1, width), f32)
    o = jnp.zeros((HEAD_DIM, width), f32)
    for k in range(nwin):
        r0 = pl.multiple_of(ws + k * LANES, LANES)
        s = s_scr[k * LANES:(k + 1) * LANES, :]
        p = jnp.where(s > MASKED_BELOW, jnp.exp(s - m), 0.0)
        l = l + jnp.sum(p, axis=0, keepdims=True)
        o = o + _dot(vwt_ref[0, 0, :, pl.ds(r0, LANES)], p.astype(bf16))
    o_win = o * (1.0 / jnp.maximum(l, 1e-30))

    sub64 = lax.broadcasted_iota(i32, (SEL_BLOCK, 1), 0)

    def body(i, carry):
        m, l, o = carry
        j = lst_ref[0, 0, i]
        r0 = pl.multiple_of(j * SEL_BLOCK, SEL_BLOCK)
        s = _dot(ks_ref[0, pl.ds(r0, SEL_BLOCK), :], qta)
        mrow = mask_ref[0, 0, 0, pl.ds(j, 1), :]
        mrow = jnp.concatenate([mrow] * HEADS_PER_GROUP, axis=1)
        s = jnp.where(r0 + sub64 <= tq, jnp.where(mrow > 0.0, s, NEG), NEG)
        m_new = jnp.maximum(m, jnp.max(s, axis=0, keepdims=True))
        alpha = jnp.exp(m - m_new)
        p = jnp.where(s > MASKED_BELOW, jnp.exp(s - m_new), 0.0)
        l = alpha * l + jnp.sum(p, axis=0, keepdims=True)
        o = alpha * o + _dot(vst_ref[0, 0, j], p.astype(bf16))
        return m_new, l, o

    _, l, o = lax.fori_loop(0, cnt_ref[0, 0, 0], body,
                            (jnp.full((1, width), NEG, f32), jnp.zeros((1, width), f32),
                             jnp.zeros((HEAD_DIM, width), f32)))
    o_slc = o * (1.0 / jnp.maximum(l, 1e-30))

    gate = gate_ref[0, 0, 0]
    mix = gate[0:1] * ocmp_ref[0, 0, 0] + gate[1:2] * o_slc + gate[2:3] * o_win
    rows = jnp.concatenate([mix[:, r * Q_BLOCK:(r + 1) * Q_BLOCK] for r in range(HEADS_PER_GROUP)], axis=0)
    out_ref[0] = rows.T


def _slc_win_attention(lst, cnt, q, tail, ks, vst, kw, vwt, mask, ocmp, gate_t):
    b, t, _ = q.shape
    g = N_GROUPS
    nqb, nsel = t // Q_BLOCK, t // SEL_BLOCK
    width = HEADS_PER_GROUP * Q_BLOCK
    flat = lambda bi, gi, ci: (bi * g + gi) * nqb + ci
    smem = lambda w_: pl.BlockSpec((1, 1, w_), lambda bi, gi, ci: (flat(bi, gi, ci), 0, 0),
                                   memory_space=pltpu.SMEM)
    per_q = lambda r_, c_: pl.BlockSpec((1, 1, 1, r_, c_), lambda bi, gi, ci: (bi, gi, ci, 0, 0))
    return pl.pallas_call(
        _slc_win_body,
        grid=(b, g, nqb),
        in_specs=[smem(nsel), smem(LANES),
                  pl.BlockSpec((1, Q_BLOCK, 2 * LANES), lambda bi, gi, ci: (bi, ci, gi)),
                  pl.BlockSpec((1, HEAD_DIM, width), lambda bi, gi, ci: (gi, 0, 0)),
                  pl.BlockSpec((1, t, LANES), lambda bi, gi, ci: (bi, 0, gi)),
                  pl.BlockSpec((1, 1, nsel, HEAD_DIM, SEL_BLOCK), lambda bi, gi, ci: (bi, gi, 0, 0, 0)),
                  pl.BlockSpec((1, t, LANES), lambda bi, gi, ci: (bi, 0, gi)),
                  pl.BlockSpec((1, 1, HEAD_DIM, t), lambda bi, gi, ci: (bi, gi, 0, 0)),
                  per_q(nsel, Q_BLOCK), per_q(HEAD_DIM, width), per_q(8, width)],
        out_specs=pl.BlockSpec((1, Q_BLOCK, 2 * LANES), lambda bi, gi, ci: (bi, ci, gi)),
        out_shape=jax.ShapeDtypeStruct((b, t, ATTN_WIDTH), f32),
        scratch_shapes=[pltpu.VMEM((WINDOW + Q_BLOCK, width), f32)],
        compiler_params=pltpu.CompilerParams(dimension_semantics=("arbitrary",) * 3,
                                             vmem_limit_bytes=VMEM_LIMIT),
        name="slc_win_attention",
    )(lst, cnt, q, tail, ks, vst, kw, vwt, mask, ocmp, gate_t)


def _rms(v, g):
    return (v * lax.rsqrt(jnp.mean(v * v, axis=-1, keepdims=True) + EPS)) * g


def _outproj_body(x_ref, attn_ref, u_ref, uprev_ref, bg_ref, cw_ref, go_ref, wo_ref, out_ref, *, tm, seq):
    first = (pl.program_id(0) * tm) % seq == 0
    u = u_ref[...]
    prev = jnp.where(first, 0.0, uprev_ref[0])
    ext = jnp.concatenate([prev, u], axis=0)
    cw = cw_ref[...]
    conv = cw[0:1] * ext[6:6 + tm] + cw[1:2] * ext[7:7 + tm] + cw[2:3] * u
    conv = bg_ref[...] * conv
    go = go_ref[...]
    mixed = jnp.concatenate([_rms(attn_ref[...], go[:, :ATTN_WIDTH]), _rms(conv, go[:, ATTN_WIDTH:])],
                            axis=1).astype(bf16)
    out_ref[...] = x_ref[...] + _dot(mixed, wo_ref[...])


def _outproj(x2, attn2, u, bgate, cw, go, wo, *, seq, tm=512):
    n = x2.shape[0]
    u8 = u.reshape(n // 8, 8, CONV_WIDTH)
    row = lambda w_: pl.BlockSpec((tm, w_), lambda i: (i, 0))
    full = lambda a: pl.BlockSpec(a.shape, lambda i: (0,) * a.ndim)
    return pl.pallas_call(
        functools.partial(_outproj_body, tm=tm, seq=seq),
        grid=(n // tm,),
        in_specs=[row(D_MODEL), row(ATTN_WIDTH), row(CONV_WIDTH),
                  pl.BlockSpec((1, 8, CONV_WIDTH), lambda i: (jnp.maximum(i * (tm // 8) - 1, 0), 0, 0)),
                  row(CONV_WIDTH), full(cw), full(go), full(wo)],
        out_specs=row(D_MODEL),
        out_shape=jax.ShapeDtypeStruct((n, D_MODEL), f32),
        compiler_params=pltpu.CompilerParams(dimension_semantics=("arbitrary",),
                                             vmem_limit_bytes=VMEM_LIMIT),
        name="outproj",
    )(x2, attn2, u, u8, bgate, cw, go, wo)


def _ffn_body(x_ref, g_ref, wu_ref, wd_ref, out_ref, *, chunk):
    x = x_ref[...]
    h = _rms(x, g_ref[...]).astype(bf16)
    acc = x
    for c in range(D_FF // chunk):
        a = jnp.maximum(_dot(h, wu_ref[:, c * chunk:(c + 1) * chunk]), 0.0)
        acc = acc + _dot((a * a).astype(bf16), wd_ref[c * chunk:(c + 1) * chunk, :])
    out_ref[...] = acc


def _ffn(x2, g, wu, wd, *, tm=512, chunk=1024):
    n = x2.shape[0]
    row = pl.BlockSpec((tm, D_MODEL), lambda i: (i, 0))
    full = lambda a: pl.BlockSpec(a.shape, lambda i: (0,) * a.ndim)
    return pl.pallas_call(
        functools.partial(_ffn_body, chunk=chunk),
        grid=(n // tm,),
        in_specs=[row, full(g), full(wu), full(wd)],
        out_specs=row,
        out_shape=jax.ShapeDtypeStruct((n, D_MODEL), f32),
        compiler_params=pltpu.CompilerParams(dimension_semantics=("arbitrary",),
                                             vmem_limit_bytes=VMEM_LIMIT),
        name="ffn",
    )(x2, g, wu, wd)


def _interleave_zero(w):
    z = jnp.zeros(w.shape[:-1] + (HEAD_DIM,), w.dtype)
    return jnp.concatenate([w[..., :HEAD_DIM], z, w[..., HEAD_DIM:], z], axis=-1)


def _prep_params(g_mix_norm, w_in, g_q, g_k, pe_cmp, w_cmp1, b_cmp1, w_cmp2, b_cmp2,
                 conv_w, g_out, w_o, g_ffn_norm, w_up, w_down):
    depth = w_in.shape[0]
    o = np.cumsum([0, ATTN_WIDTH] + [LANES] * 6 + [N_BRANCH * N_HEADS] + [CONV_WIDTH] * 3)
    part = lambda i: w_in[..., int(o[i]):int(o[i + 1])]
    q, kc, vc, ks, vs, kw, vw, gl, hc, cg, bg = [part(i) for i in range(11)]
    gl = jnp.pad(gl, ((0, 0), (0, 0), (0, LANES - gl.shape[-1])))
    w = jnp.concatenate([q, kc, vc, _interleave_zero(ks), _interleave_zero(kw), vs, vw, hc, cg, bg, gl],
                        axis=-1).astype(bf16)
    tile2 = lambda gk: _interleave_zero(jnp.concatenate([gk, gk], axis=-1))[:, None, :]
    w1 = w_cmp1.reshape(depth, 2, 2, CMP_STRIDE, HEAD_DIM, CMP_HIDDEN)
    w1g = jnp.zeros((depth, 2, 2, CMP_STRIDE, N_GROUPS, HEAD_DIM, N_GROUPS, CMP_HIDDEN), f32)
    for gi in range(N_GROUPS):
        w1g = w1g.at[:, :, :, :, gi, :, gi, :].set(w1)
    w1g = w1g.reshape(depth, 2, 2, CMP_STRIDE * LANES, N_GROUPS * CMP_HIDDEN).astype(bf16)
    pe = pe_cmp.reshape(depth, 2, 2, CMP_STRIDE, 1, HEAD_DIM)
    pe = jnp.broadcast_to(pe, (depth, 2, 2, CMP_STRIDE, N_GROUPS, HEAD_DIM)).reshape(depth, 2, 2, 1, -1)
    pe = jnp.pad(pe, ((0, 0), (0, 0), (0, 0), (0, 7), (0, 0))).astype(bf16)
    b1 = jnp.concatenate([b_cmp1, b_cmp1], axis=-1)[:, :, None, :]
    w2 = jnp.zeros((depth, 2, N_GROUPS, CMP_HIDDEN, N_GROUPS, HEAD_DIM), f32)
    for gi in range(N_GROUPS):
        w2 = w2.at[:, :, gi, :, gi, :].set(w_cmp2)
    w2 = w2.reshape(depth, 2, N_GROUPS * CMP_HIDDEN, LANES)
    b2 = jnp.concatenate([b_cmp2, b_cmp2], axis=-1)[:, :, None, :]
    return dict(
        gm=g_mix_norm[:, None, :], w=w,
        gq=jnp.tile(g_q, (1, N_HEADS))[:, None, :],
        gks=tile2(g_k[:, 1]), gkw=tile2(g_k[:, 2]), gkc=tile2(g_k[:, 0]),
        w1=w1g, pe=pe, b1=b1,
        w2k=_interleave_zero(w2[:, 0]).astype(bf16), w2v=w2[:, 1].astype(bf16),
        b2k=_interleave_zero(b2[:, 0]), b2v=b2[:, 1],
        cw=jnp.pad(conv_w, ((0, 0), (0, 8 - conv_w.shape[1]), (0, 0))),
        go=g_out[:, None, :], wo=w_o.astype(bf16),
        gf=g_ffn_norm[:, None, :], wu=w_up.astype(bf16), wd=w_down.astype(bf16),
    )


def _constants():
    lane = np.arange(LANES)
    e = (lane[:, None] // HEAD_DIM == lane[None, :] // HEAD_DIM).astype(np.float32)
    tail = np.zeros((N_GROUPS, HEAD_DIM, HEADS_PER_GROUP * Q_BLOCK), np.float32)
    for gi in range(N_GROUPS):
        for r in range(HEADS_PER_GROUP):
            slope = 2.0 ** -(gi * HEADS_PER_GROUP + r + 1)
            tail[gi, 0, r * Q_BLOCK:(r + 1) * Q_BLOCK] = slope * LANES
            tail[gi, 1, r * Q_BLOCK:(r + 1) * Q_BLOCK] = slope
    mct = np.zeros((IMP_ROWS, CMP_CHUNK), np.float32)
    for i in range(CMP_CHUNK):
        lo, hi = i * CMP_STRIDE, i * CMP_STRIDE + CMP_BLOCK
        for jj in range(SEL_PER_CHUNK + 1):
            ov = min(hi, (jj + 1) * SEL_BLOCK) - max(lo, jj * SEL_BLOCK)
            if ov > 0:
                mct[jj, i] = ov / CMP_BLOCK
    return jnp.asarray(e, bf16), jnp.asarray(tail, bf16), jnp.asarray(mct, bf16)


def _layer(x2, p, consts, *, batch, seq):
    e, tail, mct = consts
    g = N_GROUPS
    nqb, nsel, tc = seq // Q_BLOCK, seq // SEL_BLOCK, seq // CMP_STRIDE
    q, kc, vc, ks, kw, vs, vw, u, bgate, gates = _inproj(
        x2, p["gm"], p["w"], p["gq"], p["gks"], p["gkw"], e, seq=seq)
    kcmp, vcmp = _compress(kc.reshape(batch, tc, CMP_STRIDE * LANES), vc.reshape(batch, tc, CMP_STRIDE * LANES),
                           p["w1"], p["pe"], p["b1"], p["w2k"], p["w2v"], p["b2k"], p["b2v"], p["gkc"], e)
    vct = vcmp.reshape(batch, tc, g, HEAD_DIM).transpose(0, 2, 3, 1)
    vst = vs.reshape(batch, nsel, SEL_BLOCK, g, HEAD_DIM).transpose(0, 3, 1, 4, 2)
    vwt = vw.reshape(batch, seq, g, HEAD_DIM).transpose(0, 2, 3, 1)
    gate_t = gates[:, :N_HEADS * N_BRANCH].reshape(batch, nqb, Q_BLOCK, g, HEADS_PER_GROUP, N_BRANCH)
    gate_t = gate_t.transpose(0, 3, 1, 5, 4, 2).reshape(batch, g, nqb, N_BRANCH, HEADS_PER_GROUP * Q_BLOCK)
    gate_t = jnp.pad(gate_t, ((0, 0), (0, 0), (0, 0), (0, 8 - N_BRANCH), (0, 0)))
    q3 = q.reshape(batch, seq, ATTN_WIDTH)
    ocmp, mask, lst, cnt = _cmp_attention(q3, tail, kcmp, vct, mct)
    attn = _slc_win_attention(lst[:, 0:1, :], cnt[:, 0:1, :], q3, tail,
                              ks.reshape(batch, seq, 2 * LANES), vst,
                              kw.reshape(batch, seq, 2 * LANES), vwt, mask, ocmp, gate_t)
    x2 = _outproj(x2, attn.reshape(batch * seq, ATTN_WIDTH), u, bgate, p["cw"], p["go"], p["wo"], seq=seq)
    return _ffn(x2, p["gf"], p["wu"], p["wd"])


def kernel(x, g_mix_norm, w_in, g_q, g_k, pe_cmp, w_cmp1, b_cmp1, w_cmp2, b_cmp2, conv_w, g_out, w_o,
           g_ffn_norm, w_up, w_down):
    batch, seq, d = x.shape
    assert d == D_MODEL and seq % (CMP_CHUNK * CMP_STRIDE) == 0 and seq >= WINDOW + Q_BLOCK
    params = _prep_params(g_mix_norm, w_in, g_q, g_k, pe_cmp, w_cmp1, b_cmp1, w_cmp2, b_cmp2,
                          conv_w, g_out, w_o, g_ffn_norm, w_up, w_down)
    consts = _constants()

    def step(x2, p):
        return _layer(x2, p, consts, batch=batch, seq=seq), None

    x2, _ = lax.scan(step, x.reshape(batch * seq, d), params)
    return x2.reshape(batch, seq, d)
```

```python
import functools

import numpy as np
import jax
import jax.numpy as jnp
from jax import lax
from jax.experimental import pallas as pl
from jax.experimental.pallas import tpu as pltpu

f32 = jnp.float32
bf16 = jnp.bfloat16
i32 = jnp.int32

D_MODEL = 1024
HEAD_DIM = 64
N_HEADS = 8
N_GROUPS = 2
HEADS_PER_GROUP = 4
ATTN_WIDTH = 512
CONV_WIDTH = 512
N_BRANCH = 3
CMP_BLOCK = 32
CMP_STRIDE = 16
CMP_HIDDEN = 256
SEL_BLOCK = 64
SEL_TOPK = 16
WINDOW = 512
Q_BLOCK = 128
D_FF = 4096
EPS = 1e-6
NEG = -1e30
FORCE_BONUS = 1e4
MASKED_BELOW = -1e29
LANES = 128
CMP_CHUNK = 128
SEL_PER_CHUNK = CMP_CHUNK * CMP_STRIDE // SEL_BLOCK
IMP_ROWS = SEL_PER_CHUNK + 8
VMEM_LIMIT = 56 * 1024 * 1024

C_Q, C_KC, C_VC, C_KS, C_KW, C_VS, C_VW, C_HC, C_CG, C_BG, C_GL, C_END = (
    0, 512, 640, 768, 1024, 1280, 1536, 1792, 2304, 2816, 3328, 3456)
SUP_BLOCKS = 32
HEAD_BLOCKS = 16
GROUP_BLOCKS = 4


def _dot(a, b):
    return jnp.dot(a, b, preferred_element_type=f32)


def _dot_nt(a, b):
    return lax.dot_general(a, b, (((1,), (1,)), ((), ())), preferred_element_type=f32)


def _dot_tn(a, b):
    return lax.dot_general(a, b, (((0,), (0,)), ((), ())), preferred_element_type=f32)


def _split_bf16(y):
    hi = y.astype(bf16)
    lo = (y - hi.astype(f32)).astype(bf16)
    return hi, lo


def _head_norm(z, e, g):
    hi, lo = _split_bf16(z * z)
    outs = []
    for c in range(z.shape[1] // LANES):
        sl = slice(c * LANES, (c + 1) * LANES)
        ss = _dot(hi[:, sl], e) + _dot(lo[:, sl], e)
        outs.append(z[:, sl] * lax.rsqrt(ss * (1.0 / HEAD_DIM) + EPS))
    y = outs[0] if len(outs) == 1 else jnp.concatenate(outs, axis=1)
    return y * g


def _pos_cols(pos, width):
    rows = pos.shape[0]
    lane = lax.broadcasted_iota(i32, (rows, LANES), 1)
    tile = jnp.where(lane == HEAD_DIM, (pos >> 7).astype(f32),
                     jnp.where(lane == HEAD_DIM + 1, (pos & 127).astype(f32), 0.0))
    return tile if width == LANES else jnp.concatenate([tile] * (width // LANES), axis=1)


def _ones_col(rows, width):
    lane = lax.broadcasted_iota(i32, (rows, LANES), 1)
    tile = jnp.where(lane == HEAD_DIM, 1.0, 0.0)
    return tile if width == LANES else jnp.concatenate([tile] * (width // LANES), axis=1)


def _inproj_body(x_ref, gm_ref, w_ref, gq_ref, gks_ref, gkw_ref, e_ref,
                 q_ref, kc_ref, vc_ref, ks_ref, kw_ref, vs_ref, vw_ref, u_ref, bg_ref, gate_ref,
                 *, tm, seq):
    x = x_ref[...]
    ms = jnp.mean(x * x, axis=-1, keepdims=True)
    h = ((x * lax.rsqrt(ms + EPS)) * gm_ref[...]).astype(bf16)
    e = e_ref[...]

    def proj(c0, c1):
        return _dot(h, w_ref[:, c0:c1])

    q_ref[...] = (_head_norm(proj(C_Q, C_KC), e, gq_ref[...]) * (HEAD_DIM ** -0.5)).astype(bf16)
    kc_ref[...] = proj(C_KC, C_VC).astype(bf16)
    vc_ref[...] = proj(C_VC, C_KS).astype(bf16)
    t0 = (pl.program_id(0) * tm) % seq
    pos = lax.broadcasted_iota(i32, (tm, LANES), 0) + t0
    pc = _pos_cols(pos, 2 * LANES)
    ks_ref[...] = (_head_norm(proj(C_KS, C_KW), e, gks_ref[...]) + pc).astype(bf16)
    kw_ref[...] = (_head_norm(proj(C_KW, C_VS), e, gkw_ref[...]) + pc).astype(bf16)
    ones = _ones_col(tm, 2 * LANES)
    vs_ref[...] = (proj(C_VS, C_VW) + ones).astype(bf16)
    vw_ref[...] = (proj(C_VW, C_HC) + ones).astype(bf16)
    u_ref[...] = proj(C_HC, C_CG) * proj(C_CG, C_BG)
    bg_ref[...] = proj(C_BG, C_GL)
    gate_ref[...] = 1.0 / (1.0 + jnp.exp(-proj(C_GL, C_END)))


def _inproj(x2, gm, w, gq, gks, gkw, e, *, seq, tm=512):
    n = x2.shape[0]
    row = lambda w_: pl.BlockSpec((tm, w_), lambda i: (i, 0))
    full = lambda a: pl.BlockSpec(a.shape, lambda i: (0,) * a.ndim)
    outs = [(ATTN_WIDTH, bf16), (LANES, bf16), (LANES, bf16), (2 * LANES, bf16), (2 * LANES, bf16),
            (2 * LANES, bf16), (2 * LANES, bf16), (CONV_WIDTH, f32), (CONV_WIDTH, f32), (LANES, f32)]
    return pl.pallas_call(
        functools.partial(_inproj_body, tm=tm, seq=seq),
        grid=(n // tm,),
        in_specs=[row(D_MODEL), full(gm), full(w), full(gq), full(gks), full(gkw), full(e)],
        out_specs=[row(w_) for w_, _ in outs],
        out_shape=[jax.ShapeDtypeStruct((n, w_), dt) for w_, dt in outs],
        compiler_params=pltpu.CompilerParams(dimension_semantics=("arbitrary",),
                                             vmem_limit_bytes=VMEM_LIMIT),
        name="inproj",
    )(x2, gm, w, gq, gks, gkw, e)


def _gelu_tanh(x):
    return 0.5 * x * (1.0 + jnp.tanh(0.7978845608028654 * (x + 0.044715 * (x * x * x))))


def _compress_body(zk_ref, zv_ref, w1_ref, pe_ref, b1_ref, w2k_ref, w2v_ref, b2k_ref, b2v_ref,
                   gk_ref, e_ref, kc_ref, vc_ref, *, tc):
    last = lax.broadcasted_iota(i32, (tc, 1), 0) == tc - 1

    def hidden(z_ref, kind):
        z = z_ref[0]
        a = _dot(z, w1_ref[kind, 0])
        b = _dot(z, w1_ref[kind, 1])
        b = jnp.concatenate([b[1:], jnp.zeros((1, b.shape[1]), f32)], axis=0)
        bias = (_dot(pe_ref[kind, 0], w1_ref[kind, 0]) + _dot(pe_ref[kind, 1], w1_ref[kind, 1]))[0:1]
        return _gelu_tanh(a + b + bias + b1_ref[kind]).astype(bf16)

    k = _dot(hidden(zk_ref, 0), w2k_ref[...]) + b2k_ref[...]
    k = _head_norm(k, e_ref[...], gk_ref[...])
    pos = lax.broadcasted_iota(i32, (tc, LANES), 0) * CMP_STRIDE + (CMP_BLOCK - 1)
    k = k + _pos_cols(pos, 2 * LANES)
    kc_ref[0] = jnp.where(last, 0.0, k).astype(bf16)
    v = _dot(hidden(zv_ref, 1), w2v_ref[...]) + b2v_ref[...]
    vc_ref[0] = jnp.where(last, 0.0, v).astype(bf16)


def _compress(zk, zv, w1, pe, b1, w2k, w2v, b2k, b2v, gk, e):
    b, tc, _ = zk.shape
    blk = lambda a: pl.BlockSpec((1,) + a.shape[1:], lambda i: (i,) + (0,) * (a.ndim - 1))
    full = lambda a: pl.BlockSpec(a.shape, lambda i: (0,) * a.ndim)
    return pl.pallas_call(
        functools.partial(_compress_body, tc=tc),
        grid=(b,),
        in_specs=[blk(zk), blk(zv)] + [full(a) for a in (w1, pe, b1, w2k, w2v, b2k, b2v, gk, e)],
        out_specs=[pl.BlockSpec((1, tc, 2 * LANES), lambda i: (i, 0, 0)),
                   pl.BlockSpec((1, tc, LANES), lambda i: (i, 0, 0))],
        out_shape=[jax.ShapeDtypeStruct((b, tc, 2 * LANES), bf16),
                   jax.ShapeDtypeStruct((b, tc, LANES), bf16)],
        compiler_params=pltpu.CompilerParams(dimension_semantics=("arbitrary",),
                                             vmem_limit_bytes=VMEM_LIMIT),
        name="compress",
    )(zk, zv, w1, pe, b1, w2k, w2v, b2k, b2v, gk, e)


def _query_tile(q_ref, tail_ref):
    qt = q_ref[0].astype(f32).T
    qt = jnp.concatenate([qt[r * HEAD_DIM:(r + 1) * HEAD_DIM] for r in range(HEADS_PER_GROUP)], axis=1)
    return jnp.concatenate([qt.astype(bf16), tail_ref[0]], axis=0)


def _query_pos(c):
    lane = lax.broadcasted_iota(i32, (1, HEADS_PER_GROUP * Q_BLOCK), 1)
    return c * Q_BLOCK + (lane & (Q_BLOCK - 1))


def _cmp_body(q_ref, tail_ref, kc_ref, vct_ref, mct_ref,
              o_ref, mask_ref, lst_ref, cnt_ref, s_scr, imp_scr, *, nsel):
    c = pl.program_id(2)
    qta = _query_tile(q_ref, tail_ref)
    tq = _query_pos(c)
    nch = (c * (Q_BLOCK // CMP_STRIDE) + (Q_BLOCK - CMP_BLOCK) // CMP_STRIDE) // CMP_CHUNK + 1
    sub = lax.broadcasted_iota(i32, (CMP_CHUNK, 1), 0)
    imp_scr[...] = jnp.zeros(imp_scr.shape, f32)

    def pass1(k, m):
        r0 = pl.multiple_of(k * CMP_CHUNK, CMP_CHUNK)
        s = _dot(kc_ref[0, pl.ds(r0, CMP_CHUNK), :], qta)
        pend = (r0 + sub) * CMP_STRIDE + (CMP_BLOCK - 1)
        s = jnp.where(pend <= tq, s, NEG)
        s_scr[pl.ds(r0, CMP_CHUNK), :] = s
        return jnp.maximum(m, jnp.max(s, axis=0, keepdims=True))

    width = HEADS_PER_GROUP * Q_BLOCK
    m = lax.fori_loop(0, nch, pass1, jnp.full((1, width), NEG, f32))

    def pass2(k, carry):
        l, o = carry
        r0 = pl.multiple_of(k * CMP_CHUNK, CMP_CHUNK)
        s = s_scr[pl.ds(r0, CMP_CHUNK), :]
        p = jnp.where(s > MASKED_BELOW, jnp.exp(s - m), 0.0)
        l = l + jnp.sum(p, axis=0, keepdims=True)
        hi, lo = _split_bf16(p)
        o = o + _dot(vct_ref[0, 0, :, pl.ds(r0, CMP_CHUNK)], hi)
        j0 = pl.multiple_of(k * SEL_PER_CHUNK, SEL_PER_CHUNK)
        imp_scr[pl.ds(j0, IMP_ROWS), :] += _dot(mct_ref[...], hi) + _dot(mct_ref[...], lo)
        return l, o

    l, o = lax.fori_loop(0, nch, pass2, (jnp.zeros((1, width), f32), jnp.zeros((HEAD_DIM, width), f32)))
    inv = 1.0 / jnp.maximum(l, 1e-30)
    o_ref[0, 0, 0] = o * inv

    imp4 = imp_scr[0:nsel, :] * inv
    imp = imp4[:, 0:Q_BLOCK]
    for r in range(1, HEADS_PER_GROUP):
        imp = imp + imp4[:, r * Q_BLOCK:(r + 1) * Q_BLOCK]

    j = lax.broadcasted_iota(i32, (nsel, Q_BLOCK), 0)
    t1 = tq[:, 0:Q_BLOCK]
    jt = t1 >> 6
    bonus = imp + FORCE_BONUS
    v = jnp.where(j == 0, bonus, jnp.where(j == jt, bonus, jnp.where(j == jt - 1, bonus, imp)))
    valid = j * SEL_BLOCK <= t1
    v = jnp.where(valid, v, NEG)
    jf = j.astype(f32)
    sel = jnp.zeros((nsel, Q_BLOCK), f32)
    for _ in range(SEL_TOPK):
        mx = jnp.max(v, axis=0, keepdims=True)
        idx = jnp.min(jnp.where(v == mx, jf, float(nsel)), axis=0, keepdims=True)
        chosen = jf == idx
        sel = jnp.where(chosen, 1.0, sel)
        v = jnp.where(chosen, -3e38, v)
    sel = jnp.where(valid, sel, 0.0)
    mask_ref[0, 0, 0] = jnp.where(sel > 0.0, 0.0, NEG)

    ones8 = jnp.ones((8, Q_BLOCK), bf16)
    flag = jnp.where(_dot_nt(ones8, sel.astype(bf16))[0:1] > 0.0, 1.0, 0.0)
    flag8 = jnp.broadcast_to(flag, (8, nsel)).astype(bf16)
    rr = lax.broadcasted_iota(i32, (nsel, nsel), 0)
    cc = lax.broadcasted_iota(i32, (nsel, nsel), 1)
    prefix = _dot(flag8, jnp.where(rr < cc, 1.0, 0.0).astype(bf16))[0:1]
    place = jnp.where(prefix == rr.astype(f32), flag, 0.0).astype(bf16)
    jrow = jnp.broadcast_to(lax.broadcasted_iota(i32, (1, nsel), 1).astype(f32), (8, nsel)).astype(bf16)
    lst_ref[0] = _dot_nt(jrow, place).astype(i32)
    cnt_ref[0] = _dot(flag8, jnp.ones((nsel, LANES), bf16)).astype(i32)


def _cmp_attention(q, tail, kc, vct, mct):
    b, t, _ = q.shape
    g = N_GROUPS
    nqb, nsel, tc = t // Q_BLOCK, t // SEL_BLOCK, t // CMP_STRIDE
    width = HEADS_PER_GROUP * Q_BLOCK
    flat = lambda bi, gi, ci: (bi * g + gi) * nqb + ci
    return pl.pallas_call(
        functools.partial(_cmp_body, nsel=nsel),
        grid=(b, g, nqb),
        in_specs=[pl.BlockSpec((1, Q_BLOCK, 2 * LANES), lambda bi, gi, ci: (bi, ci, gi)),
                  pl.BlockSpec((1, HEAD_DIM, width), lambda bi, gi, ci: (gi, 0, 0)),
                  pl.BlockSpec((1, tc, LANES), lambda bi, gi, ci: (bi, 0, gi)),
                  pl.BlockSpec((1, 1, HEAD_DIM, tc), lambda bi, gi, ci: (bi, gi, 0, 0)),
                  pl.BlockSpec(mct.shape, lambda bi, gi, ci: (0, 0))],
        out_specs=[pl.BlockSpec((1, 1, 1, HEAD_DIM, width), lambda bi, gi, ci: (bi, gi, ci, 0, 0)),
                   pl.BlockSpec((1, 1, 1, nsel, Q_BLOCK), lambda bi, gi, ci: (bi, gi, ci, 0, 0)),
                   pl.BlockSpec((1, 8, nsel), lambda bi, gi, ci: (flat(bi, gi, ci), 0, 0)),
                   pl.BlockSpec((1, 8, LANES), lambda bi, gi, ci: (flat(bi, gi, ci), 0, 0))],
        out_shape=[jax.ShapeDtypeStruct((b, g, nqb, HEAD_DIM, width), f32),
                   jax.ShapeDtypeStruct((b, g, nqb, nsel, Q_BLOCK), f32),
                   jax.ShapeDtypeStruct((b * g * nqb, 8, nsel), i32),
                   jax.ShapeDtypeStruct((b * g * nqb, 8, LANES), i32)],
        scratch_shapes=[pltpu.VMEM((tc, width), f32), pltpu.VMEM((nsel + 8, width), f32)],
        compiler_params=pltpu.CompilerParams(dimension_semantics=("arbitrary",) * 3,
                                             vmem_limit_bytes=VMEM_LIMIT),
        name="cmp_attention",
    )(q, tail, kc, vct, mct)


def _normalize(o_aug):
    return o_aug[0:HEAD_DIM] * (1.0 / jnp.maximum(o_aug[HEAD_DIM:HEAD_DIM + 1], 1e-30))


def _slc_win_body(lst_ref, cnt_ref, q_ref, tail_ref, ks_ref, vs_ref, kw_ref, vw_ref,
                  madd_ref, ocmp_ref, gate_ref, wtab_ref, out_ref, s_scr, *, nsel):
    c = pl.program_id(2)
    q0 = pl.multiple_of(c * Q_BLOCK, Q_BLOCK)
    qta = _query_tile(q_ref, tail_ref)
    width = HEADS_PER_GROUP * Q_BLOCK
    nwin = (WINDOW + Q_BLOCK) // LANES
    wq = WINDOW // Q_BLOCK

    ws = pl.multiple_of(jnp.maximum(c - wq, 0) * Q_BLOCK, Q_BLOCK)
    s = _dot(kw_ref[0, pl.ds(ws, WINDOW + Q_BLOCK), :], qta)
    chunks = []
    for k in range(nwin):
        steady = 1 if k == 0 else (2 if k == nwin - 1 else 0)
        tab = jnp.where(c >= wq, steady, jnp.where(k < c, 0, jnp.where(k == c, 2, 3)))
        chunks.append(s[k * LANES:(k + 1) * LANES] + wtab_ref[tab])
    m = chunks[0].max(axis=0, keepdims=True)
    for sk in chunks[1:]:
        m = jnp.maximum(m, sk.max(axis=0, keepdims=True))
    p = jnp.concatenate([jnp.exp(sk - m).astype(bf16) for sk in chunks], axis=0)
    o_win = _normalize(_dot_tn(vw_ref[0, pl.ds(ws, WINDOW + Q_BLOCK), :], p))

    s_d = _dot(ks_ref[0, pl.ds(q0, Q_BLOCK), :], qta) + wtab_ref[2]
    m_run = s_d.max(axis=0, keepdims=True)
    o_run = _dot_tn(vs_ref[0, pl.ds(q0, Q_BLOCK), :], jnp.exp(s_d - m_run).astype(bf16))
    n_off = cnt_ref[0, 0, 0] - 2
    grp_rows = GROUP_BLOCKS * SEL_BLOCK
    head_rows = HEAD_BLOCKS * SEL_BLOCK

    def block_ids(first, nblk):
        ids = []
        for u in range(nblk):
            i = first + u
            j = lst_ref[0, 0, jnp.minimum(i, nsel - 1)]
            ids.append((i < n_off, j, pl.multiple_of(j * SEL_BLOCK, SEL_BLOCK)))
        return ids

    def score(first, nblk, row0, mx):
        ids = block_ids(first, nblk)
        kcat = jnp.concatenate([ks_ref[0, pl.ds(r0, SEL_BLOCK), :] for _, _, r0 in ids], axis=0)
        sg = _dot(kcat, qta)
        for u, (live, j, _) in enumerate(ids):
            mrow = jnp.where(live, madd_ref[0, 0, 0, pl.ds(j, 1), :], NEG)
            mrow = jnp.concatenate([mrow] * HEADS_PER_GROUP, axis=1)
            su = sg[u * SEL_BLOCK:(u + 1) * SEL_BLOCK] + mrow
            s_scr[pl.ds(row0 + u * SEL_BLOCK, SEL_BLOCK), :] = su
            mx = jnp.maximum(mx, su.max(axis=0, keepdims=True))
        return mx

    def weigh(first, nblk, row0, m_new):
        ids = block_ids(first, nblk)
        sg = s_scr[pl.ds(row0, nblk * SEL_BLOCK), :]
        vcat = jnp.concatenate([vs_ref[0, pl.ds(r0, SEL_BLOCK), :] for _, _, r0 in ids], axis=0)
        return _dot_tn(vcat, jnp.exp(sg - m_new).astype(bf16))

    def segment(si, carry):
        m_run, o_run = carry
        base = si * SUP_BLOCKS
        rest = jnp.minimum(SUP_BLOCKS, n_off - base) - HEAD_BLOCKS
        ngrp = jnp.maximum(rest + GROUP_BLOCKS - 1, 0) // GROUP_BLOCKS
        tail_first = lambda gi: base + HEAD_BLOCKS + gi * GROUP_BLOCKS
        tail_row = lambda gi: pl.multiple_of(head_rows + gi * grp_rows, grp_rows)
        m_new = score(base, HEAD_BLOCKS, 0, m_run)
        m_new = lax.fori_loop(
            0, ngrp, lambda gi, mx: score(tail_first(gi), GROUP_BLOCKS, tail_row(gi), mx), m_new)
        o_seg = weigh(base, HEAD_BLOCKS, 0, m_new)
        o_seg = lax.fori_loop(
            0, ngrp, lambda gi, acc: acc + weigh(tail_first(gi), GROUP_BLOCKS, tail_row(gi), m_new), o_seg)
        return m_new, jnp.exp(m_run - m_new) * o_run + o_seg

    nseg = (n_off + SUP_BLOCKS - 1) // SUP_BLOCKS
    _, o_run = lax.fori_loop(0, nseg, segment, (m_run, o_run))
    o_slc = _normalize(o_run)

    gate = gate_ref[0, 0, 0]
    mix = gate[0:1] * ocmp_ref[0, 0, 0] + gate[1:2] * o_slc + gate[2:3] * o_win
    rows = jnp.concatenate([mix[:, r * Q_BLOCK:(r + 1) * Q_BLOCK] for r in range(HEADS_PER_GROUP)], axis=0)
    out_ref[0] = rows.T


def _slc_win_attention(lst, cnt, q, tail, ks, vs, kw, vw, madd, ocmp, gate_t, wtab):
    b, t, _ = q.shape
    g = N_GROUPS
    nqb, nsel = t // Q_BLOCK, t // SEL_BLOCK
    width = HEADS_PER_GROUP * Q_BLOCK
    flat = lambda bi, gi, ci: (bi * g + gi) * nqb + ci
    smem = lambda w_: pl.BlockSpec((1, 1, w_), lambda bi, gi, ci: (flat(bi, gi, ci), 0, 0),
                                   memory_space=pltpu.SMEM)
    per_q = lambda r_, c_: pl.BlockSpec((1, 1, 1, r_, c_), lambda bi, gi, ci: (bi, gi, ci, 0, 0))
    keys = pl.BlockSpec((1, t, LANES), lambda bi, gi, ci: (bi, 0, gi))
    return pl.pallas_call(
        functools.partial(_slc_win_body, nsel=nsel),
        grid=(b, g, nqb),
        in_specs=[smem(nsel), smem(LANES),
                  pl.BlockSpec((1, Q_BLOCK, 2 * LANES), lambda bi, gi, ci: (bi, ci, gi)),
                  pl.BlockSpec((1, HEAD_DIM, width), lambda bi, gi, ci: (gi, 0, 0)),
                  keys, keys, keys, keys,
                  per_q(nsel, Q_BLOCK), per_q(HEAD_DIM, width), per_q(8, width),
                  pl.BlockSpec(wtab.shape, lambda bi, gi, ci: (0, 0, 0))],
        out_specs=pl.BlockSpec((1, Q_BLOCK, 2 * LANES), lambda bi, gi, ci: (bi, ci, gi)),
        out_shape=jax.ShapeDtypeStruct((b, t, ATTN_WIDTH), f32),
        scratch_shapes=[pltpu.VMEM((SUP_BLOCKS * SEL_BLOCK, width), f32)],
        compiler_params=pltpu.CompilerParams(dimension_semantics=("arbitrary",) * 3,
                                             vmem_limit_bytes=VMEM_LIMIT),
        name="slc_win_attention",
    )(lst, cnt, q, tail, ks, vs, kw, vw, madd, ocmp, gate_t, wtab)


def _rms(v, g):
    return (v * lax.rsqrt(jnp.mean(v * v, axis=-1, keepdims=True) + EPS)) * g


def _outproj_body(x_ref, attn_ref, u_ref, uprev_ref, bg_ref, cw_ref, go_ref, wo_ref, out_ref, *, tm, seq):
    first = (pl.program_id(0) * tm) % seq == 0
    u = u_ref[...]
    prev = jnp.where(first, 0.0, uprev_ref[0])
    ext = jnp.concatenate([prev, u], axis=0)
    cw = cw_ref[...]
    conv = cw[0:1] * ext[6:6 + tm] + cw[1:2] * ext[7:7 + tm] + cw[2:3] * u
    conv = bg_ref[...] * conv
    go = go_ref[...]
    mixed = jnp.concatenate([_rms(attn_ref[...], go[:, :ATTN_WIDTH]), _rms(conv, go[:, ATTN_WIDTH:])],
                            axis=1).astype(bf16)
    out_ref[...] = x_ref[...] + _dot(mixed, wo_ref[...])


def _outproj(x2, attn2, u, bgate, cw, go, wo, *, seq, tm=512):
    n = x2.shape[0]
    u8 = u.reshape(n // 8, 8, CONV_WIDTH)
    row = lambda w_: pl.BlockSpec((tm, w_), lambda i: (i, 0))
    full = lambda a: pl.BlockSpec(a.shape, lambda i: (0,) * a.ndim)
    return pl.pallas_call(
        functools.partial(_outproj_body, tm=tm, seq=seq),
        grid=(n // tm,),
        in_specs=[row(D_MODEL), row(ATTN_WIDTH), row(CONV_WIDTH),
                  pl.BlockSpec((1, 8, CONV_WIDTH), lambda i: (jnp.maximum(i * (tm // 8) - 1, 0), 0, 0)),
                  row(CONV_WIDTH), full(cw), full(go), full(wo)],
        out_specs=row(D_MODEL),
        out_shape=jax.ShapeDtypeStruct((n, D_MODEL), f32),
        compiler_params=pltpu.CompilerParams(dimension_semantics=("arbitrary",),
                                             vmem_limit_bytes=VMEM_LIMIT),
        name="outproj",
    )(x2, attn2, u, u8, bgate, cw, go, wo)


def _ffn_body(x_ref, g_ref, wu_ref, wd_ref, out_ref, *, chunk):
    x = x_ref[...]
    h = _rms(x, g_ref[...]).astype(bf16)
    acc = x
    for c in range(D_FF // chunk):
        a = jnp.maximum(_dot(h, wu_ref[:, c * chunk:(c + 1) * chunk]), 0.0)
        acc = acc + _dot((a * a).astype(bf16), wd_ref[c * chunk:(c + 1) * chunk, :])
    out_ref[...] = acc


def _ffn(x2, g, wu, wd, *, tm=512, chunk=1024):
    n = x2.shape[0]
    row = pl.BlockSpec((tm, D_MODEL), lambda i: (i, 0))
    full = lambda a: pl.BlockSpec(a.shape, lambda i: (0,) * a.ndim)
    return pl.pallas_call(
        functools.partial(_ffn_body, chunk=chunk),
        grid=(n // tm,),
        in_specs=[row, full(g), full(wu), full(wd)],
        out_specs=row,
        out_shape=jax.ShapeDtypeStruct((n, D_MODEL), f32),
        compiler_params=pltpu.CompilerParams(dimension_semantics=("arbitrary",),
                                             vmem_limit_bytes=VMEM_LIMIT),
        name="ffn",
    )(x2, g, wu, wd)


def _interleave_zero(w):
    z = jnp.zeros(w.shape[:-1] + (HEAD_DIM,), w.dtype)
    return jnp.concatenate([w[..., :HEAD_DIM], z, w[..., HEAD_DIM:], z], axis=-1)


def _prep_params(g_mix_norm, w_in, g_q, g_k, pe_cmp, w_cmp1, b_cmp1, w_cmp2, b_cmp2,
                 conv_w, g_out, w_o, g_ffn_norm, w_up, w_down):
    depth = w_in.shape[0]
    o = np.cumsum([0, ATTN_WIDTH] + [LANES] * 6 + [N_BRANCH * N_HEADS] + [CONV_WIDTH] * 3)
    part = lambda i: w_in[..., int(o[i]):int(o[i + 1])]
    q, kc, vc, ks, vs, kw, vw, gl, hc, cg, bg = [part(i) for i in range(11)]
    gl = jnp.pad(gl, ((0, 0), (0, 0), (0, LANES - gl.shape[-1])))
    w = jnp.concatenate([q, kc, vc, _interleave_zero(ks), _interleave_zero(kw), _interleave_zero(vs),
                         _interleave_zero(vw), hc, cg, bg, gl],
                        axis=-1).astype(bf16)
    tile2 = lambda gk: _interleave_zero(jnp.concatenate([gk, gk], axis=-1))[:, None, :]
    w1 = w_cmp1.reshape(depth, 2, 2, CMP_STRIDE, HEAD_DIM, CMP_HIDDEN)
    w1g = jnp.zeros((depth, 2, 2, CMP_STRIDE, N_GROUPS, HEAD_DIM, N_GROUPS, CMP_HIDDEN), f32)
    for gi in range(N_GROUPS):
        w1g = w1g.at[:, :, :, :, gi, :, gi, :].set(w1)
    w1g = w1g.reshape(depth, 2, 2, CMP_STRIDE * LANES, N_GROUPS * CMP_HIDDEN).astype(bf16)
    pe = pe_cmp.reshape(depth, 2, 2, CMP_STRIDE, 1, HEAD_DIM)
    pe = jnp.broadcast_to(pe, (depth, 2, 2, CMP_STRIDE, N_GROUPS, HEAD_DIM)).reshape(depth, 2, 2, 1, -1)
    pe = jnp.pad(pe, ((0, 0), (0, 0), (0, 0), (0, 7), (0, 0))).astype(bf16)
    b1 = jnp.concatenate([b_cmp1, b_cmp1], axis=-1)[:, :, None, :]
    w2 = jnp.zeros((depth, 2, N_GROUPS, CMP_HIDDEN, N_GROUPS, HEAD_DIM), f32)
    for gi in range(N_GROUPS):
        w2 = w2.at[:, :, gi, :, gi, :].set(w_cmp2)
    w2 = w2.reshape(depth, 2, N_GROUPS * CMP_HIDDEN, LANES)
    b2 = jnp.concatenate([b_cmp2, b_cmp2], axis=-1)[:, :, None, :]
    return dict(
        gm=g_mix_norm[:, None, :], w=w,
        gq=jnp.tile(g_q, (1, N_HEADS))[:, None, :],
        gks=tile2(g_k[:, 1]), gkw=tile2(g_k[:, 2]), gkc=tile2(g_k[:, 0]),
        w1=w1g, pe=pe, b1=b1,
        w2k=_interleave_zero(w2[:, 0]).astype(bf16), w2v=w2[:, 1].astype(bf16),
        b2k=_interleave_zero(b2[:, 0]), b2v=b2[:, 1],
        cw=jnp.pad(conv_w, ((0, 0), (0, 8 - conv_w.shape[1]), (0, 0))),
        go=g_out[:, None, :], wo=w_o.astype(bf16),
        gf=g_ffn_norm[:, None, :], wu=w_up.astype(bf16), wd=w_down.astype(bf16),
    )


def _constants():
    lane = np.arange(LANES)
    e = (lane[:, None] // HEAD_DIM == lane[None, :] // HEAD_DIM).astype(np.float32)
    tail = np.zeros((N_GROUPS, HEAD_DIM, HEADS_PER_GROUP * Q_BLOCK), np.float32)
    for gi in range(N_GROUPS):
        for r in range(HEADS_PER_GROUP):
            slope = 2.0 ** -(gi * HEADS_PER_GROUP + r + 1)
            tail[gi, 0, r * Q_BLOCK:(r + 1) * Q_BLOCK] = slope * LANES
            tail[gi, 1, r * Q_BLOCK:(r + 1) * Q_BLOCK] = slope
    mct = np.zeros((IMP_ROWS, CMP_CHUNK), np.float32)
    for i in range(CMP_CHUNK):
        lo, hi = i * CMP_STRIDE, i * CMP_STRIDE + CMP_BLOCK
        for jj in range(SEL_PER_CHUNK + 1):
            ov = min(hi, (jj + 1) * SEL_BLOCK) - max(lo, jj * SEL_BLOCK)
            if ov > 0:
                mct[jj, i] = ov / CMP_BLOCK
    kk = np.arange(LANES)[:, None]
    ql = np.tile(np.arange(Q_BLOCK), HEADS_PER_GROUP)[None, :]
    wtab = np.zeros((4, LANES, HEADS_PER_GROUP * Q_BLOCK), np.float32)
    wtab[1] = np.where(kk > ql, 0.0, NEG)
    wtab[2] = np.where(kk <= ql, 0.0, NEG)
    wtab[3] = NEG
    return jnp.asarray(e, bf16), jnp.asarray(tail, bf16), jnp.asarray(mct, bf16), jnp.asarray(wtab)


def _layer(x2, p, consts, *, batch, seq):
    e, tail, mct, wtab = consts
    g = N_GROUPS
    nqb, nsel, tc = seq // Q_BLOCK, seq // SEL_BLOCK, seq // CMP_STRIDE
    q, kc, vc, ks, kw, vs, vw, u, bgate, gates = _inproj(
        x2, p["gm"], p["w"], p["gq"], p["gks"], p["gkw"], e, seq=seq)
    kcmp, vcmp = _compress(kc.reshape(batch, tc, CMP_STRIDE * LANES), vc.reshape(batch, tc, CMP_STRIDE * LANES),
                           p["w1"], p["pe"], p["b1"], p["w2k"], p["w2v"], p["b2k"], p["b2v"], p["gkc"], e)
    vct = vcmp.reshape(batch, tc, g, HEAD_DIM).transpose(0, 2, 3, 1)
    gate_t = gates[:, :N_HEADS * N_BRANCH].reshape(batch, nqb, Q_BLOCK, g, HEADS_PER_GROUP, N_BRANCH)
    gate_t = gate_t.transpose(0, 3, 1, 5, 4, 2).reshape(batch, g, nqb, N_BRANCH, HEADS_PER_GROUP * Q_BLOCK)
    gate_t = jnp.pad(gate_t, ((0, 0), (0, 0), (0, 0), (0, 8 - N_BRANCH), (0, 0)))
    q3 = q.reshape(batch, seq, ATTN_WIDTH)
    ocmp, mask, lst, cnt = _cmp_attention(q3, tail, kcmp, vct, mct)
    rows3 = lambda a: a.reshape(batch, seq, 2 * LANES)
    attn = _slc_win_attention(lst[:, 0:1, :], cnt[:, 0:1, :], q3, tail, rows3(ks), rows3(vs),
                              rows3(kw), rows3(vw), mask, ocmp, gate_t, wtab)
    x2 = _outproj(x2, attn.reshape(batch * seq, ATTN_WIDTH), u, bgate, p["cw"], p["go"], p["wo"], seq=seq)
    return _ffn(x2, p["gf"], p["wu"], p["wd"])


def kernel(x, g_mix_norm, w_in, g_q, g_k, pe_cmp, w_cmp1, b_cmp1, w_cmp2, b_cmp2, conv_w, g_out, w_o,
           g_ffn_norm, w_up, w_down):
    batch, seq, d = x.shape
    assert d == D_MODEL and seq % (CMP_CHUNK * CMP_STRIDE) == 0 and seq >= WINDOW + Q_BLOCK
    params = _prep_params(g_mix_norm, w_in, g_q, g_k, pe_cmp, w_cmp1, b_cmp1, w_cmp2, b_cmp2,
                          conv_w, g_out, w_o, g_ffn_norm, w_up, w_down)
    consts = _constants()

    def step(x2, p):
        return _layer(x2, p, consts, batch=batch, seq=seq), None

    x2, _ = lax.scan(step, x.reshape(batch * seq, d), params)
    return x2.reshape(batch, seq, d)
```

```python
import functools

import numpy as np
import jax
import jax.numpy as jnp
from jax import lax
from jax.experimental import pallas as pl
from jax.experimental.pallas import tpu as pltpu

f32 = jnp.float32
bf16 = jnp.bfloat16
i32 = jnp.int32

D_MODEL = 1024
HEAD_DIM = 64
N_HEADS = 8
N_GROUPS = 2
HEADS_PER_GROUP = 4
ATTN_WIDTH = 512
CONV_WIDTH = 512
N_BRANCH = 3
CMP_BLOCK = 32
CMP_STRIDE = 16
CMP_HIDDEN = 256
SEL_BLOCK = 64
SEL_TOPK = 16
WINDOW = 512
Q_BLOCK = 128
D_FF = 4096
EPS = 1e-6
NEG = -1e30
FORCE_BONUS = 1e4
PICKED = -3e38
LANES = 128
CMP_CHUNK = 128
SEL_PER_CHUNK = CMP_CHUNK * CMP_STRIDE // SEL_BLOCK
IMP_ROWS = SEL_PER_CHUNK + 8
VMEM_LIMIT = 56 * 1024 * 1024

C_Q, C_KC, C_VC, C_KS, C_KW, C_VS, C_VW, C_HC, C_CG, C_BG, C_GL, C_END = (
    0, 512, 640, 768, 1024, 1280, 1536, 1792, 2304, 2816, 3328, 3456)
SUP_BLOCKS = 32
HEAD_BLOCKS = 16
GROUP_BLOCKS = 4


def _dot(a, b):
    return jnp.dot(a, b, preferred_element_type=f32)


def _dot_nt(a, b):
    return lax.dot_general(a, b, (((1,), (1,)), ((), ())), preferred_element_type=f32)


def _dot_tn(a, b):
    return lax.dot_general(a, b, (((0,), (0,)), ((), ())), preferred_element_type=f32)


def _split_bf16(y):
    hi = y.astype(bf16)
    lo = (y - hi.astype(f32)).astype(bf16)
    return hi, lo


def _head_norm(z, e, g):
    hi, lo = _split_bf16(z * z)
    outs = []
    for c in range(z.shape[1] // LANES):
        sl = slice(c * LANES, (c + 1) * LANES)
        ss = _dot(hi[:, sl], e) + _dot(lo[:, sl], e)
        outs.append(z[:, sl] * lax.rsqrt(ss * (1.0 / HEAD_DIM) + EPS))
    y = outs[0] if len(outs) == 1 else jnp.concatenate(outs, axis=1)
    return y * g


def _pos_cols(pos, width):
    rows = pos.shape[0]
    lane = lax.broadcasted_iota(i32, (rows, LANES), 1)
    tile = jnp.where(lane == HEAD_DIM, (pos >> 7).astype(f32),
                     jnp.where(lane == HEAD_DIM + 1, (pos & 127).astype(f32), 0.0))
    return tile if width == LANES else jnp.concatenate([tile] * (width // LANES), axis=1)


def _ones_col(rows, width):
    lane = lax.broadcasted_iota(i32, (rows, LANES), 1)
    tile = jnp.where(lane == HEAD_DIM, 1.0, 0.0)
    return tile if width == LANES else jnp.concatenate([tile] * (width // LANES), axis=1)


def _inproj_body(x_ref, gm_ref, w_ref, gq_ref, gks_ref, gkw_ref, e_ref, tail_ref,
                 qt_ref, kc_ref, vc_ref, ks_ref, kw_ref, vs_ref, vw_ref, u_ref, bg_ref, gate_ref,
                 *, tm, seq):
    x = x_ref[...]
    ms = jnp.mean(x * x, axis=-1, keepdims=True)
    h = ((x * lax.rsqrt(ms + EPS)) * gm_ref[...]).astype(bf16)
    e = e_ref[...]

    def proj(c0, c1):
        return _dot(h, w_ref[:, c0:c1])

    qn = _head_norm(proj(C_Q, C_KC), e, gq_ref[...]) * (HEAD_DIM ** -0.5)
    gw = HEADS_PER_GROUP * HEAD_DIM
    for cb in range(tm // Q_BLOCK):
        for g in range(N_GROUPS):
            blk = qn[cb * Q_BLOCK:(cb + 1) * Q_BLOCK, g * gw:(g + 1) * gw].T
            top = jnp.concatenate([blk[r * HEAD_DIM:(r + 1) * HEAD_DIM] for r in range(HEADS_PER_GROUP)],
                                  axis=1)
            qt_ref[cb, g] = jnp.concatenate([top.astype(bf16), tail_ref[g]], axis=0)
    kc_ref[...] = proj(C_KC, C_VC).astype(bf16)
    vc_ref[...] = proj(C_VC, C_KS).astype(bf16)
    t0 = (pl.program_id(0) * tm) % seq
    pos = lax.broadcasted_iota(i32, (tm, LANES), 0) + t0
    pc = _pos_cols(pos, 2 * LANES)
    ks_ref[...] = (_head_norm(proj(C_KS, C_KW), e, gks_ref[...]) + pc).astype(bf16)
    kw_ref[...] = (_head_norm(proj(C_KW, C_VS), e, gkw_ref[...]) + pc).astype(bf16)
    ones = _ones_col(tm, 2 * LANES)
    vs_ref[...] = (proj(C_VS, C_VW) + ones).astype(bf16)
    vw_ref[...] = (proj(C_VW, C_HC) + ones).astype(bf16)
    u_ref[...] = proj(C_HC, C_CG) * proj(C_CG, C_BG)
    bg_ref[...] = proj(C_BG, C_GL)
    gate_ref[...] = 1.0 / (1.0 + jnp.exp(-proj(C_GL, C_END)))


def _inproj(x2, gm, w, gq, gks, gkw, e, tail, *, seq, tm=512):
    n = x2.shape[0]
    width = HEADS_PER_GROUP * Q_BLOCK
    row = lambda w_: pl.BlockSpec((tm, w_), lambda i: (i, 0))
    full = lambda a: pl.BlockSpec(a.shape, lambda i: (0,) * a.ndim)
    outs = [(LANES, bf16), (LANES, bf16), (2 * LANES, bf16), (2 * LANES, bf16),
            (2 * LANES, bf16), (2 * LANES, bf16), (CONV_WIDTH, f32), (CONV_WIDTH, f32), (LANES, f32)]
    qt_spec = pl.BlockSpec((tm // Q_BLOCK, N_GROUPS, LANES, width), lambda i: (i, 0, 0, 0))
    qt_shape = jax.ShapeDtypeStruct((n // Q_BLOCK, N_GROUPS, LANES, width), bf16)
    return pl.pallas_call(
        functools.partial(_inproj_body, tm=tm, seq=seq),
        grid=(n // tm,),
        in_specs=[row(D_MODEL), full(gm), full(w), full(gq), full(gks), full(gkw), full(e), full(tail)],
        out_specs=[qt_spec] + [row(w_) for w_, _ in outs],
        out_shape=[qt_shape] + [jax.ShapeDtypeStruct((n, w_), dt) for w_, dt in outs],
        compiler_params=pltpu.CompilerParams(dimension_semantics=("arbitrary",),
                                             vmem_limit_bytes=VMEM_LIMIT),
        name="inproj",
    )(x2, gm, w, gq, gks, gkw, e, tail)


def _gelu_tanh(x):
    return 0.5 * x * (1.0 + jnp.tanh(0.7978845608028654 * (x + 0.044715 * (x * x * x))))


def _compress_body(zk_ref, zv_ref, w1_ref, pe_ref, b1_ref, w2k_ref, w2v_ref, b2k_ref, b2v_ref,
                   gk_ref, e_ref, kc_ref, vc_ref, *, tc):
    last = lax.broadcasted_iota(i32, (tc, 1), 0) == tc - 1

    def hidden(z_ref, kind):
        z = z_ref[0]
        a = _dot(z, w1_ref[kind, 0])
        b = _dot(z, w1_ref[kind, 1])
        b = jnp.concatenate([b[1:], jnp.zeros((1, b.shape[1]), f32)], axis=0)
        bias = (_dot(pe_ref[kind, 0], w1_ref[kind, 0]) + _dot(pe_ref[kind, 1], w1_ref[kind, 1]))[0:1]
        return _gelu_tanh(a + b + bias + b1_ref[kind]).astype(bf16)

    k = _dot(hidden(zk_ref, 0), w2k_ref[...]) + b2k_ref[...]
    k = _head_norm(k, e_ref[...], gk_ref[...])
    pos = lax.broadcasted_iota(i32, (tc, LANES), 0) * CMP_STRIDE + (CMP_BLOCK - 1)
    k = k + _pos_cols(pos, 2 * LANES)
    kc_ref[0] = jnp.where(last, 0.0, k).astype(bf16)
    v = _dot(hidden(zv_ref, 1), w2v_ref[...]) + b2v_ref[...]
    vc_ref[0] = jnp.where(last, 0.0, v + _ones_col(tc, 2 * LANES)).astype(bf16)


def _compress(zk, zv, w1, pe, b1, w2k, w2v, b2k, b2v, gk, e):
    b, tc, _ = zk.shape
    blk = lambda a: pl.BlockSpec((1,) + a.shape[1:], lambda i: (i,) + (0,) * (a.ndim - 1))
    full = lambda a: pl.BlockSpec(a.shape, lambda i: (0,) * a.ndim)
    return pl.pallas_call(
        functools.partial(_compress_body, tc=tc),
        grid=(b,),
        in_specs=[blk(zk), blk(zv)] + [full(a) for a in (w1, pe, b1, w2k, w2v, b2k, b2v, gk, e)],
        out_specs=[pl.BlockSpec((1, tc, 2 * LANES), lambda i: (i, 0, 0))] * 2,
        out_shape=[jax.ShapeDtypeStruct((b, tc, 2 * LANES), bf16)] * 2,
        compiler_params=pltpu.CompilerParams(dimension_semantics=("arbitrary",),
                                             vmem_limit_bytes=VMEM_LIMIT),
        name="compress",
    )(zk, zv, w1, pe, b1, w2k, w2v, b2k, b2v, gk, e)


def _query_pos(c):
    lane = lax.broadcasted_iota(i32, (1, HEADS_PER_GROUP * Q_BLOCK), 1)
    return c * Q_BLOCK + (lane & (Q_BLOCK - 1))


def _pick_top(vs, jf, n_pick):
    vs = list(vs)
    for _ in range(n_pick):
        for g, v in enumerate(vs):
            mx = jnp.max(v, axis=0, keepdims=True)
            idx = jnp.min(jnp.where(v == mx, jf, float(jf.shape[0])), axis=0, keepdims=True)
            vs[g] = jnp.where(jf == idx, PICKED, v)
    return tuple(vs)


def _cmp_body(qt_ref, kc_ref, vc_ref, mct_ref, tri_ref,
              o_ref, madd_ref, lst_ref, cnt_ref, s_scr, imp_scr, *, nsel):
    c = pl.program_id(1)
    groups = range(N_GROUPS)
    width = HEADS_PER_GROUP * Q_BLOCK
    tq = _query_pos(c)
    nch = (c * (Q_BLOCK // CMP_STRIDE) + (Q_BLOCK - CMP_BLOCK) // CMP_STRIDE) // CMP_CHUNK + 1
    sub = lax.broadcasted_iota(i32, (CMP_CHUNK, 1), 0)
    imp_scr[...] = jnp.zeros(imp_scr.shape, f32)
    lanes = lambda g: slice(g * LANES, (g + 1) * LANES)

    def pass1(k, ms):
        r0 = pl.multiple_of(k * CMP_CHUNK, CMP_CHUNK)
        seen = (r0 + sub) * CMP_STRIDE + (CMP_BLOCK - 1) <= tq
        out = []
        for g in groups:
            s = _dot(kc_ref[0, pl.ds(r0, CMP_CHUNK), lanes(g)], qt_ref[0, g])
            s = jnp.where(seen, s, NEG)
            s_scr[g, pl.ds(r0, CMP_CHUNK), :] = s
            out.append(jnp.maximum(ms[g], jnp.max(s, axis=0, keepdims=True)))
        return tuple(out)

    ms = lax.fori_loop(0, nch, pass1, (jnp.full((1, width), NEG, f32),) * N_GROUPS)

    def pass2(k, accs):
        r0 = pl.multiple_of(k * CMP_CHUNK, CMP_CHUNK)
        j0 = pl.multiple_of(k * SEL_PER_CHUNK, SEL_PER_CHUNK)
        out = []
        for g in groups:
            hi, lo = _split_bf16(jnp.exp(s_scr[g, pl.ds(r0, CMP_CHUNK), :] - ms[g]))
            out.append(accs[g] + _dot_tn(vc_ref[0, pl.ds(r0, CMP_CHUNK), lanes(g)], hi))
            imp_scr[g, pl.ds(j0, IMP_ROWS), :] += _dot(mct_ref[...], hi) + _dot(mct_ref[...], lo)
        return tuple(out)

    accs = lax.fori_loop(0, nch, pass2, (jnp.zeros((LANES, width), f32),) * N_GROUPS)

    any_key = tq >= CMP_BLOCK - 1
    j = lax.broadcasted_iota(i32, (nsel, Q_BLOCK), 0)
    jf = j.astype(f32)
    t1 = tq[:, 0:Q_BLOCK]
    jt = t1 >> 6
    valid = j * SEL_BLOCK <= t1
    vs = []
    for g in groups:
        inv = jnp.where(any_key, 1.0 / jnp.maximum(accs[g][HEAD_DIM:HEAD_DIM + 1], 1e-30), 0.0)
        o_ref[0, g, 0] = accs[g][0:HEAD_DIM] * inv
        imp4 = imp_scr[g, 0:nsel, :] * inv
        imp = imp4[:, 0:Q_BLOCK]
        for r in range(1, HEADS_PER_GROUP):
            imp = imp + imp4[:, r * Q_BLOCK:(r + 1) * Q_BLOCK]
        v = jnp.where(valid, imp, NEG)
        vs.append(jnp.where(j == 0, PICKED, jnp.where(j == jt, PICKED, jnp.where(j == jt - 1, PICKED, v))))

    quarter = nsel // 4

    def branch(rows):
        def run(vals):
            if rows == nsel:
                return _pick_top(vals, jf, SEL_TOPK - 3)
            top = _pick_top([v[:rows] for v in vals], jf[:rows], SEL_TOPK - 3)
            return tuple(jnp.concatenate([t, v[rows:]], axis=0) for t, v in zip(top, vals))
        return run

    vs = lax.switch((2 * c + 1) // quarter, [branch(quarter * (i + 1)) for i in range(4)], tuple(vs))

    rr = lax.broadcasted_iota(i32, (nsel, nsel), 0).astype(f32)
    ones8 = jnp.ones((8, Q_BLOCK), bf16)
    jrow = jnp.broadcast_to(lax.broadcasted_iota(i32, (1, nsel), 1).astype(f32), (8, nsel)).astype(bf16)
    for g in groups:
        picked = vs[g] == PICKED
        madd_ref[0, g, 0] = jnp.where(valid, jnp.where(picked, 0.0, NEG), NEG)
        sel = jnp.where(valid, jnp.where(picked, 1.0, 0.0), 0.0).astype(bf16)
        flag = jnp.where(_dot_nt(ones8, sel)[0:1] > 0.0, 1.0, 0.0)
        flag8 = jnp.broadcast_to(flag, (8, nsel)).astype(bf16)
        prefix = _dot(flag8, tri_ref[...])[0:1]
        place = jnp.where(prefix == rr, flag, 0.0).astype(bf16)
        lst_ref[0, g, 0] = _dot_nt(jrow, place).astype(i32)
        cnt_ref[0, g, 0] = _dot(flag8, jnp.ones((nsel, LANES), bf16)).astype(i32)


def _cmp_attention(qt, kc, vc, mct, tri, *, batch):
    g = N_GROUPS
    nqb = qt.shape[0] // batch
    tc = kc.shape[1]
    nsel = tc * CMP_STRIDE // SEL_BLOCK
    width = HEADS_PER_GROUP * Q_BLOCK
    per_q = lambda r_, c_: pl.BlockSpec((1, g, 1, r_, c_), lambda bi, ci: (bi, 0, ci, 0, 0))
    shape = lambda r_, c_, dt: jax.ShapeDtypeStruct((batch, g, nqb, r_, c_), dt)
    return pl.pallas_call(
        functools.partial(_cmp_body, nsel=nsel),
        grid=(batch, nqb),
        in_specs=[pl.BlockSpec((1, g, LANES, width), lambda bi, ci: (bi * nqb + ci, 0, 0, 0)),
                  pl.BlockSpec((1, tc, 2 * LANES), lambda bi, ci: (bi, 0, 0)),
                  pl.BlockSpec((1, tc, 2 * LANES), lambda bi, ci: (bi, 0, 0)),
                  pl.BlockSpec(mct.shape, lambda bi, ci: (0, 0)),
                  pl.BlockSpec(tri.shape, lambda bi, ci: (0, 0))],
        out_specs=[per_q(HEAD_DIM, width), per_q(nsel, Q_BLOCK), per_q(8, nsel), per_q(8, LANES)],
        out_shape=[shape(HEAD_DIM, width, f32), shape(nsel, Q_BLOCK, f32),
                   shape(8, nsel, i32), shape(8, LANES, i32)],
        scratch_shapes=[pltpu.VMEM((g, tc, width), f32), pltpu.VMEM((g, nsel + 8, width), f32)],
        compiler_params=pltpu.CompilerParams(dimension_semantics=("arbitrary",) * 2,
                                             vmem_limit_bytes=VMEM_LIMIT),
        name="cmp_attention",
    )(qt, kc, vc, mct, tri)


def _normalize(o_aug):
    return o_aug[0:HEAD_DIM] * (1.0 / jnp.maximum(o_aug[HEAD_DIM:HEAD_DIM + 1], 1e-30))


def _slc_win_body(lst_ref, cnt_ref, qt_ref, ks_ref, vs_ref, kw_ref, vw_ref,
                  madd_ref, ocmp_ref, gate_ref, wtab_ref, out_ref, s_scr, *, nsel):
    c = pl.program_id(2)
    q0 = pl.multiple_of(c * Q_BLOCK, Q_BLOCK)
    qta = qt_ref[0, 0]
    width = HEADS_PER_GROUP * Q_BLOCK
    nwin = (WINDOW + Q_BLOCK) // LANES
    wq = WINDOW // Q_BLOCK

    ws = pl.multiple_of(jnp.maximum(c - wq, 0) * Q_BLOCK, Q_BLOCK)
    s = _dot(kw_ref[0, pl.ds(ws, WINDOW + Q_BLOCK), :], qta)
    chunks = []
    for k in range(nwin):
        steady = 1 if k == 0 else (2 if k == nwin - 1 else 0)
        tab = jnp.where(c >= wq, steady, jnp.where(k < c, 0, jnp.where(k == c, 2, 3)))
        chunks.append(s[k * LANES:(k + 1) * LANES] + wtab_ref[tab])
    m = chunks[0].max(axis=0, keepdims=True)
    for sk in chunks[1:]:
        m = jnp.maximum(m, sk.max(axis=0, keepdims=True))
    p = jnp.concatenate([jnp.exp(sk - m).astype(bf16) for sk in chunks], axis=0)
    o_win = _normalize(_dot_tn(vw_ref[0, pl.ds(ws, WINDOW + Q_BLOCK), :], p))

    s_d = _dot(ks_ref[0, pl.ds(q0, Q_BLOCK), :], qta) + wtab_ref[2]
    m_run = s_d.max(axis=0, keepdims=True)
    o_run = _dot_tn(vs_ref[0, pl.ds(q0, Q_BLOCK), :], jnp.exp(s_d - m_run).astype(bf16))
    n_off = cnt_ref[0, 0, 0] - 2
    grp_rows = GROUP_BLOCKS * SEL_BLOCK
    head_rows = HEAD_BLOCKS * SEL_BLOCK

    def block_ids(first, nblk):
        ids = []
        for u in range(nblk):
            i = first + u
            j = lst_ref[0, 0, jnp.minimum(i, nsel - 1)]
            ids.append((i < n_off, j, pl.multiple_of(j * SEL_BLOCK, SEL_BLOCK)))
        return ids

    def score(first, nblk, row0, mx):
        ids = block_ids(first, nblk)
        kcat = jnp.concatenate([ks_ref[0, pl.ds(r0, SEL_BLOCK), :] for _, _, r0 in ids], axis=0)
        sg = _dot(kcat, qta)
        for u, (live, j, _) in enumerate(ids):
            mrow = jnp.where(live, madd_ref[0, 0, 0, pl.ds(j, 1), :], NEG)
            mrow = jnp.concatenate([mrow] * HEADS_PER_GROUP, axis=1)
            su = sg[u * SEL_BLOCK:(u + 1) * SEL_BLOCK] + mrow
            s_scr[pl.ds(row0 + u * SEL_BLOCK, SEL_BLOCK), :] = su
            mx = jnp.maximum(mx, su.max(axis=0, keepdims=True))
        return mx

    def weigh(first, nblk, row0, m_new):
        ids = block_ids(first, nblk)
        sg = s_scr[pl.ds(row0, nblk * SEL_BLOCK), :]
        vcat = jnp.concatenate([vs_ref[0, pl.ds(r0, SEL_BLOCK), :] for _, _, r0 in ids], axis=0)
        return _dot_tn(vcat, jnp.exp(sg - m_new).astype(bf16))

    def segment(si, carry):
        m_run, o_run = carry
        base = si * SUP_BLOCKS
        rest = jnp.minimum(SUP_BLOCKS, n_off - base) - HEAD_BLOCKS
        ngrp = jnp.maximum(rest + GROUP_BLOCKS - 1, 0) // GROUP_BLOCKS
        tail_first = lambda gi: base + HEAD_BLOCKS + gi * GROUP_BLOCKS
        tail_row = lambda gi: pl.multiple_of(head_rows + gi * grp_rows, grp_rows)
        m_new = score(base, HEAD_BLOCKS, 0, m_run)
        m_new = lax.fori_loop(
            0, ngrp, lambda gi, mx: score(tail_first(gi), GROUP_BLOCKS, tail_row(gi), mx), m_new)
        o_seg = weigh(base, HEAD_BLOCKS, 0, m_new)
        o_seg = lax.fori_loop(
            0, ngrp, lambda gi, acc: acc + weigh(tail_first(gi), GROUP_BLOCKS, tail_row(gi), m_new), o_seg)
        return m_new, jnp.exp(m_run - m_new) * o_run + o_seg

    nseg = (n_off + SUP_BLOCKS - 1) // SUP_BLOCKS
    _, o_run = lax.fori_loop(0, nseg, segment, (m_run, o_run))
    o_slc = _normalize(o_run)

    gate = gate_ref[0, 0, 0]
    mix = gate[0:1] * ocmp_ref[0, 0, 0] + gate[1:2] * o_slc + gate[2:3] * o_win
    rows = jnp.concatenate([mix[:, r * Q_BLOCK:(r + 1) * Q_BLOCK] for r in range(HEADS_PER_GROUP)], axis=0)
    out_ref[0] = rows.T


def _slc_win_attention(lst, cnt, qt, ks, vs, kw, vw, madd, ocmp, gate_t, wtab):
    b, t, _ = ks.shape
    g = N_GROUPS
    nqb, nsel = t // Q_BLOCK, t // SEL_BLOCK
    width = HEADS_PER_GROUP * Q_BLOCK
    flat = lambda bi, gi, ci: (bi * g + gi) * nqb + ci
    smem = lambda w_: pl.BlockSpec((1, 1, w_), lambda bi, gi, ci: (flat(bi, gi, ci), 0, 0),
                                   memory_space=pltpu.SMEM)
    per_q = lambda r_, c_: pl.BlockSpec((1, 1, 1, r_, c_), lambda bi, gi, ci: (bi, gi, ci, 0, 0))
    keys = pl.BlockSpec((1, t, LANES), lambda bi, gi, ci: (bi, 0, gi))
    return pl.pallas_call(
        functools.partial(_slc_win_body, nsel=nsel),
        grid=(b, g, nqb),
        in_specs=[smem(nsel), smem(LANES),
                  pl.BlockSpec((1, 1, LANES, width), lambda bi, gi, ci: (bi * nqb + ci, gi, 0, 0)),
                  keys, keys, keys, keys,
                  per_q(nsel, Q_BLOCK), per_q(HEAD_DIM, width), per_q(8, width),
                  pl.BlockSpec(wtab.shape, lambda bi, gi, ci: (0, 0, 0))],
        out_specs=pl.BlockSpec((1, Q_BLOCK, 2 * LANES), lambda bi, gi, ci: (bi, ci, gi)),
        out_shape=jax.ShapeDtypeStruct((b, t, ATTN_WIDTH), f32),
        scratch_shapes=[pltpu.VMEM((SUP_BLOCKS * SEL_BLOCK, width), f32)],
        compiler_params=pltpu.CompilerParams(dimension_semantics=("arbitrary",) * 3,
                                             vmem_limit_bytes=VMEM_LIMIT),
        name="slc_win_attention",
    )(lst, cnt, qt, ks, vs, kw, vw, madd, ocmp, gate_t, wtab)


def _rms(v, g):
    return (v * lax.rsqrt(jnp.mean(v * v, axis=-1, keepdims=True) + EPS)) * g


def _outproj_body(x_ref, attn_ref, u_ref, uprev_ref, bg_ref, cw_ref, go_ref, wo_ref, out_ref, *, tm, seq):
    first = (pl.program_id(0) * tm) % seq == 0
    u = u_ref[...]
    prev = jnp.where(first, 0.0, uprev_ref[0])
    ext = jnp.concatenate([prev, u], axis=0)
    cw = cw_ref[...]
    conv = cw[0:1] * ext[6:6 + tm] + cw[1:2] * ext[7:7 + tm] + cw[2:3] * u
    conv = bg_ref[...] * conv
    go = go_ref[...]
    mixed = jnp.concatenate([_rms(attn_ref[...], go[:, :ATTN_WIDTH]), _rms(conv, go[:, ATTN_WIDTH:])],
                            axis=1).astype(bf16)
    out_ref[...] = x_ref[...] + _dot(mixed, wo_ref[...])


def _outproj(x2, attn2, u, bgate, cw, go, wo, *, seq, tm=512):
    n = x2.shape[0]
    u8 = u.reshape(n // 8, 8, CONV_WIDTH)
    row = lambda w_: pl.BlockSpec((tm, w_), lambda i: (i, 0))
    full = lambda a: pl.BlockSpec(a.shape, lambda i: (0,) * a.ndim)
    return pl.pallas_call(
        functools.partial(_outproj_body, tm=tm, seq=seq),
        grid=(n // tm,),
        in_specs=[row(D_MODEL), row(ATTN_WIDTH), row(CONV_WIDTH),
                  pl.BlockSpec((1, 8, CONV_WIDTH), lambda i: (jnp.maximum(i * (tm // 8) - 1, 0), 0, 0)),
                  row(CONV_WIDTH), full(cw), full(go), full(wo)],
        out_specs=row(D_MODEL),
        out_shape=jax.ShapeDtypeStruct((n, D_MODEL), f32),
        compiler_params=pltpu.CompilerParams(dimension_semantics=("arbitrary",),
                                             vmem_limit_bytes=VMEM_LIMIT),
        name="outproj",
    )(x2, attn2, u, u8, bgate, cw, go, wo)


def _ffn_body(x_ref, g_ref, wu_ref, wd_ref, out_ref, *, chunk):
    x = x_ref[...]
    h = _rms(x, g_ref[...]).astype(bf16)
    acc = x
    for c in range(D_FF // chunk):
        a = jnp.maximum(_dot(h, wu_ref[:, c * chunk:(c + 1) * chunk]), 0.0)
        acc = acc + _dot((a * a).astype(bf16), wd_ref[c * chunk:(c + 1) * chunk, :])
    out_ref[...] = acc


def _ffn(x2, g, wu, wd, *, tm=512, chunk=1024):
    n = x2.shape[0]
    row = pl.BlockSpec((tm, D_MODEL), lambda i: (i, 0))
    full = lambda a: pl.BlockSpec(a.shape, lambda i: (0,) * a.ndim)
    return pl.pallas_call(
        functools.partial(_ffn_body, chunk=chunk),
        grid=(n // tm,),
        in_specs=[row, full(g), full(wu), full(wd)],
        out_specs=row,
        out_shape=jax.ShapeDtypeStruct((n, D_MODEL), f32),
        compiler_params=pltpu.CompilerParams(dimension_semantics=("arbitrary",),
                                             vmem_limit_bytes=VMEM_LIMIT),
        name="ffn",
    )(x2, g, wu, wd)


def _interleave_zero(w):
    z = jnp.zeros(w.shape[:-1] + (HEAD_DIM,), w.dtype)
    return jnp.concatenate([w[..., :HEAD_DIM], z, w[..., HEAD_DIM:], z], axis=-1)


def _prep_params(g_mix_norm, w_in, g_q, g_k, pe_cmp, w_cmp1, b_cmp1, w_cmp2, b_cmp2,
                 conv_w, g_out, w_o, g_ffn_norm, w_up, w_down):
    depth = w_in.shape[0]
    o = np.cumsum([0, ATTN_WIDTH] + [LANES] * 6 + [N_BRANCH * N_HEADS] + [CONV_WIDTH] * 3)
    part = lambda i: w_in[..., int(o[i]):int(o[i + 1])]
    q, kc, vc, ks, vs, kw, vw, gl, hc, cg, bg = [part(i) for i in range(11)]
    gl = jnp.pad(gl, ((0, 0), (0, 0), (0, LANES - gl.shape[-1])))
    w = jnp.concatenate([q, kc, vc, _interleave_zero(ks), _interleave_zero(kw), _interleave_zero(vs),
                         _interleave_zero(vw), hc, cg, bg, gl],
                        axis=-1).astype(bf16)
    tile2 = lambda gk: _interleave_zero(jnp.concatenate([gk, gk], axis=-1))[:, None, :]
    w1 = w_cmp1.astype(bf16).reshape(depth, 2, 2, CMP_STRIDE, HEAD_DIM, CMP_HIDDEN)
    z1 = jnp.zeros_like(w1)
    w1g = jnp.stack([jnp.concatenate([w1, z1], axis=-1), jnp.concatenate([z1, w1], axis=-1)], axis=4)
    w1g = w1g.reshape(depth, 2, 2, CMP_STRIDE * LANES, N_GROUPS * CMP_HIDDEN)
    pe = pe_cmp.reshape(depth, 2, 2, CMP_STRIDE, 1, HEAD_DIM)
    pe = jnp.broadcast_to(pe, (depth, 2, 2, CMP_STRIDE, N_GROUPS, HEAD_DIM)).reshape(depth, 2, 2, 1, -1)
    pe = jnp.pad(pe, ((0, 0), (0, 0), (0, 0), (0, 7), (0, 0))).astype(bf16)
    b1 = jnp.concatenate([b_cmp1, b_cmp1], axis=-1)[:, :, None, :]
    z2 = jnp.zeros_like(w_cmp2)
    w2 = jnp.concatenate([jnp.concatenate([w_cmp2, z2], axis=-1), jnp.concatenate([z2, w_cmp2], axis=-1)],
                         axis=2)
    w2 = _interleave_zero(w2).astype(bf16)
    b2 = _interleave_zero(jnp.concatenate([b_cmp2, b_cmp2], axis=-1))[:, :, None, :]
    return dict(
        gm=g_mix_norm[:, None, :], w=w,
        gq=jnp.tile(g_q, (1, N_HEADS))[:, None, :],
        gks=tile2(g_k[:, 1]), gkw=tile2(g_k[:, 2]), gkc=tile2(g_k[:, 0]),
        w1=w1g, pe=pe, b1=b1,
        w2k=w2[:, 0], w2v=w2[:, 1], b2k=b2[:, 0], b2v=b2[:, 1],
        cw=jnp.pad(conv_w, ((0, 0), (0, 8 - conv_w.shape[1]), (0, 0))),
        go=g_out[:, None, :], wo=w_o.astype(bf16),
        gf=g_ffn_norm[:, None, :], wu=w_up.astype(bf16), wd=w_down.astype(bf16),
    )


def _constants(nsel):
    lane = np.arange(LANES)
    e = (lane[:, None] // HEAD_DIM == lane[None, :] // HEAD_DIM).astype(np.float32)
    tail = np.zeros((N_GROUPS, HEAD_DIM, HEADS_PER_GROUP * Q_BLOCK), np.float32)
    for gi in range(N_GROUPS):
        for r in range(HEADS_PER_GROUP):
            slope = 2.0 ** -(gi * HEADS_PER_GROUP + r + 1)
            tail[gi, 0, r * Q_BLOCK:(r + 1) * Q_BLOCK] = slope * LANES
            tail[gi, 1, r * Q_BLOCK:(r + 1) * Q_BLOCK] = slope
    mct = np.zeros((IMP_ROWS, CMP_CHUNK), np.float32)
    for i in range(CMP_CHUNK):
        lo, hi = i * CMP_STRIDE, i * CMP_STRIDE + CMP_BLOCK
        for jj in range(SEL_PER_CHUNK + 1):
            ov = min(hi, (jj + 1) * SEL_BLOCK) - max(lo, jj * SEL_BLOCK)
            if ov > 0:
                mct[jj, i] = ov / CMP_BLOCK
    kk = np.arange(LANES)[:, None]
    ql = np.tile(np.arange(Q_BLOCK), HEADS_PER_GROUP)[None, :]
    wtab = np.zeros((4, LANES, HEADS_PER_GROUP * Q_BLOCK), np.float32)
    wtab[1] = np.where(kk > ql, 0.0, NEG)
    wtab[2] = np.where(kk <= ql, 0.0, NEG)
    wtab[3] = NEG
    tri = np.arange(nsel)[:, None] < np.arange(nsel)[None, :]
    return (jnp.asarray(e, bf16), jnp.asarray(tail, bf16), jnp.asarray(mct, bf16), jnp.asarray(wtab),
            jnp.asarray(tri, bf16))


def _layer(x2, p, consts, *, batch, seq):
    e, tail, mct, wtab, tri = consts
    g = N_GROUPS
    nqb, nsel, tc = seq // Q_BLOCK, seq // SEL_BLOCK, seq // CMP_STRIDE
    qt, kc, vc, ks, kw, vs, vw, u, bgate, gates = _inproj(
        x2, p["gm"], p["w"], p["gq"], p["gks"], p["gkw"], e, tail, seq=seq)
    kcmp, vcmp = _compress(kc.reshape(batch, tc, CMP_STRIDE * LANES), vc.reshape(batch, tc, CMP_STRIDE * LANES),
                           p["w1"], p["pe"], p["b1"], p["w2k"], p["w2v"], p["b2k"], p["b2v"], p["gkc"], e)
    gate_t = gates[:, :N_HEADS * N_BRANCH].reshape(batch, nqb, Q_BLOCK, g, HEADS_PER_GROUP, N_BRANCH)
    gate_t = gate_t.transpose(0, 3, 1, 5, 4, 2).reshape(batch, g, nqb, N_BRANCH, HEADS_PER_GROUP * Q_BLOCK)
    gate_t = jnp.pad(gate_t, ((0, 0), (0, 0), (0, 0), (0, 8 - N_BRANCH), (0, 0)))
    ocmp, madd, lst, cnt = _cmp_attention(qt, kcmp, vcmp, mct, tri, batch=batch)
    rows3 = lambda a: a.reshape(batch, seq, 2 * LANES)
    smem = lambda a: a[:, :, :, 0, :].reshape(batch * g * nqb, 1, a.shape[-1])
    attn = _slc_win_attention(smem(lst), smem(cnt), qt, rows3(ks), rows3(vs),
                              rows3(kw), rows3(vw), madd, ocmp, gate_t, wtab)
    x2 = _outproj(x2, attn.reshape(batch * seq, ATTN_WIDTH), u, bgate, p["cw"], p["go"], p["wo"], seq=seq)
    return _ffn(x2, p["gf"], p["wu"], p["wd"])


def kernel(x, g_mix_norm, w_in, g_q, g_k, pe_cmp, w_cmp1, b_cmp1, w_cmp2, b_cmp2, conv_w, g_out, w_o,
           g_ffn_norm, w_up, w_down):
    batch, seq, d = x.shape
    assert d == D_MODEL and seq % (CMP_CHUNK * CMP_STRIDE) == 0 and seq >= WINDOW + Q_BLOCK
    params = _prep_params(g_mix_norm, w_in, g_q, g_k, pe_cmp, w_cmp1, b_cmp1, w_cmp2, b_cmp2,
                          conv_w, g_out, w_o, g_ffn_norm, w_up, w_down)
    consts = _constants(seq // SEL_BLOCK)

    def step(x2, p):
        return _layer(x2, p, consts, batch=batch, seq=seq), None

    x2, _ = lax.scan(step, x.reshape(batch * seq, d), params)
    return x2.reshape(batch, seq, d)
```

```python
import functools

import numpy as np
import jax
import jax.numpy as jnp
from jax import lax
from jax.experimental import pallas as pl
from jax.experimental.pallas import tpu as pltpu

f32 = jnp.float32
bf16 = jnp.bfloat16
i32 = jnp.int32

D_MODEL = 1024
HEAD_DIM = 64
N_HEADS = 8
N_GROUPS = 2
HEADS_PER_GROUP = 4
ATTN_WIDTH = 512
CONV_WIDTH = 512
N_BRANCH = 3
CMP_BLOCK = 32
CMP_STRIDE = 16
CMP_HIDDEN = 256
SEL_BLOCK = 64
SEL_TOPK = 16
WINDOW = 512
Q_BLOCK = 128
D_FF = 4096
EPS = 1e-6
NEG = -1e30
FORCE_BONUS = 1e4
PICKED = -3e38
LANES = 128
CMP_CHUNK = 256
SEL_PER_CHUNK = CMP_CHUNK * CMP_STRIDE // SEL_BLOCK
IMP_ROWS = SEL_PER_CHUNK + 8
VMEM_LIMIT = 56 * 1024 * 1024

C_Q, C_KC, C_VC, C_KS, C_KW, C_VS, C_VW, C_HC, C_CG, C_BG, C_GL, C_END = (
    0, 512, 640, 768, 1024, 1280, 1536, 1792, 2304, 2816, 3328, 3456)
SUP_BLOCKS = 32
HEAD_BLOCKS = 16
GROUP_BLOCKS = 4


def _dot(a, b):
    return jnp.dot(a, b, preferred_element_type=f32)


def _dot_nt(a, b):
    return lax.dot_general(a, b, (((1,), (1,)), ((), ())), preferred_element_type=f32)


def _dot_tn(a, b):
    return lax.dot_general(a, b, (((0,), (0,)), ((), ())), preferred_element_type=f32)


def _split_bf16(y):
    hi = y.astype(bf16)
    lo = (y - hi.astype(f32)).astype(bf16)
    return hi, lo


def _head_norm(z, e, g):
    hi, lo = _split_bf16(z * z)
    outs = []
    for c in range(z.shape[1] // LANES):
        sl = slice(c * LANES, (c + 1) * LANES)
        ss = _dot(hi[:, sl], e) + _dot(lo[:, sl], e)
        outs.append(z[:, sl] * lax.rsqrt(ss * (1.0 / HEAD_DIM) + EPS))
    y = outs[0] if len(outs) == 1 else jnp.concatenate(outs, axis=1)
    return y * g


def _pos_cols(pos, width):
    rows = pos.shape[0]
    lane = lax.broadcasted_iota(i32, (rows, LANES), 1)
    tile = jnp.where(lane == HEAD_DIM, (pos >> 7).astype(f32),
                     jnp.where(lane == HEAD_DIM + 1, (pos & 127).astype(f32), 0.0))
    return tile if width == LANES else jnp.concatenate([tile] * (width // LANES), axis=1)


def _ones_col(rows, width):
    lane = lax.broadcasted_iota(i32, (rows, LANES), 1)
    tile = jnp.where(lane == HEAD_DIM, 1.0, 0.0)
    return tile if width == LANES else jnp.concatenate([tile] * (width // LANES), axis=1)


def _inproj_body(x_ref, gm_ref, w_ref, gq_ref, gks_ref, gkw_ref, e_ref, tail_ref,
                 qt_ref, kc_ref, vc_ref, ks_ref, kw_ref, vs_ref, vw_ref, u_ref, bg_ref, gate_ref,
                 *, tm, seq):
    x = x_ref[...]
    ms = jnp.mean(x * x, axis=-1, keepdims=True)
    h = ((x * lax.rsqrt(ms + EPS)) * gm_ref[...]).astype(bf16)
    e = e_ref[...]

    def proj(c0, c1):
        return _dot(h, w_ref[:, c0:c1])

    qn = _head_norm(proj(C_Q, C_KC), e, gq_ref[...]) * (HEAD_DIM ** -0.5)
    gw = HEADS_PER_GROUP * HEAD_DIM
    for cb in range(tm // Q_BLOCK):
        for g in range(N_GROUPS):
            blk = qn[cb * Q_BLOCK:(cb + 1) * Q_BLOCK, g * gw:(g + 1) * gw].T
            top = jnp.concatenate([blk[r * HEAD_DIM:(r + 1) * HEAD_DIM] for r in range(HEADS_PER_GROUP)],
                                  axis=1)
            qt_ref[cb, g] = jnp.concatenate([top.astype(bf16), tail_ref[g]], axis=0)
    kc_ref[...] = proj(C_KC, C_VC).astype(bf16)
    vc_ref[...] = proj(C_VC, C_KS).astype(bf16)
    t0 = (pl.program_id(0) * tm) % seq
    pos = lax.broadcasted_iota(i32, (tm, LANES), 0) + t0
    pc = _pos_cols(pos, 2 * LANES)
    ks_ref[...] = (_head_norm(proj(C_KS, C_KW), e, gks_ref[...]) + pc).astype(bf16)
    kw_ref[...] = (_head_norm(proj(C_KW, C_VS), e, gkw_ref[...]) + pc).astype(bf16)
    ones = _ones_col(tm, 2 * LANES)
    vs_ref[...] = (proj(C_VS, C_VW) + ones).astype(bf16)
    vw_ref[...] = (proj(C_VW, C_HC) + ones).astype(bf16)
    u_ref[...] = proj(C_HC, C_CG) * proj(C_CG, C_BG)
    bg_ref[...] = proj(C_BG, C_GL)
    gate_ref[...] = 1.0 / (1.0 + jnp.exp(-proj(C_GL, C_END)))


def _inproj(x2, gm, w, gq, gks, gkw, e, tail, *, seq, tm=512):
    n = x2.shape[0]
    width = HEADS_PER_GROUP * Q_BLOCK
    row = lambda w_: pl.BlockSpec((tm, w_), lambda i: (i, 0))
    full = lambda a: pl.BlockSpec(a.shape, lambda i: (0,) * a.ndim)
    outs = [(LANES, bf16), (LANES, bf16), (2 * LANES, bf16), (2 * LANES, bf16),
            (2 * LANES, bf16), (2 * LANES, bf16), (CONV_WIDTH, f32), (CONV_WIDTH, f32), (LANES, f32)]
    qt_spec = pl.BlockSpec((tm // Q_BLOCK, N_GROUPS, LANES, width), lambda i: (i, 0, 0, 0))
    qt_shape = jax.ShapeDtypeStruct((n // Q_BLOCK, N_GROUPS, LANES, width), bf16)
    return pl.pallas_call(
        functools.partial(_inproj_body, tm=tm, seq=seq),
        grid=(n // tm,),
        in_specs=[row(D_MODEL), full(gm), full(w), full(gq), full(gks), full(gkw), full(e), full(tail)],
        out_specs=[qt_spec] + [row(w_) for w_, _ in outs],
        out_shape=[qt_shape] + [jax.ShapeDtypeStruct((n, w_), dt) for w_, dt in outs],
        compiler_params=pltpu.CompilerParams(dimension_semantics=("arbitrary",),
                                             vmem_limit_bytes=VMEM_LIMIT),
        name="inproj",
    )(x2, gm, w, gq, gks, gkw, e, tail)


def _gelu_tanh(x):
    return 0.5 * x * (1.0 + jnp.tanh(0.7978845608028654 * (x + 0.044715 * (x * x * x))))


def _compress_body(zk_ref, zv_ref, w1_ref, pe_ref, b1_ref, w2k_ref, w2v_ref, b2k_ref, b2v_ref,
                   gk_ref, e_ref, kc_ref, vc_ref, *, tc):
    last = lax.broadcasted_iota(i32, (tc, 1), 0) == tc - 1

    def hidden(z_ref, kind):
        z = z_ref[0]
        a = _dot(z, w1_ref[kind, 0])
        b = _dot(z, w1_ref[kind, 1])
        b = jnp.concatenate([b[1:], jnp.zeros((1, b.shape[1]), f32)], axis=0)
        bias = (_dot(pe_ref[kind, 0], w1_ref[kind, 0]) + _dot(pe_ref[kind, 1], w1_ref[kind, 1]))[0:1]
        return _gelu_tanh(a + b + bias + b1_ref[kind]).astype(bf16)

    k = _dot(hidden(zk_ref, 0), w2k_ref[...]) + b2k_ref[...]
    k = _head_norm(k, e_ref[...], gk_ref[...])
    pos = lax.broadcasted_iota(i32, (tc, LANES), 0) * CMP_STRIDE + (CMP_BLOCK - 1)
    k = k + _pos_cols(pos, 2 * LANES)
    kc_ref[0] = jnp.where(last, 0.0, k).astype(bf16)
    v = _dot(hidden(zv_ref, 1), w2v_ref[...]) + b2v_ref[...]
    vc_ref[0] = jnp.where(last, 0.0, v + _ones_col(tc, 2 * LANES)).astype(bf16)


def _compress(zk, zv, w1, pe, b1, w2k, w2v, b2k, b2v, gk, e):
    b, tc, _ = zk.shape
    blk = lambda a: pl.BlockSpec((1,) + a.shape[1:], lambda i: (i,) + (0,) * (a.ndim - 1))
    full = lambda a: pl.BlockSpec(a.shape, lambda i: (0,) * a.ndim)
    return pl.pallas_call(
        functools.partial(_compress_body, tc=tc),
        grid=(b,),
        in_specs=[blk(zk), blk(zv)] + [full(a) for a in (w1, pe, b1, w2k, w2v, b2k, b2v, gk, e)],
        out_specs=[pl.BlockSpec((1, tc, 2 * LANES), lambda i: (i, 0, 0))] * 2,
        out_shape=[jax.ShapeDtypeStruct((b, tc, 2 * LANES), bf16)] * 2,
        compiler_params=pltpu.CompilerParams(dimension_semantics=("arbitrary",),
                                             vmem_limit_bytes=VMEM_LIMIT),
        name="compress",
    )(zk, zv, w1, pe, b1, w2k, w2v, b2k, b2v, gk, e)


def _query_pos(c):
    lane = lax.broadcasted_iota(i32, (1, HEADS_PER_GROUP * Q_BLOCK), 1)
    return c * Q_BLOCK + (lane & (Q_BLOCK - 1))


def _pick_top(vs, jf, n_pick):
    vs = list(vs)
    for _ in range(n_pick):
        for g, v in enumerate(vs):
            mx = jnp.max(v, axis=0, keepdims=True)
            idx = jnp.min(jnp.where(v == mx, jf, float(jf.shape[0])), axis=0, keepdims=True)
            vs[g] = jnp.where(jf == idx, PICKED, v)
    return tuple(vs)


def _cmp_body(qt_ref, kc_ref, vc_ref, mct_ref, tri_ref,
              o_ref, madd_ref, lst_ref, cnt_ref, s_scr, imp_scr, *, nsel):
    c = pl.program_id(1)
    groups = range(N_GROUPS)
    width = HEADS_PER_GROUP * Q_BLOCK
    tq = _query_pos(c)
    nch = (c * (Q_BLOCK // CMP_STRIDE) + (Q_BLOCK - CMP_BLOCK) // CMP_STRIDE) // CMP_CHUNK + 1
    sub = lax.broadcasted_iota(i32, (CMP_CHUNK, 1), 0)
    imp_scr[...] = jnp.zeros(imp_scr.shape, f32)
    lanes = lambda g: slice(g * LANES, (g + 1) * LANES)

    def pass1(k, ms):
        r0 = pl.multiple_of(k * CMP_CHUNK, CMP_CHUNK)
        seen = (r0 + sub) * CMP_STRIDE + (CMP_BLOCK - 1) <= tq
        out = []
        for g in groups:
            s = _dot(kc_ref[0, pl.ds(r0, CMP_CHUNK), lanes(g)], qt_ref[0, g])
            s = jnp.where(seen, s, NEG)
            s_scr[g, pl.ds(r0, CMP_CHUNK), :] = s
            out.append(jnp.maximum(ms[g], jnp.max(s, axis=0, keepdims=True)))
        return tuple(out)

    ms = lax.fori_loop(0, nch, pass1, (jnp.full((1, width), NEG, f32),) * N_GROUPS)

    def pass2(k, accs):
        r0 = pl.multiple_of(k * CMP_CHUNK, CMP_CHUNK)
        j0 = pl.multiple_of(k * SEL_PER_CHUNK, SEL_PER_CHUNK)
        out = []
        for g in groups:
            hi, lo = _split_bf16(jnp.exp(s_scr[g, pl.ds(r0, CMP_CHUNK), :] - ms[g]))
            out.append(accs[g] + _dot_tn(vc_ref[0, pl.ds(r0, CMP_CHUNK), lanes(g)], hi))
            imp_scr[g, pl.ds(j0, IMP_ROWS), :] += _dot(mct_ref[...], hi) + _dot(mct_ref[...], lo)
        return tuple(out)

    accs = lax.fori_loop(0, nch, pass2, (jnp.zeros((LANES, width), f32),) * N_GROUPS)

    any_key = tq >= CMP_BLOCK - 1
    j = lax.broadcasted_iota(i32, (nsel, Q_BLOCK), 0)
    jf = j.astype(f32)
    t1 = tq[:, 0:Q_BLOCK]
    jt = t1 >> 6
    valid = j * SEL_BLOCK <= t1
    vs = []
    for g in groups:
        inv = jnp.where(any_key, 1.0 / jnp.maximum(accs[g][HEAD_DIM:HEAD_DIM + 1], 1e-30), 0.0)
        o_ref[0, g, 0] = accs[g][0:HEAD_DIM] * inv
        imp4 = imp_scr[g, 0:nsel, :] * inv
        imp = imp4[:, 0:Q_BLOCK]
        for r in range(1, HEADS_PER_GROUP):
            imp = imp + imp4[:, r * Q_BLOCK:(r + 1) * Q_BLOCK]
        v = jnp.where(valid, imp, NEG)
        vs.append(jnp.where(j == 0, PICKED, jnp.where(j == jt, PICKED, jnp.where(j == jt - 1, PICKED, v))))

    quarter = nsel // 4

    def branch(rows):
        def run(vals):
            if rows == nsel:
                return _pick_top(vals, jf, SEL_TOPK - 3)
            top = _pick_top([v[:rows] for v in vals], jf[:rows], SEL_TOPK - 3)
            return tuple(jnp.concatenate([t, v[rows:]], axis=0) for t, v in zip(top, vals))
        return run

    vs = lax.switch((2 * c + 1) // quarter, [branch(quarter * (i + 1)) for i in range(4)], tuple(vs))

    rr = lax.broadcasted_iota(i32, (nsel, nsel), 0).astype(f32)
    ones8 = jnp.ones((8, Q_BLOCK), bf16)
    jrow = jnp.broadcast_to(lax.broadcasted_iota(i32, (1, nsel), 1).astype(f32), (8, nsel)).astype(bf16)
    for g in groups:
        picked = vs[g] == PICKED
        madd_ref[0, g, 0] = jnp.where(valid, jnp.where(picked, 0.0, NEG), NEG)
        sel = jnp.where(valid, jnp.where(picked, 1.0, 0.0), 0.0).astype(bf16)
        flag = jnp.where(_dot_nt(ones8, sel)[0:1] > 0.0, 1.0, 0.0)
        flag8 = jnp.broadcast_to(flag, (8, nsel)).astype(bf16)
        prefix = _dot(flag8, tri_ref[...])[0:1]
        place = jnp.where(prefix == rr, flag, 0.0).astype(bf16)
        lst_ref[0, g, 0] = _dot_nt(jrow, place).astype(i32)
        cnt_ref[0, g, 0] = _dot(flag8, jnp.ones((nsel, LANES), bf16)).astype(i32)


def _cmp_attention(qt, kc, vc, mct, tri, *, batch):
    g = N_GROUPS
    nqb = qt.shape[0] // batch
    tc = kc.shape[1]
    nsel = tc * CMP_STRIDE // SEL_BLOCK
    width = HEADS_PER_GROUP * Q_BLOCK
    per_q = lambda r_, c_: pl.BlockSpec((1, g, 1, r_, c_), lambda bi, ci: (bi, 0, ci, 0, 0))
    shape = lambda r_, c_, dt: jax.ShapeDtypeStruct((batch, g, nqb, r_, c_), dt)
    return pl.pallas_call(
        functools.partial(_cmp_body, nsel=nsel),
        grid=(batch, nqb),
        in_specs=[pl.BlockSpec((1, g, LANES, width), lambda bi, ci: (bi * nqb + ci, 0, 0, 0)),
                  pl.BlockSpec((1, tc, 2 * LANES), lambda bi, ci: (bi, 0, 0)),
                  pl.BlockSpec((1, tc, 2 * LANES), lambda bi, ci: (bi, 0, 0)),
                  pl.BlockSpec(mct.shape, lambda bi, ci: (0, 0)),
                  pl.BlockSpec(tri.shape, lambda bi, ci: (0, 0))],
        out_specs=[per_q(HEAD_DIM, width), per_q(nsel, Q_BLOCK), per_q(8, nsel), per_q(8, LANES)],
        out_shape=[shape(HEAD_DIM, width, f32), shape(nsel, Q_BLOCK, f32),
                   shape(8, nsel, i32), shape(8, LANES, i32)],
        scratch_shapes=[pltpu.VMEM((g, tc, width), f32), pltpu.VMEM((g, nsel + 8, width), f32)],
        compiler_params=pltpu.CompilerParams(dimension_semantics=("arbitrary",) * 2,
                                             vmem_limit_bytes=VMEM_LIMIT),
        name="cmp_attention",
    )(qt, kc, vc, mct, tri)


def _normalize(o_aug):
    return o_aug[0:HEAD_DIM] * (1.0 / jnp.maximum(o_aug[HEAD_DIM:HEAD_DIM + 1], 1e-30))


def _slc_win_body(lst_ref, cnt_ref, qt_ref, ks_ref, vs_ref, kw_ref, vw_ref,
                  madd_ref, ocmp_ref, gate_ref, wtab_ref, out_ref, s_scr, *, nsel):
    c = pl.program_id(2)
    q0 = pl.multiple_of(c * Q_BLOCK, Q_BLOCK)
    qta = qt_ref[0, 0]
    width = HEADS_PER_GROUP * Q_BLOCK
    nwin = (WINDOW + Q_BLOCK) // LANES
    wq = WINDOW // Q_BLOCK

    ws = pl.multiple_of(jnp.maximum(c - wq, 0) * Q_BLOCK, Q_BLOCK)
    s = _dot(kw_ref[0, pl.ds(ws, WINDOW + Q_BLOCK), :], qta)
    chunks = []
    for k in range(nwin):
        steady = 1 if k == 0 else (2 if k == nwin - 1 else 0)
        tab = jnp.where(c >= wq, steady, jnp.where(k < c, 0, jnp.where(k == c, 2, 3)))
        chunks.append(s[k * LANES:(k + 1) * LANES] + wtab_ref[tab])
    m = chunks[0].max(axis=0, keepdims=True)
    for sk in chunks[1:]:
        m = jnp.maximum(m, sk.max(axis=0, keepdims=True))
    p = jnp.concatenate([jnp.exp(sk - m).astype(bf16) for sk in chunks], axis=0)
    o_win = _normalize(_dot_tn(vw_ref[0, pl.ds(ws, WINDOW + Q_BLOCK), :], p))

    n_off = cnt_ref[0, 0, 0] - 2
    grp_rows = GROUP_BLOCKS * SEL_BLOCK

    def block_ids(first, nblk):
        ids = []
        for u in range(nblk):
            i = first + u
            j = lst_ref[0, 0, jnp.minimum(i, nsel - 1)]
            ids.append((i < n_off, j, pl.multiple_of(j * SEL_BLOCK, SEL_BLOCK)))
        return ids

    def scores(ids):
        kcat = jnp.concatenate([ks_ref[0, pl.ds(r0, SEL_BLOCK), :] for _, _, r0 in ids], axis=0)
        sg = _dot(kcat, qta)
        tiles = []
        for u, (live, j, _) in enumerate(ids):
            mrow = jnp.where(live, madd_ref[0, 0, 0, pl.ds(j, 1), :], NEG)
            mrow = jnp.concatenate([mrow] * HEADS_PER_GROUP, axis=1)
            tiles.append(sg[u * SEL_BLOCK:(u + 1) * SEL_BLOCK] + mrow)
        return tiles

    def values(ids):
        return jnp.concatenate([vs_ref[0, pl.ds(r0, SEL_BLOCK), :] for _, _, r0 in ids], axis=0)

    head = block_ids(0, HEAD_BLOCKS)
    tiles = [_dot(ks_ref[0, pl.ds(q0, Q_BLOCK), :], qta) + wtab_ref[2]] + scores(head)
    m_run = tiles[0].max(axis=0, keepdims=True)
    for su in tiles[1:]:
        m_run = jnp.maximum(m_run, su.max(axis=0, keepdims=True))
    p = jnp.concatenate([jnp.exp(su - m_run).astype(bf16) for su in tiles], axis=0)
    o_run = _dot_tn(jnp.concatenate([vs_ref[0, pl.ds(q0, Q_BLOCK), :], values(head)], axis=0), p)

    def segment(si, carry):
        m_run, o_run = carry
        base = HEAD_BLOCKS + si * SUP_BLOCKS
        ngrp = (jnp.minimum(SUP_BLOCKS, n_off - base) + GROUP_BLOCKS - 1) // GROUP_BLOCKS
        rows = lambda gi: pl.ds(pl.multiple_of(gi * grp_rows, grp_rows), grp_rows)

        def score(gi, mx):
            sg = jnp.concatenate(scores(block_ids(base + gi * GROUP_BLOCKS, GROUP_BLOCKS)), axis=0)
            s_scr[rows(gi), :] = sg
            return jnp.maximum(mx, sg.max(axis=0, keepdims=True))

        m_new = lax.fori_loop(0, ngrp, score, m_run)

        def weigh(gi, acc):
            pg = jnp.exp(s_scr[rows(gi), :] - m_new).astype(bf16)
            return acc + _dot_tn(values(block_ids(base + gi * GROUP_BLOCKS, GROUP_BLOCKS)), pg)

        o_seg = lax.fori_loop(0, ngrp, weigh, jnp.zeros((LANES, width), f32))
        return m_new, jnp.exp(m_run - m_new) * o_run + o_seg

    nseg = (jnp.maximum(n_off - HEAD_BLOCKS, 0) + SUP_BLOCKS - 1) // SUP_BLOCKS
    _, o_run = lax.fori_loop(0, nseg, segment, (m_run, o_run))
    o_slc = _normalize(o_run)

    gate = gate_ref[0, 0, 0]
    mix = gate[0:1] * ocmp_ref[0, 0, 0] + gate[1:2] * o_slc + gate[2:3] * o_win
    rows = jnp.concatenate([mix[:, r * Q_BLOCK:(r + 1) * Q_BLOCK] for r in range(HEADS_PER_GROUP)], axis=0)
    out_ref[0] = rows.T


def _slc_win_attention(lst, cnt, qt, ks, vs, kw, vw, madd, ocmp, gate_t, wtab):
    b, t, _ = ks.shape
    g = N_GROUPS
    nqb, nsel = t // Q_BLOCK, t // SEL_BLOCK
    width = HEADS_PER_GROUP * Q_BLOCK
    flat = lambda bi, gi, ci: (bi * g + gi) * nqb + ci
    smem = lambda w_: pl.BlockSpec((1, 1, w_), lambda bi, gi, ci: (flat(bi, gi, ci), 0, 0),
                                   memory_space=pltpu.SMEM)
    per_q = lambda r_, c_: pl.BlockSpec((1, 1, 1, r_, c_), lambda bi, gi, ci: (bi, gi, ci, 0, 0))
    keys = pl.BlockSpec((1, t, LANES), lambda bi, gi, ci: (bi, 0, gi))
    return pl.pallas_call(
        functools.partial(_slc_win_body, nsel=nsel),
        grid=(b, g, nqb),
        in_specs=[smem(nsel), smem(LANES),
                  pl.BlockSpec((1, 1, LANES, width), lambda bi, gi, ci: (bi * nqb + ci, gi, 0, 0)),
                  keys, keys, keys, keys,
                  per_q(nsel, Q_BLOCK), per_q(HEAD_DIM, width), per_q(8, width),
                  pl.BlockSpec(wtab.shape, lambda bi, gi, ci: (0, 0, 0))],
        out_specs=pl.BlockSpec((1, Q_BLOCK, 2 * LANES), lambda bi, gi, ci: (bi, ci, gi)),
        out_shape=jax.ShapeDtypeStruct((b, t, ATTN_WIDTH), f32),
        scratch_shapes=[pltpu.VMEM((SUP_BLOCKS * SEL_BLOCK, width), f32)],
        compiler_params=pltpu.CompilerParams(dimension_semantics=("arbitrary",) * 3,
                                             vmem_limit_bytes=VMEM_LIMIT),
        name="slc_win_attention",
    )(lst, cnt, qt, ks, vs, kw, vw, madd, ocmp, gate_t, wtab)


def _rms(v, g):
    return (v * lax.rsqrt(jnp.mean(v * v, axis=-1, keepdims=True) + EPS)) * g


def _outproj_body(x_ref, attn_ref, u_ref, uprev_ref, bg_ref, cw_ref, go_ref, wo_ref, out_ref, *, tm, seq):
    first = (pl.program_id(0) * tm) % seq == 0
    u = u_ref[...]
    prev = jnp.where(first, 0.0, uprev_ref[0])
    ext = jnp.concatenate([prev, u], axis=0)
    cw = cw_ref[...]
    conv = cw[0:1] * ext[6:6 + tm] + cw[1:2] * ext[7:7 + tm] + cw[2:3] * u
    conv = bg_ref[...] * conv
    go = go_ref[...]
    mixed = jnp.concatenate([_rms(attn_ref[...], go[:, :ATTN_WIDTH]), _rms(conv, go[:, ATTN_WIDTH:])],
                            axis=1).astype(bf16)
    out_ref[...] = x_ref[...] + _dot(mixed, wo_ref[...])


def _outproj(x2, attn2, u, bgate, cw, go, wo, *, seq, tm=512):
    n = x2.shape[0]
    u8 = u.reshape(n // 8, 8, CONV_WIDTH)
    row = lambda w_: pl.BlockSpec((tm, w_), lambda i: (i, 0))
    full = lambda a: pl.BlockSpec(a.shape, lambda i: (0,) * a.ndim)
    return pl.pallas_call(
        functools.partial(_outproj_body, tm=tm, seq=seq),
        grid=(n // tm,),
        in_specs=[row(D_MODEL), row(ATTN_WIDTH), row(CONV_WIDTH),
                  pl.BlockSpec((1, 8, CONV_WIDTH), lambda i: (jnp.maximum(i * (tm // 8) - 1, 0), 0, 0)),
                  row(CONV_WIDTH), full(cw), full(go), full(wo)],
        out_specs=row(D_MODEL),
        out_shape=jax.ShapeDtypeStruct((n, D_MODEL), f32),
        compiler_params=pltpu.CompilerParams(dimension_semantics=("arbitrary",),
                                             vmem_limit_bytes=VMEM_LIMIT),
        name="outproj",
    )(x2, attn2, u, u8, bgate, cw, go, wo)


def _ffn_body(x_ref, g_ref, wu_ref, wd_ref, out_ref, *, chunk):
    x = x_ref[...]
    h = _rms(x, g_ref[...]).astype(bf16)
    acc = x
    for c in range(D_FF // chunk):
        a = jnp.maximum(_dot(h, wu_ref[:, c * chunk:(c + 1) * chunk]), 0.0)
        acc = acc + _dot((a * a).astype(bf16), wd_ref[c * chunk:(c + 1) * chunk, :])
    out_ref[...] = acc


def _ffn(x2, g, wu, wd, *, tm=512, chunk=1024):
    n = x2.shape[0]
    row = pl.BlockSpec((tm, D_MODEL), lambda i: (i, 0))
    full = lambda a: pl.BlockSpec(a.shape, lambda i: (0,) * a.ndim)
    return pl.pallas_call(
        functools.partial(_ffn_body, chunk=chunk),
        grid=(n // tm,),
        in_specs=[row, full(g), full(wu), full(wd)],
        out_specs=row,
        out_shape=jax.ShapeDtypeStruct((n, D_MODEL), f32),
        compiler_params=pltpu.CompilerParams(dimension_semantics=("arbitrary",),
                                             vmem_limit_bytes=VMEM_LIMIT),
        name="ffn",
    )(x2, g, wu, wd)


def _interleave_zero(w):
    z = jnp.zeros(w.shape[:-1] + (HEAD_DIM,), w.dtype)
    return jnp.concatenate([w[..., :HEAD_DIM], z, w[..., HEAD_DIM:], z], axis=-1)


def _prep_params(g_mix_norm, w_in, g_q, g_k, pe_cmp, w_cmp1, b_cmp1, w_cmp2, b_cmp2,
                 conv_w, g_out, w_o, g_ffn_norm, w_up, w_down):
    depth = w_in.shape[0]
    o = np.cumsum([0, ATTN_WIDTH] + [LANES] * 6 + [N_BRANCH * N_HEADS] + [CONV_WIDTH] * 3)
    part = lambda i: w_in[..., int(o[i]):int(o[i + 1])]
    q, kc, vc, ks, vs, kw, vw, gl, hc, cg, bg = [part(i) for i in range(11)]
    gl = jnp.pad(gl, ((0, 0), (0, 0), (0, LANES - gl.shape[-1])))
    w = jnp.concatenate([q, kc, vc, _interleave_zero(ks), _interleave_zero(kw), _interleave_zero(vs),
                         _interleave_zero(vw), hc, cg, bg, gl],
                        axis=-1).astype(bf16)
    tile2 = lambda gk: _interleave_zero(jnp.concatenate([gk, gk], axis=-1))[:, None, :]
    w1 = w_cmp1.astype(bf16).reshape(depth, 2, 2, CMP_STRIDE, HEAD_DIM, CMP_HIDDEN)
    z1 = jnp.zeros_like(w1)
    w1g = jnp.stack([jnp.concatenate([w1, z1], axis=-1), jnp.concatenate([z1, w1], axis=-1)], axis=4)
    w1g = w1g.reshape(depth, 2, 2, CMP_STRIDE * LANES, N_GROUPS * CMP_HIDDEN)
    pe = pe_cmp.reshape(depth, 2, 2, CMP_STRIDE, 1, HEAD_DIM)
    pe = jnp.broadcast_to(pe, (depth, 2, 2, CMP_STRIDE, N_GROUPS, HEAD_DIM)).reshape(depth, 2, 2, 1, -1)
    pe = jnp.pad(pe, ((0, 0), (0, 0), (0, 0), (0, 7), (0, 0))).astype(bf16)
    b1 = jnp.concatenate([b_cmp1, b_cmp1], axis=-1)[:, :, None, :]
    z2 = jnp.zeros_like(w_cmp2)
    w2 = jnp.concatenate([jnp.concatenate([w_cmp2, z2], axis=-1), jnp.concatenate([z2, w_cmp2], axis=-1)],
                         axis=2)
    w2 = _interleave_zero(w2).astype(bf16)
    b2 = _interleave_zero(jnp.concatenate([b_cmp2, b_cmp2], axis=-1))[:, :, None, :]
    return dict(
        gm=g_mix_norm[:, None, :], w=w,
        gq=jnp.tile(g_q, (1, N_HEADS))[:, None, :],
        gks=tile2(g_k[:, 1]), gkw=tile2(g_k[:, 2]), gkc=tile2(g_k[:, 0]),
        w1=w1g, pe=pe, b1=b1,
        w2k=w2[:, 0], w2v=w2[:, 1], b2k=b2[:, 0], b2v=b2[:, 1],
        cw=jnp.pad(conv_w, ((0, 0), (0, 8 - conv_w.shape[1]), (0, 0))),
        go=g_out[:, None, :], wo=w_o.astype(bf16),
        gf=g_ffn_norm[:, None, :], wu=w_up.astype(bf16), wd=w_down.astype(bf16),
    )


def _constants(nsel):
    lane = np.arange(LANES)
    e = (lane[:, None] // HEAD_DIM == lane[None, :] // HEAD_DIM).astype(np.float32)
    tail = np.zeros((N_GROUPS, HEAD_DIM, HEADS_PER_GROUP * Q_BLOCK), np.float32)
    for gi in range(N_GROUPS):
        for r in range(HEADS_PER_GROUP):
            slope = 2.0 ** -(gi * HEADS_PER_GROUP + r + 1)
            tail[gi, 0, r * Q_BLOCK:(r + 1) * Q_BLOCK] = slope * LANES
            tail[gi, 1, r * Q_BLOCK:(r + 1) * Q_BLOCK] = slope
    mct = np.zeros((IMP_ROWS, CMP_CHUNK), np.float32)
    for i in range(CMP_CHUNK):
        lo, hi = i * CMP_STRIDE, i * CMP_STRIDE + CMP_BLOCK
        for jj in range(SEL_PER_CHUNK + 1):
            ov = min(hi, (jj + 1) * SEL_BLOCK) - max(lo, jj * SEL_BLOCK)
            if ov > 0:
                mct[jj, i] = ov / CMP_BLOCK
    kk = np.arange(LANES)[:, None]
    ql = np.tile(np.arange(Q_BLOCK), HEADS_PER_GROUP)[None, :]
    wtab = np.zeros((4, LANES, HEADS_PER_GROUP * Q_BLOCK), np.float32)
    wtab[1] = np.where(kk > ql, 0.0, NEG)
    wtab[2] = np.where(kk <= ql, 0.0, NEG)
    wtab[3] = NEG
    tri = np.arange(nsel)[:, None] < np.arange(nsel)[None, :]
    return (jnp.asarray(e, bf16), jnp.asarray(tail, bf16), jnp.asarray(mct, bf16), jnp.asarray(wtab),
            jnp.asarray(tri, bf16))


def _layer(x2, p, consts, *, batch, seq):
    e, tail, mct, wtab, tri = consts
    g = N_GROUPS
    nqb, nsel, tc = seq // Q_BLOCK, seq // SEL_BLOCK, seq // CMP_STRIDE
    qt, kc, vc, ks, kw, vs, vw, u, bgate, gates = _inproj(
        x2, p["gm"], p["w"], p["gq"], p["gks"], p["gkw"], e, tail, seq=seq)
    kcmp, vcmp = _compress(kc.reshape(batch, tc, CMP_STRIDE * LANES), vc.reshape(batch, tc, CMP_STRIDE * LANES),
                           p["w1"], p["pe"], p["b1"], p["w2k"], p["w2v"], p["b2k"], p["b2v"], p["gkc"], e)
    gate_t = gates[:, :N_HEADS * N_BRANCH].reshape(batch, nqb, Q_BLOCK, g, HEADS_PER_GROUP, N_BRANCH)
    gate_t = gate_t.transpose(0, 3, 1, 5, 4, 2).reshape(batch, g, nqb, N_BRANCH, HEADS_PER_GROUP * Q_BLOCK)
    gate_t = jnp.pad(gate_t, ((0, 0), (0, 0), (0, 0), (0, 8 - N_BRANCH), (0, 0)))
    ocmp, madd, lst, cnt = _cmp_attention(qt, kcmp, vcmp, mct, tri, batch=batch)
    rows3 = lambda a: a.reshape(batch, seq, 2 * LANES)
    smem = lambda a: a[:, :, :, 0, :].reshape(batch * g * nqb, 1, a.shape[-1])
    attn = _slc_win_attention(smem(lst), smem(cnt), qt, rows3(ks), rows3(vs),
                              rows3(kw), rows3(vw), madd, ocmp, gate_t, wtab)
    x2 = _outproj(x2, attn.reshape(batch * seq, ATTN_WIDTH), u, bgate, p["cw"], p["go"], p["wo"], seq=seq)
    return _ffn(x2, p["gf"], p["wu"], p["wd"])


def kernel(x, g_mix_norm, w_in, g_q, g_k, pe_cmp, w_cmp1, b_cmp1, w_cmp2, b_cmp2, conv_w, g_out, w_o,
           g_ffn_norm, w_up, w_down):
    batch, seq, d = x.shape
    assert d == D_MODEL and seq % (CMP_CHUNK * CMP_STRIDE) == 0 and seq >= WINDOW + Q_BLOCK
    params = _prep_params(g_mix_norm, w_in, g_q, g_k, pe_cmp, w_cmp1, b_cmp1, w_cmp2, b_cmp2,
                          conv_w, g_out, w_o, g_ffn_norm, w_up, w_down)
    consts = _constants(seq // SEL_BLOCK)

    def step(x2, p):
        return _layer(x2, p, consts, batch=batch, seq=seq), None

    x2, _ = lax.scan(step, x.reshape(batch * seq, d), params)
    return x2.reshape(batch, seq, d)
```

```python
import functools

import numpy as np
import jax
import jax.numpy as jnp
from jax import lax
from jax.experimental import pallas as pl
from jax.experimental.pallas import tpu as pltpu

f32 = jnp.float32
bf16 = jnp.bfloat16
i32 = jnp.int32

D_MODEL = 1024
HEAD_DIM = 64
N_HEADS = 8
N_GROUPS = 2
HEADS_PER_GROUP = 4
ATTN_WIDTH = 512
CONV_WIDTH = 512
N_BRANCH = 3
CMP_BLOCK = 32
CMP_STRIDE = 16
CMP_HIDDEN = 256
SEL_BLOCK = 64
SEL_TOPK = 16
WINDOW = 512
Q_BLOCK = 128
D_FF = 4096
EPS = 1e-6
NEG = -1e30
FORCE_BONUS = 1e4
PICKED = -3e38
LANES = 128
CMP_CHUNK = 256
SEL_PER_CHUNK = CMP_CHUNK * CMP_STRIDE // SEL_BLOCK
IMP_ROWS = SEL_PER_CHUNK + 8
VMEM_LIMIT = 56 * 1024 * 1024

C_Q, C_KC, C_VC, C_KS, C_KW, C_VS, C_VW, C_HC, C_CG, C_BG, C_GL, C_END = (
    0, 512, 640, 768, 1024, 1280, 1536, 1792, 2304, 2816, 3328, 3456)
SUP_BLOCKS = 32
HEAD_BLOCKS = 16
GROUP_BLOCKS = 4
QB_PER_STEP = 2


def _dot(a, b):
    return jnp.dot(a, b, preferred_element_type=f32)


def _dot_nt(a, b):
    return lax.dot_general(a, b, (((1,), (1,)), ((), ())), preferred_element_type=f32)


def _dot_tn(a, b):
    return lax.dot_general(a, b, (((0,), (0,)), ((), ())), preferred_element_type=f32)


def _split_bf16(y):
    hi = y.astype(bf16)
    lo = (y - hi.astype(f32)).astype(bf16)
    return hi, lo


def _head_norm(z, e, g):
    hi, lo = _split_bf16(z * z)
    outs = []
    for c in range(z.shape[1] // LANES):
        sl = slice(c * LANES, (c + 1) * LANES)
        ss = _dot(hi[:, sl], e) + _dot(lo[:, sl], e)
        outs.append(z[:, sl] * lax.rsqrt(ss * (1.0 / HEAD_DIM) + EPS))
    y = outs[0] if len(outs) == 1 else jnp.concatenate(outs, axis=1)
    return y * g


def _pos_cols(pos, width):
    rows = pos.shape[0]
    lane = lax.broadcasted_iota(i32, (rows, LANES), 1)
    tile = jnp.where(lane == HEAD_DIM, (pos >> 7).astype(f32),
                     jnp.where(lane == HEAD_DIM + 1, (pos & 127).astype(f32), 0.0))
    return tile if width == LANES else jnp.concatenate([tile] * (width // LANES), axis=1)


def _ones_col(rows, width):
    lane = lax.broadcasted_iota(i32, (rows, LANES), 1)
    tile = jnp.where(lane == HEAD_DIM, 1.0, 0.0)
    return tile if width == LANES else jnp.concatenate([tile] * (width // LANES), axis=1)


def _inproj_body(x_ref, gm_ref, w_ref, gq_ref, gks_ref, gkw_ref, e_ref, tail_ref,
                 qt_ref, kc_ref, vc_ref, ks_ref, kw_ref, vs_ref, vw_ref, u_ref, bg_ref, gate_ref,
                 *, tm, seq):
    x = x_ref[...]
    ms = jnp.mean(x * x, axis=-1, keepdims=True)
    h = ((x * lax.rsqrt(ms + EPS)) * gm_ref[...]).astype(bf16)
    e = e_ref[...]

    def proj(c0, c1):
        return _dot(h, w_ref[:, c0:c1])

    qn = _head_norm(proj(C_Q, C_KC), e, gq_ref[...]) * (HEAD_DIM ** -0.5)
    gw = HEADS_PER_GROUP * HEAD_DIM
    for cb in range(tm // Q_BLOCK):
        for g in range(N_GROUPS):
            blk = qn[cb * Q_BLOCK:(cb + 1) * Q_BLOCK, g * gw:(g + 1) * gw].T
            top = jnp.concatenate([blk[r * HEAD_DIM:(r + 1) * HEAD_DIM] for r in range(HEADS_PER_GROUP)],
                                  axis=1)
            qt_ref[cb, g] = jnp.concatenate([top.astype(bf16), tail_ref[g]], axis=0)
    kc_ref[...] = proj(C_KC, C_VC).astype(bf16)
    vc_ref[...] = proj(C_VC, C_KS).astype(bf16)
    t0 = (pl.program_id(0) * tm) % seq
    pos = lax.broadcasted_iota(i32, (tm, LANES), 0) + t0
    pc = _pos_cols(pos, 2 * LANES)
    ks_ref[...] = (_head_norm(proj(C_KS, C_KW), e, gks_ref[...]) + pc).astype(bf16)
    kw_ref[...] = (_head_norm(proj(C_KW, C_VS), e, gkw_ref[...]) + pc).astype(bf16)
    ones = _ones_col(tm, 2 * LANES)
    vs_ref[...] = (proj(C_VS, C_VW) + ones).astype(bf16)
    vw_ref[...] = (proj(C_VW, C_HC) + ones).astype(bf16)
    u_ref[...] = proj(C_HC, C_CG) * proj(C_CG, C_BG)
    bg_ref[...] = proj(C_BG, C_GL)
    gate_ref[...] = 1.0 / (1.0 + jnp.exp(-proj(C_GL, C_END)))


def _inproj(x2, gm, w, gq, gks, gkw, e, tail, *, seq, tm=512):
    n = x2.shape[0]
    width = HEADS_PER_GROUP * Q_BLOCK
    row = lambda w_: pl.BlockSpec((tm, w_), lambda i: (i, 0))
    full = lambda a: pl.BlockSpec(a.shape, lambda i: (0,) * a.ndim)
    outs = [(LANES, bf16), (LANES, bf16), (2 * LANES, bf16), (2 * LANES, bf16),
            (2 * LANES, bf16), (2 * LANES, bf16), (CONV_WIDTH, f32), (CONV_WIDTH, f32), (LANES, f32)]
    qt_spec = pl.BlockSpec((tm // Q_BLOCK, N_GROUPS, LANES, width), lambda i: (i, 0, 0, 0))
    qt_shape = jax.ShapeDtypeStruct((n // Q_BLOCK, N_GROUPS, LANES, width), bf16)
    return pl.pallas_call(
        functools.partial(_inproj_body, tm=tm, seq=seq),
        grid=(n // tm,),
        in_specs=[row(D_MODEL), full(gm), full(w), full(gq), full(gks), full(gkw), full(e), full(tail)],
        out_specs=[qt_spec] + [row(w_) for w_, _ in outs],
        out_shape=[qt_shape] + [jax.ShapeDtypeStruct((n, w_), dt) for w_, dt in outs],
        compiler_params=pltpu.CompilerParams(dimension_semantics=("arbitrary",),
                                             vmem_limit_bytes=VMEM_LIMIT),
        name="inproj",
    )(x2, gm, w, gq, gks, gkw, e, tail)


def _gelu_tanh(x):
    return 0.5 * x * (1.0 + jnp.tanh(0.7978845608028654 * (x + 0.044715 * (x * x * x))))


def _compress_body(zk_ref, zv_ref, w1_ref, pe_ref, b1_ref, w2k_ref, w2v_ref, b2k_ref, b2v_ref,
                   gk_ref, e_ref, kc_ref, vc_ref, *, tc):
    last = lax.broadcasted_iota(i32, (tc, 1), 0) == tc - 1

    def hidden(z_ref, kind):
        z = z_ref[0]
        a = _dot(z, w1_ref[kind, 0])
        b = _dot(z, w1_ref[kind, 1])
        b = jnp.concatenate([b[1:], jnp.zeros((1, b.shape[1]), f32)], axis=0)
        bias = (_dot(pe_ref[kind, 0], w1_ref[kind, 0]) + _dot(pe_ref[kind, 1], w1_ref[kind, 1]))[0:1]
        return _gelu_tanh(a + b + bias + b1_ref[kind]).astype(bf16)

    k = _dot(hidden(zk_ref, 0), w2k_ref[...]) + b2k_ref[...]
    k = _head_norm(k, e_ref[...], gk_ref[...])
    pos = lax.broadcasted_iota(i32, (tc, LANES), 0) * CMP_STRIDE + (CMP_BLOCK - 1)
    k = k + _pos_cols(pos, 2 * LANES)
    kc_ref[0] = jnp.where(last, 0.0, k).astype(bf16)
    v = _dot(hidden(zv_ref, 1), w2v_ref[...]) + b2v_ref[...]
    vc_ref[0] = jnp.where(last, 0.0, v + _ones_col(tc, 2 * LANES)).astype(bf16)


def _compress(zk, zv, w1, pe, b1, w2k, w2v, b2k, b2v, gk, e):
    b, tc, _ = zk.shape
    blk = lambda a: pl.BlockSpec((1,) + a.shape[1:], lambda i: (i,) + (0,) * (a.ndim - 1))
    full = lambda a: pl.BlockSpec(a.shape, lambda i: (0,) * a.ndim)
    return pl.pallas_call(
        functools.partial(_compress_body, tc=tc),
        grid=(b,),
        in_specs=[blk(zk), blk(zv)] + [full(a) for a in (w1, pe, b1, w2k, w2v, b2k, b2v, gk, e)],
        out_specs=[pl.BlockSpec((1, tc, 2 * LANES), lambda i: (i, 0, 0))] * 2,
        out_shape=[jax.ShapeDtypeStruct((b, tc, 2 * LANES), bf16)] * 2,
        compiler_params=pltpu.CompilerParams(dimension_semantics=("arbitrary",),
                                             vmem_limit_bytes=VMEM_LIMIT),
        name="compress",
    )(zk, zv, w1, pe, b1, w2k, w2v, b2k, b2v, gk, e)


def _query_pos(c):
    lane = lax.broadcasted_iota(i32, (1, HEADS_PER_GROUP * Q_BLOCK), 1)
    return c * Q_BLOCK + (lane & (Q_BLOCK - 1))


def _pick_top(vs, jf, n_pick):
    vs = list(vs)
    for _ in range(n_pick):
        for g, v in enumerate(vs):
            mx = jnp.max(v, axis=0, keepdims=True)
            idx = jnp.min(jnp.where(v == mx, jf, float(jf.shape[0])), axis=0, keepdims=True)
            vs[g] = jnp.where(jf == idx, PICKED, v)
    return tuple(vs)


def _cmp_body(qt_ref, kc_ref, vc_ref, mct_ref, tri_ref,
              o_ref, madd_ref, lst_ref, cnt_ref, s_scr, imp_scr, *, nsel):
    c = pl.program_id(1)
    groups = range(N_GROUPS)
    width = HEADS_PER_GROUP * Q_BLOCK
    tq = _query_pos(c)
    nch = (c * (Q_BLOCK // CMP_STRIDE) + (Q_BLOCK - CMP_BLOCK) // CMP_STRIDE) // CMP_CHUNK + 1
    sub = lax.broadcasted_iota(i32, (CMP_CHUNK, 1), 0)
    imp_scr[...] = jnp.zeros(imp_scr.shape, f32)
    lanes = lambda g: slice(g * LANES, (g + 1) * LANES)

    def pass1(k, ms):
        r0 = pl.multiple_of(k * CMP_CHUNK, CMP_CHUNK)
        seen = (r0 + sub) * CMP_STRIDE + (CMP_BLOCK - 1) <= tq
        out = []
        for g in groups:
            s = _dot(kc_ref[0, pl.ds(r0, CMP_CHUNK), lanes(g)], qt_ref[0, g])
            s = jnp.where(seen, s, NEG)
            s_scr[g, pl.ds(r0, CMP_CHUNK), :] = s
            out.append(jnp.maximum(ms[g], jnp.max(s, axis=0, keepdims=True)))
        return tuple(out)

    ms = lax.fori_loop(0, nch, pass1, (jnp.full((1, width), NEG, f32),) * N_GROUPS)

    def pass2(k, accs):
        r0 = pl.multiple_of(k * CMP_CHUNK, CMP_CHUNK)
        j0 = pl.multiple_of(k * SEL_PER_CHUNK, SEL_PER_CHUNK)
        out = []
        for g in groups:
            hi, lo = _split_bf16(jnp.exp(s_scr[g, pl.ds(r0, CMP_CHUNK), :] - ms[g]))
            out.append(accs[g] + _dot_tn(vc_ref[0, pl.ds(r0, CMP_CHUNK), lanes(g)], hi))
            imp_scr[g, pl.ds(j0, IMP_ROWS), :] += _dot(mct_ref[...], hi) + _dot(mct_ref[...], lo)
        return tuple(out)

    accs = lax.fori_loop(0, nch, pass2, (jnp.zeros((LANES, width), f32),) * N_GROUPS)

    any_key = tq >= CMP_BLOCK - 1
    j = lax.broadcasted_iota(i32, (nsel, Q_BLOCK), 0)
    jf = j.astype(f32)
    t1 = tq[:, 0:Q_BLOCK]
    jt = t1 >> 6
    valid = j * SEL_BLOCK <= t1
    vs = []
    for g in groups:
        inv = jnp.where(any_key, 1.0 / jnp.maximum(accs[g][HEAD_DIM:HEAD_DIM + 1], 1e-30), 0.0)
        o_ref[0, g, 0] = accs[g][0:HEAD_DIM] * inv
        imp4 = imp_scr[g, 0:nsel, :] * inv
        imp = imp4[:, 0:Q_BLOCK]
        for r in range(1, HEADS_PER_GROUP):
            imp = imp + imp4[:, r * Q_BLOCK:(r + 1) * Q_BLOCK]
        v = jnp.where(valid, imp, NEG)
        vs.append(jnp.where(j == 0, PICKED, jnp.where(j == jt, PICKED, jnp.where(j == jt - 1, PICKED, v))))

    quarter = nsel // 4

    def branch(rows):
        def run(vals):
            if rows == nsel:
                return _pick_top(vals, jf, SEL_TOPK - 3)
            top = _pick_top([v[:rows] for v in vals], jf[:rows], SEL_TOPK - 3)
            return tuple(jnp.concatenate([t, v[rows:]], axis=0) for t, v in zip(top, vals))
        return run

    vs = lax.switch((2 * c + 1) // quarter, [branch(quarter * (i + 1)) for i in range(4)], tuple(vs))

    rr = lax.broadcasted_iota(i32, (nsel, nsel), 0).astype(f32)
    ones8 = jnp.ones((8, Q_BLOCK), bf16)
    jrow = jnp.broadcast_to(lax.broadcasted_iota(i32, (1, nsel), 1).astype(f32), (8, nsel)).astype(bf16)
    for g in groups:
        picked = vs[g] == PICKED
        madd_ref[0, g, 0] = jnp.where(valid, jnp.where(picked, 0.0, NEG), NEG)
        sel = jnp.where(valid, jnp.where(picked, 1.0, 0.0), 0.0).astype(bf16)
        flag = jnp.where(_dot_nt(ones8, sel)[0:1] > 0.0, 1.0, 0.0)
        flag8 = jnp.broadcast_to(flag, (8, nsel)).astype(bf16)
        prefix = _dot(flag8, tri_ref[...])[0:1]
        place = jnp.where(prefix == rr, flag, 0.0).astype(bf16)
        lst_ref[0, g, 0] = _dot_nt(jrow, place).astype(i32)
        cnt_ref[0, g, 0] = _dot(flag8, jnp.ones((nsel, LANES), bf16)).astype(i32)


def _cmp_attention(qt, kc, vc, mct, tri, *, batch):
    g = N_GROUPS
    nqb = qt.shape[0] // batch
    tc = kc.shape[1]
    nsel = tc * CMP_STRIDE // SEL_BLOCK
    width = HEADS_PER_GROUP * Q_BLOCK
    per_q = lambda r_, c_: pl.BlockSpec((1, g, 1, r_, c_), lambda bi, ci: (bi, 0, ci, 0, 0))
    shape = lambda r_, c_, dt: jax.ShapeDtypeStruct((batch, g, nqb, r_, c_), dt)
    return pl.pallas_call(
        functools.partial(_cmp_body, nsel=nsel),
        grid=(batch, nqb),
        in_specs=[pl.BlockSpec((1, g, LANES, width), lambda bi, ci: (bi * nqb + ci, 0, 0, 0)),
                  pl.BlockSpec((1, tc, 2 * LANES), lambda bi, ci: (bi, 0, 0)),
                  pl.BlockSpec((1, tc, 2 * LANES), lambda bi, ci: (bi, 0, 0)),
                  pl.BlockSpec(mct.shape, lambda bi, ci: (0, 0)),
                  pl.BlockSpec(tri.shape, lambda bi, ci: (0, 0))],
        out_specs=[per_q(HEAD_DIM, width), per_q(nsel, Q_BLOCK), per_q(8, nsel), per_q(8, LANES)],
        out_shape=[shape(HEAD_DIM, width, f32), shape(nsel, Q_BLOCK, f32),
                   shape(8, nsel, i32), shape(8, LANES, i32)],
        scratch_shapes=[pltpu.VMEM((g, tc, width), f32), pltpu.VMEM((g, nsel + 8, width), f32)],
        compiler_params=pltpu.CompilerParams(dimension_semantics=("arbitrary",) * 2,
                                             vmem_limit_bytes=VMEM_LIMIT),
        name="cmp_attention",
    )(qt, kc, vc, mct, tri)


def _normalize(o_aug):
    return o_aug[0:HEAD_DIM] * (1.0 / jnp.maximum(o_aug[HEAD_DIM:HEAD_DIM + 1], 1e-30))


def _slc_win_body(lst_ref, cnt_ref, qt_ref, ks_ref, vs_ref, kw_ref, vw_ref,
                  madd_ref, ocmp_ref, gate_ref, wtab_ref, out_ref, s_scr, *, nsel):
    width = HEADS_PER_GROUP * Q_BLOCK
    nwin = (WINDOW + Q_BLOCK) // LANES
    wq = WINDOW // Q_BLOCK
    grp_rows = GROUP_BLOCKS * SEL_BLOCK

    class Block:
        def __init__(self, qb):
            self.qb = qb
            self.c = pl.program_id(2) * QB_PER_STEP + qb
            self.q0 = pl.multiple_of(self.c * Q_BLOCK, Q_BLOCK)
            self.qta = qt_ref[qb, 0]
            self.n_off = cnt_ref[qb, 0, 0] - 2

        def block_ids(self, first, nblk):
            ids = []
            for u in range(nblk):
                i = first + u
                j = lst_ref[self.qb, 0, jnp.minimum(i, nsel - 1)]
                ids.append((i < self.n_off, j, pl.multiple_of(j * SEL_BLOCK, SEL_BLOCK)))
            return ids

        def scores(self, ids):
            kcat = jnp.concatenate([ks_ref[0, pl.ds(r0, SEL_BLOCK), :] for _, _, r0 in ids], axis=0)
            sg = _dot(kcat, self.qta)
            tiles = []
            for u, (live, j, _) in enumerate(ids):
                mrow = jnp.where(live, madd_ref[0, 0, self.qb, pl.ds(j, 1), :], NEG)
                mrow = jnp.concatenate([mrow] * HEADS_PER_GROUP, axis=1)
                tiles.append(sg[u * SEL_BLOCK:(u + 1) * SEL_BLOCK] + mrow)
            return tiles

        def values(self, ids):
            return jnp.concatenate([vs_ref[0, pl.ds(r0, SEL_BLOCK), :] for _, _, r0 in ids], axis=0)

    def front(blk):
        c, q0, qta = blk.c, blk.q0, blk.qta
        ws = pl.multiple_of(jnp.maximum(c - wq, 0) * Q_BLOCK, Q_BLOCK)
        s = _dot(kw_ref[0, pl.ds(ws, WINDOW + Q_BLOCK), :], qta)
        chunks = []
        for k in range(nwin):
            steady = 1 if k == 0 else (2 if k == nwin - 1 else 0)
            tab = jnp.where(c >= wq, steady, jnp.where(k < c, 0, jnp.where(k == c, 2, 3)))
            chunks.append(s[k * LANES:(k + 1) * LANES] + wtab_ref[tab])
        head = blk.block_ids(0, HEAD_BLOCKS)
        tiles = [_dot(ks_ref[0, pl.ds(q0, Q_BLOCK), :], qta) + wtab_ref[2]] + blk.scores(head)
        return ws, chunks, head, tiles

    def softmax_pv(tiles, v):
        m = tiles[0].max(axis=0, keepdims=True)
        for su in tiles[1:]:
            m = jnp.maximum(m, su.max(axis=0, keepdims=True))
        p = jnp.concatenate([jnp.exp(su - m).astype(bf16) for su in tiles], axis=0)
        return m, _dot_tn(v, p)

    def middle(blk, ws, chunks, head, tiles):
        _, o_win = softmax_pv(chunks, vw_ref[0, pl.ds(ws, WINDOW + Q_BLOCK), :])
        m_run, o_run = softmax_pv(
            tiles, jnp.concatenate([vs_ref[0, pl.ds(blk.q0, Q_BLOCK), :], blk.values(head)], axis=0))
        return _normalize(o_win), m_run, o_run

    def rest(blk, m_run, o_run):
        def segment(si, carry):
            m_run, o_run = carry
            base = HEAD_BLOCKS + si * SUP_BLOCKS
            ngrp = (jnp.minimum(SUP_BLOCKS, blk.n_off - base) + GROUP_BLOCKS - 1) // GROUP_BLOCKS
            rows = lambda gi: pl.ds(pl.multiple_of(gi * grp_rows, grp_rows), grp_rows)

            def score(gi, mx):
                sg = jnp.concatenate(blk.scores(blk.block_ids(base + gi * GROUP_BLOCKS, GROUP_BLOCKS)), axis=0)
                s_scr[rows(gi), :] = sg
                return jnp.maximum(mx, sg.max(axis=0, keepdims=True))

            m_new = lax.fori_loop(0, ngrp, score, m_run)

            def weigh(gi, acc):
                pg = jnp.exp(s_scr[rows(gi), :] - m_new).astype(bf16)
                return acc + _dot_tn(blk.values(blk.block_ids(base + gi * GROUP_BLOCKS, GROUP_BLOCKS)), pg)

            o_seg = lax.fori_loop(0, ngrp, weigh, jnp.zeros((LANES, width), f32))
            return m_new, jnp.exp(m_run - m_new) * o_run + o_seg

        nseg = (jnp.maximum(blk.n_off - HEAD_BLOCKS, 0) + SUP_BLOCKS - 1) // SUP_BLOCKS
        return lax.fori_loop(0, nseg, segment, (m_run, o_run))[1]

    blocks = [Block(qb) for qb in range(QB_PER_STEP)]
    fronts = [front(blk) for blk in blocks]
    fronts = [middle(blk, *f) for blk, f in zip(blocks, fronts)]
    tails = [rest(blk, m_run, o_run) for blk, (_, m_run, o_run) in zip(blocks, fronts)]
    for blk, (o_win, _, _), o_run in zip(blocks, fronts, tails):
        qb = blk.qb
        gate = gate_ref[0, 0, qb]
        mix = gate[0:1] * ocmp_ref[0, 0, qb] + gate[1:2] * _normalize(o_run) + gate[2:3] * o_win
        rows = jnp.concatenate([mix[:, r * Q_BLOCK:(r + 1) * Q_BLOCK] for r in range(HEADS_PER_GROUP)],
                               axis=0)
        out_ref[0, qb * Q_BLOCK:(qb + 1) * Q_BLOCK, :] = rows.T


def _slc_win_attention(lst, cnt, qt, ks, vs, kw, vw, madd, ocmp, gate_t, wtab):
    b, t, _ = ks.shape
    g = N_GROUPS
    nqb, nsel = t // Q_BLOCK, t // SEL_BLOCK
    width = HEADS_PER_GROUP * Q_BLOCK
    nstep = nqb // QB_PER_STEP
    flat = lambda bi, gi, ci: (bi * g + gi) * nstep + ci
    smem = lambda w_: pl.BlockSpec((QB_PER_STEP, 1, w_), lambda bi, gi, ci: (flat(bi, gi, ci), 0, 0),
                                   memory_space=pltpu.SMEM)
    per_q = lambda r_, c_: pl.BlockSpec((1, 1, QB_PER_STEP, r_, c_), lambda bi, gi, ci: (bi, gi, ci, 0, 0))
    keys = pl.BlockSpec((1, t, LANES), lambda bi, gi, ci: (bi, 0, gi))
    return pl.pallas_call(
        functools.partial(_slc_win_body, nsel=nsel),
        grid=(b, g, nstep),
        in_specs=[smem(nsel), smem(LANES),
                  pl.BlockSpec((QB_PER_STEP, 1, LANES, width), lambda bi, gi, ci: (bi * nstep + ci, gi, 0, 0)),
                  keys, keys, keys, keys,
                  per_q(nsel, Q_BLOCK), per_q(HEAD_DIM, width), per_q(8, width),
                  pl.BlockSpec(wtab.shape, lambda bi, gi, ci: (0, 0, 0))],
        out_specs=pl.BlockSpec((1, QB_PER_STEP * Q_BLOCK, 2 * LANES), lambda bi, gi, ci: (bi, ci, gi)),
        out_shape=jax.ShapeDtypeStruct((b, t, ATTN_WIDTH), f32),
        scratch_shapes=[pltpu.VMEM((SUP_BLOCKS * SEL_BLOCK, width), f32)],
        compiler_params=pltpu.CompilerParams(dimension_semantics=("arbitrary",) * 3,
                                             vmem_limit_bytes=VMEM_LIMIT),
        name="slc_win_attention",
    )(lst, cnt, qt, ks, vs, kw, vw, madd, ocmp, gate_t, wtab)


def _rms(v, g):
    return (v * lax.rsqrt(jnp.mean(v * v, axis=-1, keepdims=True) + EPS)) * g


def _outproj_body(x_ref, attn_ref, u_ref, uprev_ref, bg_ref, cw_ref, go_ref, wo_ref, out_ref, *, tm, seq):
    first = (pl.program_id(0) * tm) % seq == 0
    u = u_ref[...]
    prev = jnp.where(first, 0.0, uprev_ref[0])
    ext = jnp.concatenate([prev, u], axis=0)
    cw = cw_ref[...]
    conv = cw[0:1] * ext[6:6 + tm] + cw[1:2] * ext[7:7 + tm] + cw[2:3] * u
    conv = bg_ref[...] * conv
    go = go_ref[...]
    mixed = jnp.concatenate([_rms(attn_ref[...], go[:, :ATTN_WIDTH]), _rms(conv, go[:, ATTN_WIDTH:])],
                            axis=1).astype(bf16)
    out_ref[...] = x_ref[...] + _dot(mixed, wo_ref[...])


def _outproj(x2, attn2, u, bgate, cw, go, wo, *, seq, tm=512):
    n = x2.shape[0]
    u8 = u.reshape(n // 8, 8, CONV_WIDTH)
    row = lambda w_: pl.BlockSpec((tm, w_), lambda i: (i, 0))
    full = lambda a: pl.BlockSpec(a.shape, lambda i: (0,) * a.ndim)
    return pl.pallas_call(
        functools.partial(_outproj_body, tm=tm, seq=seq),
        grid=(n // tm,),
        in_specs=[row(D_MODEL), row(ATTN_WIDTH), row(CONV_WIDTH),
                  pl.BlockSpec((1, 8, CONV_WIDTH), lambda i: (jnp.maximum(i * (tm // 8) - 1, 0), 0, 0)),
                  row(CONV_WIDTH), full(cw), full(go), full(wo)],
        out_specs=row(D_MODEL),
        out_shape=jax.ShapeDtypeStruct((n, D_MODEL), f32),
        compiler_params=pltpu.CompilerParams(dimension_semantics=("arbitrary",),
                                             vmem_limit_bytes=VMEM_LIMIT),
        name="outproj",
    )(x2, attn2, u, u8, bgate, cw, go, wo)


def _ffn_body(x_ref, g_ref, wu_ref, wd_ref, out_ref, *, chunk):
    x = x_ref[...]
    h = _rms(x, g_ref[...]).astype(bf16)
    acc = x
    for c in range(D_FF // chunk):
        a = jnp.maximum(_dot(h, wu_ref[:, c * chunk:(c + 1) * chunk]), 0.0)
        acc = acc + _dot((a * a).astype(bf16), wd_ref[c * chunk:(c + 1) * chunk, :])
    out_ref[...] = acc


def _ffn(x2, g, wu, wd, *, tm=512, chunk=1024):
    n = x2.shape[0]
    row = pl.BlockSpec((tm, D_MODEL), lambda i: (i, 0))
    full = lambda a: pl.BlockSpec(a.shape, lambda i: (0,) * a.ndim)
    return pl.pallas_call(
        functools.partial(_ffn_body, chunk=chunk),
        grid=(n // tm,),
        in_specs=[row, full(g), full(wu), full(wd)],
        out_specs=row,
        out_shape=jax.ShapeDtypeStruct((n, D_MODEL), f32),
        compiler_params=pltpu.CompilerParams(dimension_semantics=("arbitrary",),
                                             vmem_limit_bytes=VMEM_LIMIT),
        name="ffn",
    )(x2, g, wu, wd)


def _interleave_zero(w):
    z = jnp.zeros(w.shape[:-1] + (HEAD_DIM,), w.dtype)
    return jnp.concatenate([w[..., :HEAD_DIM], z, w[..., HEAD_DIM:], z], axis=-1)


def _prep_params(g_mix_norm, w_in, g_q, g_k, pe_cmp, w_cmp1, b_cmp1, w_cmp2, b_cmp2,
                 conv_w, g_out, w_o, g_ffn_norm, w_up, w_down):
    depth = w_in.shape[0]
    o = np.cumsum([0, ATTN_WIDTH] + [LANES] * 6 + [N_BRANCH * N_HEADS] + [CONV_WIDTH] * 3)
    part = lambda i: w_in[..., int(o[i]):int(o[i + 1])]
    q, kc, vc, ks, vs, kw, vw, gl, hc, cg, bg = [part(i) for i in range(11)]
    gl = jnp.pad(gl, ((0, 0), (0, 0), (0, LANES - gl.shape[-1])))
    w = jnp.concatenate([q, kc, vc, _interleave_zero(ks), _interleave_zero(kw), _interleave_zero(vs),
                         _interleave_zero(vw), hc, cg, bg, gl],
                        axis=-1).astype(bf16)
    tile2 = lambda gk: _interleave_zero(jnp.concatenate([gk, gk], axis=-1))[:, None, :]
    w1 = w_cmp1.astype(bf16).reshape(depth, 2, 2, CMP_STRIDE, HEAD_DIM, CMP_HIDDEN)
    z1 = jnp.zeros_like(w1)
    w1g = jnp.stack([jnp.concatenate([w1, z1], axis=-1), jnp.concatenate([z1, w1], axis=-1)], axis=4)
    w1g = w1g.reshape(depth, 2, 2, CMP_STRIDE * LANES, N_GROUPS * CMP_HIDDEN)
    pe = pe_cmp.reshape(depth, 2, 2, CMP_STRIDE, 1, HEAD_DIM)
    pe = jnp.broadcast_to(pe, (depth, 2, 2, CMP_STRIDE, N_GROUPS, HEAD_DIM)).reshape(depth, 2, 2, 1, -1)
    pe = jnp.pad(pe, ((0, 0), (0, 0), (0, 0), (0, 7), (0, 0))).astype(bf16)
    b1 = jnp.concatenate([b_cmp1, b_cmp1], axis=-1)[:, :, None, :]
    z2 = jnp.zeros_like(w_cmp2)
    w2 = jnp.concatenate([jnp.concatenate([w_cmp2, z2], axis=-1), jnp.concatenate([z2, w_cmp2], axis=-1)],
                         axis=2)
    w2 = _interleave_zero(w2).astype(bf16)
    b2 = _interleave_zero(jnp.concatenate([b_cmp2, b_cmp2], axis=-1))[:, :, None, :]
    return dict(
        gm=g_mix_norm[:, None, :], w=w,
        gq=jnp.tile(g_q, (1, N_HEADS))[:, None, :],
        gks=tile2(g_k[:, 1]), gkw=tile2(g_k[:, 2]), gkc=tile2(g_k[:, 0]),
        w1=w1g, pe=pe, b1=b1,
        w2k=w2[:, 0], w2v=w2[:, 1], b2k=b2[:, 0], b2v=b2[:, 1],
        cw=jnp.pad(conv_w, ((0, 0), (0, 8 - conv_w.shape[1]), (0, 0))),
        go=g_out[:, None, :], wo=w_o.astype(bf16),
        gf=g_ffn_norm[:, None, :], wu=w_up.astype(bf16), wd=w_down.astype(bf16),
    )


def _constants(nsel):
    lane = np.arange(LANES)
    e = (lane[:, None] // HEAD_DIM == lane[None, :] // HEAD_DIM).astype(np.float32)
    tail = np.zeros((N_GROUPS, HEAD_DIM, HEADS_PER_GROUP * Q_BLOCK), np.float32)
    for gi in range(N_GROUPS):
        for r in range(HEADS_PER_GROUP):
            slope = 2.0 ** -(gi * HEADS_PER_GROUP + r + 1)
            tail[gi, 0, r * Q_BLOCK:(r + 1) * Q_BLOCK] = slope * LANES
            tail[gi, 1, r * Q_BLOCK:(r + 1) * Q_BLOCK] = slope
    mct = np.zeros((IMP_ROWS, CMP_CHUNK), np.float32)
    for i in range(CMP_CHUNK):
        lo, hi = i * CMP_STRIDE, i * CMP_STRIDE + CMP_BLOCK
        for jj in range(SEL_PER_CHUNK + 1):
            ov = min(hi, (jj + 1) * SEL_BLOCK) - max(lo, jj * SEL_BLOCK)
            if ov > 0:
                mct[jj, i] = ov / CMP_BLOCK
    kk = np.arange(LANES)[:, None]
    ql = np.tile(np.arange(Q_BLOCK), HEADS_PER_GROUP)[None, :]
    wtab = np.zeros((4, LANES, HEADS_PER_GROUP * Q_BLOCK), np.float32)
    wtab[1] = np.where(kk > ql, 0.0, NEG)
    wtab[2] = np.where(kk <= ql, 0.0, NEG)
    wtab[3] = NEG
    tri = np.arange(nsel)[:, None] < np.arange(nsel)[None, :]
    return (jnp.asarray(e, bf16), jnp.asarray(tail, bf16), jnp.asarray(mct, bf16), jnp.asarray(wtab),
            jnp.asarray(tri, bf16))


def _layer(x2, p, consts, *, batch, seq):
    e, tail, mct, wtab, tri = consts
    g = N_GROUPS
    nqb, nsel, tc = seq // Q_BLOCK, seq // SEL_BLOCK, seq // CMP_STRIDE
    qt, kc, vc, ks, kw, vs, vw, u, bgate, gates = _inproj(
        x2, p["gm"], p["w"], p["gq"], p["gks"], p["gkw"], e, tail, seq=seq)
    kcmp, vcmp = _compress(kc.reshape(batch, tc, CMP_STRIDE * LANES), vc.reshape(batch, tc, CMP_STRIDE * LANES),
                           p["w1"], p["pe"], p["b1"], p["w2k"], p["w2v"], p["b2k"], p["b2v"], p["gkc"], e)
    gate_t = gates[:, :N_HEADS * N_BRANCH].reshape(batch, nqb, Q_BLOCK, g, HEADS_PER_GROUP, N_BRANCH)
    gate_t = gate_t.transpose(0, 3, 1, 5, 4, 2).reshape(batch, g, nqb, N_BRANCH, HEADS_PER_GROUP * Q_BLOCK)
    gate_t = jnp.pad(gate_t, ((0, 0), (0, 0), (0, 0), (0, 8 - N_BRANCH), (0, 0)))
    ocmp, madd, lst, cnt = _cmp_attention(qt, kcmp, vcmp, mct, tri, batch=batch)
    rows3 = lambda a: a.reshape(batch, seq, 2 * LANES)
    smem = lambda a: a[:, :, :, 0, :].reshape(batch * g * nqb, 1, a.shape[-1])
    attn = _slc_win_attention(smem(lst), smem(cnt), qt, rows3(ks), rows3(vs),
                              rows3(kw), rows3(vw), madd, ocmp, gate_t, wtab)
    x2 = _outproj(x2, attn.reshape(batch * seq, ATTN_WIDTH), u, bgate, p["cw"], p["go"], p["wo"], seq=seq)
    return _ffn(x2, p["gf"], p["wu"], p["wd"])


def kernel(x, g_mix_norm, w_in, g_q, g_k, pe_cmp, w_cmp1, b_cmp1, w_cmp2, b_cmp2, conv_w, g_out, w_o,
           g_ffn_norm, w_up, w_down):
    batch, seq, d = x.shape
    assert d == D_MODEL and seq % (CMP_CHUNK * CMP_STRIDE) == 0 and seq >= WINDOW + Q_BLOCK
    params = _prep_params(g_mix_norm, w_in, g_q, g_k, pe_cmp, w_cmp1, b_cmp1, w_cmp2, b_cmp2,
                          conv_w, g_out, w_o, g_ffn_norm, w_up, w_down)
    consts = _constants(seq // SEL_BLOCK)

    def step(x2, p):
        return _layer(x2, p, consts, batch=batch, seq=seq), None

    x2, _ = lax.scan(step, x.reshape(batch * seq, d), params)
    return x2.reshape(batch, seq, d)
```

```python
import functools

import numpy as np
import jax
import jax.numpy as jnp
from jax import lax
from jax.experimental import pallas as pl
from jax.experimental.pallas import tpu as pltpu

f32 = jnp.float32
bf16 = jnp.bfloat16
i32 = jnp.int32

D_MODEL = 1024
HEAD_DIM = 64
N_HEADS = 8
N_GROUPS = 2
HEADS_PER_GROUP = 4
ATTN_WIDTH = 512
CONV_WIDTH = 512
N_BRANCH = 3
CMP_BLOCK = 32
CMP_STRIDE = 16
CMP_HIDDEN = 256
SEL_BLOCK = 64
SEL_TOPK = 16
WINDOW = 512
Q_BLOCK = 128
D_FF = 4096
EPS = 1e-6
NEG = -1e30
FORCE_BONUS = 1e4
PICKED = -3e38
LANES = 128
CMP_CHUNK = 256
SEL_PER_CHUNK = CMP_CHUNK * CMP_STRIDE // SEL_BLOCK
IMP_ROWS = SEL_PER_CHUNK + 8
VMEM_LIMIT = 56 * 1024 * 1024

C_Q, C_KC, C_VC, C_KS, C_KW, C_VS, C_VW, C_HC, C_CG, C_BG, C_GL, C_END = (
    0, 512, 640, 768, 1024, 1280, 1536, 1792, 2304, 2816, 3328, 3456)
SUP_BLOCKS = 32
HEAD_BLOCKS = 16
GROUP_BLOCKS = 4
QB_PER_STEP = 2


def _dot(a, b):
    return jnp.dot(a, b, preferred_element_type=f32)


def _dot_nt(a, b):
    return lax.dot_general(a, b, (((1,), (1,)), ((), ())), preferred_element_type=f32)


def _dot_tn(a, b):
    return lax.dot_general(a, b, (((0,), (0,)), ((), ())), preferred_element_type=f32)


def _split_bf16(y):
    hi = y.astype(bf16)
    lo = (y - hi.astype(f32)).astype(bf16)
    return hi, lo


def _head_norm(z, e, g):
    hi, lo = _split_bf16(z * z)
    outs = []
    for c in range(z.shape[1] // LANES):
        sl = slice(c * LANES, (c + 1) * LANES)
        ss = _dot(hi[:, sl], e) + _dot(lo[:, sl], e)
        outs.append(z[:, sl] * lax.rsqrt(ss * (1.0 / HEAD_DIM) + EPS))
    y = outs[0] if len(outs) == 1 else jnp.concatenate(outs, axis=1)
    return y * g


def _pos_cols(pos, width):
    rows = pos.shape[0]
    lane = lax.broadcasted_iota(i32, (rows, LANES), 1)
    tile = jnp.where(lane == HEAD_DIM, (pos >> 7).astype(f32),
                     jnp.where(lane == HEAD_DIM + 1, (pos & 127).astype(f32), 0.0))
    return tile if width == LANES else jnp.concatenate([tile] * (width // LANES), axis=1)


def _ones_col(rows, width):
    lane = lax.broadcasted_iota(i32, (rows, LANES), 1)
    tile = jnp.where(lane == HEAD_DIM, 1.0, 0.0)
    return tile if width == LANES else jnp.concatenate([tile] * (width // LANES), axis=1)


def _inproj_body(x_ref, gm_ref, w_ref, gq_ref, gks_ref, gkw_ref, e_ref, tail_ref,
                 qt_ref, kc_ref, vc_ref, ks_ref, kw_ref, vs_ref, vw_ref, u_ref, bg_ref, gate_ref,
                 *, tm, seq):
    x = x_ref[...]
    ms = jnp.mean(x * x, axis=-1, keepdims=True)
    h = ((x * lax.rsqrt(ms + EPS)) * gm_ref[...]).astype(bf16)
    e = e_ref[...]

    def proj(c0, c1):
        return _dot(h, w_ref[:, c0:c1])

    qn = _head_norm(proj(C_Q, C_KC), e, gq_ref[...]) * (HEAD_DIM ** -0.5)
    gw = HEADS_PER_GROUP * HEAD_DIM
    for cb in range(tm // Q_BLOCK):
        for g in range(N_GROUPS):
            blk = qn[cb * Q_BLOCK:(cb + 1) * Q_BLOCK, g * gw:(g + 1) * gw].T
            top = jnp.concatenate([blk[r * HEAD_DIM:(r + 1) * HEAD_DIM] for r in range(HEADS_PER_GROUP)],
                                  axis=1)
            qt_ref[cb, g] = jnp.concatenate([top.astype(bf16), tail_ref[g]], axis=0)
    kc_ref[...] = proj(C_KC, C_VC).astype(bf16)
    vc_ref[...] = proj(C_VC, C_KS).astype(bf16)
    t0 = (pl.program_id(0) * tm) % seq
    pos = lax.broadcasted_iota(i32, (tm, LANES), 0) + t0
    pc = _pos_cols(pos, 2 * LANES)
    ks_ref[...] = (_head_norm(proj(C_KS, C_KW), e, gks_ref[...]) + pc).astype(bf16)
    kw_ref[...] = (_head_norm(proj(C_KW, C_VS), e, gkw_ref[...]) + pc).astype(bf16)
    ones = _ones_col(tm, 2 * LANES)
    vs_ref[...] = (proj(C_VS, C_VW) + ones).astype(bf16)
    vw_ref[...] = (proj(C_VW, C_HC) + ones).astype(bf16)
    u_ref[...] = proj(C_HC, C_CG) * proj(C_CG, C_BG)
    bg_ref[...] = proj(C_BG, C_GL)
    gate_ref[...] = 1.0 / (1.0 + jnp.exp(-proj(C_GL, C_END)))


def _inproj(x2, gm, w, gq, gks, gkw, e, tail, *, seq, tm=512):
    n = x2.shape[0]
    width = HEADS_PER_GROUP * Q_BLOCK
    row = lambda w_: pl.BlockSpec((tm, w_), lambda i: (i, 0))
    full = lambda a: pl.BlockSpec(a.shape, lambda i: (0,) * a.ndim)
    outs = [(LANES, bf16), (LANES, bf16), (2 * LANES, bf16), (2 * LANES, bf16),
            (2 * LANES, bf16), (2 * LANES, bf16), (CONV_WIDTH, f32), (CONV_WIDTH, f32), (LANES, f32)]
    qt_spec = pl.BlockSpec((tm // Q_BLOCK, N_GROUPS, LANES, width), lambda i: (i, 0, 0, 0))
    qt_shape = jax.ShapeDtypeStruct((n // Q_BLOCK, N_GROUPS, LANES, width), bf16)
    return pl.pallas_call(
        functools.partial(_inproj_body, tm=tm, seq=seq),
        grid=(n // tm,),
        in_specs=[row(D_MODEL), full(gm), full(w), full(gq), full(gks), full(gkw), full(e), full(tail)],
        out_specs=[qt_spec] + [row(w_) for w_, _ in outs],
        out_shape=[qt_shape] + [jax.ShapeDtypeStruct((n, w_), dt) for w_, dt in outs],
        compiler_params=pltpu.CompilerParams(dimension_semantics=("arbitrary",),
                                             vmem_limit_bytes=VMEM_LIMIT),
        name="inproj",
    )(x2, gm, w, gq, gks, gkw, e, tail)


def _gelu_tanh(x):
    return 0.5 * x * (1.0 + jnp.tanh(0.7978845608028654 * (x + 0.044715 * (x * x * x))))


def _compress_body(zk_ref, zv_ref, w1_ref, pe_ref, b1_ref, w2k_ref, w2v_ref, b2k_ref, b2v_ref,
                   gk_ref, e_ref, kc_ref, vc_ref, *, tc):
    last = lax.broadcasted_iota(i32, (tc, 1), 0) == tc - 1

    def hidden(z_ref, kind):
        z = z_ref[0]
        a = _dot(z, w1_ref[kind, 0])
        b = _dot(z, w1_ref[kind, 1])
        b = jnp.concatenate([b[1:], jnp.zeros((1, b.shape[1]), f32)], axis=0)
        bias = (_dot(pe_ref[kind, 0], w1_ref[kind, 0]) + _dot(pe_ref[kind, 1], w1_ref[kind, 1]))[0:1]
        return _gelu_tanh(a + b + bias + b1_ref[kind]).astype(bf16)

    k = _dot(hidden(zk_ref, 0), w2k_ref[...]) + b2k_ref[...]
    k = _head_norm(k, e_ref[...], gk_ref[...])
    pos = lax.broadcasted_iota(i32, (tc, LANES), 0) * CMP_STRIDE + (CMP_BLOCK - 1)
    k = k + _pos_cols(pos, 2 * LANES)
    kc_ref[0] = jnp.where(last, 0.0, k).astype(bf16)
    v = _dot(hidden(zv_ref, 1), w2v_ref[...]) + b2v_ref[...]
    vc_ref[0] = jnp.where(last, 0.0, v + _ones_col(tc, 2 * LANES)).astype(bf16)


def _compress(zk, zv, w1, pe, b1, w2k, w2v, b2k, b2v, gk, e):
    b, tc, _ = zk.shape
    blk = lambda a: pl.BlockSpec((1,) + a.shape[1:], lambda i: (i,) + (0,) * (a.ndim - 1))
    full = lambda a: pl.BlockSpec(a.shape, lambda i: (0,) * a.ndim)
    return pl.pallas_call(
        functools.partial(_compress_body, tc=tc),
        grid=(b,),
        in_specs=[blk(zk), blk(zv)] + [full(a) for a in (w1, pe, b1, w2k, w2v, b2k, b2v, gk, e)],
        out_specs=[pl.BlockSpec((1, tc, 2 * LANES), lambda i: (i, 0, 0))] * 2,
        out_shape=[jax.ShapeDtypeStruct((b, tc, 2 * LANES), bf16)] * 2,
        compiler_params=pltpu.CompilerParams(dimension_semantics=("arbitrary",),
                                             vmem_limit_bytes=VMEM_LIMIT),
        name="compress",
    )(zk, zv, w1, pe, b1, w2k, w2v, b2k, b2v, gk, e)


def _query_pos(c):
    lane = lax.broadcasted_iota(i32, (1, HEADS_PER_GROUP * Q_BLOCK), 1)
    return c * Q_BLOCK + (lane & (Q_BLOCK - 1))


def _pick_top(vs, jf, n_pick):
    vs = list(vs)
    for _ in range(n_pick):
        for g, v in enumerate(vs):
            mx = jnp.max(v, axis=0, keepdims=True)
            idx = jnp.min(jnp.where(v == mx, jf, float(jf.shape[0])), axis=0, keepdims=True)
            vs[g] = jnp.where(jf == idx, PICKED, v)
    return tuple(vs)


def _cmp_variant(nchunk, c, qt_ref, kc_ref, vc_ref, mct_ref, tri_ref, o_ref, madd_ref, lst_ref, cnt_ref,
                 *, nsel):
    groups = range(N_GROUPS)
    nrow = nchunk * CMP_CHUNK
    npre = nchunk * SEL_PER_CHUNK
    tq = _query_pos(c)
    sub = lax.broadcasted_iota(i32, (CMP_CHUNK, 1), 0)
    lanes = lambda g: slice(g * LANES, (g + 1) * LANES)
    chunk = lambda a, k: a[k * CMP_CHUNK:(k + 1) * CMP_CHUNK]

    tiles = []
    for g in groups:
        s = _dot(kc_ref[0, 0:nrow, lanes(g)], qt_ref[0, g])
        row = []
        for k in range(nchunk):
            t = chunk(s, k)
            if k >= nchunk - 2:
                seen = (k * CMP_CHUNK + sub) * CMP_STRIDE + (CMP_BLOCK - 1) <= tq
                t = jnp.where(seen, t, NEG)
            row.append(t)
        tiles.append(row)

    accs, imps = [], []
    for g in groups:
        m = tiles[g][0].max(axis=0, keepdims=True)
        for t in tiles[g][1:]:
            m = jnp.maximum(m, t.max(axis=0, keepdims=True))
        parts = [_split_bf16(jnp.exp(t - m)) for t in tiles[g]]
        accs.append(_dot_tn(vc_ref[0, 0:nrow, lanes(g)], jnp.concatenate([hi for hi, _ in parts], axis=0)))
        rows, carry = [], None
        for hi, lo in parts:
            piece = _dot(mct_ref[...], hi) + _dot(mct_ref[...], lo)
            body = piece[0:SEL_PER_CHUNK]
            if carry is not None:
                body = jnp.concatenate([body[0:8] + carry, body[8:]], axis=0)
            rows.append(body)
            carry = piece[SEL_PER_CHUNK:IMP_ROWS]
        imps.append(rows[0] if nchunk == 1 else jnp.concatenate(rows, axis=0))

    any_key = tq >= CMP_BLOCK - 1
    j = lax.broadcasted_iota(i32, (npre, Q_BLOCK), 0)
    jf = j.astype(f32)
    t1 = tq[:, 0:Q_BLOCK]
    jt = t1 >> 6
    valid = j * SEL_BLOCK <= t1
    vs = []
    for g in groups:
        inv = jnp.where(any_key, 1.0 / jnp.maximum(accs[g][HEAD_DIM:HEAD_DIM + 1], 1e-30), 0.0)
        o_ref[0, g, 0] = accs[g][0:HEAD_DIM] * inv
        imp4 = imps[g] * inv
        imp = imp4[:, 0:Q_BLOCK]
        for r in range(1, HEADS_PER_GROUP):
            imp = imp + imp4[:, r * Q_BLOCK:(r + 1) * Q_BLOCK]
        v = jnp.where(valid, imp, NEG)
        vs.append(jnp.where(j == 0, PICKED, jnp.where(j == jt, PICKED, jnp.where(j == jt - 1, PICKED, v))))
    vs = _pick_top(vs, jf, SEL_TOPK - 3)

    rr = lax.broadcasted_iota(i32, (nsel, nsel), 0).astype(f32)
    ones8 = jnp.ones((8, Q_BLOCK), bf16)
    jrow = jnp.broadcast_to(lax.broadcasted_iota(i32, (1, nsel), 1).astype(f32), (8, nsel)).astype(bf16)
    for g in groups:
        picked = vs[g] == PICKED
        madd = jnp.where(valid, jnp.where(picked, 0.0, NEG), NEG)
        sel = jnp.where(valid, jnp.where(picked, 1.0, 0.0), 0.0).astype(bf16)
        if npre < nsel:
            madd = jnp.concatenate([madd, jnp.full((nsel - npre, Q_BLOCK), NEG, f32)], axis=0)
            sel = jnp.concatenate([sel, jnp.zeros((nsel - npre, Q_BLOCK), bf16)], axis=0)
        madd_ref[0, g, 0] = madd
        flag = jnp.where(_dot_nt(ones8, sel)[0:1] > 0.0, 1.0, 0.0)
        flag8 = jnp.broadcast_to(flag, (8, nsel)).astype(bf16)
        prefix = _dot(flag8, tri_ref[...])[0:1]
        place = jnp.where(prefix == rr, flag, 0.0).astype(bf16)
        lst_ref[0, g, 0] = _dot_nt(jrow, place).astype(i32)
        cnt_ref[0, g, 0] = _dot(flag8, jnp.ones((nsel, LANES), bf16)).astype(i32)


def _cmp_body(qt_ref, kc_ref, vc_ref, mct_ref, tri_ref, o_ref, madd_ref, lst_ref, cnt_ref, *, nsel, nvar):
    c = pl.program_id(1)
    nch = (c * (Q_BLOCK // CMP_STRIDE) + (Q_BLOCK - CMP_BLOCK) // CMP_STRIDE) // CMP_CHUNK + 1
    for n in range(1, nvar + 1):
        pl.when(nch == n)(functools.partial(
            _cmp_variant, n, c, qt_ref, kc_ref, vc_ref, mct_ref, tri_ref, o_ref, madd_ref, lst_ref, cnt_ref,
            nsel=nsel))


def _cmp_attention(qt, kc, vc, mct, tri, *, batch):
    g = N_GROUPS
    nqb = qt.shape[0] // batch
    tc = kc.shape[1]
    nsel = tc * CMP_STRIDE // SEL_BLOCK
    width = HEADS_PER_GROUP * Q_BLOCK
    per_q = lambda r_, c_: pl.BlockSpec((1, g, 1, r_, c_), lambda bi, ci: (bi, 0, ci, 0, 0))
    shape = lambda r_, c_, dt: jax.ShapeDtypeStruct((batch, g, nqb, r_, c_), dt)
    return pl.pallas_call(
        functools.partial(_cmp_body, nsel=nsel, nvar=tc // CMP_CHUNK),
        grid=(batch, nqb),
        in_specs=[pl.BlockSpec((1, g, LANES, width), lambda bi, ci: (bi * nqb + ci, 0, 0, 0)),
                  pl.BlockSpec((1, tc, 2 * LANES), lambda bi, ci: (bi, 0, 0)),
                  pl.BlockSpec((1, tc, 2 * LANES), lambda bi, ci: (bi, 0, 0)),
                  pl.BlockSpec(mct.shape, lambda bi, ci: (0, 0)),
                  pl.BlockSpec(tri.shape, lambda bi, ci: (0, 0))],
        out_specs=[per_q(HEAD_DIM, width), per_q(nsel, Q_BLOCK), per_q(8, nsel), per_q(8, LANES)],
        out_shape=[shape(HEAD_DIM, width, f32), shape(nsel, Q_BLOCK, f32),
                   shape(8, nsel, i32), shape(8, LANES, i32)],
        compiler_params=pltpu.CompilerParams(dimension_semantics=("arbitrary",) * 2,
                                             vmem_limit_bytes=VMEM_LIMIT),
        name="cmp_attention",
    )(qt, kc, vc, mct, tri)


def _normalize(o_aug):
    return o_aug[0:HEAD_DIM] * (1.0 / jnp.maximum(o_aug[HEAD_DIM:HEAD_DIM + 1], 1e-30))


def _slc_win_body(lst_ref, cnt_ref, qt_ref, ks_ref, vs_ref, kw_ref, vw_ref,
                  madd_ref, ocmp_ref, gate_ref, wtab_ref, out_ref, s_scr, *, nsel):
    width = HEADS_PER_GROUP * Q_BLOCK
    nwin = (WINDOW + Q_BLOCK) // LANES
    wq = WINDOW // Q_BLOCK
    grp_rows = GROUP_BLOCKS * SEL_BLOCK

    class Block:
        def __init__(self, qb):
            self.qb = qb
            self.c = pl.program_id(2) * QB_PER_STEP + qb
            self.q0 = pl.multiple_of(self.c * Q_BLOCK, Q_BLOCK)
            self.qta = qt_ref[qb, 0]
            self.n_off = cnt_ref[qb, 0, 0] - 2

        def block_ids(self, first, nblk):
            ids = []
            for u in range(nblk):
                i = first + u
                j = lst_ref[self.qb, 0, jnp.minimum(i, nsel - 1)]
                ids.append((i < self.n_off, j, pl.multiple_of(j * SEL_BLOCK, SEL_BLOCK)))
            return ids

        def scores(self, ids):
            kcat = jnp.concatenate([ks_ref[0, pl.ds(r0, SEL_BLOCK), :] for _, _, r0 in ids], axis=0)
            sg = _dot(kcat, self.qta)
            tiles = []
            for u, (live, j, _) in enumerate(ids):
                mrow = jnp.where(live, madd_ref[0, 0, self.qb, pl.ds(j, 1), :], NEG)
                mrow = jnp.concatenate([mrow] * HEADS_PER_GROUP, axis=1)
                tiles.append(sg[u * SEL_BLOCK:(u + 1) * SEL_BLOCK] + mrow)
            return tiles

        def values(self, ids):
            return jnp.concatenate([vs_ref[0, pl.ds(r0, SEL_BLOCK), :] for _, _, r0 in ids], axis=0)

    def front(blk):
        c, q0, qta = blk.c, blk.q0, blk.qta
        ws = pl.multiple_of(jnp.maximum(c - wq, 0) * Q_BLOCK, Q_BLOCK)
        s = _dot(kw_ref[0, pl.ds(ws, WINDOW + Q_BLOCK), :], qta)
        chunks = []
        for k in range(nwin):
            steady = 1 if k == 0 else (2 if k == nwin - 1 else 0)
            tab = jnp.where(c >= wq, steady, jnp.where(k < c, 0, jnp.where(k == c, 2, 3)))
            chunks.append(s[k * LANES:(k + 1) * LANES] + wtab_ref[tab])
        head = blk.block_ids(0, HEAD_BLOCKS)
        tiles = [_dot(ks_ref[0, pl.ds(q0, Q_BLOCK), :], qta) + wtab_ref[2]] + blk.scores(head)
        return ws, chunks, head, tiles

    def softmax_pv(tiles, v):
        m = tiles[0].max(axis=0, keepdims=True)
        for su in tiles[1:]:
            m = jnp.maximum(m, su.max(axis=0, keepdims=True))
        p = jnp.concatenate([jnp.exp(su - m).astype(bf16) for su in tiles], axis=0)
        return m, _dot_tn(v, p)

    def middle(blk, ws, chunks, head, tiles):
        _, o_win = softmax_pv(chunks, vw_ref[0, pl.ds(ws, WINDOW + Q_BLOCK), :])
        m_run, o_run = softmax_pv(
            tiles, jnp.concatenate([vs_ref[0, pl.ds(blk.q0, Q_BLOCK), :], blk.values(head)], axis=0))
        return _normalize(o_win), m_run, o_run

    def rest(blk, m_run, o_run):
        def segment(si, carry):
            m_run, o_run = carry
            base = HEAD_BLOCKS + si * SUP_BLOCKS
            ngrp = (jnp.minimum(SUP_BLOCKS, blk.n_off - base) + GROUP_BLOCKS - 1) // GROUP_BLOCKS
            rows = lambda gi: pl.ds(pl.multiple_of(gi * grp_rows, grp_rows), grp_rows)

            def score(gi, mx):
                sg = jnp.concatenate(blk.scores(blk.block_ids(base + gi * GROUP_BLOCKS, GROUP_BLOCKS)), axis=0)
                s_scr[rows(gi), :] = sg
                return jnp.maximum(mx, sg.max(axis=0, keepdims=True))

            m_new = lax.fori_loop(0, ngrp, score, m_run)

            def weigh(gi, acc):
                pg = jnp.exp(s_scr[rows(gi), :] - m_new).astype(bf16)
                return acc + _dot_tn(blk.values(blk.block_ids(base + gi * GROUP_BLOCKS, GROUP_BLOCKS)), pg)

            o_seg = lax.fori_loop(0, ngrp, weigh, jnp.zeros((LANES, width), f32))
            return m_new, jnp.exp(m_run - m_new) * o_run + o_seg

        nseg = (jnp.maximum(blk.n_off - HEAD_BLOCKS, 0) + SUP_BLOCKS - 1) // SUP_BLOCKS
        return lax.fori_loop(0, nseg, segment, (m_run, o_run))[1]

    blocks = [Block(qb) for qb in range(QB_PER_STEP)]
    fronts = [front(blk) for blk in blocks]
    fronts = [middle(blk, *f) for blk, f in zip(blocks, fronts)]
    tails = [rest(blk, m_run, o_run) for blk, (_, m_run, o_run) in zip(blocks, fronts)]
    for blk, (o_win, _, _), o_run in zip(blocks, fronts, tails):
        qb = blk.qb
        gate = gate_ref[0, 0, qb]
        mix = gate[0:1] * ocmp_ref[0, 0, qb] + gate[1:2] * _normalize(o_run) + gate[2:3] * o_win
        rows = jnp.concatenate([mix[:, r * Q_BLOCK:(r + 1) * Q_BLOCK] for r in range(HEADS_PER_GROUP)],
                               axis=0)
        out_ref[0, qb * Q_BLOCK:(qb + 1) * Q_BLOCK, :] = rows.T


def _slc_win_attention(lst, cnt, qt, ks, vs, kw, vw, madd, ocmp, gate_t, wtab):
    b, t, _ = ks.shape
    g = N_GROUPS
    nqb, nsel = t // Q_BLOCK, t // SEL_BLOCK
    width = HEADS_PER_GROUP * Q_BLOCK
    nstep = nqb // QB_PER_STEP
    flat = lambda bi, gi, ci: (bi * g + gi) * nstep + ci
    smem = lambda w_: pl.BlockSpec((QB_PER_STEP, 1, w_), lambda bi, gi, ci: (flat(bi, gi, ci), 0, 0),
                                   memory_space=pltpu.SMEM)
    per_q = lambda r_, c_: pl.BlockSpec((1, 1, QB_PER_STEP, r_, c_), lambda bi, gi, ci: (bi, gi, ci, 0, 0))
    keys = pl.BlockSpec((1, t, LANES), lambda bi, gi, ci: (bi, 0, gi))
    return pl.pallas_call(
        functools.partial(_slc_win_body, nsel=nsel),
        grid=(b, g, nstep),
        in_specs=[smem(nsel), smem(LANES),
                  pl.BlockSpec((QB_PER_STEP, 1, LANES, width), lambda bi, gi, ci: (bi * nstep + ci, gi, 0, 0)),
                  keys, keys, keys, keys,
                  per_q(nsel, Q_BLOCK), per_q(HEAD_DIM, width), per_q(8, width),
                  pl.BlockSpec(wtab.shape, lambda bi, gi, ci: (0, 0, 0))],
        out_specs=pl.BlockSpec((1, QB_PER_STEP * Q_BLOCK, 2 * LANES), lambda bi, gi, ci: (bi, ci, gi)),
        out_shape=jax.ShapeDtypeStruct((b, t, ATTN_WIDTH), f32),
        scratch_shapes=[pltpu.VMEM((SUP_BLOCKS * SEL_BLOCK, width), f32)],
        compiler_params=pltpu.CompilerParams(dimension_semantics=("arbitrary",) * 3,
                                             vmem_limit_bytes=VMEM_LIMIT),
        name="slc_win_attention",
    )(lst, cnt, qt, ks, vs, kw, vw, madd, ocmp, gate_t, wtab)


def _rms(v, g):
    return (v * lax.rsqrt(jnp.mean(v * v, axis=-1, keepdims=True) + EPS)) * g


def _outproj_body(x_ref, attn_ref, u_ref, uprev_ref, bg_ref, cw_ref, go_ref, wo_ref, out_ref, *, tm, seq):
    first = (pl.program_id(0) * tm) % seq == 0
    u = u_ref[...]
    prev = jnp.where(first, 0.0, uprev_ref[0])
    ext = jnp.concatenate([prev, u], axis=0)
    cw = cw_ref[...]
    conv = cw[0:1] * ext[6:6 + tm] + cw[1:2] * ext[7:7 + tm] + cw[2:3] * u
    conv = bg_ref[...] * conv
    go = go_ref[...]
    mixed = jnp.concatenate([_rms(attn_ref[...], go[:, :ATTN_WIDTH]), _rms(conv, go[:, ATTN_WIDTH:])],
                            axis=1).astype(bf16)
    out_ref[...] = x_ref[...] + _dot(mixed, wo_ref[...])


def _outproj(x2, attn2, u, bgate, cw, go, wo, *, seq, tm=512):
    n = x2.shape[0]
    u8 = u.reshape(n // 8, 8, CONV_WIDTH)
    row = lambda w_: pl.BlockSpec((tm, w_), lambda i: (i, 0))
    full = lambda a: pl.BlockSpec(a.shape, lambda i: (0,) * a.ndim)
    return pl.pallas_call(
        functools.partial(_outproj_body, tm=tm, seq=seq),
        grid=(n // tm,),
        in_specs=[row(D_MODEL), row(ATTN_WIDTH), row(CONV_WIDTH),
                  pl.BlockSpec((1, 8, CONV_WIDTH), lambda i: (jnp.maximum(i * (tm // 8) - 1, 0), 0, 0)),
                  row(CONV_WIDTH), full(cw), full(go), full(wo)],
        out_specs=row(D_MODEL),
        out_shape=jax.ShapeDtypeStruct((n, D_MODEL), f32),
        compiler_params=pltpu.CompilerParams(dimension_semantics=("arbitrary",),
                                             vmem_limit_bytes=VMEM_LIMIT),
        name="outproj",
    )(x2, attn2, u, u8, bgate, cw, go, wo)


def _ffn_body(x_ref, g_ref, wu_ref, wd_ref, out_ref, *, chunk):
    x = x_ref[...]
    h = _rms(x, g_ref[...]).astype(bf16)
    acc = x
    for c in range(D_FF // chunk):
        a = jnp.maximum(_dot(h, wu_ref[:, c * chunk:(c + 1) * chunk]), 0.0)
        acc = acc + _dot((a * a).astype(bf16), wd_ref[c * chunk:(c + 1) * chunk, :])
    out_ref[...] = acc


def _ffn(x2, g, wu, wd, *, tm=512, chunk=1024):
    n = x2.shape[0]
    row = pl.BlockSpec((tm, D_MODEL), lambda i: (i, 0))
    full = lambda a: pl.BlockSpec(a.shape, lambda i: (0,) * a.ndim)
    return pl.pallas_call(
        functools.partial(_ffn_body, chunk=chunk),
        grid=(n // tm,),
        in_specs=[row, full(g), full(wu), full(wd)],
        out_specs=row,
        out_shape=jax.ShapeDtypeStruct((n, D_MODEL), f32),
        compiler_params=pltpu.CompilerParams(dimension_semantics=("arbitrary",),
                                             vmem_limit_bytes=VMEM_LIMIT),
        name="ffn",
    )(x2, g, wu, wd)


def _interleave_zero(w):
    z = jnp.zeros(w.shape[:-1] + (HEAD_DIM,), w.dtype)
    return jnp.concatenate([w[..., :HEAD_DIM], z, w[..., HEAD_DIM:], z], axis=-1)


def _prep_params(g_mix_norm, w_in, g_q, g_k, pe_cmp, w_cmp1, b_cmp1, w_cmp2, b_cmp2,
                 conv_w, g_out, w_o, g_ffn_norm, w_up, w_down):
    depth = w_in.shape[0]
    o = np.cumsum([0, ATTN_WIDTH] + [LANES] * 6 + [N_BRANCH * N_HEADS] + [CONV_WIDTH] * 3)
    part = lambda i: w_in[..., int(o[i]):int(o[i + 1])]
    q, kc, vc, ks, vs, kw, vw, gl, hc, cg, bg = [part(i) for i in range(11)]
    gl = jnp.pad(gl, ((0, 0), (0, 0), (0, LANES - gl.shape[-1])))
    w = jnp.concatenate([q, kc, vc, _interleave_zero(ks), _interleave_zero(kw), _interleave_zero(vs),
                         _interleave_zero(vw), hc, cg, bg, gl],
                        axis=-1).astype(bf16)
    tile2 = lambda gk: _interleave_zero(jnp.concatenate([gk, gk], axis=-1))[:, None, :]
    w1 = w_cmp1.astype(bf16).reshape(depth, 2, 2, CMP_STRIDE, HEAD_DIM, CMP_HIDDEN)
    z1 = jnp.zeros_like(w1)
    w1g = jnp.stack([jnp.concatenate([w1, z1], axis=-1), jnp.concatenate([z1, w1], axis=-1)], axis=4)
    w1g = w1g.reshape(depth, 2, 2, CMP_STRIDE * LANES, N_GROUPS * CMP_HIDDEN)
    pe = pe_cmp.reshape(depth, 2, 2, CMP_STRIDE, 1, HEAD_DIM)
    pe = jnp.broadcast_to(pe, (depth, 2, 2, CMP_STRIDE, N_GROUPS, HEAD_DIM)).reshape(depth, 2, 2, 1, -1)
    pe = jnp.pad(pe, ((0, 0), (0, 0), (0, 0), (0, 7), (0, 0))).astype(bf16)
    b1 = jnp.concatenate([b_cmp1, b_cmp1], axis=-1)[:, :, None, :]
    z2 = jnp.zeros_like(w_cmp2)
    w2 = jnp.concatenate([jnp.concatenate([w_cmp2, z2], axis=-1), jnp.concatenate([z2, w_cmp2], axis=-1)],
                         axis=2)
    w2 = _interleave_zero(w2).astype(bf16)
    b2 = _interleave_zero(jnp.concatenate([b_cmp2, b_cmp2], axis=-1))[:, :, None, :]
    return dict(
        gm=g_mix_norm[:, None, :], w=w,
        gq=jnp.tile(g_q, (1, N_HEADS))[:, None, :],
        gks=tile2(g_k[:, 1]), gkw=tile2(g_k[:, 2]), gkc=tile2(g_k[:, 0]),
        w1=w1g, pe=pe, b1=b1,
        w2k=w2[:, 0], w2v=w2[:, 1], b2k=b2[:, 0], b2v=b2[:, 1],
        cw=jnp.pad(conv_w, ((0, 0), (0, 8 - conv_w.shape[1]), (0, 0))),
        go=g_out[:, None, :], wo=w_o.astype(bf16),
        gf=g_ffn_norm[:, None, :], wu=w_up.astype(bf16), wd=w_down.astype(bf16),
    )


def _constants(nsel):
    lane = np.arange(LANES)
    e = (lane[:, None] // HEAD_DIM == lane[None, :] // HEAD_DIM).astype(np.float32)
    tail = np.zeros((N_GROUPS, HEAD_DIM, HEADS_PER_GROUP * Q_BLOCK), np.float32)
    for gi in range(N_GROUPS):
        for r in range(HEADS_PER_GROUP):
            slope = 2.0 ** -(gi * HEADS_PER_GROUP + r + 1)
            tail[gi, 0, r * Q_BLOCK:(r + 1) * Q_BLOCK] = slope * LANES
            tail[gi, 1, r * Q_BLOCK:(r + 1) * Q_BLOCK] = slope
    mct = np.zeros((IMP_ROWS, CMP_CHUNK), np.float32)
    for i in range(CMP_CHUNK):
        lo, hi = i * CMP_STRIDE, i * CMP_STRIDE + CMP_BLOCK
        for jj in range(SEL_PER_CHUNK + 1):
            ov = min(hi, (jj + 1) * SEL_BLOCK) - max(lo, jj * SEL_BLOCK)
            if ov > 0:
                mct[jj, i] = ov / CMP_BLOCK
    kk = np.arange(LANES)[:, None]
    ql = np.tile(np.arange(Q_BLOCK), HEADS_PER_GROUP)[None, :]
    wtab = np.zeros((4, LANES, HEADS_PER_GROUP * Q_BLOCK), np.float32)
    wtab[1] = np.where(kk > ql, 0.0, NEG)
    wtab[2] = np.where(kk <= ql, 0.0, NEG)
    wtab[3] = NEG
    tri = np.arange(nsel)[:, None] < np.arange(nsel)[None, :]
    return (jnp.asarray(e, bf16), jnp.asarray(tail, bf16), jnp.asarray(mct, bf16), jnp.asarray(wtab),
            jnp.asarray(tri, bf16))


def _layer(x2, p, consts, *, batch, seq):
    e, tail, mct, wtab, tri = consts
    g = N_GROUPS
    nqb, nsel, tc = seq // Q_BLOCK, seq // SEL_BLOCK, seq // CMP_STRIDE
    qt, kc, vc, ks, kw, vs, vw, u, bgate, gates = _inproj(
        x2, p["gm"], p["w"], p["gq"], p["gks"], p["gkw"], e, tail, seq=seq)
    kcmp, vcmp = _compress(kc.reshape(batch, tc, CMP_STRIDE * LANES), vc.reshape(batch, tc, CMP_STRIDE * LANES),
                           p["w1"], p["pe"], p["b1"], p["w2k"], p["w2v"], p["b2k"], p["b2v"], p["gkc"], e)
    gate_t = gates[:, :N_HEADS * N_BRANCH].reshape(batch, nqb, Q_BLOCK, g, HEADS_PER_GROUP, N_BRANCH)
    gate_t = gate_t.transpose(0, 3, 1, 5, 4, 2).reshape(batch, g, nqb, N_BRANCH, HEADS_PER_GROUP * Q_BLOCK)
    gate_t = jnp.pad(gate_t, ((0, 0), (0, 0), (0, 0), (0, 8 - N_BRANCH), (0, 0)))
    ocmp, madd, lst, cnt = _cmp_attention(qt, kcmp, vcmp, mct, tri, batch=batch)
    rows3 = lambda a: a.reshape(batch, seq, 2 * LANES)
    smem = lambda a: a[:, :, :, 0, :].reshape(batch * g * nqb, 1, a.shape[-1])
    attn = _slc_win_attention(smem(lst), smem(cnt), qt, rows3(ks), rows3(vs),
                              rows3(kw), rows3(vw), madd, ocmp, gate_t, wtab)
    x2 = _outproj(x2, attn.reshape(batch * seq, ATTN_WIDTH), u, bgate, p["cw"], p["go"], p["wo"], seq=seq)
    return _ffn(x2, p["gf"], p["wu"], p["wd"])


def kernel(x, g_mix_norm, w_in, g_q, g_k, pe_cmp, w_cmp1, b_cmp1, w_cmp2, b_cmp2, conv_w, g_out, w_o,
           g_ffn_norm, w_up, w_down):
    batch, seq, d = x.shape
    assert d == D_MODEL and seq % (CMP_CHUNK * CMP_STRIDE) == 0 and seq >= WINDOW + Q_BLOCK
    params = _prep_params(g_mix_norm, w_in, g_q, g_k, pe_cmp, w_cmp1, b_cmp1, w_cmp2, b_cmp2,
                          conv_w, g_out, w_o, g_ffn_norm, w_up, w_down)
    consts = _constants(seq // SEL_BLOCK)

    def step(x2, p):
        return _layer(x2, p, consts, batch=batch, seq=seq), None

    x2, _ = lax.scan(step, x.reshape(batch * seq, d), params)
    return x2.reshape(batch, seq, d)
```

```python
import functools

import numpy as np
import jax
import jax.numpy as jnp
from jax import lax
from jax.experimental import pallas as pl
from jax.experimental.pallas import tpu as pltpu

f32 = jnp.float32
bf16 = jnp.bfloat16
i32 = jnp.int32

D_MODEL = 1024
HEAD_DIM = 64
N_HEADS = 8
N_GROUPS = 2
HEADS_PER_GROUP = 4
ATTN_WIDTH = 512
CONV_WIDTH = 512
N_BRANCH = 3
CMP_BLOCK = 32
CMP_STRIDE = 16
CMP_HIDDEN = 256
SEL_BLOCK = 64
SEL_TOPK = 16
WINDOW = 512
Q_BLOCK = 128
D_FF = 4096
EPS = 1e-6
NEG = -1e30
LOG2E = 1.4426950408889634
SLOPE_TERMS = 3
PICKED = -3e38
LANES = 128
CMP_CHUNK = 256
SEL_PER_CHUNK = CMP_CHUNK * CMP_STRIDE // SEL_BLOCK
IMP_ROWS = SEL_PER_CHUNK + 8
VMEM_LIMIT = 56 * 1024 * 1024

C_Q, C_KC, C_VC, C_KS, C_KW, C_VS, C_VW, C_HC, C_CG, C_BG, C_GL, C_END = (
    0, 512, 640, 768, 1024, 1280, 1536, 1792, 2304, 2816, 3328, 3456)
SUP_BLOCKS = 32
HEAD_BLOCKS = 16
GROUP_BLOCKS = 4
QB_PER_STEP = 2


def _dot(a, b):
    return jnp.dot(a, b, preferred_element_type=f32)


def _dot_nt(a, b):
    return lax.dot_general(a, b, (((1,), (1,)), ((), ())), preferred_element_type=f32)


def _dot_tn(a, b):
    return lax.dot_general(a, b, (((0,), (0,)), ((), ())), preferred_element_type=f32)


def _head_norm(z, e, g):
    sq = (z * z).astype(bf16)
    outs = []
    for c in range(z.shape[1] // LANES):
        sl = slice(c * LANES, (c + 1) * LANES)
        outs.append(z[:, sl] * lax.rsqrt(_dot(sq[:, sl], e) * (1.0 / HEAD_DIM) + EPS))
    y = outs[0] if len(outs) == 1 else jnp.concatenate(outs, axis=1)
    return y * g


def _pos_cols(pos, width):
    rows = pos.shape[0]
    lane = lax.broadcasted_iota(i32, (rows, LANES), 1) - HEAD_DIM
    tile = jnp.where(lane < 0, 0.0, jnp.where(lane < SLOPE_TERMS, (pos >> 7).astype(f32),
                                              jnp.where(lane < 2 * SLOPE_TERMS, (pos & 127).astype(f32), 0.0)))
    return tile if width == LANES else jnp.concatenate([tile] * (width // LANES), axis=1)


def _ones_col(rows, width):
    lane = lax.broadcasted_iota(i32, (rows, LANES), 1)
    tile = jnp.where(lane == HEAD_DIM, 1.0, 0.0)
    return tile if width == LANES else jnp.concatenate([tile] * (width // LANES), axis=1)


def _inproj_body(x_ref, gm_ref, w_ref, gq_ref, gks_ref, gkw_ref, e_ref, tail_ref,
                 qt_ref, kc_ref, vc_ref, ks_ref, kw_ref, vs_ref, vw_ref, u_ref, bg_ref, gate_ref,
                 *, tm, seq):
    x = x_ref[...]
    ms = jnp.mean(x * x, axis=-1, keepdims=True)
    h = ((x * lax.rsqrt(ms + EPS)) * gm_ref[...]).astype(bf16)
    e = e_ref[...]

    def proj(c0, c1):
        return _dot(h, w_ref[:, c0:c1])

    qn = _head_norm(proj(C_Q, C_KC), e, gq_ref[...]) * (HEAD_DIM ** -0.5 * LOG2E)
    gw = HEADS_PER_GROUP * HEAD_DIM
    for cb in range(tm // Q_BLOCK):
        for g in range(N_GROUPS):
            blk = qn[cb * Q_BLOCK:(cb + 1) * Q_BLOCK, g * gw:(g + 1) * gw].T
            top = jnp.concatenate([blk[r * HEAD_DIM:(r + 1) * HEAD_DIM] for r in range(HEADS_PER_GROUP)],
                                  axis=1)
            qt_ref[cb, g] = jnp.concatenate([top.astype(bf16), tail_ref[g]], axis=0)
    kc_ref[...] = proj(C_KC, C_VC).astype(bf16)
    vc_ref[...] = proj(C_VC, C_KS).astype(bf16)
    t0 = (pl.program_id(0) * tm) % seq
    pos = lax.broadcasted_iota(i32, (tm, LANES), 0) + t0
    pc = _pos_cols(pos, 2 * LANES)
    ks_ref[...] = (_head_norm(proj(C_KS, C_KW), e, gks_ref[...]) + pc).astype(bf16)
    kw_ref[...] = (_head_norm(proj(C_KW, C_VS), e, gkw_ref[...]) + pc).astype(bf16)
    ones = _ones_col(tm, 2 * LANES)
    vs_ref[...] = (proj(C_VS, C_VW) + ones).astype(bf16)
    vw_ref[...] = (proj(C_VW, C_HC) + ones).astype(bf16)
    u_ref[...] = proj(C_HC, C_CG) * proj(C_CG, C_BG)
    bg_ref[...] = proj(C_BG, C_GL)
    gate_ref[...] = 1.0 / (1.0 + jnp.exp(-proj(C_GL, C_END)))


def _inproj(x2, gm, w, gq, gks, gkw, e, tail, *, seq, tm=512):
    n = x2.shape[0]
    width = HEADS_PER_GROUP * Q_BLOCK
    row = lambda w_: pl.BlockSpec((tm, w_), lambda i: (i, 0))
    full = lambda a: pl.BlockSpec(a.shape, lambda i: (0,) * a.ndim)
    outs = [(LANES, bf16), (LANES, bf16), (2 * LANES, bf16), (2 * LANES, bf16),
            (2 * LANES, bf16), (2 * LANES, bf16), (CONV_WIDTH, f32), (CONV_WIDTH, f32), (LANES, f32)]
    qt_spec = pl.BlockSpec((tm // Q_BLOCK, N_GROUPS, LANES, width), lambda i: (i, 0, 0, 0))
    qt_shape = jax.ShapeDtypeStruct((n // Q_BLOCK, N_GROUPS, LANES, width), bf16)
    return pl.pallas_call(
        functools.partial(_inproj_body, tm=tm, seq=seq),
        grid=(n // tm,),
        in_specs=[row(D_MODEL), full(gm), full(w), full(gq), full(gks), full(gkw), full(e), full(tail)],
        out_specs=[qt_spec] + [row(w_) for w_, _ in outs],
        out_shape=[qt_shape] + [jax.ShapeDtypeStruct((n, w_), dt) for w_, dt in outs],
        compiler_params=pltpu.CompilerParams(dimension_semantics=("arbitrary",),
                                             vmem_limit_bytes=VMEM_LIMIT),
        name="inproj",
    )(x2, gm, w, gq, gks, gkw, e, tail)


def _gelu_tanh(x):
    return 0.5 * x * (1.0 + jnp.tanh(0.7978845608028654 * (x + 0.044715 * (x * x * x))))


def _compress_body(zk_ref, zv_ref, w1_ref, pe_ref, b1_ref, w2k_ref, w2v_ref, b2k_ref, b2v_ref,
                   gk_ref, e_ref, kc_ref, vc_ref, *, tc):
    last = lax.broadcasted_iota(i32, (tc, 1), 0) == tc - 1

    def hidden(z_ref, kind):
        z = z_ref[0]
        a = _dot(z, w1_ref[kind, 0])
        b = _dot(z, w1_ref[kind, 1])
        b = jnp.concatenate([b[1:], jnp.zeros((1, b.shape[1]), f32)], axis=0)
        bias = (_dot(pe_ref[kind, 0], w1_ref[kind, 0]) + _dot(pe_ref[kind, 1], w1_ref[kind, 1]))[0:1]
        return _gelu_tanh(a + b + bias + b1_ref[kind]).astype(bf16)

    k = _dot(hidden(zk_ref, 0), w2k_ref[...]) + b2k_ref[...]
    k = _head_norm(k, e_ref[...], gk_ref[...])
    pos = lax.broadcasted_iota(i32, (tc, LANES), 0) * CMP_STRIDE + (CMP_BLOCK - 1)
    k = k + _pos_cols(pos, 2 * LANES)
    kc_ref[0] = jnp.where(last, 0.0, k).astype(bf16)
    v = _dot(hidden(zv_ref, 1), w2v_ref[...]) + b2v_ref[...]
    vc_ref[0] = jnp.where(last, 0.0, v + _ones_col(tc, 2 * LANES)).astype(bf16)


def _compress(zk, zv, w1, pe, b1, w2k, w2v, b2k, b2v, gk, e):
    b, tc, _ = zk.shape
    blk = lambda a: pl.BlockSpec((1,) + a.shape[1:], lambda i: (i,) + (0,) * (a.ndim - 1))
    full = lambda a: pl.BlockSpec(a.shape, lambda i: (0,) * a.ndim)
    return pl.pallas_call(
        functools.partial(_compress_body, tc=tc),
        grid=(b,),
        in_specs=[blk(zk), blk(zv)] + [full(a) for a in (w1, pe, b1, w2k, w2v, b2k, b2v, gk, e)],
        out_specs=[pl.BlockSpec((1, tc, 2 * LANES), lambda i: (i, 0, 0))] * 2,
        out_shape=[jax.ShapeDtypeStruct((b, tc, 2 * LANES), bf16)] * 2,
        compiler_params=pltpu.CompilerParams(dimension_semantics=("arbitrary",),
                                             vmem_limit_bytes=VMEM_LIMIT),
        name="compress",
    )(zk, zv, w1, pe, b1, w2k, w2v, b2k, b2v, gk, e)


def _query_pos(c):
    lane = lax.broadcasted_iota(i32, (1, HEADS_PER_GROUP * Q_BLOCK), 1)
    return c * Q_BLOCK + (lane & (Q_BLOCK - 1))


def _pick_top(vs, jf, n_pick):
    vs = list(vs)
    for _ in range(n_pick):
        for g, v in enumerate(vs):
            mx = jnp.max(v, axis=0, keepdims=True)
            idx = jnp.min(jnp.where(v == mx, jf, float(jf.shape[0])), axis=0, keepdims=True)
            vs[g] = jnp.where(jf == idx, PICKED, v)
    return tuple(vs)


def _cmp_variant(nchunk, c, qt_ref, kc_ref, vc_ref, mct_ref, tri_ref, o_ref, madd_ref, lst_ref, cnt_ref,
                 *, nsel):
    groups = range(N_GROUPS)
    nrow = nchunk * CMP_CHUNK
    npre = nchunk * SEL_PER_CHUNK
    tq = _query_pos(c)
    sub = lax.broadcasted_iota(i32, (CMP_CHUNK, 1), 0)
    lanes = lambda g: slice(g * LANES, (g + 1) * LANES)
    chunk = lambda a, k: a[k * CMP_CHUNK:(k + 1) * CMP_CHUNK]

    tiles = []
    for g in groups:
        s = _dot(kc_ref[0, 0:nrow, lanes(g)], qt_ref[0, g])
        row = []
        for k in range(nchunk):
            t = chunk(s, k)
            if k >= nchunk - 2:
                seen = (k * CMP_CHUNK + sub) * CMP_STRIDE + (CMP_BLOCK - 1) <= tq
                t = jnp.where(seen, t, NEG)
            row.append(t)
        tiles.append(row)

    accs, imps = [], []
    for g in groups:
        m = tiles[g][0].max(axis=0, keepdims=True)
        for t in tiles[g][1:]:
            m = jnp.maximum(m, t.max(axis=0, keepdims=True))
        parts = [jnp.exp2(t - m).astype(bf16) for t in tiles[g]]
        accs.append(_dot_tn(vc_ref[0, 0:nrow, lanes(g)], jnp.concatenate(parts, axis=0)))
        rows, carry = [], None
        for part in parts:
            piece = _dot(mct_ref[...], part)
            body = piece[0:SEL_PER_CHUNK]
            if carry is not None:
                body = jnp.concatenate([body[0:8] + carry, body[8:]], axis=0)
            rows.append(body)
            carry = piece[SEL_PER_CHUNK:IMP_ROWS]
        imps.append(rows[0] if nchunk == 1 else jnp.concatenate(rows, axis=0))

    any_key = tq >= CMP_BLOCK - 1
    j = lax.broadcasted_iota(i32, (npre, Q_BLOCK), 0)
    jf = j.astype(f32)
    t1 = tq[:, 0:Q_BLOCK]
    jt = t1 >> 6
    valid = j * SEL_BLOCK <= t1
    vs = []
    for g in groups:
        inv = jnp.where(any_key, 1.0 / jnp.maximum(accs[g][HEAD_DIM:HEAD_DIM + 1], 1e-30), 0.0)
        o_ref[0, g, 0] = accs[g][0:HEAD_DIM] * inv
        imp4 = imps[g] * inv
        imp = imp4[:, 0:Q_BLOCK]
        for r in range(1, HEADS_PER_GROUP):
            imp = imp + imp4[:, r * Q_BLOCK:(r + 1) * Q_BLOCK]
        v = jnp.where(valid, imp, NEG)
        vs.append(jnp.where(j == 0, PICKED, jnp.where(j == jt, PICKED, jnp.where(j == jt - 1, PICKED, v))))
    vs = _pick_top(vs, jf, SEL_TOPK - 3)

    rr = lax.broadcasted_iota(i32, (nsel, nsel), 0).astype(f32)
    ones8 = jnp.ones((8, Q_BLOCK), bf16)
    jrow = jnp.broadcast_to(lax.broadcasted_iota(i32, (1, nsel), 1).astype(f32), (8, nsel)).astype(bf16)
    for g in groups:
        picked = vs[g] == PICKED
        madd = jnp.where(valid, jnp.where(picked, 0.0, NEG), NEG)
        sel = jnp.where(valid, jnp.where(picked, 1.0, 0.0), 0.0).astype(bf16)
        if npre < nsel:
            madd = jnp.concatenate([madd, jnp.full((nsel - npre, Q_BLOCK), NEG, f32)], axis=0)
            sel = jnp.concatenate([sel, jnp.zeros((nsel - npre, Q_BLOCK), bf16)], axis=0)
        madd_ref[0, g, 0] = madd
        flag = jnp.where(_dot_nt(ones8, sel)[0:1] > 0.0, 1.0, 0.0)
        flag8 = jnp.broadcast_to(flag, (8, nsel)).astype(bf16)
        prefix = _dot(flag8, tri_ref[...])[0:1]
        place = jnp.where(prefix == rr, flag, 0.0).astype(bf16)
        lst_ref[0, g, 0] = _dot_nt(jrow, place).astype(i32)
        cnt_ref[0, g, 0] = _dot(flag8, jnp.ones((nsel, LANES), bf16)).astype(i32)


def _cmp_body(qt_ref, kc_ref, vc_ref, mct_ref, tri_ref, o_ref, madd_ref, lst_ref, cnt_ref, *, nsel, nvar):
    c = pl.program_id(1)
    nch = (c * (Q_BLOCK // CMP_STRIDE) + (Q_BLOCK - CMP_BLOCK) // CMP_STRIDE) // CMP_CHUNK + 1
    for n in range(1, nvar + 1):
        pl.when(nch == n)(functools.partial(
            _cmp_variant, n, c, qt_ref, kc_ref, vc_ref, mct_ref, tri_ref, o_ref, madd_ref, lst_ref, cnt_ref,
            nsel=nsel))


def _cmp_attention(qt, kc, vc, mct, tri, *, batch):
    g = N_GROUPS
    nqb = qt.shape[0] // batch
    tc = kc.shape[1]
    nsel = tc * CMP_STRIDE // SEL_BLOCK
    width = HEADS_PER_GROUP * Q_BLOCK
    per_q = lambda r_, c_: pl.BlockSpec((1, g, 1, r_, c_), lambda bi, ci: (bi, 0, ci, 0, 0))
    shape = lambda r_, c_, dt: jax.ShapeDtypeStruct((batch, g, nqb, r_, c_), dt)
    return pl.pallas_call(
        functools.partial(_cmp_body, nsel=nsel, nvar=tc // CMP_CHUNK),
        grid=(batch, nqb),
        in_specs=[pl.BlockSpec((1, g, LANES, width), lambda bi, ci: (bi * nqb + ci, 0, 0, 0)),
                  pl.BlockSpec((1, tc, 2 * LANES), lambda bi, ci: (bi, 0, 0)),
                  pl.BlockSpec((1, tc, 2 * LANES), lambda bi, ci: (bi, 0, 0)),
                  pl.BlockSpec(mct.shape, lambda bi, ci: (0, 0)),
                  pl.BlockSpec(tri.shape, lambda bi, ci: (0, 0))],
        out_specs=[per_q(HEAD_DIM, width), per_q(nsel, Q_BLOCK), per_q(8, nsel), per_q(8, LANES)],
        out_shape=[shape(HEAD_DIM, width, f32), shape(nsel, Q_BLOCK, f32),
                   shape(8, nsel, i32), shape(8, LANES, i32)],
        compiler_params=pltpu.CompilerParams(dimension_semantics=("arbitrary",) * 2,
                                             vmem_limit_bytes=VMEM_LIMIT),
        name="cmp_attention",
    )(qt, kc, vc, mct, tri)


def _normalize(o_aug):
    return o_aug[0:HEAD_DIM] * (1.0 / jnp.maximum(o_aug[HEAD_DIM:HEAD_DIM + 1], 1e-30))


def _slc_win_body(lst_ref, cnt_ref, qt_ref, ks_ref, vs_ref, kw_ref, vw_ref,
                  madd_ref, ocmp_ref, gate_ref, wtab_ref, out_ref, s_scr, *, nsel):
    width = HEADS_PER_GROUP * Q_BLOCK
    nwin = (WINDOW + Q_BLOCK) // LANES
    wq = WINDOW // Q_BLOCK
    grp_rows = GROUP_BLOCKS * SEL_BLOCK

    class Block:
        def __init__(self, qb):
            self.qb = qb
            self.c = pl.program_id(2) * QB_PER_STEP + qb
            self.q0 = pl.multiple_of(self.c * Q_BLOCK, Q_BLOCK)
            self.qta = qt_ref[qb, 0]
            self.n_off = cnt_ref[qb, 0, 0] - 2

        def block_ids(self, first, nblk):
            ids = []
            for u in range(nblk):
                i = first + u
                j = lst_ref[self.qb, 0, jnp.minimum(i, nsel - 1)]
                ids.append((i < self.n_off, j, pl.multiple_of(j * SEL_BLOCK, SEL_BLOCK)))
            return ids

        def scores(self, ids):
            kcat = jnp.concatenate([ks_ref[0, pl.ds(r0, SEL_BLOCK), :] for _, _, r0 in ids], axis=0)
            sg = _dot(kcat, self.qta)
            tiles = []
            for u, (live, j, _) in enumerate(ids):
                mrow = jnp.where(live, madd_ref[0, 0, self.qb, pl.ds(j, 1), :], NEG)
                mrow = jnp.concatenate([mrow] * HEADS_PER_GROUP, axis=1)
                tiles.append(sg[u * SEL_BLOCK:(u + 1) * SEL_BLOCK] + mrow)
            return tiles

        def values(self, ids):
            return jnp.concatenate([vs_ref[0, pl.ds(r0, SEL_BLOCK), :] for _, _, r0 in ids], axis=0)

    def front(blk):
        c, q0, qta = blk.c, blk.q0, blk.qta
        ws = pl.multiple_of(jnp.maximum(c - wq, 0) * Q_BLOCK, Q_BLOCK)
        s = _dot(kw_ref[0, pl.ds(ws, WINDOW + Q_BLOCK), :], qta)
        chunks = []
        for k in range(nwin):
            steady = 1 if k == 0 else (2 if k == nwin - 1 else 0)
            tab = jnp.where(c >= wq, steady, jnp.where(k < c, 0, jnp.where(k == c, 2, 3)))
            chunks.append(s[k * LANES:(k + 1) * LANES] + wtab_ref[tab])
        head = blk.block_ids(0, HEAD_BLOCKS)
        tiles = [_dot(ks_ref[0, pl.ds(q0, Q_BLOCK), :], qta) + wtab_ref[2]] + blk.scores(head)
        return ws, chunks, head, tiles

    def softmax_pv(tiles, v):
        m = tiles[0].max(axis=0, keepdims=True)
        for su in tiles[1:]:
            m = jnp.maximum(m, su.max(axis=0, keepdims=True))
        p = jnp.concatenate([jnp.exp2(su - m).astype(bf16) for su in tiles], axis=0)
        return m, _dot_tn(v, p)

    def middle(blk, ws, chunks, head, tiles):
        _, o_win = softmax_pv(chunks, vw_ref[0, pl.ds(ws, WINDOW + Q_BLOCK), :])
        m_run, o_run = softmax_pv(
            tiles, jnp.concatenate([vs_ref[0, pl.ds(blk.q0, Q_BLOCK), :], blk.values(head)], axis=0))
        return _normalize(o_win), m_run, o_run

    def rest(blk, m_run, o_run):
        def segment(si, carry):
            m_run, o_run = carry
            base = HEAD_BLOCKS + si * SUP_BLOCKS
            ngrp = (jnp.minimum(SUP_BLOCKS, blk.n_off - base) + GROUP_BLOCKS - 1) // GROUP_BLOCKS
            rows = lambda gi: pl.ds(pl.multiple_of(gi * grp_rows, grp_rows), grp_rows)

            def score(gi, mx):
                sg = jnp.concatenate(blk.scores(blk.block_ids(base + gi * GROUP_BLOCKS, GROUP_BLOCKS)), axis=0)
                s_scr[rows(gi), :] = sg
                return jnp.maximum(mx, sg.max(axis=0, keepdims=True))

            m_new = lax.fori_loop(0, ngrp, score, m_run)

            def weigh(gi, acc):
                pg = jnp.exp2(s_scr[rows(gi), :] - m_new).astype(bf16)
                return acc + _dot_tn(blk.values(blk.block_ids(base + gi * GROUP_BLOCKS, GROUP_BLOCKS)), pg)

            o_seg = lax.fori_loop(0, ngrp, weigh, jnp.zeros((LANES, width), f32))
            return m_new, jnp.exp2(m_run - m_new) * o_run + o_seg

        nseg = (jnp.maximum(blk.n_off - HEAD_BLOCKS, 0) + SUP_BLOCKS - 1) // SUP_BLOCKS
        return lax.fori_loop(0, nseg, segment, (m_run, o_run))[1]

    blocks = [Block(qb) for qb in range(QB_PER_STEP)]
    fronts = [front(blk) for blk in blocks]
    fronts = [middle(blk, *f) for blk, f in zip(blocks, fronts)]
    tails = [rest(blk, m_run, o_run) for blk, (_, m_run, o_run) in zip(blocks, fronts)]
    for blk, (o_win, _, _), o_run in zip(blocks, fronts, tails):
        qb = blk.qb
        gate = gate_ref[0, 0, qb]
        mix = gate[0:1] * ocmp_ref[0, 0, qb] + gate[1:2] * _normalize(o_run) + gate[2:3] * o_win
        rows = jnp.concatenate([mix[:, r * Q_BLOCK:(r + 1) * Q_BLOCK] for r in range(HEADS_PER_GROUP)],
                               axis=0)
        out_ref[0, qb * Q_BLOCK:(qb + 1) * Q_BLOCK, :] = rows.T


def _slc_win_attention(lst, cnt, qt, ks, vs, kw, vw, madd, ocmp, gate_t, wtab):
    b, t, _ = ks.shape
    g = N_GROUPS
    nqb, nsel = t // Q_BLOCK, t // SEL_BLOCK
    width = HEADS_PER_GROUP * Q_BLOCK
    nstep = nqb // QB_PER_STEP
    flat = lambda bi, gi, ci: (bi * g + gi) * nstep + ci
    smem = lambda w_: pl.BlockSpec((QB_PER_STEP, 1, w_), lambda bi, gi, ci: (flat(bi, gi, ci), 0, 0),
                                   memory_space=pltpu.SMEM)
    per_q = lambda r_, c_: pl.BlockSpec((1, 1, QB_PER_STEP, r_, c_), lambda bi, gi, ci: (bi, gi, ci, 0, 0))
    keys = pl.BlockSpec((1, t, LANES), lambda bi, gi, ci: (bi, 0, gi))
    return pl.pallas_call(
        functools.partial(_slc_win_body, nsel=nsel),
        grid=(b, g, nstep),
        in_specs=[smem(nsel), smem(LANES),
                  pl.BlockSpec((QB_PER_STEP, 1, LANES, width), lambda bi, gi, ci: (bi * nstep + ci, gi, 0, 0)),
                  keys, keys, keys, keys,
                  per_q(nsel, Q_BLOCK), per_q(HEAD_DIM, width), per_q(8, width),
                  pl.BlockSpec(wtab.shape, lambda bi, gi, ci: (0, 0, 0))],
        out_specs=pl.BlockSpec((1, QB_PER_STEP * Q_BLOCK, 2 * LANES), lambda bi, gi, ci: (bi, ci, gi)),
        out_shape=jax.ShapeDtypeStruct((b, t, ATTN_WIDTH), f32),
        scratch_shapes=[pltpu.VMEM((SUP_BLOCKS * SEL_BLOCK, width), f32)],
        compiler_params=pltpu.CompilerParams(dimension_semantics=("arbitrary",) * 3,
                                             vmem_limit_bytes=VMEM_LIMIT),
        name="slc_win_attention",
    )(lst, cnt, qt, ks, vs, kw, vw, madd, ocmp, gate_t, wtab)


def _rms(v, g):
    return (v * lax.rsqrt(jnp.mean(v * v, axis=-1, keepdims=True) + EPS)) * g


def _outproj_body(x_ref, attn_ref, u_ref, uprev_ref, bg_ref, cw_ref, go_ref, wo_ref, out_ref, *, tm, seq):
    first = (pl.program_id(0) * tm) % seq == 0
    u = u_ref[...]
    prev = jnp.where(first, 0.0, uprev_ref[0])
    ext = jnp.concatenate([prev, u], axis=0)
    cw = cw_ref[...]
    conv = cw[0:1] * ext[6:6 + tm] + cw[1:2] * ext[7:7 + tm] + cw[2:3] * u
    conv = bg_ref[...] * conv
    go = go_ref[...]
    mixed = jnp.concatenate([_rms(attn_ref[...], go[:, :ATTN_WIDTH]), _rms(conv, go[:, ATTN_WIDTH:])],
                            axis=1).astype(bf16)
    out_ref[...] = x_ref[...] + _dot(mixed, wo_ref[...])


def _outproj(x2, attn2, u, bgate, cw, go, wo, *, seq, tm=512):
    n = x2.shape[0]
    u8 = u.reshape(n // 8, 8, CONV_WIDTH)
    row = lambda w_: pl.BlockSpec((tm, w_), lambda i: (i, 0))
    full = lambda a: pl.BlockSpec(a.shape, lambda i: (0,) * a.ndim)
    return pl.pallas_call(
        functools.partial(_outproj_body, tm=tm, seq=seq),
        grid=(n // tm,),
        in_specs=[row(D_MODEL), row(ATTN_WIDTH), row(CONV_WIDTH),
                  pl.BlockSpec((1, 8, CONV_WIDTH), lambda i: (jnp.maximum(i * (tm // 8) - 1, 0), 0, 0)),
                  row(CONV_WIDTH), full(cw), full(go), full(wo)],
        out_specs=row(D_MODEL),
        out_shape=jax.ShapeDtypeStruct((n, D_MODEL), f32),
        compiler_params=pltpu.CompilerParams(dimension_semantics=("arbitrary",),
                                             vmem_limit_bytes=VMEM_LIMIT),
        name="outproj",
    )(x2, attn2, u, u8, bgate, cw, go, wo)


def _ffn_body(x_ref, g_ref, wu_ref, wd_ref, out_ref, *, chunk):
    x = x_ref[...]
    h = _rms(x, g_ref[...]).astype(bf16)
    acc = x
    for c in range(D_FF // chunk):
        a = jnp.maximum(_dot(h, wu_ref[:, c * chunk:(c + 1) * chunk]), 0.0)
        acc = acc + _dot((a * a).astype(bf16), wd_ref[c * chunk:(c + 1) * chunk, :])
    out_ref[...] = acc


def _ffn(x2, g, wu, wd, *, tm=512, chunk=1024):
    n = x2.shape[0]
    row = pl.BlockSpec((tm, D_MODEL), lambda i: (i, 0))
    full = lambda a: pl.BlockSpec(a.shape, lambda i: (0,) * a.ndim)
    return pl.pallas_call(
        functools.partial(_ffn_body, chunk=chunk),
        grid=(n // tm,),
        in_specs=[row, full(g), full(wu), full(wd)],
        out_specs=row,
        out_shape=jax.ShapeDtypeStruct((n, D_MODEL), f32),
        compiler_params=pltpu.CompilerParams(dimension_semantics=("arbitrary",),
                                             vmem_limit_bytes=VMEM_LIMIT),
        name="ffn",
    )(x2, g, wu, wd)


def _interleave_zero(w):
    z = jnp.zeros(w.shape[:-1] + (HEAD_DIM,), w.dtype)
    return jnp.concatenate([w[..., :HEAD_DIM], z, w[..., HEAD_DIM:], z], axis=-1)


def _prep_params(g_mix_norm, w_in, g_q, g_k, pe_cmp, w_cmp1, b_cmp1, w_cmp2, b_cmp2,
                 conv_w, g_out, w_o, g_ffn_norm, w_up, w_down):
    depth = w_in.shape[0]
    o = np.cumsum([0, ATTN_WIDTH] + [LANES] * 6 + [N_BRANCH * N_HEADS] + [CONV_WIDTH] * 3)
    part = lambda i: w_in[..., int(o[i]):int(o[i + 1])]
    q, kc, vc, ks, vs, kw, vw, gl, hc, cg, bg = [part(i) for i in range(11)]
    gl = jnp.pad(gl, ((0, 0), (0, 0), (0, LANES - gl.shape[-1])))
    w = jnp.concatenate([q, kc, vc, _interleave_zero(ks), _interleave_zero(kw), _interleave_zero(vs),
                         _interleave_zero(vw), hc, cg, bg, gl],
                        axis=-1).astype(bf16)
    tile2 = lambda gk: _interleave_zero(jnp.concatenate([gk, gk], axis=-1))[:, None, :]
    w1 = w_cmp1.astype(bf16).reshape(depth, 2, 2, CMP_STRIDE, HEAD_DIM, CMP_HIDDEN)
    z1 = jnp.zeros_like(w1)
    w1g = jnp.stack([jnp.concatenate([w1, z1], axis=-1), jnp.concatenate([z1, w1], axis=-1)], axis=4)
    w1g = w1g.reshape(depth, 2, 2, CMP_STRIDE * LANES, N_GROUPS * CMP_HIDDEN)
    pe = pe_cmp.reshape(depth, 2, 2, CMP_STRIDE, 1, HEAD_DIM)
    pe = jnp.broadcast_to(pe, (depth, 2, 2, CMP_STRIDE, N_GROUPS, HEAD_DIM)).reshape(depth, 2, 2, 1, -1)
    pe = jnp.pad(pe, ((0, 0), (0, 0), (0, 0), (0, 7), (0, 0))).astype(bf16)
    b1 = jnp.concatenate([b_cmp1, b_cmp1], axis=-1)[:, :, None, :]
    z2 = jnp.zeros_like(w_cmp2)
    w2 = jnp.concatenate([jnp.concatenate([w_cmp2, z2], axis=-1), jnp.concatenate([z2, w_cmp2], axis=-1)],
                         axis=2)
    w2 = _interleave_zero(w2).astype(bf16)
    b2 = _interleave_zero(jnp.concatenate([b_cmp2, b_cmp2], axis=-1))[:, :, None, :]
    return dict(
        gm=g_mix_norm[:, None, :], w=w,
        gq=jnp.tile(g_q, (1, N_HEADS))[:, None, :],
        gks=tile2(g_k[:, 1]), gkw=tile2(g_k[:, 2]), gkc=tile2(g_k[:, 0]),
        w1=w1g, pe=pe, b1=b1,
        w2k=w2[:, 0], w2v=w2[:, 1], b2k=b2[:, 0], b2v=b2[:, 1],
        cw=jnp.pad(conv_w, ((0, 0), (0, 8 - conv_w.shape[1]), (0, 0))),
        go=g_out[:, None, :], wo=w_o.astype(bf16),
        gf=g_ffn_norm[:, None, :], wu=w_up.astype(bf16), wd=w_down.astype(bf16),
    )


def _constants(nsel):
    lane = np.arange(LANES)
    e = (lane[:, None] // HEAD_DIM == lane[None, :] // HEAD_DIM).astype(np.float32)
    tail = np.zeros((N_GROUPS, HEAD_DIM, HEADS_PER_GROUP * Q_BLOCK), np.float32)
    for gi in range(N_GROUPS):
        for r in range(HEADS_PER_GROUP):
            rest = np.float64(2.0 ** -(gi * HEADS_PER_GROUP + r + 1)) * LOG2E
            for k in range(SLOPE_TERMS):
                term = np.float64(np.asarray(rest, np.float32).astype(jnp.bfloat16).astype(np.float32))
                tail[gi, k, r * Q_BLOCK:(r + 1) * Q_BLOCK] = term * LANES
                tail[gi, SLOPE_TERMS + k, r * Q_BLOCK:(r + 1) * Q_BLOCK] = term
                rest = rest - term
    mct = np.zeros((IMP_ROWS, CMP_CHUNK), np.float32)
    for i in range(CMP_CHUNK):
        lo, hi = i * CMP_STRIDE, i * CMP_STRIDE + CMP_BLOCK
        for jj in range(SEL_PER_CHUNK + 1):
            ov = min(hi, (jj + 1) * SEL_BLOCK) - max(lo, jj * SEL_BLOCK)
            if ov > 0:
                mct[jj, i] = ov / CMP_BLOCK
    kk = np.arange(LANES)[:, None]
    ql = np.tile(np.arange(Q_BLOCK), HEADS_PER_GROUP)[None, :]
    wtab = np.zeros((4, LANES, HEADS_PER_GROUP * Q_BLOCK), np.float32)
    wtab[1] = np.where(kk > ql, 0.0, NEG)
    wtab[2] = np.where(kk <= ql, 0.0, NEG)
    wtab[3] = NEG
    tri = np.arange(nsel)[:, None] < np.arange(nsel)[None, :]
    return (jnp.asarray(e, bf16), jnp.asarray(tail, bf16), jnp.asarray(mct, bf16), jnp.asarray(wtab),
            jnp.asarray(tri, bf16))


def _layer(x2, p, consts, *, batch, seq):
    e, tail, mct, wtab, tri = consts
    g = N_GROUPS
    nqb, nsel, tc = seq // Q_BLOCK, seq // SEL_BLOCK, seq // CMP_STRIDE
    qt, kc, vc, ks, kw, vs, vw, u, bgate, gates = _inproj(
        x2, p["gm"], p["w"], p["gq"], p["gks"], p["gkw"], e, tail, seq=seq)
    kcmp, vcmp = _compress(kc.reshape(batch, tc, CMP_STRIDE * LANES), vc.reshape(batch, tc, CMP_STRIDE * LANES),
                           p["w1"], p["pe"], p["b1"], p["w2k"], p["w2v"], p["b2k"], p["b2v"], p["gkc"], e)
    gate_t = gates[:, :N_HEADS * N_BRANCH].reshape(batch, nqb, Q_BLOCK, g, HEADS_PER_GROUP, N_BRANCH)
    gate_t = gate_t.transpose(0, 3, 1, 5, 4, 2).reshape(batch, g, nqb, N_BRANCH, HEADS_PER_GROUP * Q_BLOCK)
    gate_t = jnp.pad(gate_t, ((0, 0), (0, 0), (0, 0), (0, 8 - N_BRANCH), (0, 0)))
    ocmp, madd, lst, cnt = _cmp_attention(qt, kcmp, vcmp, mct, tri, batch=batch)
    rows3 = lambda a: a.reshape(batch, seq, 2 * LANES)
    smem = lambda a: a[:, :, :, 0, :].reshape(batch * g * nqb, 1, a.shape[-1])
    attn = _slc_win_attention(smem(lst), smem(cnt), qt, rows3(ks), rows3(vs),
                              rows3(kw), rows3(vw), madd, ocmp, gate_t, wtab)
    x2 = _outproj(x2, attn.reshape(batch * seq, ATTN_WIDTH), u, bgate, p["cw"], p["go"], p["wo"], seq=seq)
    return _ffn(x2, p["gf"], p["wu"], p["wd"])


def kernel(x, g_mix_norm, w_in, g_q, g_k, pe_cmp, w_cmp1, b_cmp1, w_cmp2, b_cmp2, conv_w, g_out, w_o,
           g_ffn_norm, w_up, w_down):
    batch, seq, d = x.shape
    assert d == D_MODEL and seq % (CMP_CHUNK * CMP_STRIDE) == 0 and seq >= WINDOW + Q_BLOCK
    params = _prep_params(g_mix_norm, w_in, g_q, g_k, pe_cmp, w_cmp1, b_cmp1, w_cmp2, b_cmp2,
                          conv_w, g_out, w_o, g_ffn_norm, w_up, w_down)
    consts = _constants(seq // SEL_BLOCK)

    def step(x2, p):
        return _layer(x2, p, consts, batch=batch, seq=seq), None

    x2, _ = lax.scan(step, x.reshape(batch * seq, d), params)
    return x2.reshape(batch, seq, d)
```

```python
import functools

import numpy as np
import jax
import jax.numpy as jnp
from jax import lax
from jax.experimental import pallas as pl
from jax.experimental.pallas import tpu as pltpu

f32 = jnp.float32
bf16 = jnp.bfloat16
i32 = jnp.int32

D_MODEL = 1024
HEAD_DIM = 64
N_HEADS = 8
N_GROUPS = 2
HEADS_PER_GROUP = 4
ATTN_WIDTH = 512
CONV_WIDTH = 512
N_BRANCH = 3
CMP_BLOCK = 32
CMP_STRIDE = 16
CMP_HIDDEN = 256
SEL_BLOCK = 64
SEL_TOPK = 16
WINDOW = 512
Q_BLOCK = 128
D_FF = 4096
EPS = 1e-6
NEG = -1e30
LOG2E = 1.4426950408889634
SLOPE_TERMS = 3
PICKED = -3e38
LANES = 128
CMP_CHUNK = 256
SEL_PER_CHUNK = CMP_CHUNK * CMP_STRIDE // SEL_BLOCK
IMP_ROWS = SEL_PER_CHUNK + 8
VMEM_LIMIT = 56 * 1024 * 1024

PROJ_CHUNK = 512
SUP_BLOCKS = 32
HEAD_BLOCKS = 16
GROUP_BLOCKS = 4
QB_PER_STEP = 2


def _dot(a, b):
    return jnp.dot(a, b, preferred_element_type=f32)


def _dot_nt(a, b):
    return lax.dot_general(a, b, (((1,), (1,)), ((), ())), preferred_element_type=f32)


def _dot_tn(a, b):
    return lax.dot_general(a, b, (((0,), (0,)), ((), ())), preferred_element_type=f32)


def _head_norm(z, e, g):
    sq = (z * z).astype(bf16)
    outs = []
    for c in range(z.shape[1] // LANES):
        sl = slice(c * LANES, (c + 1) * LANES)
        outs.append(z[:, sl] * lax.rsqrt(_dot(sq[:, sl], e) * (1.0 / HEAD_DIM) + EPS))
    y = outs[0] if len(outs) == 1 else jnp.concatenate(outs, axis=1)
    return y * g


def _pos_cols(pos, width):
    rows = pos.shape[0]
    lane = lax.broadcasted_iota(i32, (rows, LANES), 1) - HEAD_DIM
    tile = jnp.where(lane < 0, 0.0, jnp.where(lane < SLOPE_TERMS, (pos >> 7).astype(f32),
                                              jnp.where(lane < 2 * SLOPE_TERMS, (pos & 127).astype(f32), 0.0)))
    return tile if width == LANES else jnp.concatenate([tile] * (width // LANES), axis=1)


def _ones_col(rows, width):
    lane = lax.broadcasted_iota(i32, (rows, LANES), 1)
    tile = jnp.where(lane == HEAD_DIM, 1.0, 0.0)
    return tile if width == LANES else jnp.concatenate([tile] * (width // LANES), axis=1)


def _inproj_body(x_ref, gm_ref, w_ref, gq_ref, gks_ref, gkw_ref, e_ref, tail_ref,
                 qt_ref, kc_ref, vc_ref, ks_ref, kw_ref, vs_ref, vw_ref, u_ref, bg_ref, gate_ref,
                 *, tm, seq):
    x = x_ref[...]
    ms = jnp.mean(x * x, axis=-1, keepdims=True)
    h = ((x * lax.rsqrt(ms + EPS)) * gm_ref[...]).astype(bf16)
    e = e_ref[...]
    two = 2 * LANES

    def proj(k):
        return _dot(h, w_ref[:, k * PROJ_CHUNK:(k + 1) * PROJ_CHUNK])

    def emit_queries(z):
        qn = _head_norm(z, e, gq_ref[...]) * (HEAD_DIM ** -0.5 * LOG2E)
        gw = HEADS_PER_GROUP * HEAD_DIM
        for cb in range(tm // Q_BLOCK):
            for g in range(N_GROUPS):
                blk = qn[cb * Q_BLOCK:(cb + 1) * Q_BLOCK, g * gw:(g + 1) * gw].T
                top = jnp.concatenate([blk[r * HEAD_DIM:(r + 1) * HEAD_DIM] for r in range(HEADS_PER_GROUP)],
                                      axis=1)
                qt_ref[cb, g] = jnp.concatenate([top.astype(bf16), tail_ref[g]], axis=0)

    def emit_compress_inputs(z):
        kc_ref[...] = z[:, 0:LANES].astype(bf16)
        vc_ref[...] = z[:, LANES:two].astype(bf16)
        gate_ref[...] = 1.0 / (1.0 + jnp.exp(-z[:, two:two + LANES]))

    def emit_keys(z):
        t0 = (pl.program_id(0) * tm) % seq
        pc = _pos_cols(lax.broadcasted_iota(i32, (tm, LANES), 0) + t0, two)
        ks_ref[...] = (_head_norm(z[:, 0:two], e, gks_ref[...]) + pc).astype(bf16)
        kw_ref[...] = (_head_norm(z[:, two:], e, gkw_ref[...]) + pc).astype(bf16)

    def emit_values(z):
        ones = _ones_col(tm, two)
        vs_ref[...] = (z[:, 0:two] + ones).astype(bf16)
        vw_ref[...] = (z[:, two:] + ones).astype(bf16)

    z0 = proj(0)
    z1 = proj(1)
    emit_queries(z0)
    z2 = proj(2)
    emit_compress_inputs(z1)
    z3 = proj(3)
    emit_keys(z2)
    z4 = proj(4)
    emit_values(z3)
    z5 = proj(5)
    z6 = proj(6)
    u_ref[...] = (z4 * z5).astype(bf16)
    bg_ref[...] = z6.astype(bf16)


def _inproj(x2, gm, w, gq, gks, gkw, e, tail, *, seq, tm=512):
    n = x2.shape[0]
    width = HEADS_PER_GROUP * Q_BLOCK
    row = lambda w_: pl.BlockSpec((tm, w_), lambda i: (i, 0))
    full = lambda a: pl.BlockSpec(a.shape, lambda i: (0,) * a.ndim)
    outs = [(LANES, bf16), (LANES, bf16), (2 * LANES, bf16), (2 * LANES, bf16),
            (2 * LANES, bf16), (2 * LANES, bf16), (CONV_WIDTH, bf16), (CONV_WIDTH, bf16), (LANES, f32)]
    qt_spec = pl.BlockSpec((tm // Q_BLOCK, N_GROUPS, LANES, width), lambda i: (i, 0, 0, 0))
    qt_shape = jax.ShapeDtypeStruct((n // Q_BLOCK, N_GROUPS, LANES, width), bf16)
    return pl.pallas_call(
        functools.partial(_inproj_body, tm=tm, seq=seq),
        grid=(n // tm,),
        in_specs=[row(D_MODEL), full(gm), full(w), full(gq), full(gks), full(gkw), full(e), full(tail)],
        out_specs=[qt_spec] + [row(w_) for w_, _ in outs],
        out_shape=[qt_shape] + [jax.ShapeDtypeStruct((n, w_), dt) for w_, dt in outs],
        compiler_params=pltpu.CompilerParams(dimension_semantics=("arbitrary",),
                                             vmem_limit_bytes=VMEM_LIMIT),
        name="inproj",
    )(x2, gm, w, gq, gks, gkw, e, tail)


def _gelu_tanh(x):
    return 0.5 * x * (1.0 + jnp.tanh(0.7978845608028654 * (x + 0.044715 * (x * x * x))))


def _compress_body(zk_ref, zv_ref, w1_ref, pe_ref, b1_ref, w2k_ref, w2v_ref, b2k_ref, b2v_ref,
                   gk_ref, e_ref, kc_ref, vc_ref, *, tc):
    last = lax.broadcasted_iota(i32, (tc, 1), 0) == tc - 1

    def hidden(z_ref, kind):
        z = z_ref[0]
        a = _dot(z, w1_ref[kind, 0])
        b = _dot(z, w1_ref[kind, 1])
        b = jnp.concatenate([b[1:], jnp.zeros((1, b.shape[1]), f32)], axis=0)
        bias = (_dot(pe_ref[kind, 0], w1_ref[kind, 0]) + _dot(pe_ref[kind, 1], w1_ref[kind, 1]))[0:1]
        return _gelu_tanh(a + b + bias + b1_ref[kind]).astype(bf16)

    k = _dot(hidden(zk_ref, 0), w2k_ref[...]) + b2k_ref[...]
    k = _head_norm(k, e_ref[...], gk_ref[...])
    pos = lax.broadcasted_iota(i32, (tc, LANES), 0) * CMP_STRIDE + (CMP_BLOCK - 1)
    k = k + _pos_cols(pos, 2 * LANES)
    kc_ref[0] = jnp.where(last, 0.0, k).astype(bf16)
    v = _dot(hidden(zv_ref, 1), w2v_ref[...]) + b2v_ref[...]
    vc_ref[0] = jnp.where(last, 0.0, v + _ones_col(tc, 2 * LANES)).astype(bf16)


def _compress(zk, zv, w1, pe, b1, w2k, w2v, b2k, b2v, gk, e):
    b, tc, _ = zk.shape
    blk = lambda a: pl.BlockSpec((1,) + a.shape[1:], lambda i: (i,) + (0,) * (a.ndim - 1))
    full = lambda a: pl.BlockSpec(a.shape, lambda i: (0,) * a.ndim)
    return pl.pallas_call(
        functools.partial(_compress_body, tc=tc),
        grid=(b,),
        in_specs=[blk(zk), blk(zv)] + [full(a) for a in (w1, pe, b1, w2k, w2v, b2k, b2v, gk, e)],
        out_specs=[pl.BlockSpec((1, tc, 2 * LANES), lambda i: (i, 0, 0))] * 2,
        out_shape=[jax.ShapeDtypeStruct((b, tc, 2 * LANES), bf16)] * 2,
        compiler_params=pltpu.CompilerParams(dimension_semantics=("arbitrary",),
                                             vmem_limit_bytes=VMEM_LIMIT),
        name="compress",
    )(zk, zv, w1, pe, b1, w2k, w2v, b2k, b2v, gk, e)


def _query_pos(c):
    lane = lax.broadcasted_iota(i32, (1, HEADS_PER_GROUP * Q_BLOCK), 1)
    return c * Q_BLOCK + (lane & (Q_BLOCK - 1))


def _pick_top(vs, jf, n_pick):
    vs = list(vs)
    for _ in range(n_pick):
        for g, v in enumerate(vs):
            mx = jnp.max(v, axis=0, keepdims=True)
            idx = jnp.min(jnp.where(v == mx, jf, float(jf.shape[0])), axis=0, keepdims=True)
            vs[g] = jnp.where(jf == idx, PICKED, v)
    return tuple(vs)


def _cmp_variant(nchunk, c, qt_ref, kc_ref, vc_ref, mct_ref, tri_ref, o_ref, madd_ref, lst_ref, cnt_ref,
                 *, nsel):
    groups = range(N_GROUPS)
    nrow = nchunk * CMP_CHUNK
    npre = nchunk * SEL_PER_CHUNK
    tq = _query_pos(c)
    sub = lax.broadcasted_iota(i32, (CMP_CHUNK, 1), 0)
    lanes = lambda g: slice(g * LANES, (g + 1) * LANES)
    chunk = lambda a, k: a[k * CMP_CHUNK:(k + 1) * CMP_CHUNK]

    tiles = []
    for g in groups:
        s = _dot(kc_ref[0, 0:nrow, lanes(g)], qt_ref[0, g])
        row = []
        for k in range(nchunk):
            t = chunk(s, k)
            if k >= nchunk - 2:
                seen = (k * CMP_CHUNK + sub) * CMP_STRIDE + (CMP_BLOCK - 1) <= tq
                t = jnp.where(seen, t, NEG)
            row.append(t)
        tiles.append(row)

    accs, imps = [], []
    for g in groups:
        m = tiles[g][0].max(axis=0, keepdims=True)
        for t in tiles[g][1:]:
            m = jnp.maximum(m, t.max(axis=0, keepdims=True))
        parts = [jnp.exp2(t - m).astype(bf16) for t in tiles[g]]
        accs.append(_dot_tn(vc_ref[0, 0:nrow, lanes(g)], jnp.concatenate(parts, axis=0)))
        rows, carry = [], None
        for part in parts:
            piece = _dot(mct_ref[...], part)
            body = piece[0:SEL_PER_CHUNK]
            if carry is not None:
                body = jnp.concatenate([body[0:8] + carry, body[8:]], axis=0)
            rows.append(body)
            carry = piece[SEL_PER_CHUNK:IMP_ROWS]
        imps.append(rows[0] if nchunk == 1 else jnp.concatenate(rows, axis=0))

    any_key = tq >= CMP_BLOCK - 1
    j = lax.broadcasted_iota(i32, (npre, Q_BLOCK), 0)
    jf = j.astype(f32)
    t1 = tq[:, 0:Q_BLOCK]
    jt = t1 >> 6
    valid = j * SEL_BLOCK <= t1
    vs = []
    for g in groups:
        inv = jnp.where(any_key, 1.0 / jnp.maximum(accs[g][HEAD_DIM:HEAD_DIM + 1], 1e-30), 0.0)
        o_ref[0, g, 0] = accs[g][0:HEAD_DIM] * inv
        imp4 = imps[g] * inv
        imp = imp4[:, 0:Q_BLOCK]
        for r in range(1, HEADS_PER_GROUP):
            imp = imp + imp4[:, r * Q_BLOCK:(r + 1) * Q_BLOCK]
        v = jnp.where(valid, imp, NEG)
        vs.append(jnp.where(j == 0, PICKED, jnp.where(j == jt, PICKED, jnp.where(j == jt - 1, PICKED, v))))
    vs = _pick_top(vs, jf, SEL_TOPK - 3)

    rr = lax.broadcasted_iota(i32, (nsel, nsel), 0).astype(f32)
    ones8 = jnp.ones((8, Q_BLOCK), bf16)
    jrow = jnp.broadcast_to(lax.broadcasted_iota(i32, (1, nsel), 1).astype(f32), (8, nsel)).astype(bf16)
    for g in groups:
        picked = vs[g] == PICKED
        madd = jnp.where(valid, jnp.where(picked, 0.0, NEG), NEG)
        sel = jnp.where(valid, jnp.where(picked, 1.0, 0.0), 0.0).astype(bf16)
        if npre < nsel:
            madd = jnp.concatenate([madd, jnp.full((nsel - npre, Q_BLOCK), NEG, f32)], axis=0)
            sel = jnp.concatenate([sel, jnp.zeros((nsel - npre, Q_BLOCK), bf16)], axis=0)
        madd_ref[0, g, 0] = madd
        flag = jnp.where(_dot_nt(ones8, sel)[0:1] > 0.0, 1.0, 0.0)
        flag8 = jnp.broadcast_to(flag, (8, nsel)).astype(bf16)
        prefix = _dot(flag8, tri_ref[...])[0:1]
        place = jnp.where(prefix == rr, flag, 0.0).astype(bf16)
        lst_ref[0, g, 0] = _dot_nt(jrow, place).astype(i32)
        cnt_ref[0, g, 0] = _dot(flag8, jnp.ones((nsel, LANES), bf16)).astype(i32)


def _cmp_body(qt_ref, kc_ref, vc_ref, mct_ref, tri_ref, o_ref, madd_ref, lst_ref, cnt_ref, *, nsel, nvar):
    c = pl.program_id(1)
    nch = (c * (Q_BLOCK // CMP_STRIDE) + (Q_BLOCK - CMP_BLOCK) // CMP_STRIDE) // CMP_CHUNK + 1
    for n in range(1, nvar + 1):
        pl.when(nch == n)(functools.partial(
            _cmp_variant, n, c, qt_ref, kc_ref, vc_ref, mct_ref, tri_ref, o_ref, madd_ref, lst_ref, cnt_ref,
            nsel=nsel))


def _cmp_attention(qt, kc, vc, mct, tri, *, batch):
    g = N_GROUPS
    nqb = qt.shape[0] // batch
    tc = kc.shape[1]
    nsel = tc * CMP_STRIDE // SEL_BLOCK
    width = HEADS_PER_GROUP * Q_BLOCK
    per_q = lambda r_, c_: pl.BlockSpec((1, g, 1, r_, c_), lambda bi, ci: (bi, 0, ci, 0, 0))
    shape = lambda r_, c_, dt: jax.ShapeDtypeStruct((batch, g, nqb, r_, c_), dt)
    return pl.pallas_call(
        functools.partial(_cmp_body, nsel=nsel, nvar=tc // CMP_CHUNK),
        grid=(batch, nqb),
        in_specs=[pl.BlockSpec((1, g, LANES, width), lambda bi, ci: (bi * nqb + ci, 0, 0, 0)),
                  pl.BlockSpec((1, tc, 2 * LANES), lambda bi, ci: (bi, 0, 0)),
                  pl.BlockSpec((1, tc, 2 * LANES), lambda bi, ci: (bi, 0, 0)),
                  pl.BlockSpec(mct.shape, lambda bi, ci: (0, 0)),
                  pl.BlockSpec(tri.shape, lambda bi, ci: (0, 0))],
        out_specs=[per_q(HEAD_DIM, width), per_q(nsel, Q_BLOCK), per_q(8, nsel), per_q(8, LANES)],
        out_shape=[shape(HEAD_DIM, width, f32), shape(nsel, Q_BLOCK, f32),
                   shape(8, nsel, i32), shape(8, LANES, i32)],
        compiler_params=pltpu.CompilerParams(dimension_semantics=("arbitrary",) * 2,
                                             vmem_limit_bytes=VMEM_LIMIT),
        name="cmp_attention",
    )(qt, kc, vc, mct, tri)


def _normalize(o_aug):
    return o_aug[0:HEAD_DIM] * (1.0 / jnp.maximum(o_aug[HEAD_DIM:HEAD_DIM + 1], 1e-30))


def _slc_win_body(lst_ref, cnt_ref, qt_ref, ks_ref, vs_ref, kw_ref, vw_ref,
                  madd_ref, ocmp_ref, gate_ref, wtab_ref, out_ref, s_scr, *, nsel):
    width = HEADS_PER_GROUP * Q_BLOCK
    nwin = (WINDOW + Q_BLOCK) // LANES
    wq = WINDOW // Q_BLOCK
    grp_rows = GROUP_BLOCKS * SEL_BLOCK

    class Block:
        def __init__(self, qb):
            self.qb = qb
            self.c = pl.program_id(2) * QB_PER_STEP + qb
            self.q0 = pl.multiple_of(self.c * Q_BLOCK, Q_BLOCK)
            self.qta = qt_ref[qb, 0]
            self.n_off = cnt_ref[qb, 0, 0] - 2

        def block_ids(self, first, nblk):
            ids = []
            for u in range(nblk):
                i = first + u
                j = lst_ref[self.qb, 0, jnp.minimum(i, nsel - 1)]
                ids.append((i < self.n_off, j, pl.multiple_of(j * SEL_BLOCK, SEL_BLOCK)))
            return ids

        def scores(self, ids):
            kcat = jnp.concatenate([ks_ref[0, pl.ds(r0, SEL_BLOCK), :] for _, _, r0 in ids], axis=0)
            sg = _dot(kcat, self.qta)
            tiles = []
            for u, (live, j, _) in enumerate(ids):
                mrow = jnp.where(live, madd_ref[0, 0, self.qb, pl.ds(j, 1), :], NEG)
                mrow = jnp.concatenate([mrow] * HEADS_PER_GROUP, axis=1)
                tiles.append(sg[u * SEL_BLOCK:(u + 1) * SEL_BLOCK] + mrow)
            return tiles

        def values(self, ids):
            return jnp.concatenate([vs_ref[0, pl.ds(r0, SEL_BLOCK), :] for _, _, r0 in ids], axis=0)

    def front(blk):
        c, q0, qta = blk.c, blk.q0, blk.qta
        ws = pl.multiple_of(jnp.maximum(c - wq, 0) * Q_BLOCK, Q_BLOCK)
        s = _dot(kw_ref[0, pl.ds(ws, WINDOW + Q_BLOCK), :], qta)
        chunks = []
        for k in range(nwin):
            steady = 1 if k == 0 else (2 if k == nwin - 1 else 0)
            tab = jnp.where(c >= wq, steady, jnp.where(k < c, 0, jnp.where(k == c, 2, 3)))
            chunks.append(s[k * LANES:(k + 1) * LANES] + wtab_ref[tab])
        head = blk.block_ids(0, HEAD_BLOCKS)
        tiles = [_dot(ks_ref[0, pl.ds(q0, Q_BLOCK), :], qta) + wtab_ref[2]] + blk.scores(head)
        return ws, chunks, head, tiles

    def softmax_pv(tiles, v):
        m = tiles[0].max(axis=0, keepdims=True)
        for su in tiles[1:]:
            m = jnp.maximum(m, su.max(axis=0, keepdims=True))
        p = jnp.concatenate([jnp.exp2(su - m).astype(bf16) for su in tiles], axis=0)
        return m, _dot_tn(v, p)

    def middle(blk, ws, chunks, head, tiles):
        _, o_win = softmax_pv(chunks, vw_ref[0, pl.ds(ws, WINDOW + Q_BLOCK), :])
        m_run, o_run = softmax_pv(
            tiles, jnp.concatenate([vs_ref[0, pl.ds(blk.q0, Q_BLOCK), :], blk.values(head)], axis=0))
        return _normalize(o_win), m_run, o_run

    def rest(blk, m_run, o_run):
        def segment(si, carry):
            m_run, o_run = carry
            base = HEAD_BLOCKS + si * SUP_BLOCKS
            ngrp = (jnp.minimum(SUP_BLOCKS, blk.n_off - base) + GROUP_BLOCKS - 1) // GROUP_BLOCKS
            rows = lambda gi: pl.ds(pl.multiple_of(gi * grp_rows, grp_rows), grp_rows)

            def score(gi, mx):
                sg = jnp.concatenate(blk.scores(blk.block_ids(base + gi * GROUP_BLOCKS, GROUP_BLOCKS)), axis=0)
                s_scr[rows(gi), :] = sg
                return jnp.maximum(mx, sg.max(axis=0, keepdims=True))

            m_new = lax.fori_loop(0, ngrp, score, m_run)

            def weigh(gi, acc):
                pg = jnp.exp2(s_scr[rows(gi), :] - m_new).astype(bf16)
                return acc + _dot_tn(blk.values(blk.block_ids(base + gi * GROUP_BLOCKS, GROUP_BLOCKS)), pg)

            o_seg = lax.fori_loop(0, ngrp, weigh, jnp.zeros((LANES, width), f32))
            return m_new, jnp.exp2(m_run - m_new) * o_run + o_seg

        nseg = (jnp.maximum(blk.n_off - HEAD_BLOCKS, 0) + SUP_BLOCKS - 1) // SUP_BLOCKS
        return lax.fori_loop(0, nseg, segment, (m_run, o_run))[1]

    blocks = [Block(qb) for qb in range(QB_PER_STEP)]
    fronts = [front(blk) for blk in blocks]
    fronts = [middle(blk, *f) for blk, f in zip(blocks, fronts)]
    tails = [rest(blk, m_run, o_run) for blk, (_, m_run, o_run) in zip(blocks, fronts)]
    for blk, (o_win, _, _), o_run in zip(blocks, fronts, tails):
        qb = blk.qb
        gate = gate_ref[0, 0, qb]
        mix = gate[0:1] * ocmp_ref[0, 0, qb] + gate[1:2] * _normalize(o_run) + gate[2:3] * o_win
        rows = jnp.concatenate([mix[:, r * Q_BLOCK:(r + 1) * Q_BLOCK] for r in range(HEADS_PER_GROUP)],
                               axis=0)
        out_ref[0, qb * Q_BLOCK:(qb + 1) * Q_BLOCK, :] = rows.T.astype(bf16)


def _slc_win_attention(lst, cnt, qt, ks, vs, kw, vw, madd, ocmp, gate_t, wtab):
    b, t, _ = ks.shape
    g = N_GROUPS
    nqb, nsel = t // Q_BLOCK, t // SEL_BLOCK
    width = HEADS_PER_GROUP * Q_BLOCK
    nstep = nqb // QB_PER_STEP
    flat = lambda bi, gi, ci: (bi * g + gi) * nstep + ci
    smem = lambda w_: pl.BlockSpec((QB_PER_STEP, 1, w_), lambda bi, gi, ci: (flat(bi, gi, ci), 0, 0),
                                   memory_space=pltpu.SMEM)
    per_q = lambda r_, c_: pl.BlockSpec((1, 1, QB_PER_STEP, r_, c_), lambda bi, gi, ci: (bi, gi, ci, 0, 0))
    keys = pl.BlockSpec((1, t, LANES), lambda bi, gi, ci: (bi, 0, gi))
    return pl.pallas_call(
        functools.partial(_slc_win_body, nsel=nsel),
        grid=(b, g, nstep),
        in_specs=[smem(nsel), smem(LANES),
                  pl.BlockSpec((QB_PER_STEP, 1, LANES, width), lambda bi, gi, ci: (bi * nstep + ci, gi, 0, 0)),
                  keys, keys, keys, keys,
                  per_q(nsel, Q_BLOCK), per_q(HEAD_DIM, width), per_q(8, width),
                  pl.BlockSpec(wtab.shape, lambda bi, gi, ci: (0, 0, 0))],
        out_specs=pl.BlockSpec((1, QB_PER_STEP * Q_BLOCK, 2 * LANES), lambda bi, gi, ci: (bi, ci, gi)),
        out_shape=jax.ShapeDtypeStruct((b, t, ATTN_WIDTH), bf16),
        scratch_shapes=[pltpu.VMEM((SUP_BLOCKS * SEL_BLOCK, width), f32)],
        compiler_params=pltpu.CompilerParams(dimension_semantics=("arbitrary",) * 3,
                                             vmem_limit_bytes=VMEM_LIMIT),
        name="slc_win_attention",
    )(lst, cnt, qt, ks, vs, kw, vw, madd, ocmp, gate_t, wtab)


PREV_ROWS = 16


def _rms(v, g):
    return (v * lax.rsqrt(jnp.mean(v * v, axis=-1, keepdims=True) + EPS)) * g


def _outproj_body(x_ref, attn_ref, u_ref, uprev_ref, bg_ref, cw_ref, go_ref, wo_ref, out_ref, *, tm, seq):
    first = (pl.program_id(0) * tm) % seq == 0
    u = u_ref[...].astype(f32)
    prev = jnp.where(first, 0.0, uprev_ref[0].astype(f32))
    ext = jnp.concatenate([prev, u], axis=0)
    cw = cw_ref[...]
    conv = (cw[0:1] * ext[PREV_ROWS - 2:PREV_ROWS - 2 + tm] + cw[1:2] * ext[PREV_ROWS - 1:PREV_ROWS - 1 + tm]
            + cw[2:3] * u)
    conv = bg_ref[...].astype(f32) * conv
    go = go_ref[...]
    mixed = jnp.concatenate([_rms(attn_ref[...].astype(f32), go[:, :ATTN_WIDTH]),
                             _rms(conv, go[:, ATTN_WIDTH:])], axis=1).astype(bf16)
    out_ref[...] = x_ref[...] + _dot(mixed, wo_ref[...])


def _outproj(x2, attn2, u, bgate, cw, go, wo, *, seq, tm=512):
    n = x2.shape[0]
    u8 = u.reshape(n // PREV_ROWS, PREV_ROWS, CONV_WIDTH)
    row = lambda w_: pl.BlockSpec((tm, w_), lambda i: (i, 0))
    full = lambda a: pl.BlockSpec(a.shape, lambda i: (0,) * a.ndim)
    return pl.pallas_call(
        functools.partial(_outproj_body, tm=tm, seq=seq),
        grid=(n // tm,),
        in_specs=[row(D_MODEL), row(ATTN_WIDTH), row(CONV_WIDTH),
                  pl.BlockSpec((1, PREV_ROWS, CONV_WIDTH),
                               lambda i: (jnp.maximum(i * (tm // PREV_ROWS) - 1, 0), 0, 0)),
                  row(CONV_WIDTH), full(cw), full(go), full(wo)],
        out_specs=row(D_MODEL),
        out_shape=jax.ShapeDtypeStruct((n, D_MODEL), f32),
        compiler_params=pltpu.CompilerParams(dimension_semantics=("arbitrary",),
                                             vmem_limit_bytes=VMEM_LIMIT),
        name="outproj",
    )(x2, attn2, u, u8, bgate, cw, go, wo)


def _ffn_body(x_ref, g_ref, wu_ref, wd_ref, out_ref, *, chunk):
    x = x_ref[...]
    h = _rms(x, g_ref[...]).astype(bf16)
    acc = x
    for c in range(D_FF // chunk):
        a = jnp.maximum(_dot(h, wu_ref[:, c * chunk:(c + 1) * chunk]), 0.0)
        acc = acc + _dot((a * a).astype(bf16), wd_ref[c * chunk:(c + 1) * chunk, :])
    out_ref[...] = acc


def _ffn(x2, g, wu, wd, *, tm=512, chunk=1024):
    n = x2.shape[0]
    row = pl.BlockSpec((tm, D_MODEL), lambda i: (i, 0))
    full = lambda a: pl.BlockSpec(a.shape, lambda i: (0,) * a.ndim)
    return pl.pallas_call(
        functools.partial(_ffn_body, chunk=chunk),
        grid=(n // tm,),
        in_specs=[row, full(g), full(wu), full(wd)],
        out_specs=row,
        out_shape=jax.ShapeDtypeStruct((n, D_MODEL), f32),
        compiler_params=pltpu.CompilerParams(dimension_semantics=("arbitrary",),
                                             vmem_limit_bytes=VMEM_LIMIT),
        name="ffn",
    )(x2, g, wu, wd)


def _interleave_zero(w):
    z = jnp.zeros(w.shape[:-1] + (HEAD_DIM,), w.dtype)
    return jnp.concatenate([w[..., :HEAD_DIM], z, w[..., HEAD_DIM:], z], axis=-1)


def _prep_params(g_mix_norm, w_in, g_q, g_k, pe_cmp, w_cmp1, b_cmp1, w_cmp2, b_cmp2,
                 conv_w, g_out, w_o, g_ffn_norm, w_up, w_down):
    depth = w_in.shape[0]
    o = np.cumsum([0, ATTN_WIDTH] + [LANES] * 6 + [N_BRANCH * N_HEADS] + [CONV_WIDTH] * 3)
    part = lambda i: w_in[..., int(o[i]):int(o[i + 1])]
    q, kc, vc, ks, vs, kw, vw, gl, hc, cg, bg = [part(i) for i in range(11)]
    gl = jnp.pad(gl, ((0, 0), (0, 0), (0, LANES - gl.shape[-1])))
    w = jnp.concatenate([q, kc, vc, gl, jnp.zeros_like(gl), _interleave_zero(ks), _interleave_zero(kw),
                         _interleave_zero(vs), _interleave_zero(vw), hc, cg, bg], axis=-1).astype(bf16)
    tile2 = lambda gk: _interleave_zero(jnp.concatenate([gk, gk], axis=-1))[:, None, :]
    w1 = w_cmp1.astype(bf16).reshape(depth, 2, 2, CMP_STRIDE, HEAD_DIM, CMP_HIDDEN)
    z1 = jnp.zeros_like(w1)
    w1g = jnp.stack([jnp.concatenate([w1, z1], axis=-1), jnp.concatenate([z1, w1], axis=-1)], axis=4)
    w1g = w1g.reshape(depth, 2, 2, CMP_STRIDE * LANES, N_GROUPS * CMP_HIDDEN)
    pe = pe_cmp.reshape(depth, 2, 2, CMP_STRIDE, 1, HEAD_DIM)
    pe = jnp.broadcast_to(pe, (depth, 2, 2, CMP_STRIDE, N_GROUPS, HEAD_DIM)).reshape(depth, 2, 2, 1, -1)
    pe = jnp.pad(pe, ((0, 0), (0, 0), (0, 0), (0, 7), (0, 0))).astype(bf16)
    b1 = jnp.concatenate([b_cmp1, b_cmp1], axis=-1)[:, :, None, :]
    z2 = jnp.zeros_like(w_cmp2)
    w2 = jnp.concatenate([jnp.concatenate([w_cmp2, z2], axis=-1), jnp.concatenate([z2, w_cmp2], axis=-1)],
                         axis=2)
    w2 = _interleave_zero(w2).astype(bf16)
    b2 = _interleave_zero(jnp.concatenate([b_cmp2, b_cmp2], axis=-1))[:, :, None, :]
    return dict(
        gm=g_mix_norm[:, None, :], w=w,
        gq=jnp.tile(g_q, (1, N_HEADS))[:, None, :],
        gks=tile2(g_k[:, 1]), gkw=tile2(g_k[:, 2]), gkc=tile2(g_k[:, 0]),
        w1=w1g, pe=pe, b1=b1,
        w2k=w2[:, 0], w2v=w2[:, 1], b2k=b2[:, 0], b2v=b2[:, 1],
        cw=jnp.pad(conv_w, ((0, 0), (0, 8 - conv_w.shape[1]), (0, 0))),
        go=g_out[:, None, :], wo=w_o.astype(bf16),
        gf=g_ffn_norm[:, None, :], wu=w_up.astype(bf16), wd=w_down.astype(bf16),
    )


def _constants(nsel):
    lane = np.arange(LANES)
    e = (lane[:, None] // HEAD_DIM == lane[None, :] // HEAD_DIM).astype(np.float32)
    tail = np.zeros((N_GROUPS, HEAD_DIM, HEADS_PER_GROUP * Q_BLOCK), np.float32)
    for gi in range(N_GROUPS):
        for r in range(HEADS_PER_GROUP):
            rest = np.float64(2.0 ** -(gi * HEADS_PER_GROUP + r + 1)) * LOG2E
            for k in range(SLOPE_TERMS):
                term = np.float64(np.asarray(rest, np.float32).astype(jnp.bfloat16).astype(np.float32))
                tail[gi, k, r * Q_BLOCK:(r + 1) * Q_BLOCK] = term * LANES
                tail[gi, SLOPE_TERMS + k, r * Q_BLOCK:(r + 1) * Q_BLOCK] = term
                rest = rest - term
    mct = np.zeros((IMP_ROWS, CMP_CHUNK), np.float32)
    for i in range(CMP_CHUNK):
        lo, hi = i * CMP_STRIDE, i * CMP_STRIDE + CMP_BLOCK
        for jj in range(SEL_PER_CHUNK + 1):
            ov = min(hi, (jj + 1) * SEL_BLOCK) - max(lo, jj * SEL_BLOCK)
            if ov > 0:
                mct[jj, i] = ov / CMP_BLOCK
    kk = np.arange(LANES)[:, None]
    ql = np.tile(np.arange(Q_BLOCK), HEADS_PER_GROUP)[None, :]
    wtab = np.zeros((4, LANES, HEADS_PER_GROUP * Q_BLOCK), np.float32)
    wtab[1] = np.where(kk > ql, 0.0, NEG)
    wtab[2] = np.where(kk <= ql, 0.0, NEG)
    wtab[3] = NEG
    tri = np.arange(nsel)[:, None] < np.arange(nsel)[None, :]
    return (jnp.asarray(e, bf16), jnp.asarray(tail, bf16), jnp.asarray(mct, bf16), jnp.asarray(wtab),
            jnp.asarray(tri, bf16))


def _layer(x2, p, consts, *, batch, seq):
    e, tail, mct, wtab, tri = consts
    g = N_GROUPS
    nqb, nsel, tc = seq // Q_BLOCK, seq // SEL_BLOCK, seq // CMP_STRIDE
    qt, kc, vc, ks, kw, vs, vw, u, bgate, gates = _inproj(
        x2, p["gm"], p["w"], p["gq"], p["gks"], p["gkw"], e, tail, seq=seq)
    kcmp, vcmp = _compress(kc.reshape(batch, tc, CMP_STRIDE * LANES), vc.reshape(batch, tc, CMP_STRIDE * LANES),
                           p["w1"], p["pe"], p["b1"], p["w2k"], p["w2v"], p["b2k"], p["b2v"], p["gkc"], e)
    gate_t = gates[:, :N_HEADS * N_BRANCH].reshape(batch, nqb, Q_BLOCK, g, HEADS_PER_GROUP, N_BRANCH)
    gate_t = gate_t.transpose(0, 3, 1, 5, 4, 2).reshape(batch, g, nqb, N_BRANCH, HEADS_PER_GROUP * Q_BLOCK)
    gate_t = jnp.pad(gate_t, ((0, 0), (0, 0), (0, 0), (0, 8 - N_BRANCH), (0, 0)))
    ocmp, madd, lst, cnt = _cmp_attention(qt, kcmp, vcmp, mct, tri, batch=batch)
    rows3 = lambda a: a.reshape(batch, seq, 2 * LANES)
    smem = lambda a: a[:, :, :, 0, :].reshape(batch * g * nqb, 1, a.shape[-1])
    attn = _slc_win_attention(smem(lst), smem(cnt), qt, rows3(ks), rows3(vs),
                              rows3(kw), rows3(vw), madd, ocmp, gate_t, wtab)
    x2 = _outproj(x2, attn.reshape(batch * seq, ATTN_WIDTH), u, bgate, p["cw"], p["go"], p["wo"], seq=seq)
    return _ffn(x2, p["gf"], p["wu"], p["wd"])


def kernel(x, g_mix_norm, w_in, g_q, g_k, pe_cmp, w_cmp1, b_cmp1, w_cmp2, b_cmp2, conv_w, g_out, w_o,
           g_ffn_norm, w_up, w_down):
    batch, seq, d = x.shape
    assert d == D_MODEL and seq % (CMP_CHUNK * CMP_STRIDE) == 0 and seq >= WINDOW + Q_BLOCK
    params = _prep_params(g_mix_norm, w_in, g_q, g_k, pe_cmp, w_cmp1, b_cmp1, w_cmp2, b_cmp2,
                          conv_w, g_out, w_o, g_ffn_norm, w_up, w_down)
    consts = _constants(seq // SEL_BLOCK)

    def step(x2, p):
        return _layer(x2, p, consts, batch=batch, seq=seq), None

    x2, _ = lax.scan(step, x.reshape(batch * seq, d), params)
    return x2.reshape(batch, seq, d)
```

```python
import functools

import numpy as np
import jax
import jax.numpy as jnp
from jax import lax
from jax.experimental import pallas as pl
from jax.experimental.pallas import tpu as pltpu

f32 = jnp.float32
bf16 = jnp.bfloat16
i32 = jnp.int32

D_MODEL = 1024
HEAD_DIM = 64
N_HEADS = 8
N_GROUPS = 2
HEADS_PER_GROUP = 4
ATTN_WIDTH = 512
CONV_WIDTH = 512
N_BRANCH = 3
CMP_BLOCK = 32
CMP_STRIDE = 16
CMP_HIDDEN = 256
SEL_BLOCK = 64
SEL_TOPK = 16
WINDOW = 512
Q_BLOCK = 128
D_FF = 4096
EPS = 1e-6
NEG = -1e30
LOG2E = 1.4426950408889634
SLOPE_TERMS = 3
PICKED = -3e38
LANES = 128
CMP_CHUNK = 256
SEL_PER_CHUNK = CMP_CHUNK * CMP_STRIDE // SEL_BLOCK
IMP_ROWS = SEL_PER_CHUNK + 8
VMEM_LIMIT = 56 * 1024 * 1024

PROJ_CHUNK = 512
SUP_BLOCKS = 32
HEAD_BLOCKS = 16
GROUP_BLOCKS = 4
QB_PER_STEP = 4


def _dot(a, b):
    return jnp.dot(a, b, preferred_element_type=f32)


def _dot_nt(a, b):
    return lax.dot_general(a, b, (((1,), (1,)), ((), ())), preferred_element_type=f32)


def _dot_tn(a, b):
    return lax.dot_general(a, b, (((0,), (0,)), ((), ())), preferred_element_type=f32)


def _head_norm(z, e, g):
    sq = (z * z).astype(bf16)
    outs = []
    for c in range(z.shape[1] // LANES):
        sl = slice(c * LANES, (c + 1) * LANES)
        outs.append(z[:, sl] * lax.rsqrt(_dot(sq[:, sl], e) * (1.0 / HEAD_DIM) + EPS))
    y = outs[0] if len(outs) == 1 else jnp.concatenate(outs, axis=1)
    return y * g


def _pos_cols(pos, width):
    rows = pos.shape[0]
    lane = lax.broadcasted_iota(i32, (rows, LANES), 1) - HEAD_DIM
    tile = jnp.where(lane < 0, 0.0, jnp.where(lane < SLOPE_TERMS, (pos >> 7).astype(f32),
                                              jnp.where(lane < 2 * SLOPE_TERMS, (pos & 127).astype(f32), 0.0)))
    return tile if width == LANES else jnp.concatenate([tile] * (width // LANES), axis=1)


def _ones_col(rows, width):
    lane = lax.broadcasted_iota(i32, (rows, LANES), 1)
    tile = jnp.where(lane == HEAD_DIM, 1.0, 0.0)
    return tile if width == LANES else jnp.concatenate([tile] * (width // LANES), axis=1)


def _inproj_body(x_ref, gm_ref, w_ref, gq_ref, gks_ref, gkw_ref, e_ref, tail_ref,
                 qt_ref, kc_ref, vc_ref, ks_ref, kw_ref, vs_ref, vw_ref, u_ref, bg_ref, gate_ref,
                 *, tm, seq):
    x = x_ref[...]
    ms = jnp.mean(x * x, axis=-1, keepdims=True)
    h = ((x * lax.rsqrt(ms + EPS)) * gm_ref[...]).astype(bf16)
    e = e_ref[...]
    two = 2 * LANES

    def proj(k):
        return _dot(h, w_ref[:, k * PROJ_CHUNK:(k + 1) * PROJ_CHUNK])

    def emit_queries(z):
        qn = _head_norm(z, e, gq_ref[...]) * (HEAD_DIM ** -0.5 * LOG2E)
        gw = HEADS_PER_GROUP * HEAD_DIM
        for cb in range(tm // Q_BLOCK):
            for g in range(N_GROUPS):
                blk = qn[cb * Q_BLOCK:(cb + 1) * Q_BLOCK, g * gw:(g + 1) * gw].T
                top = jnp.concatenate([blk[r * HEAD_DIM:(r + 1) * HEAD_DIM] for r in range(HEADS_PER_GROUP)],
                                      axis=1)
                qt_ref[cb, g] = jnp.concatenate([top.astype(bf16), tail_ref[g]], axis=0)

    def emit_compress_inputs(z):
        kc_ref[...] = z[:, 0:LANES].astype(bf16)
        vc_ref[...] = z[:, LANES:two].astype(bf16)
        gate_ref[...] = 1.0 / (1.0 + jnp.exp(-z[:, two:two + LANES]))

    def emit_keys(z):
        t0 = (pl.program_id(0) * tm) % seq
        pc = _pos_cols(lax.broadcasted_iota(i32, (tm, LANES), 0) + t0, two)
        ks_ref[...] = (_head_norm(z[:, 0:two], e, gks_ref[...]) + pc).astype(bf16)
        kw_ref[...] = (_head_norm(z[:, two:], e, gkw_ref[...]) + pc).astype(bf16)

    def emit_values(z):
        ones = _ones_col(tm, two)
        vs_ref[...] = (z[:, 0:two] + ones).astype(bf16)
        vw_ref[...] = (z[:, two:] + ones).astype(bf16)

    z0 = proj(0)
    z1 = proj(1)
    emit_queries(z0)
    z2 = proj(2)
    emit_compress_inputs(z1)
    z3 = proj(3)
    emit_keys(z2)
    z4 = proj(4)
    emit_values(z3)
    z5 = proj(5)
    z6 = proj(6)
    u_ref[...] = (z4 * z5).astype(bf16)
    bg_ref[...] = z6.astype(bf16)


def _inproj(x2, gm, w, gq, gks, gkw, e, tail, *, seq, tm=512):
    n = x2.shape[0]
    width = HEADS_PER_GROUP * Q_BLOCK
    row = lambda w_: pl.BlockSpec((tm, w_), lambda i: (i, 0))
    full = lambda a: pl.BlockSpec(a.shape, lambda i: (0,) * a.ndim)
    outs = [(LANES, bf16), (LANES, bf16), (2 * LANES, bf16), (2 * LANES, bf16),
            (2 * LANES, bf16), (2 * LANES, bf16), (CONV_WIDTH, bf16), (CONV_WIDTH, bf16), (LANES, f32)]
    qt_spec = pl.BlockSpec((tm // Q_BLOCK, N_GROUPS, LANES, width), lambda i: (i, 0, 0, 0))
    qt_shape = jax.ShapeDtypeStruct((n // Q_BLOCK, N_GROUPS, LANES, width), bf16)
    return pl.pallas_call(
        functools.partial(_inproj_body, tm=tm, seq=seq),
        grid=(n // tm,),
        in_specs=[row(D_MODEL), full(gm), full(w), full(gq), full(gks), full(gkw), full(e), full(tail)],
        out_specs=[qt_spec] + [row(w_) for w_, _ in outs],
        out_shape=[qt_shape] + [jax.ShapeDtypeStruct((n, w_), dt) for w_, dt in outs],
        compiler_params=pltpu.CompilerParams(dimension_semantics=("arbitrary",),
                                             vmem_limit_bytes=VMEM_LIMIT),
        name="inproj",
    )(x2, gm, w, gq, gks, gkw, e, tail)


def _gelu_tanh(x):
    return 0.5 * x * (1.0 + jnp.tanh(0.7978845608028654 * (x + 0.044715 * (x * x * x))))


def _compress_body(zk_ref, zv_ref, w1_ref, pe_ref, b1_ref, w2k_ref, w2v_ref, b2k_ref, b2v_ref,
                   gk_ref, e_ref, kc_ref, vc_ref, *, tc):
    last = lax.broadcasted_iota(i32, (tc, 1), 0) == tc - 1

    def hidden(z_ref, kind):
        z = z_ref[0]
        a = _dot(z, w1_ref[kind, 0])
        b = _dot(z, w1_ref[kind, 1])
        b = jnp.concatenate([b[1:], jnp.zeros((1, b.shape[1]), f32)], axis=0)
        bias = (_dot(pe_ref[kind, 0], w1_ref[kind, 0]) + _dot(pe_ref[kind, 1], w1_ref[kind, 1]))[0:1]
        return _gelu_tanh(a + b + bias + b1_ref[kind]).astype(bf16)

    k = _dot(hidden(zk_ref, 0), w2k_ref[...]) + b2k_ref[...]
    k = _head_norm(k, e_ref[...], gk_ref[...])
    pos = lax.broadcasted_iota(i32, (tc, LANES), 0) * CMP_STRIDE + (CMP_BLOCK - 1)
    k = k + _pos_cols(pos, 2 * LANES)
    kc_ref[0] = jnp.where(last, 0.0, k).astype(bf16)
    v = _dot(hidden(zv_ref, 1), w2v_ref[...]) + b2v_ref[...]
    vc_ref[0] = jnp.where(last, 0.0, v + _ones_col(tc, 2 * LANES)).astype(bf16)


def _compress(zk, zv, w1, pe, b1, w2k, w2v, b2k, b2v, gk, e):
    b, tc, _ = zk.shape
    blk = lambda a: pl.BlockSpec((1,) + a.shape[1:], lambda i: (i,) + (0,) * (a.ndim - 1))
    full = lambda a: pl.BlockSpec(a.shape, lambda i: (0,) * a.ndim)
    return pl.pallas_call(
        functools.partial(_compress_body, tc=tc),
        grid=(b,),
        in_specs=[blk(zk), blk(zv)] + [full(a) for a in (w1, pe, b1, w2k, w2v, b2k, b2v, gk, e)],
        out_specs=[pl.BlockSpec((1, tc, 2 * LANES), lambda i: (i, 0, 0))] * 2,
        out_shape=[jax.ShapeDtypeStruct((b, tc, 2 * LANES), bf16)] * 2,
        compiler_params=pltpu.CompilerParams(dimension_semantics=("arbitrary",),
                                             vmem_limit_bytes=VMEM_LIMIT),
        name="compress",
    )(zk, zv, w1, pe, b1, w2k, w2v, b2k, b2v, gk, e)


def _query_pos(c):
    lane = lax.broadcasted_iota(i32, (1, HEADS_PER_GROUP * Q_BLOCK), 1)
    return c * Q_BLOCK + (lane & (Q_BLOCK - 1))


def _pick_top(vs, jf, n_pick):
    vs = list(vs)
    for _ in range(n_pick):
        for g, v in enumerate(vs):
            mx = jnp.max(v, axis=0, keepdims=True)
            idx = jnp.min(jnp.where(v == mx, jf, float(jf.shape[0])), axis=0, keepdims=True)
            vs[g] = jnp.where(jf == idx, PICKED, v)
    return tuple(vs)


def _cmp_variant(nchunk, c, qt_ref, kc_ref, vc_ref, mct_ref, tri_ref, o_ref, madd_ref, lst_ref, cnt_ref,
                 *, nsel):
    groups = range(N_GROUPS)
    nrow = nchunk * CMP_CHUNK
    npre = nchunk * SEL_PER_CHUNK
    tq = _query_pos(c)
    sub = lax.broadcasted_iota(i32, (CMP_CHUNK, 1), 0)
    lanes = lambda g: slice(g * LANES, (g + 1) * LANES)
    chunk = lambda a, k: a[k * CMP_CHUNK:(k + 1) * CMP_CHUNK]

    tiles = []
    for g in groups:
        s = _dot(kc_ref[0, 0:nrow, lanes(g)], qt_ref[0, g])
        row = []
        for k in range(nchunk):
            t = chunk(s, k)
            if k >= nchunk - 2:
                seen = (k * CMP_CHUNK + sub) * CMP_STRIDE + (CMP_BLOCK - 1) <= tq
                t = jnp.where(seen, t, NEG)
            row.append(t)
        tiles.append(row)

    accs, imps = [], []
    for g in groups:
        m = tiles[g][0].max(axis=0, keepdims=True)
        for t in tiles[g][1:]:
            m = jnp.maximum(m, t.max(axis=0, keepdims=True))
        parts = [jnp.exp2(t - m).astype(bf16) for t in tiles[g]]
        accs.append(_dot_tn(vc_ref[0, 0:nrow, lanes(g)], jnp.concatenate(parts, axis=0)))
        rows, carry = [], None
        for part in parts:
            piece = _dot(mct_ref[...], part)
            body = piece[0:SEL_PER_CHUNK]
            if carry is not None:
                body = jnp.concatenate([body[0:8] + carry, body[8:]], axis=0)
            rows.append(body)
            carry = piece[SEL_PER_CHUNK:IMP_ROWS]
        imps.append(rows[0] if nchunk == 1 else jnp.concatenate(rows, axis=0))

    any_key = tq >= CMP_BLOCK - 1
    j = lax.broadcasted_iota(i32, (npre, Q_BLOCK), 0)
    jf = j.astype(f32)
    t1 = tq[:, 0:Q_BLOCK]
    jt = t1 >> 6
    valid = j * SEL_BLOCK <= t1
    vs = []
    for g in groups:
        inv = jnp.where(any_key, 1.0 / jnp.maximum(accs[g][HEAD_DIM:HEAD_DIM + 1], 1e-30), 0.0)
        o_ref[0, g, 0] = accs[g][0:HEAD_DIM] * inv
        imp4 = imps[g] * inv
        imp = imp4[:, 0:Q_BLOCK]
        for r in range(1, HEADS_PER_GROUP):
            imp = imp + imp4[:, r * Q_BLOCK:(r + 1) * Q_BLOCK]
        v = jnp.where(valid, imp, NEG)
        vs.append(jnp.where(j == 0, PICKED, jnp.where(j == jt, PICKED, jnp.where(j == jt - 1, PICKED, v))))
    vs = _pick_top(vs, jf, SEL_TOPK - 3)

    rr = lax.broadcasted_iota(i32, (nsel, nsel), 0).astype(f32)
    ones8 = jnp.ones((8, Q_BLOCK), bf16)
    jrow = jnp.broadcast_to(lax.broadcasted_iota(i32, (1, nsel), 1).astype(f32), (8, nsel)).astype(bf16)
    for g in groups:
        picked = vs[g] == PICKED
        madd = jnp.where(valid, jnp.where(picked, 0.0, NEG), NEG)
        sel = jnp.where(valid, jnp.where(picked, 1.0, 0.0), 0.0).astype(bf16)
        if npre < nsel:
            madd = jnp.concatenate([madd, jnp.full((nsel - npre, Q_BLOCK), NEG, f32)], axis=0)
            sel = jnp.concatenate([sel, jnp.zeros((nsel - npre, Q_BLOCK), bf16)], axis=0)
        madd_ref[0, g, 0] = madd
        flag = jnp.where(_dot_nt(ones8, sel)[0:1] > 0.0, 1.0, 0.0)
        flag8 = jnp.broadcast_to(flag, (8, nsel)).astype(bf16)
        prefix = _dot(flag8, tri_ref[...])[0:1]
        place = jnp.where(prefix == rr, flag, 0.0).astype(bf16)
        lst_ref[0, g, 0] = _dot_nt(jrow, place).astype(i32)
        cnt_ref[0, g, 0] = _dot(flag8, jnp.ones((nsel, LANES), bf16)).astype(i32)


def _cmp_body(qt_ref, kc_ref, vc_ref, mct_ref, tri_ref, o_ref, madd_ref, lst_ref, cnt_ref, *, nsel, nvar):
    c = pl.program_id(1)
    nch = (c * (Q_BLOCK // CMP_STRIDE) + (Q_BLOCK - CMP_BLOCK) // CMP_STRIDE) // CMP_CHUNK + 1
    for n in range(1, nvar + 1):
        pl.when(nch == n)(functools.partial(
            _cmp_variant, n, c, qt_ref, kc_ref, vc_ref, mct_ref, tri_ref, o_ref, madd_ref, lst_ref, cnt_ref,
            nsel=nsel))


def _cmp_attention(qt, kc, vc, mct, tri, *, batch):
    g = N_GROUPS
    nqb = qt.shape[0] // batch
    tc = kc.shape[1]
    nsel = tc * CMP_STRIDE // SEL_BLOCK
    width = HEADS_PER_GROUP * Q_BLOCK
    per_q = lambda r_, c_: pl.BlockSpec((1, g, 1, r_, c_), lambda bi, ci: (bi, 0, ci, 0, 0))
    shape = lambda r_, c_, dt: jax.ShapeDtypeStruct((batch, g, nqb, r_, c_), dt)
    return pl.pallas_call(
        functools.partial(_cmp_body, nsel=nsel, nvar=tc // CMP_CHUNK),
        grid=(batch, nqb),
        in_specs=[pl.BlockSpec((1, g, LANES, width), lambda bi, ci: (bi * nqb + ci, 0, 0, 0)),
                  pl.BlockSpec((1, tc, 2 * LANES), lambda bi, ci: (bi, 0, 0)),
                  pl.BlockSpec((1, tc, 2 * LANES), lambda bi, ci: (bi, 0, 0)),
                  pl.BlockSpec(mct.shape, lambda bi, ci: (0, 0)),
                  pl.BlockSpec(tri.shape, lambda bi, ci: (0, 0))],
        out_specs=[per_q(HEAD_DIM, width), per_q(nsel, Q_BLOCK), per_q(8, nsel), per_q(8, LANES)],
        out_shape=[shape(HEAD_DIM, width, f32), shape(nsel, Q_BLOCK, f32),
                   shape(8, nsel, i32), shape(8, LANES, i32)],
        compiler_params=pltpu.CompilerParams(dimension_semantics=("arbitrary",) * 2,
                                             vmem_limit_bytes=VMEM_LIMIT),
        name="cmp_attention",
    )(qt, kc, vc, mct, tri)


def _normalize(o_aug):
    return o_aug[0:HEAD_DIM] * (1.0 / jnp.maximum(o_aug[HEAD_DIM:HEAD_DIM + 1], 1e-30))


def _slc_win_body(lst_ref, cnt_ref, qt_ref, ks_ref, vs_ref, kw_ref, vw_ref,
                  madd_ref, ocmp_ref, gate_ref, wtab_ref, out_ref, s_scr, *, nsel):
    width = HEADS_PER_GROUP * Q_BLOCK
    nwin = (WINDOW + Q_BLOCK) // LANES
    wq = WINDOW // Q_BLOCK
    grp_rows = GROUP_BLOCKS * SEL_BLOCK

    class Block:
        def __init__(self, qb):
            self.qb = qb
            self.c = pl.program_id(2) * QB_PER_STEP + qb
            self.q0 = pl.multiple_of(self.c * Q_BLOCK, Q_BLOCK)
            self.qta = qt_ref[qb, 0]
            self.n_off = cnt_ref[qb, 0, 0] - 2

        def block_ids(self, first, nblk):
            ids = []
            for u in range(nblk):
                i = first + u
                j = lst_ref[self.qb, 0, jnp.minimum(i, nsel - 1)]
                ids.append((i < self.n_off, j, pl.multiple_of(j * SEL_BLOCK, SEL_BLOCK)))
            return ids

        def scores(self, ids):
            kcat = jnp.concatenate([ks_ref[0, pl.ds(r0, SEL_BLOCK), :] for _, _, r0 in ids], axis=0)
            sg = _dot(kcat, self.qta)
            tiles = []
            for u, (live, j, _) in enumerate(ids):
                mrow = jnp.where(live, madd_ref[0, 0, self.qb, pl.ds(j, 1), :], NEG)
                mrow = jnp.concatenate([mrow] * HEADS_PER_GROUP, axis=1)
                tiles.append(sg[u * SEL_BLOCK:(u + 1) * SEL_BLOCK] + mrow)
            return tiles

        def values(self, ids):
            return jnp.concatenate([vs_ref[0, pl.ds(r0, SEL_BLOCK), :] for _, _, r0 in ids], axis=0)

    def front(blk):
        c, q0, qta = blk.c, blk.q0, blk.qta
        ws = pl.multiple_of(jnp.maximum(c - wq, 0) * Q_BLOCK, Q_BLOCK)
        s = _dot(kw_ref[0, pl.ds(ws, WINDOW + Q_BLOCK), :], qta)
        chunks = []
        for k in range(nwin):
            steady = 1 if k == 0 else (2 if k == nwin - 1 else 0)
            tab = jnp.where(c >= wq, steady, jnp.where(k < c, 0, jnp.where(k == c, 2, 3)))
            chunks.append(s[k * LANES:(k + 1) * LANES] + wtab_ref[tab])
        head = blk.block_ids(0, HEAD_BLOCKS)
        tiles = [_dot(ks_ref[0, pl.ds(q0, Q_BLOCK), :], qta) + wtab_ref[2]] + blk.scores(head)
        return ws, chunks, head, tiles

    def softmax_pv(tiles, v):
        m = tiles[0].max(axis=0, keepdims=True)
        for su in tiles[1:]:
            m = jnp.maximum(m, su.max(axis=0, keepdims=True))
        p = jnp.concatenate([jnp.exp2(su - m).astype(bf16) for su in tiles], axis=0)
        return m, _dot_tn(v, p)

    def middle(blk, ws, chunks, head, tiles):
        _, o_win = softmax_pv(chunks, vw_ref[0, pl.ds(ws, WINDOW + Q_BLOCK), :])
        m_run, o_run = softmax_pv(
            tiles, jnp.concatenate([vs_ref[0, pl.ds(blk.q0, Q_BLOCK), :], blk.values(head)], axis=0))
        return _normalize(o_win), m_run, o_run

    def rest(blk, m_run, o_run):
        def segment(si, carry):
            m_run, o_run = carry
            base = HEAD_BLOCKS + si * SUP_BLOCKS
            ngrp = (jnp.minimum(SUP_BLOCKS, blk.n_off - base) + GROUP_BLOCKS - 1) // GROUP_BLOCKS
            rows = lambda gi: pl.ds(pl.multiple_of(gi * grp_rows, grp_rows), grp_rows)

            def score(gi, mx):
                sg = jnp.concatenate(blk.scores(blk.block_ids(base + gi * GROUP_BLOCKS, GROUP_BLOCKS)), axis=0)
                s_scr[rows(gi), :] = sg
                return jnp.maximum(mx, sg.max(axis=0, keepdims=True))

            m_new = lax.fori_loop(0, ngrp, score, m_run)

            def weigh(gi, acc):
                pg = jnp.exp2(s_scr[rows(gi), :] - m_new).astype(bf16)
                return acc + _dot_tn(blk.values(blk.block_ids(base + gi * GROUP_BLOCKS, GROUP_BLOCKS)), pg)

            o_seg = lax.fori_loop(0, ngrp, weigh, jnp.zeros((LANES, width), f32))
            return m_new, jnp.exp2(m_run - m_new) * o_run + o_seg

        nseg = (jnp.maximum(blk.n_off - HEAD_BLOCKS, 0) + SUP_BLOCKS - 1) // SUP_BLOCKS
        return lax.fori_loop(0, nseg, segment, (m_run, o_run))[1]

    blocks = [Block(qb) for qb in range(QB_PER_STEP)]
    fronts = [front(blk) for blk in blocks]
    fronts = [middle(blk, *f) for blk, f in zip(blocks, fronts)]
    tails = [rest(blk, m_run, o_run) for blk, (_, m_run, o_run) in zip(blocks, fronts)]
    for blk, (o_win, _, _), o_run in zip(blocks, fronts, tails):
        qb = blk.qb
        gate = gate_ref[0, 0, qb]
        mix = gate[0:1] * ocmp_ref[0, 0, qb] + gate[1:2] * _normalize(o_run) + gate[2:3] * o_win
        rows = jnp.concatenate([mix[:, r * Q_BLOCK:(r + 1) * Q_BLOCK] for r in range(HEADS_PER_GROUP)],
                               axis=0)
        out_ref[0, qb * Q_BLOCK:(qb + 1) * Q_BLOCK, :] = rows.T.astype(bf16)


def _slc_win_attention(lst, cnt, qt, ks, vs, kw, vw, madd, ocmp, gate_t, wtab):
    b, t, _ = ks.shape
    g = N_GROUPS
    nqb, nsel = t // Q_BLOCK, t // SEL_BLOCK
    width = HEADS_PER_GROUP * Q_BLOCK
    nstep = nqb // QB_PER_STEP
    flat = lambda bi, gi, ci: (bi * g + gi) * nstep + ci
    smem = lambda w_: pl.BlockSpec((QB_PER_STEP, 1, w_), lambda bi, gi, ci: (flat(bi, gi, ci), 0, 0),
                                   memory_space=pltpu.SMEM)
    per_q = lambda r_, c_: pl.BlockSpec((1, 1, QB_PER_STEP, r_, c_), lambda bi, gi, ci: (bi, gi, ci, 0, 0))
    keys = pl.BlockSpec((1, t, LANES), lambda bi, gi, ci: (bi, 0, gi))
    return pl.pallas_call(
        functools.partial(_slc_win_body, nsel=nsel),
        grid=(b, g, nstep),
        in_specs=[smem(nsel), smem(LANES),
                  pl.BlockSpec((QB_PER_STEP, 1, LANES, width), lambda bi, gi, ci: (bi * nstep + ci, gi, 0, 0)),
                  keys, keys, keys, keys,
                  per_q(nsel, Q_BLOCK), per_q(HEAD_DIM, width), per_q(8, width),
                  pl.BlockSpec(wtab.shape, lambda bi, gi, ci: (0, 0, 0))],
        out_specs=pl.BlockSpec((1, QB_PER_STEP * Q_BLOCK, 2 * LANES), lambda bi, gi, ci: (bi, ci, gi)),
        out_shape=jax.ShapeDtypeStruct((b, t, ATTN_WIDTH), bf16),
        scratch_shapes=[pltpu.VMEM((SUP_BLOCKS * SEL_BLOCK, width), f32)],
        compiler_params=pltpu.CompilerParams(dimension_semantics=("arbitrary",) * 3,
                                             vmem_limit_bytes=VMEM_LIMIT),
        name="slc_win_attention",
    )(lst, cnt, qt, ks, vs, kw, vw, madd, ocmp, gate_t, wtab)


PREV_ROWS = 16


def _rms(v, g):
    return (v * lax.rsqrt(jnp.mean(v * v, axis=-1, keepdims=True) + EPS)) * g


def _mix_ffn_body(x_ref, attn_ref, u_ref, uprev_ref, bg_ref, cw_ref, go_ref, wo_ref, gf_ref, wu_ref, wd_ref,
                  out_ref, *, tm, seq, chunk):
    first = (pl.program_id(0) * tm) % seq == 0
    u = u_ref[...].astype(f32)
    prev = jnp.where(first, 0.0, uprev_ref[0].astype(f32))
    ext = jnp.concatenate([prev, u], axis=0)
    cw = cw_ref[...]
    conv = (cw[0:1] * ext[PREV_ROWS - 2:PREV_ROWS - 2 + tm] + cw[1:2] * ext[PREV_ROWS - 1:PREV_ROWS - 1 + tm]
            + cw[2:3] * u)
    conv = bg_ref[...].astype(f32) * conv
    go = go_ref[...]
    mixed = jnp.concatenate([_rms(attn_ref[...].astype(f32), go[:, :ATTN_WIDTH]),
                             _rms(conv, go[:, ATTN_WIDTH:])], axis=1).astype(bf16)
    x = x_ref[...] + _dot(mixed, wo_ref[...])
    h = _rms(x, gf_ref[...]).astype(bf16)
    acc = x
    for c in range(D_FF // chunk):
        a = jnp.maximum(_dot(h, wu_ref[:, c * chunk:(c + 1) * chunk]), 0.0)
        acc = acc + _dot((a * a).astype(bf16), wd_ref[c * chunk:(c + 1) * chunk, :])
    out_ref[...] = acc


def _mix_ffn(x2, attn2, u, bgate, cw, go, wo, gf, wu, wd, *, seq, tm=512, chunk=1024):
    n = x2.shape[0]
    uprev = u.reshape(n // PREV_ROWS, PREV_ROWS, CONV_WIDTH)
    row = lambda w_: pl.BlockSpec((tm, w_), lambda i: (i, 0))
    full = lambda a: pl.BlockSpec(a.shape, lambda i: (0,) * a.ndim)
    once = lambda a: pl.BlockSpec(a.shape, lambda i: (0,) * a.ndim, pipeline_mode=pl.Buffered(1))
    return pl.pallas_call(
        functools.partial(_mix_ffn_body, tm=tm, seq=seq, chunk=chunk),
        grid=(n // tm,),
        in_specs=[row(D_MODEL), row(ATTN_WIDTH), row(CONV_WIDTH),
                  pl.BlockSpec((1, PREV_ROWS, CONV_WIDTH),
                               lambda i: (jnp.maximum(i * (tm // PREV_ROWS) - 1, 0), 0, 0)),
                  row(CONV_WIDTH), full(cw), full(go), once(wo), full(gf), once(wu), once(wd)],
        out_specs=row(D_MODEL),
        out_shape=jax.ShapeDtypeStruct((n, D_MODEL), f32),
        compiler_params=pltpu.CompilerParams(dimension_semantics=("arbitrary",),
                                             vmem_limit_bytes=VMEM_LIMIT),
        name="mix_ffn",
    )(x2, attn2, u, uprev, bgate, cw, go, wo, gf, wu, wd)


def _interleave_zero(w):
    z = jnp.zeros(w.shape[:-1] + (HEAD_DIM,), w.dtype)
    return jnp.concatenate([w[..., :HEAD_DIM], z, w[..., HEAD_DIM:], z], axis=-1)


def _prep_params(g_mix_norm, w_in, g_q, g_k, pe_cmp, w_cmp1, b_cmp1, w_cmp2, b_cmp2,
                 conv_w, g_out, w_o, g_ffn_norm, w_up, w_down):
    depth = w_in.shape[0]
    o = np.cumsum([0, ATTN_WIDTH] + [LANES] * 6 + [N_BRANCH * N_HEADS] + [CONV_WIDTH] * 3)
    part = lambda i: w_in[..., int(o[i]):int(o[i + 1])]
    q, kc, vc, ks, vs, kw, vw, gl, hc, cg, bg = [part(i) for i in range(11)]
    gl = jnp.pad(gl, ((0, 0), (0, 0), (0, LANES - gl.shape[-1])))
    w = jnp.concatenate([q, kc, vc, gl, jnp.zeros_like(gl), _interleave_zero(ks), _interleave_zero(kw),
                         _interleave_zero(vs), _interleave_zero(vw), hc, cg, bg], axis=-1).astype(bf16)
    tile2 = lambda gk: _interleave_zero(jnp.concatenate([gk, gk], axis=-1))[:, None, :]
    w1 = w_cmp1.astype(bf16).reshape(depth, 2, 2, CMP_STRIDE, HEAD_DIM, CMP_HIDDEN)
    z1 = jnp.zeros_like(w1)
    w1g = jnp.stack([jnp.concatenate([w1, z1], axis=-1), jnp.concatenate([z1, w1], axis=-1)], axis=4)
    w1g = w1g.reshape(depth, 2, 2, CMP_STRIDE * LANES, N_GROUPS * CMP_HIDDEN)
    pe = pe_cmp.reshape(depth, 2, 2, CMP_STRIDE, 1, HEAD_DIM)
    pe = jnp.broadcast_to(pe, (depth, 2, 2, CMP_STRIDE, N_GROUPS, HEAD_DIM)).reshape(depth, 2, 2, 1, -1)
    pe = jnp.pad(pe, ((0, 0), (0, 0), (0, 0), (0, 7), (0, 0))).astype(bf16)
    b1 = jnp.concatenate([b_cmp1, b_cmp1], axis=-1)[:, :, None, :]
    z2 = jnp.zeros_like(w_cmp2)
    w2 = jnp.concatenate([jnp.concatenate([w_cmp2, z2], axis=-1), jnp.concatenate([z2, w_cmp2], axis=-1)],
                         axis=2)
    w2 = _interleave_zero(w2).astype(bf16)
    b2 = _interleave_zero(jnp.concatenate([b_cmp2, b_cmp2], axis=-1))[:, :, None, :]
    return dict(
        gm=g_mix_norm[:, None, :], w=w,
        gq=jnp.tile(g_q, (1, N_HEADS))[:, None, :],
        gks=tile2(g_k[:, 1]), gkw=tile2(g_k[:, 2]), gkc=tile2(g_k[:, 0]),
        w1=w1g, pe=pe, b1=b1,
        w2k=w2[:, 0], w2v=w2[:, 1], b2k=b2[:, 0], b2v=b2[:, 1],
        cw=jnp.pad(conv_w, ((0, 0), (0, 8 - conv_w.shape[1]), (0, 0))),
        go=g_out[:, None, :], wo=w_o.astype(bf16),
        gf=g_ffn_norm[:, None, :], wu=w_up.astype(bf16), wd=w_down.astype(bf16),
    )


def _constants(nsel):
    lane = np.arange(LANES)
    e = (lane[:, None] // HEAD_DIM == lane[None, :] // HEAD_DIM).astype(np.float32)
    tail = np.zeros((N_GROUPS, HEAD_DIM, HEADS_PER_GROUP * Q_BLOCK), np.float32)
    for gi in range(N_GROUPS):
        for r in range(HEADS_PER_GROUP):
            rest = np.float64(2.0 ** -(gi * HEADS_PER_GROUP + r + 1)) * LOG2E
            for k in range(SLOPE_TERMS):
                term = np.float64(np.asarray(rest, np.float32).astype(jnp.bfloat16).astype(np.float32))
                tail[gi, k, r * Q_BLOCK:(r + 1) * Q_BLOCK] = term * LANES
                tail[gi, SLOPE_TERMS + k, r * Q_BLOCK:(r + 1) * Q_BLOCK] = term
                rest = rest - term
    mct = np.zeros((IMP_ROWS, CMP_CHUNK), np.float32)
    for i in range(CMP_CHUNK):
        lo, hi = i * CMP_STRIDE, i * CMP_STRIDE + CMP_BLOCK
        for jj in range(SEL_PER_CHUNK + 1):
            ov = min(hi, (jj + 1) * SEL_BLOCK) - max(lo, jj * SEL_BLOCK)
            if ov > 0:
                mct[jj, i] = ov / CMP_BLOCK
    kk = np.arange(LANES)[:, None]
    ql = np.tile(np.arange(Q_BLOCK), HEADS_PER_GROUP)[None, :]
    wtab = np.zeros((4, LANES, HEADS_PER_GROUP * Q_BLOCK), np.float32)
    wtab[1] = np.where(kk > ql, 0.0, NEG)
    wtab[2] = np.where(kk <= ql, 0.0, NEG)
    wtab[3] = NEG
    tri = np.arange(nsel)[:, None] < np.arange(nsel)[None, :]
    return (jnp.asarray(e, bf16), jnp.asarray(tail, bf16), jnp.asarray(mct, bf16), jnp.asarray(wtab),
            jnp.asarray(tri, bf16))


def _layer(x2, p, consts, *, batch, seq):
    e, tail, mct, wtab, tri = consts
    g = N_GROUPS
    nqb, nsel, tc = seq // Q_BLOCK, seq // SEL_BLOCK, seq // CMP_STRIDE
    qt, kc, vc, ks, kw, vs, vw, u, bgate, gates = _inproj(
        x2, p["gm"], p["w"], p["gq"], p["gks"], p["gkw"], e, tail, seq=seq)
    kcmp, vcmp = _compress(kc.reshape(batch, tc, CMP_STRIDE * LANES), vc.reshape(batch, tc, CMP_STRIDE * LANES),
                           p["w1"], p["pe"], p["b1"], p["w2k"], p["w2v"], p["b2k"], p["b2v"], p["gkc"], e)
    gate_t = gates[:, :N_HEADS * N_BRANCH].reshape(batch, nqb, Q_BLOCK, g, HEADS_PER_GROUP, N_BRANCH)
    gate_t = gate_t.transpose(0, 3, 1, 5, 4, 2).reshape(batch, g, nqb, N_BRANCH, HEADS_PER_GROUP * Q_BLOCK)
    gate_t = jnp.pad(gate_t, ((0, 0), (0, 0), (0, 0), (0, 8 - N_BRANCH), (0, 0)))
    ocmp, madd, lst, cnt = _cmp_attention(qt, kcmp, vcmp, mct, tri, batch=batch)
    rows3 = lambda a: a.reshape(batch, seq, 2 * LANES)
    smem = lambda a: a[:, :, :, 0, :].reshape(batch * g * nqb, 1, a.shape[-1])
    attn = _slc_win_attention(smem(lst), smem(cnt), qt, rows3(ks), rows3(vs),
                              rows3(kw), rows3(vw), madd, ocmp, gate_t, wtab)
    return _mix_ffn(x2, attn.reshape(batch * seq, ATTN_WIDTH), u, bgate, p["cw"], p["go"], p["wo"],
                    p["gf"], p["wu"], p["wd"], seq=seq)


def kernel(x, g_mix_norm, w_in, g_q, g_k, pe_cmp, w_cmp1, b_cmp1, w_cmp2, b_cmp2, conv_w, g_out, w_o,
           g_ffn_norm, w_up, w_down):
    batch, seq, d = x.shape
    assert d == D_MODEL and seq % (CMP_CHUNK * CMP_STRIDE) == 0 and seq >= WINDOW + Q_BLOCK
    params = _prep_params(g_mix_norm, w_in, g_q, g_k, pe_cmp, w_cmp1, b_cmp1, w_cmp2, b_cmp2,
                          conv_w, g_out, w_o, g_ffn_norm, w_up, w_down)
    consts = _constants(seq // SEL_BLOCK)

    def step(x2, p):
        return _layer(x2, p, consts, batch=batch, seq=seq), None

    x2, _ = lax.scan(step, x.reshape(batch * seq, d), params)
    return x2.reshape(batch, seq, d)
```

```python
import functools

import numpy as np
import jax
import jax.numpy as jnp
from jax import lax
from jax.experimental import pallas as pl
from jax.experimental.pallas import tpu as pltpu

f32 = jnp.float32
bf16 = jnp.bfloat16
i32 = jnp.int32

D_MODEL = 1024
HEAD_DIM = 64
N_HEADS = 8
N_GROUPS = 2
HEADS_PER_GROUP = 4
ATTN_WIDTH = 512
CONV_WIDTH = 512
N_BRANCH = 3
CMP_BLOCK = 32
CMP_STRIDE = 16
CMP_HIDDEN = 256
SEL_BLOCK = 64
SEL_TOPK = 16
WINDOW = 512
Q_BLOCK = 128
D_FF = 4096
EPS = 1e-6
NEG = -1e30
LOG2E = 1.4426950408889634
SLOPE_TERMS = 3
PICKED = -3e38
LANES = 128
CMP_CHUNK = 256
SEL_PER_CHUNK = CMP_CHUNK * CMP_STRIDE // SEL_BLOCK
IMP_ROWS = SEL_PER_CHUNK + 8
VMEM_LIMIT = 56 * 1024 * 1024

PROJ_CHUNK = 512
SUP_BLOCKS = 32
HEAD_BLOCKS = 16
GROUP_BLOCKS = 4
QB_PER_STEP = 4


def _dot(a, b):
    return jnp.dot(a, b, preferred_element_type=f32)


def _dot_nt(a, b):
    return lax.dot_general(a, b, (((1,), (1,)), ((), ())), preferred_element_type=f32)


def _dot_tn(a, b):
    return lax.dot_general(a, b, (((0,), (0,)), ((), ())), preferred_element_type=f32)


def _head_norm(z, e, g):
    sq = (z * z).astype(bf16)
    outs = []
    for c in range(z.shape[1] // LANES):
        sl = slice(c * LANES, (c + 1) * LANES)
        outs.append(z[:, sl] * lax.rsqrt(_dot(sq[:, sl], e) * (1.0 / HEAD_DIM) + EPS))
    y = outs[0] if len(outs) == 1 else jnp.concatenate(outs, axis=1)
    return y * g


def _pos_cols(pos, width):
    rows = pos.shape[0]
    lane = lax.broadcasted_iota(i32, (rows, LANES), 1) - HEAD_DIM
    tile = jnp.where(lane < 0, 0.0, jnp.where(lane < SLOPE_TERMS, (pos >> 7).astype(f32),
                                              jnp.where(lane < 2 * SLOPE_TERMS, (pos & 127).astype(f32), 0.0)))
    return tile if width == LANES else jnp.concatenate([tile] * (width // LANES), axis=1)


def _ones_col(rows, width):
    lane = lax.broadcasted_iota(i32, (rows, LANES), 1)
    tile = jnp.where(lane == HEAD_DIM, 1.0, 0.0)
    return tile if width == LANES else jnp.concatenate([tile] * (width // LANES), axis=1)


def _inproj_body(x_ref, gm_ref, w_ref, gq_ref, gks_ref, gkw_ref, e_ref, tail_ref,
                 qt_ref, kc_ref, vc_ref, ks_ref, kw_ref, vs_ref, vw_ref, u_ref, bg_ref, gate_ref,
                 *, tm, seq):
    x = x_ref[...]
    ms = jnp.mean(x * x, axis=-1, keepdims=True)
    h = ((x * lax.rsqrt(ms + EPS)) * gm_ref[...]).astype(bf16)
    e = e_ref[...]
    two = 2 * LANES

    def proj(k):
        return _dot(h, w_ref[:, k * PROJ_CHUNK:(k + 1) * PROJ_CHUNK])

    def emit_queries(z):
        qn = _head_norm(z, e, gq_ref[...]) * (HEAD_DIM ** -0.5 * LOG2E)
        gw = HEADS_PER_GROUP * HEAD_DIM
        for cb in range(tm // Q_BLOCK):
            for g in range(N_GROUPS):
                blk = qn[cb * Q_BLOCK:(cb + 1) * Q_BLOCK, g * gw:(g + 1) * gw].T
                top = jnp.concatenate([blk[r * HEAD_DIM:(r + 1) * HEAD_DIM] for r in range(HEADS_PER_GROUP)],
                                      axis=1)
                qt_ref[cb, g] = jnp.concatenate([top.astype(bf16), tail_ref[g]], axis=0)

    def emit_compress_inputs(z):
        kc_ref[...] = z[:, 0:LANES].astype(bf16)
        vc_ref[...] = z[:, LANES:two].astype(bf16)
        gate_ref[...] = 1.0 / (1.0 + jnp.exp(-z[:, two:two + LANES]))

    def emit_keys(z):
        t0 = (pl.program_id(0) * tm) % seq
        pc = _pos_cols(lax.broadcasted_iota(i32, (tm, LANES), 0) + t0, two)
        ks_ref[...] = (_head_norm(z[:, 0:two], e, gks_ref[...]) + pc).astype(bf16)
        kw_ref[...] = (_head_norm(z[:, two:], e, gkw_ref[...]) + pc).astype(bf16)

    def emit_values(z):
        ones = _ones_col(tm, two)
        vs_ref[...] = (z[:, 0:two] + ones).astype(bf16)
        vw_ref[...] = (z[:, two:] + ones).astype(bf16)

    z0 = proj(0)
    z1 = proj(1)
    emit_queries(z0)
    z2 = proj(2)
    emit_compress_inputs(z1)
    z3 = proj(3)
    emit_keys(z2)
    z4 = proj(4)
    emit_values(z3)
    z5 = proj(5)
    z6 = proj(6)
    u_ref[...] = (z4 * z5).astype(bf16)
    bg_ref[...] = z6.astype(bf16)


def _inproj(x2, gm, w, gq, gks, gkw, e, tail, *, seq, tm=512):
    n = x2.shape[0]
    width = HEADS_PER_GROUP * Q_BLOCK
    row = lambda w_: pl.BlockSpec((tm, w_), lambda i: (i, 0))
    full = lambda a: pl.BlockSpec(a.shape, lambda i: (0,) * a.ndim)
    outs = [(LANES, bf16), (LANES, bf16), (2 * LANES, bf16), (2 * LANES, bf16),
            (2 * LANES, bf16), (2 * LANES, bf16), (CONV_WIDTH, bf16), (CONV_WIDTH, bf16), (LANES, f32)]
    qt_spec = pl.BlockSpec((tm // Q_BLOCK, N_GROUPS, LANES, width), lambda i: (i, 0, 0, 0))
    qt_shape = jax.ShapeDtypeStruct((n // Q_BLOCK, N_GROUPS, LANES, width), bf16)
    return pl.pallas_call(
        functools.partial(_inproj_body, tm=tm, seq=seq),
        grid=(n // tm,),
        in_specs=[row(D_MODEL), full(gm), full(w), full(gq), full(gks), full(gkw), full(e), full(tail)],
        out_specs=[qt_spec] + [row(w_) for w_, _ in outs],
        out_shape=[qt_shape] + [jax.ShapeDtypeStruct((n, w_), dt) for w_, dt in outs],
        compiler_params=pltpu.CompilerParams(dimension_semantics=("arbitrary",),
                                             vmem_limit_bytes=VMEM_LIMIT),
        name="inproj",
    )(x2, gm, w, gq, gks, gkw, e, tail)


def _gelu_tanh(x):
    return 0.5 * x * (1.0 + jnp.tanh(0.7978845608028654 * (x + 0.044715 * (x * x * x))))


def _compress_body(zk_ref, zv_ref, w1_ref, pe_ref, b1_ref, w2k_ref, w2v_ref, b2k_ref, b2v_ref,
                   gk_ref, e_ref, kc_ref, vc_ref, *, tc):
    last = lax.broadcasted_iota(i32, (tc, 1), 0) == tc - 1

    def hidden(z_ref, kind):
        z = z_ref[0]
        a = _dot(z, w1_ref[kind, 0])
        b = _dot(z, w1_ref[kind, 1])
        b = jnp.concatenate([b[1:], jnp.zeros((1, b.shape[1]), f32)], axis=0)
        bias = (_dot(pe_ref[kind, 0], w1_ref[kind, 0]) + _dot(pe_ref[kind, 1], w1_ref[kind, 1]))[0:1]
        return _gelu_tanh(a + b + bias + b1_ref[kind]).astype(bf16)

    k = _dot(hidden(zk_ref, 0), w2k_ref[...]) + b2k_ref[...]
    k = _head_norm(k, e_ref[...], gk_ref[...])
    pos = lax.broadcasted_iota(i32, (tc, LANES), 0) * CMP_STRIDE + (CMP_BLOCK - 1)
    k = k + _pos_cols(pos, 2 * LANES)
    kc_ref[0] = jnp.where(last, 0.0, k).astype(bf16)
    v = _dot(hidden(zv_ref, 1), w2v_ref[...]) + b2v_ref[...]
    vc_ref[0] = jnp.where(last, 0.0, v + _ones_col(tc, 2 * LANES)).astype(bf16)


def _compress(zk, zv, w1, pe, b1, w2k, w2v, b2k, b2v, gk, e):
    b, tc, _ = zk.shape
    blk = lambda a: pl.BlockSpec((1,) + a.shape[1:], lambda i: (i,) + (0,) * (a.ndim - 1))
    full = lambda a: pl.BlockSpec(a.shape, lambda i: (0,) * a.ndim)
    return pl.pallas_call(
        functools.partial(_compress_body, tc=tc),
        grid=(b,),
        in_specs=[blk(zk), blk(zv)] + [full(a) for a in (w1, pe, b1, w2k, w2v, b2k, b2v, gk, e)],
        out_specs=[pl.BlockSpec((1, tc, 2 * LANES), lambda i: (i, 0, 0))] * 2,
        out_shape=[jax.ShapeDtypeStruct((b, tc, 2 * LANES), bf16)] * 2,
        compiler_params=pltpu.CompilerParams(dimension_semantics=("arbitrary",),
                                             vmem_limit_bytes=VMEM_LIMIT),
        name="compress",
    )(zk, zv, w1, pe, b1, w2k, w2v, b2k, b2v, gk, e)


def _query_pos(c):
    lane = lax.broadcasted_iota(i32, (1, HEADS_PER_GROUP * Q_BLOCK), 1)
    return c * Q_BLOCK + (lane & (Q_BLOCK - 1))


def _pick_top(vs, jf, n_pick):
    vs = list(vs)
    for _ in range(n_pick):
        for g, v in enumerate(vs):
            mx = jnp.max(v, axis=0, keepdims=True)
            idx = jnp.min(jnp.where(v == mx, jf, float(jf.shape[0])), axis=0, keepdims=True)
            vs[g] = jnp.where(jf == idx, PICKED, v)
    return tuple(vs)


def _cmp_variant(nchunk, c, qt_ref, kc_ref, vc_ref, mct_ref, tri_ref, o_ref, madd_ref, lst_ref, cnt_ref,
                 *, nsel):
    groups = range(N_GROUPS)
    nrow = nchunk * CMP_CHUNK
    npre = nchunk * SEL_PER_CHUNK
    tq = _query_pos(c)
    sub = lax.broadcasted_iota(i32, (CMP_CHUNK, 1), 0)
    lanes = lambda g: slice(g * LANES, (g + 1) * LANES)
    chunk = lambda a, k: a[k * CMP_CHUNK:(k + 1) * CMP_CHUNK]

    tiles = []
    for g in groups:
        s = _dot(kc_ref[0, 0:nrow, lanes(g)], qt_ref[0, g])
        row = []
        for k in range(nchunk):
            t = chunk(s, k)
            if k >= nchunk - 2:
                seen = (k * CMP_CHUNK + sub) * CMP_STRIDE + (CMP_BLOCK - 1) <= tq
                t = jnp.where(seen, t, NEG)
            row.append(t)
        tiles.append(row)

    accs, imps = [], []
    for g in groups:
        m = tiles[g][0].max(axis=0, keepdims=True)
        for t in tiles[g][1:]:
            m = jnp.maximum(m, t.max(axis=0, keepdims=True))
        parts = [jnp.exp2(t - m).astype(bf16) for t in tiles[g]]
        accs.append(_dot_tn(vc_ref[0, 0:nrow, lanes(g)], jnp.concatenate(parts, axis=0)))
        rows, carry = [], None
        for part in parts:
            piece = _dot(mct_ref[...], part)
            body = piece[0:SEL_PER_CHUNK]
            if carry is not None:
                body = jnp.concatenate([body[0:8] + carry, body[8:]], axis=0)
            rows.append(body)
            carry = piece[SEL_PER_CHUNK:IMP_ROWS]
        imps.append(rows[0] if nchunk == 1 else jnp.concatenate(rows, axis=0))

    any_key = tq >= CMP_BLOCK - 1
    j = lax.broadcasted_iota(i32, (npre, Q_BLOCK), 0)
    jf = j.astype(f32)
    t1 = tq[:, 0:Q_BLOCK]
    jt = t1 >> 6
    valid = j * SEL_BLOCK <= t1
    vs = []
    for g in groups:
        inv = jnp.where(any_key, 1.0 / jnp.maximum(accs[g][HEAD_DIM:HEAD_DIM + 1], 1e-30), 0.0)
        o_ref[0, g, 0] = accs[g][0:HEAD_DIM] * inv
        imp4 = imps[g] * inv
        imp = imp4[:, 0:Q_BLOCK]
        for r in range(1, HEADS_PER_GROUP):
            imp = imp + imp4[:, r * Q_BLOCK:(r + 1) * Q_BLOCK]
        v = jnp.where(valid, imp, NEG)
        vs.append(jnp.where(j == 0, PICKED, jnp.where(j == jt, PICKED, jnp.where(j == jt - 1, PICKED, v))))
    vs = _pick_top(vs, jf, SEL_TOPK - 3)

    rr = lax.broadcasted_iota(i32, (nsel, nsel), 0).astype(f32)
    ones8 = jnp.ones((8, Q_BLOCK), bf16)
    jrow = jnp.broadcast_to(lax.broadcasted_iota(i32, (1, nsel), 1).astype(f32), (8, nsel)).astype(bf16)
    for g in groups:
        picked = vs[g] == PICKED
        madd = jnp.where(valid, jnp.where(picked, 0.0, NEG), NEG)
        sel = jnp.where(valid, jnp.where(picked, 1.0, 0.0), 0.0).astype(bf16)
        if npre < nsel:
            madd = jnp.concatenate([madd, jnp.full((nsel - npre, Q_BLOCK), NEG, f32)], axis=0)
            sel = jnp.concatenate([sel, jnp.zeros((nsel - npre, Q_BLOCK), bf16)], axis=0)
        madd_ref[0, g, 0] = madd
        flag = jnp.where(_dot_nt(ones8, sel)[0:1] > 0.0, 1.0, 0.0)
        flag8 = jnp.broadcast_to(flag, (8, nsel)).astype(bf16)
        prefix = _dot(flag8, tri_ref[...])[0:1]
        place = jnp.where(prefix == rr, flag, 0.0).astype(bf16)
        lst_ref[0, g, 0] = _dot_nt(jrow, place).astype(i32)
        cnt_ref[0, g, 0] = _dot(flag8, jnp.ones((nsel, LANES), bf16)).astype(i32)


def _cmp_body(qt_ref, kc_ref, vc_ref, mct_ref, tri_ref, o_ref, madd_ref, lst_ref, cnt_ref, *, nsel, nvar):
    c = pl.program_id(1)
    nch = (c * (Q_BLOCK // CMP_STRIDE) + (Q_BLOCK - CMP_BLOCK) // CMP_STRIDE) // CMP_CHUNK + 1
    for n in range(1, nvar + 1):
        pl.when(nch == n)(functools.partial(
            _cmp_variant, n, c, qt_ref, kc_ref, vc_ref, mct_ref, tri_ref, o_ref, madd_ref, lst_ref, cnt_ref,
            nsel=nsel))


def _cmp_attention(qt, kc, vc, mct, tri, *, batch):
    g = N_GROUPS
    nqb = qt.shape[0] // batch
    tc = kc.shape[1]
    nsel = tc * CMP_STRIDE // SEL_BLOCK
    width = HEADS_PER_GROUP * Q_BLOCK
    per_q = lambda r_, c_: pl.BlockSpec((1, g, 1, r_, c_), lambda bi, ci: (bi, 0, ci, 0, 0))
    shape = lambda r_, c_, dt: jax.ShapeDtypeStruct((batch, g, nqb, r_, c_), dt)
    return pl.pallas_call(
        functools.partial(_cmp_body, nsel=nsel, nvar=tc // CMP_CHUNK),
        grid=(batch, nqb),
        in_specs=[pl.BlockSpec((1, g, LANES, width), lambda bi, ci: (bi * nqb + ci, 0, 0, 0)),
                  pl.BlockSpec((1, tc, 2 * LANES), lambda bi, ci: (bi, 0, 0)),
                  pl.BlockSpec((1, tc, 2 * LANES), lambda bi, ci: (bi, 0, 0)),
                  pl.BlockSpec(mct.shape, lambda bi, ci: (0, 0)),
                  pl.BlockSpec(tri.shape, lambda bi, ci: (0, 0))],
        out_specs=[per_q(HEAD_DIM, width), per_q(nsel, Q_BLOCK), per_q(8, nsel), per_q(8, LANES)],
        out_shape=[shape(HEAD_DIM, width, f32), shape(nsel, Q_BLOCK, f32),
                   shape(8, nsel, i32), shape(8, LANES, i32)],
        compiler_params=pltpu.CompilerParams(dimension_semantics=("arbitrary",) * 2,
                                             vmem_limit_bytes=VMEM_LIMIT),
        name="cmp_attention",
    )(qt, kc, vc, mct, tri)


def _normalize(o_aug):
    return o_aug[0:HEAD_DIM] * (1.0 / jnp.maximum(o_aug[HEAD_DIM:HEAD_DIM + 1], 1e-30))


def _slc_win_body(lst_ref, cnt_ref, qt_ref, ks_ref, vs_ref, kw_ref, vw_ref,
                  madd_ref, ocmp_ref, gate_ref, wtab_ref, out_ref, s_scr, *, nsel):
    width = HEADS_PER_GROUP * Q_BLOCK
    nwin = (WINDOW + Q_BLOCK) // LANES
    wq = WINDOW // Q_BLOCK
    grp_rows = GROUP_BLOCKS * SEL_BLOCK

    class Block:
        def __init__(self, qb):
            self.qb = qb
            self.c = pl.program_id(2) * QB_PER_STEP + qb
            self.q0 = pl.multiple_of(self.c * Q_BLOCK, Q_BLOCK)
            self.qta = qt_ref[qb, 0]
            self.n_off = cnt_ref[qb, 0, 0] - 2

        def block_ids(self, first, nblk):
            ids = []
            for u in range(nblk):
                i = first + u
                j = lst_ref[self.qb, 0, jnp.minimum(i, nsel - 1)]
                ids.append((i < self.n_off, j, pl.multiple_of(j * SEL_BLOCK, SEL_BLOCK)))
            return ids

        def scores(self, ids):
            kcat = jnp.concatenate([ks_ref[0, pl.ds(r0, SEL_BLOCK), :] for _, _, r0 in ids], axis=0)
            sg = _dot(kcat, self.qta)
            tiles = []
            for u, (live, j, _) in enumerate(ids):
                mrow = jnp.where(live, madd_ref[0, 0, self.qb, pl.ds(j, 1), :], NEG)
                mrow = jnp.concatenate([mrow] * HEADS_PER_GROUP, axis=1)
                tiles.append(sg[u * SEL_BLOCK:(u + 1) * SEL_BLOCK] + mrow)
            return tiles

        def values(self, ids):
            return jnp.concatenate([vs_ref[0, pl.ds(r0, SEL_BLOCK), :] for _, _, r0 in ids], axis=0)

    def front(blk):
        c, q0, qta = blk.c, blk.q0, blk.qta
        ws = pl.multiple_of(jnp.maximum(c - wq, 0) * Q_BLOCK, Q_BLOCK)
        s = _dot(kw_ref[0, pl.ds(ws, WINDOW + Q_BLOCK), :], qta)
        chunks = []
        for k in range(nwin):
            steady = 1 if k == 0 else (2 if k == nwin - 1 else 0)
            tab = jnp.where(c >= wq, steady, jnp.where(k < c, 0, jnp.where(k == c, 2, 3)))
            chunks.append(s[k * LANES:(k + 1) * LANES] + wtab_ref[tab])
        head = blk.block_ids(0, HEAD_BLOCKS)
        tiles = [_dot(ks_ref[0, pl.ds(q0, Q_BLOCK), :], qta) + wtab_ref[2]] + blk.scores(head)
        return ws, chunks, head, tiles

    def softmax_pv(tiles, v):
        m = tiles[0].max(axis=0, keepdims=True)
        for su in tiles[1:]:
            m = jnp.maximum(m, su.max(axis=0, keepdims=True))
        p = jnp.concatenate([jnp.exp2(su - m).astype(bf16) for su in tiles], axis=0)
        return m, _dot_tn(v, p)

    def middle(blk, ws, chunks, head, tiles):
        _, o_win = softmax_pv(chunks, vw_ref[0, pl.ds(ws, WINDOW + Q_BLOCK), :])
        m_run, o_run = softmax_pv(
            tiles, jnp.concatenate([vs_ref[0, pl.ds(blk.q0, Q_BLOCK), :], blk.values(head)], axis=0))
        return _normalize(o_win), m_run, o_run

    def rest(blk, m_run, o_run):
        def segment(si, carry):
            m_run, o_run = carry
            base = HEAD_BLOCKS + si * SUP_BLOCKS
            ngrp = (jnp.minimum(SUP_BLOCKS, blk.n_off - base) + GROUP_BLOCKS - 1) // GROUP_BLOCKS
            rows = lambda gi: pl.ds(pl.multiple_of(gi * grp_rows, grp_rows), grp_rows)

            def score(gi, mx):
                sg = jnp.concatenate(blk.scores(blk.block_ids(base + gi * GROUP_BLOCKS, GROUP_BLOCKS)), axis=0)
                s_scr[rows(gi), :] = sg
                return jnp.maximum(mx, sg.max(axis=0, keepdims=True))

            m_new = lax.fori_loop(0, ngrp, score, m_run)

            def weigh(gi, acc):
                pg = jnp.exp2(s_scr[rows(gi), :] - m_new).astype(bf16)
                return acc + _dot_tn(blk.values(blk.block_ids(base + gi * GROUP_BLOCKS, GROUP_BLOCKS)), pg)

            o_seg = lax.fori_loop(0, ngrp, weigh, jnp.zeros((LANES, width), f32))
            return m_new, jnp.exp2(m_run - m_new) * o_run + o_seg

        nseg = (jnp.maximum(blk.n_off - HEAD_BLOCKS, 0) + SUP_BLOCKS - 1) // SUP_BLOCKS
        return lax.fori_loop(0, nseg, segment, (m_run, o_run))[1]

    blocks = [Block(qb) for qb in range(QB_PER_STEP)]
    fronts = [front(blk) for blk in blocks]
    fronts = [middle(blk, *f) for blk, f in zip(blocks, fronts)]
    tails = [rest(blk, m_run, o_run) for blk, (_, m_run, o_run) in zip(blocks, fronts)]
    for blk, (o_win, _, _), o_run in zip(blocks, fronts, tails):
        qb = blk.qb
        gate = gate_ref[0, 0, qb]
        mix = gate[0:1] * ocmp_ref[0, 0, qb] + gate[1:2] * _normalize(o_run) + gate[2:3] * o_win
        rows = jnp.concatenate([mix[:, r * Q_BLOCK:(r + 1) * Q_BLOCK] for r in range(HEADS_PER_GROUP)],
                               axis=0)
        out_ref[0, qb * Q_BLOCK:(qb + 1) * Q_BLOCK, :] = rows.T.astype(bf16)


def _slc_win_attention(lst, cnt, qt, ks, vs, kw, vw, madd, ocmp, gate_t, wtab):
    b, t, _ = ks.shape
    g = N_GROUPS
    nqb, nsel = t // Q_BLOCK, t // SEL_BLOCK
    width = HEADS_PER_GROUP * Q_BLOCK
    nstep = nqb // QB_PER_STEP
    flat = lambda bi, gi, ci: (bi * g + gi) * nstep + ci
    smem = lambda w_: pl.BlockSpec((QB_PER_STEP, 1, w_), lambda bi, gi, ci: (flat(bi, gi, ci), 0, 0),
                                   memory_space=pltpu.SMEM)
    per_q = lambda r_, c_: pl.BlockSpec((1, 1, QB_PER_STEP, r_, c_), lambda bi, gi, ci: (bi, gi, ci, 0, 0))
    keys = pl.BlockSpec((1, t, LANES), lambda bi, gi, ci: (bi, 0, gi))
    return pl.pallas_call(
        functools.partial(_slc_win_body, nsel=nsel),
        grid=(b, g, nstep),
        in_specs=[smem(nsel), smem(LANES),
                  pl.BlockSpec((QB_PER_STEP, 1, LANES, width), lambda bi, gi, ci: (bi * nstep + ci, gi, 0, 0)),
                  keys, keys, keys, keys,
                  per_q(nsel, Q_BLOCK), per_q(HEAD_DIM, width), per_q(8, width),
                  pl.BlockSpec(wtab.shape, lambda bi, gi, ci: (0, 0, 0))],
        out_specs=pl.BlockSpec((1, QB_PER_STEP * Q_BLOCK, 2 * LANES), lambda bi, gi, ci: (bi, ci, gi)),
        out_shape=jax.ShapeDtypeStruct((b, t, ATTN_WIDTH), bf16),
        scratch_shapes=[pltpu.VMEM((SUP_BLOCKS * SEL_BLOCK, width), f32)],
        compiler_params=pltpu.CompilerParams(dimension_semantics=("arbitrary",) * 3,
                                             vmem_limit_bytes=VMEM_LIMIT),
        name="slc_win_attention",
    )(lst, cnt, qt, ks, vs, kw, vw, madd, ocmp, gate_t, wtab)


PREV_ROWS = 16


def _rms(v, g):
    return (v * lax.rsqrt(jnp.mean(v * v, axis=-1, keepdims=True) + EPS)) * g


def _mix_ffn_body(x_ref, attn_ref, u_ref, uprev_ref, bg_ref, cw_ref, go_ref, wo_ref, gf_ref, wu_ref, wd_ref,
                  out_ref, *, tm, seq, chunk):
    first = (pl.program_id(0) * tm) % seq == 0
    u = u_ref[...].astype(f32)
    prev = jnp.where(first, 0.0, uprev_ref[0].astype(f32))
    ext = jnp.concatenate([prev, u], axis=0)
    cw = cw_ref[...]
    conv = (cw[0:1] * ext[PREV_ROWS - 2:PREV_ROWS - 2 + tm] + cw[1:2] * ext[PREV_ROWS - 1:PREV_ROWS - 1 + tm]
            + cw[2:3] * u)
    conv = bg_ref[...].astype(f32) * conv
    go = go_ref[...]
    mixed = jnp.concatenate([_rms(attn_ref[...].astype(f32), go[:, :ATTN_WIDTH]),
                             _rms(conv, go[:, ATTN_WIDTH:])], axis=1).astype(bf16)
    x = x_ref[...] + _dot(mixed, wo_ref[...])
    h = _rms(x, gf_ref[...]).astype(bf16)
    acc = x
    for c in range(D_FF // chunk):
        a = jnp.maximum(_dot(h, wu_ref[:, c * chunk:(c + 1) * chunk]), 0.0)
        acc = acc + _dot((a * a).astype(bf16), wd_ref[c * chunk:(c + 1) * chunk, :])
    out_ref[...] = acc


def _mix_ffn(x2, attn2, u, bgate, cw, go, wo, gf, wu, wd, *, seq, tm=512, chunk=1024):
    n = x2.shape[0]
    uprev = u.reshape(n // PREV_ROWS, PREV_ROWS, CONV_WIDTH)
    row = lambda w_: pl.BlockSpec((tm, w_), lambda i: (i, 0))
    full = lambda a: pl.BlockSpec(a.shape, lambda i: (0,) * a.ndim)
    once = lambda a: pl.BlockSpec(a.shape, lambda i: (0,) * a.ndim, pipeline_mode=pl.Buffered(1))
    return pl.pallas_call(
        functools.partial(_mix_ffn_body, tm=tm, seq=seq, chunk=chunk),
        grid=(n // tm,),
        in_specs=[row(D_MODEL), row(ATTN_WIDTH), row(CONV_WIDTH),
                  pl.BlockSpec((1, PREV_ROWS, CONV_WIDTH),
                               lambda i: (jnp.maximum(i * (tm // PREV_ROWS) - 1, 0), 0, 0)),
                  row(CONV_WIDTH), full(cw), full(go), once(wo), full(gf), once(wu), once(wd)],
        out_specs=row(D_MODEL),
        out_shape=jax.ShapeDtypeStruct((n, D_MODEL), f32),
        input_output_aliases={0: 0},
        compiler_params=pltpu.CompilerParams(dimension_semantics=("arbitrary",),
                                             vmem_limit_bytes=VMEM_LIMIT),
        name="mix_ffn",
    )(x2, attn2, u, uprev, bgate, cw, go, wo, gf, wu, wd)


def _interleave_zero(w):
    z = jnp.zeros(w.shape[:-1] + (HEAD_DIM,), w.dtype)
    return jnp.concatenate([w[..., :HEAD_DIM], z, w[..., HEAD_DIM:], z], axis=-1)


def _prep_params(g_mix_norm, w_in, g_q, g_k, pe_cmp, w_cmp1, b_cmp1, w_cmp2, b_cmp2,
                 conv_w, g_out, w_o, g_ffn_norm, w_up, w_down):
    depth = w_in.shape[0]
    o = np.cumsum([0, ATTN_WIDTH] + [LANES] * 6 + [N_BRANCH * N_HEADS] + [CONV_WIDTH] * 3)
    part = lambda i: w_in[..., int(o[i]):int(o[i + 1])]
    q, kc, vc, ks, vs, kw, vw, gl, hc, cg, bg = [part(i) for i in range(11)]
    gl = jnp.pad(gl, ((0, 0), (0, 0), (0, LANES - gl.shape[-1])))
    w = jnp.concatenate([q, kc, vc, gl, jnp.zeros_like(gl), _interleave_zero(ks), _interleave_zero(kw),
                         _interleave_zero(vs), _interleave_zero(vw), hc, cg, bg], axis=-1).astype(bf16)
    tile2 = lambda gk: _interleave_zero(jnp.concatenate([gk, gk], axis=-1))[:, None, :]
    w1 = w_cmp1.astype(bf16).reshape(depth, 2, 2, CMP_STRIDE, HEAD_DIM, CMP_HIDDEN)
    z1 = jnp.zeros_like(w1)
    w1g = jnp.stack([jnp.concatenate([w1, z1], axis=-1), jnp.concatenate([z1, w1], axis=-1)], axis=4)
    w1g = w1g.reshape(depth, 2, 2, CMP_STRIDE * LANES, N_GROUPS * CMP_HIDDEN)
    pe = pe_cmp.reshape(depth, 2, 2, CMP_STRIDE, 1, HEAD_DIM)
    pe = jnp.broadcast_to(pe, (depth, 2, 2, CMP_STRIDE, N_GROUPS, HEAD_DIM)).reshape(depth, 2, 2, 1, -1)
    pe = jnp.pad(pe, ((0, 0), (0, 0), (0, 0), (0, 7), (0, 0))).astype(bf16)
    b1 = jnp.concatenate([b_cmp1, b_cmp1], axis=-1)[:, :, None, :]
    z2 = jnp.zeros_like(w_cmp2)
    w2 = jnp.concatenate([jnp.concatenate([w_cmp2, z2], axis=-1), jnp.concatenate([z2, w_cmp2], axis=-1)],
                         axis=2)
    w2 = _interleave_zero(w2).astype(bf16)
    b2 = _interleave_zero(jnp.concatenate([b_cmp2, b_cmp2], axis=-1))[:, :, None, :]
    return dict(
        gm=g_mix_norm[:, None, :], w=w,
        gq=jnp.tile(g_q, (1, N_HEADS))[:, None, :],
        gks=tile2(g_k[:, 1]), gkw=tile2(g_k[:, 2]), gkc=tile2(g_k[:, 0]),
        w1=w1g, pe=pe, b1=b1,
        w2k=w2[:, 0], w2v=w2[:, 1], b2k=b2[:, 0], b2v=b2[:, 1],
        cw=jnp.pad(conv_w, ((0, 0), (0, 8 - conv_w.shape[1]), (0, 0))),
        go=g_out[:, None, :], wo=w_o.astype(bf16),
        gf=g_ffn_norm[:, None, :], wu=w_up.astype(bf16), wd=w_down.astype(bf16),
    )


def _constants(nsel):
    lane = np.arange(LANES)
    e = (lane[:, None] // HEAD_DIM == lane[None, :] // HEAD_DIM).astype(np.float32)
    tail = np.zeros((N_GROUPS, HEAD_DIM, HEADS_PER_GROUP * Q_BLOCK), np.float32)
    for gi in range(N_GROUPS):
        for r in range(HEADS_PER_GROUP):
            rest = np.float64(2.0 ** -(gi * HEADS_PER_GROUP + r + 1)) * LOG2E
            for k in range(SLOPE_TERMS):
                term = np.float64(np.asarray(rest, np.float32).astype(jnp.bfloat16).astype(np.float32))
                tail[gi, k, r * Q_BLOCK:(r + 1) * Q_BLOCK] = term * LANES
                tail[gi, SLOPE_TERMS + k, r * Q_BLOCK:(r + 1) * Q_BLOCK] = term
                rest = rest - term
    mct = np.zeros((IMP_ROWS, CMP_CHUNK), np.float32)
    for i in range(CMP_CHUNK):
        lo, hi = i * CMP_STRIDE, i * CMP_STRIDE + CMP_BLOCK
        for jj in range(SEL_PER_CHUNK + 1):
            ov = min(hi, (jj + 1) * SEL_BLOCK) - max(lo, jj * SEL_BLOCK)
            if ov > 0:
                mct[jj, i] = ov / CMP_BLOCK
    kk = np.arange(LANES)[:, None]
    ql = np.tile(np.arange(Q_BLOCK), HEADS_PER_GROUP)[None, :]
    wtab = np.zeros((4, LANES, HEADS_PER_GROUP * Q_BLOCK), np.float32)
    wtab[1] = np.where(kk > ql, 0.0, NEG)
    wtab[2] = np.where(kk <= ql, 0.0, NEG)
    wtab[3] = NEG
    tri = np.arange(nsel)[:, None] < np.arange(nsel)[None, :]
    return (jnp.asarray(e, bf16), jnp.asarray(tail, bf16), jnp.asarray(mct, bf16), jnp.asarray(wtab),
            jnp.asarray(tri, bf16))


def _layer(x2, p, consts, *, batch, seq):
    e, tail, mct, wtab, tri = consts
    g = N_GROUPS
    nqb, nsel, tc = seq // Q_BLOCK, seq // SEL_BLOCK, seq // CMP_STRIDE
    qt, kc, vc, ks, kw, vs, vw, u, bgate, gates = _inproj(
        x2, p["gm"], p["w"], p["gq"], p["gks"], p["gkw"], e, tail, seq=seq)
    kcmp, vcmp = _compress(kc.reshape(batch, tc, CMP_STRIDE * LANES), vc.reshape(batch, tc, CMP_STRIDE * LANES),
                           p["w1"], p["pe"], p["b1"], p["w2k"], p["w2v"], p["b2k"], p["b2v"], p["gkc"], e)
    gate_t = gates[:, :N_HEADS * N_BRANCH].reshape(batch, nqb, Q_BLOCK, g, HEADS_PER_GROUP, N_BRANCH)
    gate_t = gate_t.transpose(0, 3, 1, 5, 4, 2).reshape(batch, g, nqb, N_BRANCH, HEADS_PER_GROUP * Q_BLOCK)
    gate_t = jnp.pad(gate_t, ((0, 0), (0, 0), (0, 0), (0, 8 - N_BRANCH), (0, 0)))
    ocmp, madd, lst, cnt = _cmp_attention(qt, kcmp, vcmp, mct, tri, batch=batch)
    rows3 = lambda a: a.reshape(batch, seq, 2 * LANES)
    smem = lambda a: a[:, :, :, 0, :].reshape(batch * g * nqb, 1, a.shape[-1])
    attn = _slc_win_attention(smem(lst), smem(cnt), qt, rows3(ks), rows3(vs),
                              rows3(kw), rows3(vw), madd, ocmp, gate_t, wtab)
    return _mix_ffn(x2, attn.reshape(batch * seq, ATTN_WIDTH), u, bgate, p["cw"], p["go"], p["wo"],
                    p["gf"], p["wu"], p["wd"], seq=seq)


def kernel(x, g_mix_norm, w_in, g_q, g_k, pe_cmp, w_cmp1, b_cmp1, w_cmp2, b_cmp2, conv_w, g_out, w_o,
           g_ffn_norm, w_up, w_down):
    batch, seq, d = x.shape
    assert d == D_MODEL and seq % (CMP_CHUNK * CMP_STRIDE) == 0 and seq >= WINDOW + Q_BLOCK
    params = _prep_params(g_mix_norm, w_in, g_q, g_k, pe_cmp, w_cmp1, b_cmp1, w_cmp2, b_cmp2,
                          conv_w, g_out, w_o, g_ffn_norm, w_up, w_down)
    consts = _constants(seq // SEL_BLOCK)

    def step(x2, p):
        return _layer(x2, p, consts, batch=batch, seq=seq), None

    x2, _ = lax.scan(step, x.reshape(batch * seq, d), params)
    return x2.reshape(batch, seq, d)
```

```python
import functools

import numpy as np
import jax
import jax.numpy as jnp
from jax import lax
from jax.experimental import pallas as pl
from jax.experimental.pallas import tpu as pltpu

f32 = jnp.float32
bf16 = jnp.bfloat16
i32 = jnp.int32

D_MODEL = 1024
HEAD_DIM = 64
N_HEADS = 8
N_GROUPS = 2
HEADS_PER_GROUP = 4
ATTN_WIDTH = 512
CONV_WIDTH = 512
N_BRANCH = 3
CMP_BLOCK = 32
CMP_STRIDE = 16
CMP_HIDDEN = 256
SEL_BLOCK = 64
SEL_TOPK = 16
WINDOW = 512
Q_BLOCK = 128
D_FF = 4096
EPS = 1e-6
NEG = -1e30
LOG2E = 1.4426950408889634
SLOPE_TERMS = 3
PICKED = -3e38
LANES = 128
CMP_CHUNK = 256
SEL_PER_CHUNK = CMP_CHUNK * CMP_STRIDE // SEL_BLOCK
IMP_ROWS = SEL_PER_CHUNK + 8
VMEM_LIMIT = 56 * 1024 * 1024

PROJ_CHUNK = 512
SUP_BLOCKS = 32
HEAD_BLOCKS = 16
GROUP_BLOCKS = 4
QB_PER_STEP = 4
CMP_QB = 2


def _dot(a, b):
    return jnp.dot(a, b, preferred_element_type=f32)


def _dot_nt(a, b):
    return lax.dot_general(a, b, (((1,), (1,)), ((), ())), preferred_element_type=f32)


def _dot_tn(a, b):
    return lax.dot_general(a, b, (((0,), (0,)), ((), ())), preferred_element_type=f32)


def _head_norm(z, e, g):
    sq = (z * z).astype(bf16)
    outs = []
    for c in range(z.shape[1] // LANES):
        sl = slice(c * LANES, (c + 1) * LANES)
        outs.append(z[:, sl] * lax.rsqrt(_dot(sq[:, sl], e) * (1.0 / HEAD_DIM) + EPS))
    y = outs[0] if len(outs) == 1 else jnp.concatenate(outs, axis=1)
    return y * g


def _pos_cols(pos, width):
    rows = pos.shape[0]
    lane = lax.broadcasted_iota(i32, (rows, LANES), 1) - HEAD_DIM
    tile = jnp.where(lane < 0, 0.0, jnp.where(lane < SLOPE_TERMS, (pos >> 7).astype(f32),
                                              jnp.where(lane < 2 * SLOPE_TERMS, (pos & 127).astype(f32), 0.0)))
    return tile if width == LANES else jnp.concatenate([tile] * (width // LANES), axis=1)


def _ones_col(rows, width):
    lane = lax.broadcasted_iota(i32, (rows, LANES), 1)
    tile = jnp.where(lane == HEAD_DIM, 1.0, 0.0)
    return tile if width == LANES else jnp.concatenate([tile] * (width // LANES), axis=1)


def _inproj_body(x_ref, gm_ref, w_ref, gq_ref, gks_ref, gkw_ref, e_ref, tail_ref,
                 qt_ref, kc_ref, vc_ref, ks_ref, kw_ref, vs_ref, vw_ref, u_ref, bg_ref, gate_ref,
                 *, tm, seq):
    x = x_ref[...]
    ms = jnp.mean(x * x, axis=-1, keepdims=True)
    h = ((x * lax.rsqrt(ms + EPS)) * gm_ref[...]).astype(bf16)
    e = e_ref[...]
    two = 2 * LANES

    def proj(k):
        return _dot(h, w_ref[:, k * PROJ_CHUNK:(k + 1) * PROJ_CHUNK])

    def emit_queries(z):
        qn = _head_norm(z, e, gq_ref[...]) * (HEAD_DIM ** -0.5 * LOG2E)
        gw = HEADS_PER_GROUP * HEAD_DIM
        for cb in range(tm // Q_BLOCK):
            for g in range(N_GROUPS):
                blk = qn[cb * Q_BLOCK:(cb + 1) * Q_BLOCK, g * gw:(g + 1) * gw].T
                top = jnp.concatenate([blk[r * HEAD_DIM:(r + 1) * HEAD_DIM] for r in range(HEADS_PER_GROUP)],
                                      axis=1)
                qt_ref[cb, g] = jnp.concatenate([top.astype(bf16), tail_ref[g]], axis=0)

    def emit_compress_inputs(z):
        kc_ref[...] = z[:, 0:LANES].astype(bf16)
        vc_ref[...] = z[:, LANES:two].astype(bf16)
        gate_ref[...] = 1.0 / (1.0 + jnp.exp(-z[:, two:two + LANES]))

    def emit_keys(z):
        t0 = (pl.program_id(0) * tm) % seq
        pc = _pos_cols(lax.broadcasted_iota(i32, (tm, LANES), 0) + t0, two)
        ks_ref[...] = (_head_norm(z[:, 0:two], e, gks_ref[...]) + pc).astype(bf16)
        kw_ref[...] = (_head_norm(z[:, two:], e, gkw_ref[...]) + pc).astype(bf16)

    def emit_values(z):
        ones = _ones_col(tm, two)
        vs_ref[...] = (z[:, 0:two] + ones).astype(bf16)
        vw_ref[...] = (z[:, two:] + ones).astype(bf16)

    z0 = proj(0)
    z1 = proj(1)
    emit_queries(z0)
    z2 = proj(2)
    emit_compress_inputs(z1)
    z3 = proj(3)
    emit_keys(z2)
    z4 = proj(4)
    emit_values(z3)
    z5 = proj(5)
    z6 = proj(6)
    u_ref[...] = (z4 * z5).astype(bf16)
    bg_ref[...] = z6.astype(bf16)


def _inproj(x2, gm, w, gq, gks, gkw, e, tail, *, seq, tm=512):
    n = x2.shape[0]
    width = HEADS_PER_GROUP * Q_BLOCK
    row = lambda w_: pl.BlockSpec((tm, w_), lambda i: (i, 0))
    full = lambda a: pl.BlockSpec(a.shape, lambda i: (0,) * a.ndim)
    outs = [(LANES, bf16), (LANES, bf16), (2 * LANES, bf16), (2 * LANES, bf16),
            (2 * LANES, bf16), (2 * LANES, bf16), (CONV_WIDTH, bf16), (CONV_WIDTH, bf16), (LANES, f32)]
    qt_spec = pl.BlockSpec((tm // Q_BLOCK, N_GROUPS, LANES, width), lambda i: (i, 0, 0, 0))
    qt_shape = jax.ShapeDtypeStruct((n // Q_BLOCK, N_GROUPS, LANES, width), bf16)
    return pl.pallas_call(
        functools.partial(_inproj_body, tm=tm, seq=seq),
        grid=(n // tm,),
        in_specs=[row(D_MODEL), full(gm), full(w), full(gq), full(gks), full(gkw), full(e), full(tail)],
        out_specs=[qt_spec] + [row(w_) for w_, _ in outs],
        out_shape=[qt_shape] + [jax.ShapeDtypeStruct((n, w_), dt) for w_, dt in outs],
        compiler_params=pltpu.CompilerParams(dimension_semantics=("arbitrary",),
                                             vmem_limit_bytes=VMEM_LIMIT),
        name="inproj",
    )(x2, gm, w, gq, gks, gkw, e, tail)


def _gelu_tanh(x):
    return 0.5 * x * (1.0 + jnp.tanh(0.7978845608028654 * (x + 0.044715 * (x * x * x))))


def _compress_body(zk_ref, zv_ref, w1_ref, pe_ref, b1_ref, w2k_ref, w2v_ref, b2k_ref, b2v_ref,
                   gk_ref, e_ref, kc_ref, vc_ref, *, tc):
    last = lax.broadcasted_iota(i32, (tc, 1), 0) == tc - 1

    def hidden(z_ref, kind):
        z = z_ref[0]
        a = _dot(z, w1_ref[kind, 0])
        b = _dot(z, w1_ref[kind, 1])
        b = jnp.concatenate([b[1:], jnp.zeros((1, b.shape[1]), f32)], axis=0)
        bias = (_dot(pe_ref[kind, 0], w1_ref[kind, 0]) + _dot(pe_ref[kind, 1], w1_ref[kind, 1]))[0:1]
        return _gelu_tanh(a + b + bias + b1_ref[kind]).astype(bf16)

    k = _dot(hidden(zk_ref, 0), w2k_ref[...]) + b2k_ref[...]
    k = _head_norm(k, e_ref[...], gk_ref[...])
    pos = lax.broadcasted_iota(i32, (tc, LANES), 0) * CMP_STRIDE + (CMP_BLOCK - 1)
    k = k + _pos_cols(pos, 2 * LANES)
    kc_ref[0] = jnp.where(last, 0.0, k).astype(bf16)
    v = _dot(hidden(zv_ref, 1), w2v_ref[...]) + b2v_ref[...]
    vc_ref[0] = jnp.where(last, 0.0, v + _ones_col(tc, 2 * LANES)).astype(bf16)


def _compress(zk, zv, w1, pe, b1, w2k, w2v, b2k, b2v, gk, e):
    b, tc, _ = zk.shape
    blk = lambda a: pl.BlockSpec((1,) + a.shape[1:], lambda i: (i,) + (0,) * (a.ndim - 1))
    full = lambda a: pl.BlockSpec(a.shape, lambda i: (0,) * a.ndim)
    return pl.pallas_call(
        functools.partial(_compress_body, tc=tc),
        grid=(b,),
        in_specs=[blk(zk), blk(zv)] + [full(a) for a in (w1, pe, b1, w2k, w2v, b2k, b2v, gk, e)],
        out_specs=[pl.BlockSpec((1, tc, 2 * LANES), lambda i: (i, 0, 0))] * 2,
        out_shape=[jax.ShapeDtypeStruct((b, tc, 2 * LANES), bf16)] * 2,
        compiler_params=pltpu.CompilerParams(dimension_semantics=("arbitrary",),
                                             vmem_limit_bytes=VMEM_LIMIT),
        name="compress",
    )(zk, zv, w1, pe, b1, w2k, w2v, b2k, b2v, gk, e)


def _query_pos(c):
    lane = lax.broadcasted_iota(i32, (1, HEADS_PER_GROUP * Q_BLOCK), 1)
    return c * Q_BLOCK + (lane & (Q_BLOCK - 1))


def _pick_top(vs, jf, n_pick):
    vs = list(vs)
    for _ in range(n_pick):
        for g, v in enumerate(vs):
            mx = jnp.max(v, axis=0, keepdims=True)
            idx = jnp.min(jnp.where(v == mx, jf, float(jf.shape[0])), axis=0, keepdims=True)
            vs[g] = jnp.where(jf == idx, PICKED, v)
    return tuple(vs)


def _cmp_variant(nchunk, c0, qt_ref, kc_ref, vc_ref, mct_ref, tri_ref, o_ref, madd_ref, lst_ref, cnt_ref,
                 *, nsel):
    units = [(qb, g) for qb in range(CMP_QB) for g in range(N_GROUPS)]
    nrow = nchunk * CMP_CHUNK
    npre = nchunk * SEL_PER_CHUNK
    tqs = [_query_pos(c0 + qb) for qb in range(CMP_QB)]
    sub = lax.broadcasted_iota(i32, (CMP_CHUNK, 1), 0)
    lanes = lambda g: slice(g * LANES, (g + 1) * LANES)
    chunk = lambda a, k: a[k * CMP_CHUNK:(k + 1) * CMP_CHUNK]

    tiles = []
    for qb, g in units:
        s = _dot(kc_ref[0, 0:nrow, lanes(g)], qt_ref[qb, g])
        row = []
        for k in range(nchunk):
            t = chunk(s, k)
            if k >= nchunk - 2:
                seen = (k * CMP_CHUNK + sub) * CMP_STRIDE + (CMP_BLOCK - 1) <= tqs[qb]
                t = jnp.where(seen, t, NEG)
            row.append(t)
        tiles.append(row)

    accs, imps = [], []
    for u, (qb, g) in enumerate(units):
        m = tiles[u][0].max(axis=0, keepdims=True)
        for t in tiles[u][1:]:
            m = jnp.maximum(m, t.max(axis=0, keepdims=True))
        parts = [jnp.exp2(t - m).astype(bf16) for t in tiles[u]]
        accs.append(_dot_tn(vc_ref[0, 0:nrow, lanes(g)], jnp.concatenate(parts, axis=0)))
        rows, carry = [], None
        for part in parts:
            piece = _dot(mct_ref[...], part)
            body = piece[0:SEL_PER_CHUNK]
            if carry is not None:
                body = jnp.concatenate([body[0:8] + carry, body[8:]], axis=0)
            rows.append(body)
            carry = piece[SEL_PER_CHUNK:IMP_ROWS]
        imps.append(rows[0] if nchunk == 1 else jnp.concatenate(rows, axis=0))

    j = lax.broadcasted_iota(i32, (npre, Q_BLOCK), 0)
    jf = j.astype(f32)
    t1s = [tq[:, 0:Q_BLOCK] for tq in tqs]
    valids = [j * SEL_BLOCK <= t1 for t1 in t1s]
    vs = []
    for u, (qb, g) in enumerate(units):
        any_key = tqs[qb] >= CMP_BLOCK - 1
        inv = jnp.where(any_key, 1.0 / jnp.maximum(accs[u][HEAD_DIM:HEAD_DIM + 1], 1e-30), 0.0)
        o_ref[0, g, qb] = accs[u][0:HEAD_DIM] * inv
        imp4 = imps[u] * inv
        imp = imp4[:, 0:Q_BLOCK]
        for r in range(1, HEADS_PER_GROUP):
            imp = imp + imp4[:, r * Q_BLOCK:(r + 1) * Q_BLOCK]
        jt = t1s[qb] >> 6
        v = jnp.where(valids[qb], imp, NEG)
        vs.append(jnp.where(j == 0, PICKED, jnp.where(j == jt, PICKED, jnp.where(j == jt - 1, PICKED, v))))
    vs = _pick_top(vs, jf, SEL_TOPK - 3)

    rr = lax.broadcasted_iota(i32, (nsel, nsel), 0).astype(f32)
    ones8 = jnp.ones((8, Q_BLOCK), bf16)
    jrow = jnp.broadcast_to(lax.broadcasted_iota(i32, (1, nsel), 1).astype(f32), (8, nsel)).astype(bf16)
    for u, (qb, g) in enumerate(units):
        picked = vs[u] == PICKED
        madd = jnp.where(valids[qb], jnp.where(picked, 0.0, NEG), NEG)
        sel = jnp.where(valids[qb], jnp.where(picked, 1.0, 0.0), 0.0).astype(bf16)
        if npre < nsel:
            madd = jnp.concatenate([madd, jnp.full((nsel - npre, Q_BLOCK), NEG, f32)], axis=0)
            sel = jnp.concatenate([sel, jnp.zeros((nsel - npre, Q_BLOCK), bf16)], axis=0)
        madd_ref[0, g, qb] = madd
        flag = jnp.where(_dot_nt(ones8, sel)[0:1] > 0.0, 1.0, 0.0)
        flag8 = jnp.broadcast_to(flag, (8, nsel)).astype(bf16)
        prefix = _dot(flag8, tri_ref[...])[0:1]
        place = jnp.where(prefix == rr, flag, 0.0).astype(bf16)
        lst_ref[0, g, qb] = _dot_nt(jrow, place).astype(i32)
        cnt_ref[0, g, qb] = _dot(flag8, jnp.ones((nsel, LANES), bf16)).astype(i32)


def _cmp_body(qt_ref, kc_ref, vc_ref, mct_ref, tri_ref, o_ref, madd_ref, lst_ref, cnt_ref, *, nsel, nvar):
    c0 = pl.program_id(1) * CMP_QB
    c_last = c0 + CMP_QB - 1
    nch = (c_last * (Q_BLOCK // CMP_STRIDE) + (Q_BLOCK - CMP_BLOCK) // CMP_STRIDE) // CMP_CHUNK + 1
    for n in range(1, nvar + 1):
        pl.when(nch == n)(functools.partial(
            _cmp_variant, n, c0, qt_ref, kc_ref, vc_ref, mct_ref, tri_ref, o_ref, madd_ref, lst_ref, cnt_ref,
            nsel=nsel))


def _cmp_attention(qt, kc, vc, mct, tri, *, batch):
    g = N_GROUPS
    nqb = qt.shape[0] // batch
    tc = kc.shape[1]
    nsel = tc * CMP_STRIDE // SEL_BLOCK
    width = HEADS_PER_GROUP * Q_BLOCK
    nstep = nqb // CMP_QB
    per_q = lambda r_, c_: pl.BlockSpec((1, g, CMP_QB, r_, c_), lambda bi, ci: (bi, 0, ci, 0, 0))
    shape = lambda r_, c_, dt: jax.ShapeDtypeStruct((batch, g, nqb, r_, c_), dt)
    return pl.pallas_call(
        functools.partial(_cmp_body, nsel=nsel, nvar=tc // CMP_CHUNK),
        grid=(batch, nstep),
        in_specs=[pl.BlockSpec((CMP_QB, g, LANES, width), lambda bi, ci: (bi * nstep + ci, 0, 0, 0)),
                  pl.BlockSpec((1, tc, 2 * LANES), lambda bi, ci: (bi, 0, 0)),
                  pl.BlockSpec((1, tc, 2 * LANES), lambda bi, ci: (bi, 0, 0)),
                  pl.BlockSpec(mct.shape, lambda bi, ci: (0, 0)),
                  pl.BlockSpec(tri.shape, lambda bi, ci: (0, 0))],
        out_specs=[per_q(HEAD_DIM, width), per_q(nsel, Q_BLOCK), per_q(8, nsel), per_q(8, LANES)],
        out_shape=[shape(HEAD_DIM, width, f32), shape(nsel, Q_BLOCK, f32),
                   shape(8, nsel, i32), shape(8, LANES, i32)],
        compiler_params=pltpu.CompilerParams(dimension_semantics=("arbitrary",) * 2,
                                             vmem_limit_bytes=VMEM_LIMIT),
        name="cmp_attention",
    )(qt, kc, vc, mct, tri)


def _normalize(o_aug):
    return o_aug[0:HEAD_DIM] * (1.0 / jnp.maximum(o_aug[HEAD_DIM:HEAD_DIM + 1], 1e-30))


def _slc_win_body(lst_ref, cnt_ref, qt_ref, ks_ref, vs_ref, kw_ref, vw_ref,
                  madd_ref, ocmp_ref, gate_ref, wtab_ref, out_ref, s_scr, *, nsel):
    width = HEADS_PER_GROUP * Q_BLOCK
    nwin = (WINDOW + Q_BLOCK) // LANES
    wq = WINDOW // Q_BLOCK
    grp_rows = GROUP_BLOCKS * SEL_BLOCK

    class Block:
        def __init__(self, qb):
            self.qb = qb
            self.c = pl.program_id(2) * QB_PER_STEP + qb
            self.q0 = pl.multiple_of(self.c * Q_BLOCK, Q_BLOCK)
            self.qta = qt_ref[qb, 0]
            self.n_off = cnt_ref[qb, 0, 0] - 2

        def block_ids(self, first, nblk):
            ids = []
            for u in range(nblk):
                i = first + u
                j = lst_ref[self.qb, 0, jnp.minimum(i, nsel - 1)]
                ids.append((i < self.n_off, j, pl.multiple_of(j * SEL_BLOCK, SEL_BLOCK)))
            return ids

        def scores(self, ids):
            kcat = jnp.concatenate([ks_ref[0, pl.ds(r0, SEL_BLOCK), :] for _, _, r0 in ids], axis=0)
            sg = _dot(kcat, self.qta)
            tiles = []
            for u, (live, j, _) in enumerate(ids):
                mrow = jnp.where(live, madd_ref[0, 0, self.qb, pl.ds(j, 1), :], NEG)
                mrow = jnp.concatenate([mrow] * HEADS_PER_GROUP, axis=1)
                tiles.append(sg[u * SEL_BLOCK:(u + 1) * SEL_BLOCK] + mrow)
            return tiles

        def values(self, ids):
            return jnp.concatenate([vs_ref[0, pl.ds(r0, SEL_BLOCK), :] for _, _, r0 in ids], axis=0)

    def front(blk):
        c, q0, qta = blk.c, blk.q0, blk.qta
        ws = pl.multiple_of(jnp.maximum(c - wq, 0) * Q_BLOCK, Q_BLOCK)
        s = _dot(kw_ref[0, pl.ds(ws, WINDOW + Q_BLOCK), :], qta)
        chunks = []
        for k in range(nwin):
            steady = 1 if k == 0 else (2 if k == nwin - 1 else 0)
            tab = jnp.where(c >= wq, steady, jnp.where(k < c, 0, jnp.where(k == c, 2, 3)))
            chunks.append(s[k * LANES:(k + 1) * LANES] + wtab_ref[tab])
        head = blk.block_ids(0, HEAD_BLOCKS)
        tiles = [_dot(ks_ref[0, pl.ds(q0, Q_BLOCK), :], qta) + wtab_ref[2]] + blk.scores(head)
        return ws, chunks, head, tiles

    def softmax_pv(tiles, v):
        m = tiles[0].max(axis=0, keepdims=True)
        for su in tiles[1:]:
            m = jnp.maximum(m, su.max(axis=0, keepdims=True))
        p = jnp.concatenate([jnp.exp2(su - m).astype(bf16) for su in tiles], axis=0)
        return m, _dot_tn(v, p)

    def middle(blk, ws, chunks, head, tiles):
        _, o_win = softmax_pv(chunks, vw_ref[0, pl.ds(ws, WINDOW + Q_BLOCK), :])
        m_run, o_run = softmax_pv(
            tiles, jnp.concatenate([vs_ref[0, pl.ds(blk.q0, Q_BLOCK), :], blk.values(head)], axis=0))
        return _normalize(o_win), m_run, o_run

    def rest(blk, m_run, o_run):
        def segment(si, carry):
            m_run, o_run = carry
            base = HEAD_BLOCKS + si * SUP_BLOCKS
            ngrp = (jnp.minimum(SUP_BLOCKS, blk.n_off - base) + GROUP_BLOCKS - 1) // GROUP_BLOCKS
            rows = lambda gi: pl.ds(pl.multiple_of(gi * grp_rows, grp_rows), grp_rows)

            def score(gi, mx):
                sg = jnp.concatenate(blk.scores(blk.block_ids(base + gi * GROUP_BLOCKS, GROUP_BLOCKS)), axis=0)
                s_scr[rows(gi), :] = sg
                return jnp.maximum(mx, sg.max(axis=0, keepdims=True))

            m_new = lax.fori_loop(0, ngrp, score, m_run)

            def weigh(gi, acc):
                pg = jnp.exp2(s_scr[rows(gi), :] - m_new).astype(bf16)
                return acc + _dot_tn(blk.values(blk.block_ids(base + gi * GROUP_BLOCKS, GROUP_BLOCKS)), pg)

            o_seg = lax.fori_loop(0, ngrp, weigh, jnp.zeros((LANES, width), f32))
            return m_new, jnp.exp2(m_run - m_new) * o_run + o_seg

        nseg = (jnp.maximum(blk.n_off - HEAD_BLOCKS, 0) + SUP_BLOCKS - 1) // SUP_BLOCKS
        return lax.fori_loop(0, nseg, segment, (m_run, o_run))[1]

    blocks = [Block(qb) for qb in range(QB_PER_STEP)]
    fronts = [front(blk) for blk in blocks]
    fronts = [middle(blk, *f) for blk, f in zip(blocks, fronts)]
    tails = [rest(blk, m_run, o_run) for blk, (_, m_run, o_run) in zip(blocks, fronts)]
    for blk, (o_win, _, _), o_run in zip(blocks, fronts, tails):
        qb = blk.qb
        gate = gate_ref[0, 0, qb]
        mix = gate[0:1] * ocmp_ref[0, 0, qb] + gate[1:2] * _normalize(o_run) + gate[2:3] * o_win
        rows = jnp.concatenate([mix[:, r * Q_BLOCK:(r + 1) * Q_BLOCK] for r in range(HEADS_PER_GROUP)],
                               axis=0)
        out_ref[0, qb * Q_BLOCK:(qb + 1) * Q_BLOCK, :] = rows.T.astype(bf16)


def _slc_win_attention(lst, cnt, qt, ks, vs, kw, vw, madd, ocmp, gate_t, wtab):
    b, t, _ = ks.shape
    g = N_GROUPS
    nqb, nsel = t // Q_BLOCK, t // SEL_BLOCK
    width = HEADS_PER_GROUP * Q_BLOCK
    nstep = nqb // QB_PER_STEP
    flat = lambda bi, gi, ci: (bi * g + gi) * nstep + ci
    smem = lambda w_: pl.BlockSpec((QB_PER_STEP, 1, w_), lambda bi, gi, ci: (flat(bi, gi, ci), 0, 0),
                                   memory_space=pltpu.SMEM)
    per_q = lambda r_, c_: pl.BlockSpec((1, 1, QB_PER_STEP, r_, c_), lambda bi, gi, ci: (bi, gi, ci, 0, 0))
    keys = pl.BlockSpec((1, t, LANES), lambda bi, gi, ci: (bi, 0, gi))
    return pl.pallas_call(
        functools.partial(_slc_win_body, nsel=nsel),
        grid=(b, g, nstep),
        in_specs=[smem(nsel), smem(LANES),
                  pl.BlockSpec((QB_PER_STEP, 1, LANES, width), lambda bi, gi, ci: (bi * nstep + ci, gi, 0, 0)),
                  keys, keys, keys, keys,
                  per_q(nsel, Q_BLOCK), per_q(HEAD_DIM, width), per_q(8, width),
                  pl.BlockSpec(wtab.shape, lambda bi, gi, ci: (0, 0, 0))],
        out_specs=pl.BlockSpec((1, QB_PER_STEP * Q_BLOCK, 2 * LANES), lambda bi, gi, ci: (bi, ci, gi)),
        out_shape=jax.ShapeDtypeStruct((b, t, ATTN_WIDTH), bf16),
        scratch_shapes=[pltpu.VMEM((SUP_BLOCKS * SEL_BLOCK, width), f32)],
        compiler_params=pltpu.CompilerParams(dimension_semantics=("arbitrary",) * 3,
                                             vmem_limit_bytes=VMEM_LIMIT),
        name="slc_win_attention",
    )(lst, cnt, qt, ks, vs, kw, vw, madd, ocmp, gate_t, wtab)


PREV_ROWS = 16


def _rms(v, g):
    return (v * lax.rsqrt(jnp.mean(v * v, axis=-1, keepdims=True) + EPS)) * g


def _mix_ffn_body(x_ref, attn_ref, u_ref, uprev_ref, bg_ref, cw_ref, go_ref, wo_ref, gf_ref, wu_ref, wd_ref,
                  out_ref, *, tm, seq, chunk):
    first = (pl.program_id(0) * tm) % seq == 0
    u = u_ref[...].astype(f32)
    prev = jnp.where(first, 0.0, uprev_ref[0].astype(f32))
    ext = jnp.concatenate([prev, u], axis=0)
    cw = cw_ref[...]
    conv = (cw[0:1] * ext[PREV_ROWS - 2:PREV_ROWS - 2 + tm] + cw[1:2] * ext[PREV_ROWS - 1:PREV_ROWS - 1 + tm]
            + cw[2:3] * u)
    conv = bg_ref[...].astype(f32) * conv
    go = go_ref[...]
    mixed = jnp.concatenate([_rms(attn_ref[...].astype(f32), go[:, :ATTN_WIDTH]),
                             _rms(conv, go[:, ATTN_WIDTH:])], axis=1).astype(bf16)
    x = x_ref[...] + _dot(mixed, wo_ref[...])
    h = _rms(x, gf_ref[...]).astype(bf16)
    acc = x
    for c in range(D_FF // chunk):
        a = jnp.maximum(_dot(h, wu_ref[:, c * chunk:(c + 1) * chunk]), 0.0)
        acc = acc + _dot((a * a).astype(bf16), wd_ref[c * chunk:(c + 1) * chunk, :])
    out_ref[...] = acc


def _mix_ffn(x2, attn2, u, bgate, cw, go, wo, gf, wu, wd, *, seq, tm=512, chunk=1024):
    n = x2.shape[0]
    uprev = u.reshape(n // PREV_ROWS, PREV_ROWS, CONV_WIDTH)
    row = lambda w_: pl.BlockSpec((tm, w_), lambda i: (i, 0))
    full = lambda a: pl.BlockSpec(a.shape, lambda i: (0,) * a.ndim)
    once = lambda a: pl.BlockSpec(a.shape, lambda i: (0,) * a.ndim, pipeline_mode=pl.Buffered(1))
    return pl.pallas_call(
        functools.partial(_mix_ffn_body, tm=tm, seq=seq, chunk=chunk),
        grid=(n // tm,),
        in_specs=[row(D_MODEL), row(ATTN_WIDTH), row(CONV_WIDTH),
                  pl.BlockSpec((1, PREV_ROWS, CONV_WIDTH),
                               lambda i: (jnp.maximum(i * (tm // PREV_ROWS) - 1, 0), 0, 0)),
                  row(CONV_WIDTH), full(cw), full(go), once(wo), full(gf), once(wu), once(wd)],
        out_specs=row(D_MODEL),
        out_shape=jax.ShapeDtypeStruct((n, D_MODEL), f32),
        input_output_aliases={0: 0},
        compiler_params=pltpu.CompilerParams(dimension_semantics=("arbitrary",),
                                             vmem_limit_bytes=VMEM_LIMIT),
        name="mix_ffn",
    )(x2, attn2, u, uprev, bgate, cw, go, wo, gf, wu, wd)


def _interleave_zero(w):
    z = jnp.zeros(w.shape[:-1] + (HEAD_DIM,), w.dtype)
    return jnp.concatenate([w[..., :HEAD_DIM], z, w[..., HEAD_DIM:], z], axis=-1)


def _prep_params(g_mix_norm, w_in, g_q, g_k, pe_cmp, w_cmp1, b_cmp1, w_cmp2, b_cmp2,
                 conv_w, g_out, w_o, g_ffn_norm, w_up, w_down):
    depth = w_in.shape[0]
    o = np.cumsum([0, ATTN_WIDTH] + [LANES] * 6 + [N_BRANCH * N_HEADS] + [CONV_WIDTH] * 3)
    part = lambda i: w_in[..., int(o[i]):int(o[i + 1])]
    q, kc, vc, ks, vs, kw, vw, gl, hc, cg, bg = [part(i) for i in range(11)]
    gl = jnp.pad(gl, ((0, 0), (0, 0), (0, LANES - gl.shape[-1])))
    w = jnp.concatenate([q, kc, vc, gl, jnp.zeros_like(gl), _interleave_zero(ks), _interleave_zero(kw),
                         _interleave_zero(vs), _interleave_zero(vw), hc, cg, bg], axis=-1).astype(bf16)
    tile2 = lambda gk: _interleave_zero(jnp.concatenate([gk, gk], axis=-1))[:, None, :]
    w1 = w_cmp1.astype(bf16).reshape(depth, 2, 2, CMP_STRIDE, HEAD_DIM, CMP_HIDDEN)
    z1 = jnp.zeros_like(w1)
    w1g = jnp.stack([jnp.concatenate([w1, z1], axis=-1), jnp.concatenate([z1, w1], axis=-1)], axis=4)
    w1g = w1g.reshape(depth, 2, 2, CMP_STRIDE * LANES, N_GROUPS * CMP_HIDDEN)
    pe = pe_cmp.reshape(depth, 2, 2, CMP_STRIDE, 1, HEAD_DIM)
    pe = jnp.broadcast_to(pe, (depth, 2, 2, CMP_STRIDE, N_GROUPS, HEAD_DIM)).reshape(depth, 2, 2, 1, -1)
    pe = jnp.pad(pe, ((0, 0), (0, 0), (0, 0), (0, 7), (0, 0))).astype(bf16)
    b1 = jnp.concatenate([b_cmp1, b_cmp1], axis=-1)[:, :, None, :]
    z2 = jnp.zeros_like(w_cmp2)
    w2 = jnp.concatenate([jnp.concatenate([w_cmp2, z2], axis=-1), jnp.concatenate([z2, w_cmp2], axis=-1)],
                         axis=2)
    w2 = _interleave_zero(w2).astype(bf16)
    b2 = _interleave_zero(jnp.concatenate([b_cmp2, b_cmp2], axis=-1))[:, :, None, :]
    return dict(
        gm=g_mix_norm[:, None, :], w=w,
        gq=jnp.tile(g_q, (1, N_HEADS))[:, None, :],
        gks=tile2(g_k[:, 1]), gkw=tile2(g_k[:, 2]), gkc=tile2(g_k[:, 0]),
        w1=w1g, pe=pe, b1=b1,
        w2k=w2[:, 0], w2v=w2[:, 1], b2k=b2[:, 0], b2v=b2[:, 1],
        cw=jnp.pad(conv_w, ((0, 0), (0, 8 - conv_w.shape[1]), (0, 0))),
        go=g_out[:, None, :], wo=w_o.astype(bf16),
        gf=g_ffn_norm[:, None, :], wu=w_up.astype(bf16), wd=w_down.astype(bf16),
    )


def _constants(nsel):
    lane = np.arange(LANES)
    e = (lane[:, None] // HEAD_DIM == lane[None, :] // HEAD_DIM).astype(np.float32)
    tail = np.zeros((N_GROUPS, HEAD_DIM, HEADS_PER_GROUP * Q_BLOCK), np.float32)
    for gi in range(N_GROUPS):
        for r in range(HEADS_PER_GROUP):
            rest = np.float64(2.0 ** -(gi * HEADS_PER_GROUP + r + 1)) * LOG2E
            for k in range(SLOPE_TERMS):
                term = np.float64(np.asarray(rest, np.float32).astype(jnp.bfloat16).astype(np.float32))
                tail[gi, k, r * Q_BLOCK:(r + 1) * Q_BLOCK] = term * LANES
                tail[gi, SLOPE_TERMS + k, r * Q_BLOCK:(r + 1) * Q_BLOCK] = term
                rest = rest - term
    mct = np.zeros((IMP_ROWS, CMP_CHUNK), np.float32)
    for i in range(CMP_CHUNK):
        lo, hi = i * CMP_STRIDE, i * CMP_STRIDE + CMP_BLOCK
        for jj in range(SEL_PER_CHUNK + 1):
            ov = min(hi, (jj + 1) * SEL_BLOCK) - max(lo, jj * SEL_BLOCK)
            if ov > 0:
                mct[jj, i] = ov / CMP_BLOCK
    kk = np.arange(LANES)[:, None]
    ql = np.tile(np.arange(Q_BLOCK), HEADS_PER_GROUP)[None, :]
    wtab = np.zeros((4, LANES, HEADS_PER_GROUP * Q_BLOCK), np.float32)
    wtab[1] = np.where(kk > ql, 0.0, NEG)
    wtab[2] = np.where(kk <= ql, 0.0, NEG)
    wtab[3] = NEG
    tri = np.arange(nsel)[:, None] < np.arange(nsel)[None, :]
    return (jnp.asarray(e, bf16), jnp.asarray(tail, bf16), jnp.asarray(mct, bf16), jnp.asarray(wtab),
            jnp.asarray(tri, bf16))


def _layer(x2, p, consts, *, batch, seq):
    e, tail, mct, wtab, tri = consts
    g = N_GROUPS
    nqb, nsel, tc = seq // Q_BLOCK, seq // SEL_BLOCK, seq // CMP_STRIDE
    qt, kc, vc, ks, kw, vs, vw, u, bgate, gates = _inproj(
        x2, p["gm"], p["w"], p["gq"], p["gks"], p["gkw"], e, tail, seq=seq)
    kcmp, vcmp = _compress(kc.reshape(batch, tc, CMP_STRIDE * LANES), vc.reshape(batch, tc, CMP_STRIDE * LANES),
                           p["w1"], p["pe"], p["b1"], p["w2k"], p["w2v"], p["b2k"], p["b2v"], p["gkc"], e)
    gate_t = gates[:, :N_HEADS * N_BRANCH].reshape(batch, nqb, Q_BLOCK, g, HEADS_PER_GROUP, N_BRANCH)
    gate_t = gate_t.transpose(0, 3, 1, 5, 4, 2).reshape(batch, g, nqb, N_BRANCH, HEADS_PER_GROUP * Q_BLOCK)
    gate_t = jnp.pad(gate_t, ((0, 0), (0, 0), (0, 0), (0, 8 - N_BRANCH), (0, 0)))
    ocmp, madd, lst, cnt = _cmp_attention(qt, kcmp, vcmp, mct, tri, batch=batch)
    rows3 = lambda a: a.reshape(batch, seq, 2 * LANES)
    smem = lambda a: a[:, :, :, 0, :].reshape(batch * g * nqb, 1, a.shape[-1])
    attn = _slc_win_attention(smem(lst), smem(cnt), qt, rows3(ks), rows3(vs),
                              rows3(kw), rows3(vw), madd, ocmp, gate_t, wtab)
    return _mix_ffn(x2, attn.reshape(batch * seq, ATTN_WIDTH), u, bgate, p["cw"], p["go"], p["wo"],
                    p["gf"], p["wu"], p["wd"], seq=seq)


def kernel(x, g_mix_norm, w_in, g_q, g_k, pe_cmp, w_cmp1, b_cmp1, w_cmp2, b_cmp2, conv_w, g_out, w_o,
           g_ffn_norm, w_up, w_down):
    batch, seq, d = x.shape
    assert d == D_MODEL and seq % (CMP_CHUNK * CMP_STRIDE) == 0 and seq >= WINDOW + Q_BLOCK
    params = _prep_params(g_mix_norm, w_in, g_q, g_k, pe_cmp, w_cmp1, b_cmp1, w_cmp2, b_cmp2,
                          conv_w, g_out, w_o, g_ffn_norm, w_up, w_down)
    consts = _constants(seq // SEL_BLOCK)

    def step(x2, p):
        return _layer(x2, p, consts, batch=batch, seq=seq), None

    x2, _ = lax.scan(step, x.reshape(batch * seq, d), params)
    return x2.reshape(batch, seq, d)
```

```python
import functools

import numpy as np
import jax
import jax.numpy as jnp
from jax import lax
from jax.experimental import pallas as pl
from jax.experimental.pallas import tpu as pltpu

f32 = jnp.float32
bf16 = jnp.bfloat16
i32 = jnp.int32

D_MODEL = 1024
HEAD_DIM = 64
N_HEADS = 8
N_GROUPS = 2
HEADS_PER_GROUP = 4
ATTN_WIDTH = 512
CONV_WIDTH = 512
N_BRANCH = 3
CMP_BLOCK = 32
CMP_STRIDE = 16
CMP_HIDDEN = 256
SEL_BLOCK = 64
SEL_TOPK = 16
WINDOW = 512
Q_BLOCK = 128
D_FF = 4096
EPS = 1e-6
NEG = -1e30
LOG2E = 1.4426950408889634
SLOPE_TERMS = 3
PICKED = -3e38
LANES = 128
CMP_CHUNK = 256
SEL_PER_CHUNK = CMP_CHUNK * CMP_STRIDE // SEL_BLOCK
IMP_ROWS = SEL_PER_CHUNK + 8
VMEM_LIMIT = 56 * 1024 * 1024

PROJ_CHUNK = 512
SUP_BLOCKS = 32
HEAD_BLOCKS = 20
GROUP_BLOCKS = 4
QB_PER_STEP = 4
CMP_QB = 2


def _dot(a, b):
    return jnp.dot(a, b, preferred_element_type=f32)


def _dot_nt(a, b):
    return lax.dot_general(a, b, (((1,), (1,)), ((), ())), preferred_element_type=f32)


def _dot_tn(a, b):
    return lax.dot_general(a, b, (((0,), (0,)), ((), ())), preferred_element_type=f32)


def _head_norm(z, e, g):
    sq = (z * z).astype(bf16)
    outs = []
    for c in range(z.shape[1] // LANES):
        sl = slice(c * LANES, (c + 1) * LANES)
        outs.append(z[:, sl] * lax.rsqrt(_dot(sq[:, sl], e) * (1.0 / HEAD_DIM) + EPS))
    y = outs[0] if len(outs) == 1 else jnp.concatenate(outs, axis=1)
    return y * g


def _pos_cols(pos, width):
    rows = pos.shape[0]
    lane = lax.broadcasted_iota(i32, (rows, LANES), 1) - HEAD_DIM
    tile = jnp.where(lane < 0, 0.0, jnp.where(lane < SLOPE_TERMS, (pos >> 7).astype(f32),
                                              jnp.where(lane < 2 * SLOPE_TERMS, (pos & 127).astype(f32), 0.0)))
    return tile if width == LANES else jnp.concatenate([tile] * (width // LANES), axis=1)


def _ones_col(rows, width):
    lane = lax.broadcasted_iota(i32, (rows, LANES), 1)
    tile = jnp.where(lane == HEAD_DIM, 1.0, 0.0)
    return tile if width == LANES else jnp.concatenate([tile] * (width // LANES), axis=1)


def _inproj_body(x_ref, gm_ref, w_ref, gq_ref, gks_ref, gkw_ref, e_ref, tail_ref,
                 qt_ref, kc_ref, vc_ref, ks_ref, kw_ref, vs_ref, vw_ref, u_ref, bg_ref, gate_ref,
                 *, tm, seq):
    x = x_ref[...]
    ms = jnp.mean(x * x, axis=-1, keepdims=True)
    h = ((x * lax.rsqrt(ms + EPS)) * gm_ref[...]).astype(bf16)
    e = e_ref[...]
    two = 2 * LANES

    def proj(k):
        return _dot(h, w_ref[:, k * PROJ_CHUNK:(k + 1) * PROJ_CHUNK])

    def emit_queries(z):
        qn = _head_norm(z, e, gq_ref[...]) * (HEAD_DIM ** -0.5 * LOG2E)
        gw = HEADS_PER_GROUP * HEAD_DIM
        for cb in range(tm // Q_BLOCK):
            for g in range(N_GROUPS):
                blk = qn[cb * Q_BLOCK:(cb + 1) * Q_BLOCK, g * gw:(g + 1) * gw].T
                top = jnp.concatenate([blk[r * HEAD_DIM:(r + 1) * HEAD_DIM] for r in range(HEADS_PER_GROUP)],
                                      axis=1)
                qt_ref[cb, g] = jnp.concatenate([top.astype(bf16), tail_ref[g]], axis=0)

    def emit_compress_inputs(z):
        kc_ref[...] = z[:, 0:LANES].astype(bf16)
        vc_ref[...] = z[:, LANES:two].astype(bf16)
        gate_ref[...] = 1.0 / (1.0 + jnp.exp(-z[:, two:two + LANES]))

    def emit_keys(z):
        t0 = (pl.program_id(0) * tm) % seq
        pc = _pos_cols(lax.broadcasted_iota(i32, (tm, LANES), 0) + t0, two)
        ks_ref[...] = (_head_norm(z[:, 0:two], e, gks_ref[...]) + pc).astype(bf16)
        kw_ref[...] = (_head_norm(z[:, two:], e, gkw_ref[...]) + pc).astype(bf16)

    def emit_values(z):
        ones = _ones_col(tm, two)
        vs_ref[...] = (z[:, 0:two] + ones).astype(bf16)
        vw_ref[...] = (z[:, two:] + ones).astype(bf16)

    z0 = proj(0)
    z1 = proj(1)
    emit_queries(z0)
    z2 = proj(2)
    emit_compress_inputs(z1)
    z3 = proj(3)
    emit_keys(z2)
    z4 = proj(4)
    emit_values(z3)
    z5 = proj(5)
    z6 = proj(6)
    u_ref[...] = (z4 * z5).astype(bf16)
    bg_ref[...] = z6.astype(bf16)


def _inproj(x2, gm, w, gq, gks, gkw, e, tail, *, seq, tm=512):
    n = x2.shape[0]
    width = HEADS_PER_GROUP * Q_BLOCK
    row = lambda w_: pl.BlockSpec((tm, w_), lambda i: (i, 0))
    full = lambda a: pl.BlockSpec(a.shape, lambda i: (0,) * a.ndim)
    outs = [(LANES, bf16), (LANES, bf16), (2 * LANES, bf16), (2 * LANES, bf16),
            (2 * LANES, bf16), (2 * LANES, bf16), (CONV_WIDTH, bf16), (CONV_WIDTH, bf16), (LANES, f32)]
    qt_spec = pl.BlockSpec((tm // Q_BLOCK, N_GROUPS, LANES, width), lambda i: (i, 0, 0, 0))
    qt_shape = jax.ShapeDtypeStruct((n // Q_BLOCK, N_GROUPS, LANES, width), bf16)
    return pl.pallas_call(
        functools.partial(_inproj_body, tm=tm, seq=seq),
        grid=(n // tm,),
        in_specs=[row(D_MODEL), full(gm), full(w), full(gq), full(gks), full(gkw), full(e), full(tail)],
        out_specs=[qt_spec] + [row(w_) for w_, _ in outs],
        out_shape=[qt_shape] + [jax.ShapeDtypeStruct((n, w_), dt) for w_, dt in outs],
        compiler_params=pltpu.CompilerParams(dimension_semantics=("arbitrary",),
                                             vmem_limit_bytes=VMEM_LIMIT),
        name="inproj",
    )(x2, gm, w, gq, gks, gkw, e, tail)


def _gelu_tanh(x):
    return 0.5 * x * (1.0 + jnp.tanh(0.7978845608028654 * (x + 0.044715 * (x * x * x))))


def _compress_body(zk_ref, zv_ref, w1_ref, pe_ref, b1_ref, w2k_ref, w2v_ref, b2k_ref, b2v_ref,
                   gk_ref, e_ref, kc_ref, vc_ref, *, tc):
    last = lax.broadcasted_iota(i32, (tc, 1), 0) == tc - 1

    def hidden(z_ref, kind):
        z = z_ref[0]
        a = _dot(z, w1_ref[kind, 0])
        b = _dot(z, w1_ref[kind, 1])
        b = jnp.concatenate([b[1:], jnp.zeros((1, b.shape[1]), f32)], axis=0)
        bias = (_dot(pe_ref[kind, 0], w1_ref[kind, 0]) + _dot(pe_ref[kind, 1], w1_ref[kind, 1]))[0:1]
        return _gelu_tanh(a + b + bias + b1_ref[kind]).astype(bf16)

    k = _dot(hidden(zk_ref, 0), w2k_ref[...]) + b2k_ref[...]
    k = _head_norm(k, e_ref[...], gk_ref[...])
    pos = lax.broadcasted_iota(i32, (tc, LANES), 0) * CMP_STRIDE + (CMP_BLOCK - 1)
    k = k + _pos_cols(pos, 2 * LANES)
    kc_ref[0] = jnp.where(last, 0.0, k).astype(bf16)
    v = _dot(hidden(zv_ref, 1), w2v_ref[...]) + b2v_ref[...]
    vc_ref[0] = jnp.where(last, 0.0, v + _ones_col(tc, 2 * LANES)).astype(bf16)


def _compress(zk, zv, w1, pe, b1, w2k, w2v, b2k, b2v, gk, e):
    b, tc, _ = zk.shape
    blk = lambda a: pl.BlockSpec((1,) + a.shape[1:], lambda i: (i,) + (0,) * (a.ndim - 1))
    full = lambda a: pl.BlockSpec(a.shape, lambda i: (0,) * a.ndim)
    return pl.pallas_call(
        functools.partial(_compress_body, tc=tc),
        grid=(b,),
        in_specs=[blk(zk), blk(zv)] + [full(a) for a in (w1, pe, b1, w2k, w2v, b2k, b2v, gk, e)],
        out_specs=[pl.BlockSpec((1, tc, 2 * LANES), lambda i: (i, 0, 0))] * 2,
        out_shape=[jax.ShapeDtypeStruct((b, tc, 2 * LANES), bf16)] * 2,
        compiler_params=pltpu.CompilerParams(dimension_semantics=("arbitrary",),
                                             vmem_limit_bytes=VMEM_LIMIT),
        name="compress",
    )(zk, zv, w1, pe, b1, w2k, w2v, b2k, b2v, gk, e)


def _query_pos(c):
    lane = lax.broadcasted_iota(i32, (1, HEADS_PER_GROUP * Q_BLOCK), 1)
    return c * Q_BLOCK + (lane & (Q_BLOCK - 1))


def _pick_top(vs, jf, n_pick):
    vs = list(vs)
    for _ in range(n_pick):
        for g, v in enumerate(vs):
            mx = jnp.max(v, axis=0, keepdims=True)
            idx = jnp.min(jnp.where(v == mx, jf, float(jf.shape[0])), axis=0, keepdims=True)
            vs[g] = jnp.where(jf == idx, PICKED, v)
    return tuple(vs)


def _cmp_variant(nchunk, c0, qt_ref, kc_ref, vc_ref, mct_ref, tri_ref, o_ref, madd_ref, lst_ref, cnt_ref,
                 *, nsel):
    units = [(qb, g) for qb in range(CMP_QB) for g in range(N_GROUPS)]
    nrow = nchunk * CMP_CHUNK
    npre = nchunk * SEL_PER_CHUNK
    tqs = [_query_pos(c0 + qb) for qb in range(CMP_QB)]
    sub = lax.broadcasted_iota(i32, (CMP_CHUNK, 1), 0)
    lanes = lambda g: slice(g * LANES, (g + 1) * LANES)
    chunk = lambda a, k: a[k * CMP_CHUNK:(k + 1) * CMP_CHUNK]

    tiles = []
    for qb, g in units:
        s = _dot(kc_ref[0, 0:nrow, lanes(g)], qt_ref[qb, g])
        row = []
        for k in range(nchunk):
            t = chunk(s, k)
            if k >= nchunk - 2:
                seen = (k * CMP_CHUNK + sub) * CMP_STRIDE + (CMP_BLOCK - 1) <= tqs[qb]
                t = jnp.where(seen, t, NEG)
            row.append(t)
        tiles.append(row)

    accs, imps = [], []
    for u, (qb, g) in enumerate(units):
        m = tiles[u][0].max(axis=0, keepdims=True)
        for t in tiles[u][1:]:
            m = jnp.maximum(m, t.max(axis=0, keepdims=True))
        parts = [jnp.exp2(t - m).astype(bf16) for t in tiles[u]]
        accs.append(_dot_tn(vc_ref[0, 0:nrow, lanes(g)], jnp.concatenate(parts, axis=0)))
        rows, carry = [], None
        for part in parts:
            piece = _dot(mct_ref[...], part)
            body = piece[0:SEL_PER_CHUNK]
            if carry is not None:
                body = jnp.concatenate([body[0:8] + carry, body[8:]], axis=0)
            rows.append(body)
            carry = piece[SEL_PER_CHUNK:IMP_ROWS]
        imps.append(rows[0] if nchunk == 1 else jnp.concatenate(rows, axis=0))

    j = lax.broadcasted_iota(i32, (npre, Q_BLOCK), 0)
    jf = j.astype(f32)
    t1s = [tq[:, 0:Q_BLOCK] for tq in tqs]
    valids = [j * SEL_BLOCK <= t1 for t1 in t1s]
    vs = []
    for u, (qb, g) in enumerate(units):
        any_key = tqs[qb] >= CMP_BLOCK - 1
        inv = jnp.where(any_key, 1.0 / jnp.maximum(accs[u][HEAD_DIM:HEAD_DIM + 1], 1e-30), 0.0)
        o_ref[0, g, qb] = accs[u][0:HEAD_DIM] * inv
        imp4 = imps[u] * inv
        imp = imp4[:, 0:Q_BLOCK]
        for r in range(1, HEADS_PER_GROUP):
            imp = imp + imp4[:, r * Q_BLOCK:(r + 1) * Q_BLOCK]
        jt = t1s[qb] >> 6
        v = jnp.where(valids[qb], imp, NEG)
        vs.append(jnp.where(j == 0, PICKED, jnp.where(j == jt, PICKED, jnp.where(j == jt - 1, PICKED, v))))
    vs = _pick_top(vs, jf, SEL_TOPK - 3)

    rr = lax.broadcasted_iota(i32, (nsel, nsel), 0).astype(f32)
    ones8 = jnp.ones((8, Q_BLOCK), bf16)
    jrow = jnp.broadcast_to(lax.broadcasted_iota(i32, (1, nsel), 1).astype(f32), (8, nsel)).astype(bf16)
    for u, (qb, g) in enumerate(units):
        picked = vs[u] == PICKED
        madd = jnp.where(valids[qb], jnp.where(picked, 0.0, NEG), NEG)
        sel = jnp.where(valids[qb], jnp.where(picked, 1.0, 0.0), 0.0).astype(bf16)
        if npre < nsel:
            madd = jnp.concatenate([madd, jnp.full((nsel - npre, Q_BLOCK), NEG, f32)], axis=0)
            sel = jnp.concatenate([sel, jnp.zeros((nsel - npre, Q_BLOCK), bf16)], axis=0)
        madd_ref[0, g, qb] = madd
        flag = jnp.where(_dot_nt(ones8, sel)[0:1] > 0.0, 1.0, 0.0)
        flag8 = jnp.broadcast_to(flag, (8, nsel)).astype(bf16)
        prefix = _dot(flag8, tri_ref[...])[0:1]
        place = jnp.where(prefix == rr, flag, 0.0).astype(bf16)
        lst_ref[0, g, qb] = _dot_nt(jrow, place).astype(i32)
        cnt_ref[0, g, qb] = _dot(flag8, jnp.ones((nsel, LANES), bf16)).astype(i32)


def _cmp_body(qt_ref, kc_ref, vc_ref, mct_ref, tri_ref, o_ref, madd_ref, lst_ref, cnt_ref, *, nsel, nvar):
    c0 = pl.program_id(1) * CMP_QB
    c_last = c0 + CMP_QB - 1
    nch = (c_last * (Q_BLOCK // CMP_STRIDE) + (Q_BLOCK - CMP_BLOCK) // CMP_STRIDE) // CMP_CHUNK + 1
    for n in range(1, nvar + 1):
        pl.when(nch == n)(functools.partial(
            _cmp_variant, n, c0, qt_ref, kc_ref, vc_ref, mct_ref, tri_ref, o_ref, madd_ref, lst_ref, cnt_ref,
            nsel=nsel))


def _cmp_attention(qt, kc, vc, mct, tri, *, batch):
    g = N_GROUPS
    nqb = qt.shape[0] // batch
    tc = kc.shape[1]
    nsel = tc * CMP_STRIDE // SEL_BLOCK
    width = HEADS_PER_GROUP * Q_BLOCK
    nstep = nqb // CMP_QB
    per_q = lambda r_, c_: pl.BlockSpec((1, g, CMP_QB, r_, c_), lambda bi, ci: (bi, 0, ci, 0, 0))
    shape = lambda r_, c_, dt: jax.ShapeDtypeStruct((batch, g, nqb, r_, c_), dt)
    return pl.pallas_call(
        functools.partial(_cmp_body, nsel=nsel, nvar=tc // CMP_CHUNK),
        grid=(batch, nstep),
        in_specs=[pl.BlockSpec((CMP_QB, g, LANES, width), lambda bi, ci: (bi * nstep + ci, 0, 0, 0)),
                  pl.BlockSpec((1, tc, 2 * LANES), lambda bi, ci: (bi, 0, 0)),
                  pl.BlockSpec((1, tc, 2 * LANES), lambda bi, ci: (bi, 0, 0)),
                  pl.BlockSpec(mct.shape, lambda bi, ci: (0, 0)),
                  pl.BlockSpec(tri.shape, lambda bi, ci: (0, 0))],
        out_specs=[per_q(HEAD_DIM, width), per_q(nsel, Q_BLOCK), per_q(8, nsel), per_q(8, LANES)],
        out_shape=[shape(HEAD_DIM, width, f32), shape(nsel, Q_BLOCK, f32),
                   shape(8, nsel, i32), shape(8, LANES, i32)],
        compiler_params=pltpu.CompilerParams(dimension_semantics=("arbitrary",) * 2,
                                             vmem_limit_bytes=VMEM_LIMIT),
        name="cmp_attention",
    )(qt, kc, vc, mct, tri)


def _normalize(o_aug):
    return o_aug[0:HEAD_DIM] * (1.0 / jnp.maximum(o_aug[HEAD_DIM:HEAD_DIM + 1], 1e-30))


def _slc_win_body(lst_ref, cnt_ref, qt_ref, ks_ref, vs_ref, kw_ref, vw_ref,
                  madd_ref, ocmp_ref, gate_ref, wtab_ref, out_ref, s_scr, *, nsel):
    width = HEADS_PER_GROUP * Q_BLOCK
    nwin = (WINDOW + Q_BLOCK) // LANES
    wq = WINDOW // Q_BLOCK
    grp_rows = GROUP_BLOCKS * SEL_BLOCK

    class Block:
        def __init__(self, qb):
            self.qb = qb
            self.c = pl.program_id(2) * QB_PER_STEP + qb
            self.q0 = pl.multiple_of(self.c * Q_BLOCK, Q_BLOCK)
            self.qta = qt_ref[qb, 0]
            self.n_off = cnt_ref[qb, 0, 0] - 2

        def block_ids(self, first, nblk):
            ids = []
            for u in range(nblk):
                i = first + u
                j = lst_ref[self.qb, 0, jnp.minimum(i, nsel - 1)]
                ids.append((i < self.n_off, j, pl.multiple_of(j * SEL_BLOCK, SEL_BLOCK)))
            return ids

        def scores(self, ids):
            kcat = jnp.concatenate([ks_ref[0, pl.ds(r0, SEL_BLOCK), :] for _, _, r0 in ids], axis=0)
            sg = _dot(kcat, self.qta)
            tiles = []
            for u, (live, j, _) in enumerate(ids):
                mrow = jnp.where(live, madd_ref[0, 0, self.qb, pl.ds(j, 1), :], NEG)
                mrow = jnp.concatenate([mrow] * HEADS_PER_GROUP, axis=1)
                tiles.append(sg[u * SEL_BLOCK:(u + 1) * SEL_BLOCK] + mrow)
            return tiles

        def values(self, ids):
            return jnp.concatenate([vs_ref[0, pl.ds(r0, SEL_BLOCK), :] for _, _, r0 in ids], axis=0)

    def front(blk):
        c, q0, qta = blk.c, blk.q0, blk.qta
        ws = pl.multiple_of(jnp.maximum(c - wq, 0) * Q_BLOCK, Q_BLOCK)
        s = _dot(kw_ref[0, pl.ds(ws, WINDOW + Q_BLOCK), :], qta)
        chunks = []
        for k in range(nwin):
            steady = 1 if k == 0 else (2 if k == nwin - 1 else 0)
            tab = jnp.where(c >= wq, steady, jnp.where(k < c, 0, jnp.where(k == c, 2, 3)))
            chunks.append(s[k * LANES:(k + 1) * LANES] + wtab_ref[tab])
        head = blk.block_ids(0, HEAD_BLOCKS)
        tiles = [_dot(ks_ref[0, pl.ds(q0, Q_BLOCK), :], qta) + wtab_ref[2]] + blk.scores(head)
        return ws, chunks, head, tiles

    def softmax_pv(tiles, v):
        m = tiles[0].max(axis=0, keepdims=True)
        for su in tiles[1:]:
            m = jnp.maximum(m, su.max(axis=0, keepdims=True))
        p = jnp.concatenate([jnp.exp2(su - m).astype(bf16) for su in tiles], axis=0)
        return m, _dot_tn(v, p)

    def middle(blk, ws, chunks, head, tiles):
        _, o_win = softmax_pv(chunks, vw_ref[0, pl.ds(ws, WINDOW + Q_BLOCK), :])
        m_run, o_run = softmax_pv(
            tiles, jnp.concatenate([vs_ref[0, pl.ds(blk.q0, Q_BLOCK), :], blk.values(head)], axis=0))
        return _normalize(o_win), m_run, o_run

    def rest(blk, m_run, o_run):
        def segment(si, carry):
            m_run, o_run = carry
            base = HEAD_BLOCKS + si * SUP_BLOCKS
            ngrp = (jnp.minimum(SUP_BLOCKS, blk.n_off - base) + GROUP_BLOCKS - 1) // GROUP_BLOCKS
            rows = lambda gi: pl.ds(pl.multiple_of(gi * grp_rows, grp_rows), grp_rows)

            def score(gi, mx):
                sg = jnp.concatenate(blk.scores(blk.block_ids(base + gi * GROUP_BLOCKS, GROUP_BLOCKS)), axis=0)
                s_scr[rows(gi), :] = sg
                return jnp.maximum(mx, sg.max(axis=0, keepdims=True))

            m_new = lax.fori_loop(0, ngrp, score, m_run)

            def weigh(gi, acc):
                pg = jnp.exp2(s_scr[rows(gi), :] - m_new).astype(bf16)
                return acc + _dot_tn(blk.values(blk.block_ids(base + gi * GROUP_BLOCKS, GROUP_BLOCKS)), pg)

            o_seg = lax.fori_loop(0, ngrp, weigh, jnp.zeros((LANES, width), f32))
            return m_new, jnp.exp2(m_run - m_new) * o_run + o_seg

        nseg = (jnp.maximum(blk.n_off - HEAD_BLOCKS, 0) + SUP_BLOCKS - 1) // SUP_BLOCKS
        return lax.fori_loop(0, nseg, segment, (m_run, o_run))[1]

    blocks = [Block(qb) for qb in range(QB_PER_STEP)]
    fronts = [front(blk) for blk in blocks]
    fronts = [middle(blk, *f) for blk, f in zip(blocks, fronts)]
    tails = [rest(blk, m_run, o_run) for blk, (_, m_run, o_run) in zip(blocks, fronts)]
    for blk, (o_win, _, _), o_run in zip(blocks, fronts, tails):
        qb = blk.qb
        gate = gate_ref[0, 0, qb]
        mix = gate[0:1] * ocmp_ref[0, 0, qb] + gate[1:2] * _normalize(o_run) + gate[2:3] * o_win
        rows = jnp.concatenate([mix[:, r * Q_BLOCK:(r + 1) * Q_BLOCK] for r in range(HEADS_PER_GROUP)],
                               axis=0)
        out_ref[0, qb * Q_BLOCK:(qb + 1) * Q_BLOCK, :] = rows.T.astype(bf16)


def _slc_win_attention(lst, cnt, qt, ks, vs, kw, vw, madd, ocmp, gate_t, wtab):
    b, t, _ = ks.shape
    g = N_GROUPS
    nqb, nsel = t // Q_BLOCK, t // SEL_BLOCK
    width = HEADS_PER_GROUP * Q_BLOCK
    nstep = nqb // QB_PER_STEP
    flat = lambda bi, gi, ci: (bi * g + gi) * nstep + ci
    smem = lambda w_: pl.BlockSpec((QB_PER_STEP, 1, w_), lambda bi, gi, ci: (flat(bi, gi, ci), 0, 0),
                                   memory_space=pltpu.SMEM)
    per_q = lambda r_, c_: pl.BlockSpec((1, 1, QB_PER_STEP, r_, c_), lambda bi, gi, ci: (bi, gi, ci, 0, 0))
    keys = pl.BlockSpec((1, t, LANES), lambda bi, gi, ci: (bi, 0, gi))
    return pl.pallas_call(
        functools.partial(_slc_win_body, nsel=nsel),
        grid=(b, g, nstep),
        in_specs=[smem(nsel), smem(LANES),
                  pl.BlockSpec((QB_PER_STEP, 1, LANES, width), lambda bi, gi, ci: (bi * nstep + ci, gi, 0, 0)),
                  keys, keys, keys, keys,
                  per_q(nsel, Q_BLOCK), per_q(HEAD_DIM, width), per_q(8, width),
                  pl.BlockSpec(wtab.shape, lambda bi, gi, ci: (0, 0, 0))],
        out_specs=pl.BlockSpec((1, QB_PER_STEP * Q_BLOCK, 2 * LANES), lambda bi, gi, ci: (bi, ci, gi)),
        out_shape=jax.ShapeDtypeStruct((b, t, ATTN_WIDTH), bf16),
        scratch_shapes=[pltpu.VMEM((SUP_BLOCKS * SEL_BLOCK, width), f32)],
        compiler_params=pltpu.CompilerParams(dimension_semantics=("arbitrary",) * 3,
                                             vmem_limit_bytes=VMEM_LIMIT),
        name="slc_win_attention",
    )(lst, cnt, qt, ks, vs, kw, vw, madd, ocmp, gate_t, wtab)


PREV_ROWS = 16


def _rms(v, g):
    return (v * lax.rsqrt(jnp.mean(v * v, axis=-1, keepdims=True) + EPS)) * g


def _mix_ffn_body(x_ref, attn_ref, u_ref, uprev_ref, bg_ref, cw_ref, go_ref, wo_ref, gf_ref, wu_ref, wd_ref,
                  out_ref, *, tm, seq, chunk):
    first = (pl.program_id(0) * tm) % seq == 0
    u = u_ref[...].astype(f32)
    prev = jnp.where(first, 0.0, uprev_ref[0].astype(f32))
    ext = jnp.concatenate([prev, u], axis=0)
    cw = cw_ref[...]
    conv = (cw[0:1] * ext[PREV_ROWS - 2:PREV_ROWS - 2 + tm] + cw[1:2] * ext[PREV_ROWS - 1:PREV_ROWS - 1 + tm]
            + cw[2:3] * u)
    conv = bg_ref[...].astype(f32) * conv
    go = go_ref[...]
    mixed = jnp.concatenate([_rms(attn_ref[...].astype(f32), go[:, :ATTN_WIDTH]),
                             _rms(conv, go[:, ATTN_WIDTH:])], axis=1).astype(bf16)
    x = x_ref[...] + _dot(mixed, wo_ref[...])
    h = _rms(x, gf_ref[...]).astype(bf16)
    acc = x
    for c in range(D_FF // chunk):
        a = jnp.maximum(_dot(h, wu_ref[:, c * chunk:(c + 1) * chunk]), 0.0)
        acc = acc + _dot((a * a).astype(bf16), wd_ref[c * chunk:(c + 1) * chunk, :])
    out_ref[...] = acc


def _mix_ffn(x2, attn2, u, bgate, cw, go, wo, gf, wu, wd, *, seq, tm=512, chunk=1024):
    n = x2.shape[0]
    uprev = u.reshape(n // PREV_ROWS, PREV_ROWS, CONV_WIDTH)
    row = lambda w_: pl.BlockSpec((tm, w_), lambda i: (i, 0))
    full = lambda a: pl.BlockSpec(a.shape, lambda i: (0,) * a.ndim)
    once = lambda a: pl.BlockSpec(a.shape, lambda i: (0,) * a.ndim, pipeline_mode=pl.Buffered(1))
    return pl.pallas_call(
        functools.partial(_mix_ffn_body, tm=tm, seq=seq, chunk=chunk),
        grid=(n // tm,),
        in_specs=[row(D_MODEL), row(ATTN_WIDTH), row(CONV_WIDTH),
                  pl.BlockSpec((1, PREV_ROWS, CONV_WIDTH),
                               lambda i: (jnp.maximum(i * (tm // PREV_ROWS) - 1, 0), 0, 0)),
                  row(CONV_WIDTH), full(cw), full(go), once(wo), full(gf), once(wu), once(wd)],
        out_specs=row(D_MODEL),
        out_shape=jax.ShapeDtypeStruct((n, D_MODEL), f32),
        input_output_aliases={0: 0},
        compiler_params=pltpu.CompilerParams(dimension_semantics=("arbitrary",),
                                             vmem_limit_bytes=VMEM_LIMIT),
        name="mix_ffn",
    )(x2, attn2, u, uprev, bgate, cw, go, wo, gf, wu, wd)


def _interleave_zero(w):
    z = jnp.zeros(w.shape[:-1] + (HEAD_DIM,), w.dtype)
    return jnp.concatenate([w[..., :HEAD_DIM], z, w[..., HEAD_DIM:], z], axis=-1)


def _prep_params(g_mix_norm, w_in, g_q, g_k, pe_cmp, w_cmp1, b_cmp1, w_cmp2, b_cmp2,
                 conv_w, g_out, w_o, g_ffn_norm, w_up, w_down):
    depth = w_in.shape[0]
    o = np.cumsum([0, ATTN_WIDTH] + [LANES] * 6 + [N_BRANCH * N_HEADS] + [CONV_WIDTH] * 3)
    part = lambda i: w_in[..., int(o[i]):int(o[i + 1])]
    q, kc, vc, ks, vs, kw, vw, gl, hc, cg, bg = [part(i) for i in range(11)]
    gl = jnp.pad(gl, ((0, 0), (0, 0), (0, LANES - gl.shape[-1])))
    w = jnp.concatenate([q, kc, vc, gl, jnp.zeros_like(gl), _interleave_zero(ks), _interleave_zero(kw),
                         _interleave_zero(vs), _interleave_zero(vw), hc, cg, bg], axis=-1).astype(bf16)
    tile2 = lambda gk: _interleave_zero(jnp.concatenate([gk, gk], axis=-1))[:, None, :]
    w1 = w_cmp1.astype(bf16).reshape(depth, 2, 2, CMP_STRIDE, HEAD_DIM, CMP_HIDDEN)
    z1 = jnp.zeros_like(w1)
    w1g = jnp.stack([jnp.concatenate([w1, z1], axis=-1), jnp.concatenate([z1, w1], axis=-1)], axis=4)
    w1g = w1g.reshape(depth, 2, 2, CMP_STRIDE * LANES, N_GROUPS * CMP_HIDDEN)
    pe = pe_cmp.reshape(depth, 2, 2, CMP_STRIDE, 1, HEAD_DIM)
    pe = jnp.broadcast_to(pe, (depth, 2, 2, CMP_STRIDE, N_GROUPS, HEAD_DIM)).reshape(depth, 2, 2, 1, -1)
    pe = jnp.pad(pe, ((0, 0), (0, 0), (0, 0), (0, 7), (0, 0))).astype(bf16)
    b1 = jnp.concatenate([b_cmp1, b_cmp1], axis=-1)[:, :, None, :]
    z2 = jnp.zeros_like(w_cmp2)
    w2 = jnp.concatenate([jnp.concatenate([w_cmp2, z2], axis=-1), jnp.concatenate([z2, w_cmp2], axis=-1)],
                         axis=2)
    w2 = _interleave_zero(w2).astype(bf16)
    b2 = _interleave_zero(jnp.concatenate([b_cmp2, b_cmp2], axis=-1))[:, :, None, :]
    return dict(
        gm=g_mix_norm[:, None, :], w=w,
        gq=jnp.tile(g_q, (1, N_HEADS))[:, None, :],
        gks=tile2(g_k[:, 1]), gkw=tile2(g_k[:, 2]), gkc=tile2(g_k[:, 0]),
        w1=w1g, pe=pe, b1=b1,
        w2k=w2[:, 0], w2v=w2[:, 1], b2k=b2[:, 0], b2v=b2[:, 1],
        cw=jnp.pad(conv_w, ((0, 0), (0, 8 - conv_w.shape[1]), (0, 0))),
        go=g_out[:, None, :], wo=w_o.astype(bf16),
        gf=g_ffn_norm[:, None, :], wu=w_up.astype(bf16), wd=w_down.astype(bf16),
    )


def _constants(nsel):
    lane = np.arange(LANES)
    e = (lane[:, None] // HEAD_DIM == lane[None, :] // HEAD_DIM).astype(np.float32)
    tail = np.zeros((N_GROUPS, HEAD_DIM, HEADS_PER_GROUP * Q_BLOCK), np.float32)
    for gi in range(N_GROUPS):
        for r in range(HEADS_PER_GROUP):
            rest = np.float64(2.0 ** -(gi * HEADS_PER_GROUP + r + 1)) * LOG2E
            for k in range(SLOPE_TERMS):
                term = np.float64(np.asarray(rest, np.float32).astype(jnp.bfloat16).astype(np.float32))
                tail[gi, k, r * Q_BLOCK:(r + 1) * Q_BLOCK] = term * LANES
                tail[gi, SLOPE_TERMS + k, r * Q_BLOCK:(r + 1) * Q_BLOCK] = term
                rest = rest - term
    mct = np.zeros((IMP_ROWS, CMP_CHUNK), np.float32)
    for i in range(CMP_CHUNK):
        lo, hi = i * CMP_STRIDE, i * CMP_STRIDE + CMP_BLOCK
        for jj in range(SEL_PER_CHUNK + 1):
            ov = min(hi, (jj + 1) * SEL_BLOCK) - max(lo, jj * SEL_BLOCK)
            if ov > 0:
                mct[jj, i] = ov / CMP_BLOCK
    kk = np.arange(LANES)[:, None]
    ql = np.tile(np.arange(Q_BLOCK), HEADS_PER_GROUP)[None, :]
    wtab = np.zeros((4, LANES, HEADS_PER_GROUP * Q_BLOCK), np.float32)
    wtab[1] = np.where(kk > ql, 0.0, NEG)
    wtab[2] = np.where(kk <= ql, 0.0, NEG)
    wtab[3] = NEG
    tri = np.arange(nsel)[:, None] < np.arange(nsel)[None, :]
    return (jnp.asarray(e, bf16), jnp.asarray(tail, bf16), jnp.asarray(mct, bf16), jnp.asarray(wtab),
            jnp.asarray(tri, bf16))


def _layer(x2, p, consts, *, batch, seq):
    e, tail, mct, wtab, tri = consts
    g = N_GROUPS
    nqb, nsel, tc = seq // Q_BLOCK, seq // SEL_BLOCK, seq // CMP_STRIDE
    qt, kc, vc, ks, kw, vs, vw, u, bgate, gates = _inproj(
        x2, p["gm"], p["w"], p["gq"], p["gks"], p["gkw"], e, tail, seq=seq)
    kcmp, vcmp = _compress(kc.reshape(batch, tc, CMP_STRIDE * LANES), vc.reshape(batch, tc, CMP_STRIDE * LANES),
                           p["w1"], p["pe"], p["b1"], p["w2k"], p["w2v"], p["b2k"], p["b2v"], p["gkc"], e)
    gate_t = gates[:, :N_HEADS * N_BRANCH].reshape(batch, nqb, Q_BLOCK, g, HEADS_PER_GROUP, N_BRANCH)
    gate_t = gate_t.transpose(0, 3, 1, 5, 4, 2).reshape(batch, g, nqb, N_BRANCH, HEADS_PER_GROUP * Q_BLOCK)
    gate_t = jnp.pad(gate_t, ((0, 0), (0, 0), (0, 0), (0, 8 - N_BRANCH), (0, 0)))
    ocmp, madd, lst, cnt = _cmp_attention(qt, kcmp, vcmp, mct, tri, batch=batch)
    rows3 = lambda a: a.reshape(batch, seq, 2 * LANES)
    smem = lambda a: a[:, :, :, 0, :].reshape(batch * g * nqb, 1, a.shape[-1])
    attn = _slc_win_attention(smem(lst), smem(cnt), qt, rows3(ks), rows3(vs),
                              rows3(kw), rows3(vw), madd, ocmp, gate_t, wtab)
    return _mix_ffn(x2, attn.reshape(batch * seq, ATTN_WIDTH), u, bgate, p["cw"], p["go"], p["wo"],
                    p["gf"], p["wu"], p["wd"], seq=seq)


def kernel(x, g_mix_norm, w_in, g_q, g_k, pe_cmp, w_cmp1, b_cmp1, w_cmp2, b_cmp2, conv_w, g_out, w_o,
           g_ffn_norm, w_up, w_down):
    batch, seq, d = x.shape
    assert d == D_MODEL and seq % (CMP_CHUNK * CMP_STRIDE) == 0 and seq >= WINDOW + Q_BLOCK
    params = _prep_params(g_mix_norm, w_in, g_q, g_k, pe_cmp, w_cmp1, b_cmp1, w_cmp2, b_cmp2,
                          conv_w, g_out, w_o, g_ffn_norm, w_up, w_down)
    consts = _constants(seq // SEL_BLOCK)

    def step(x2, p):
        return _layer(x2, p, consts, batch=batch, seq=seq), None

    x2, _ = lax.scan(step, x.reshape(batch * seq, d), params)
    return x2.reshape(batch, seq, d)
```

```python
import functools

import numpy as np
import jax
import jax.numpy as jnp
from jax import lax
from jax.experimental import pallas as pl
from jax.experimental.pallas import tpu as pltpu

f32 = jnp.float32
bf16 = jnp.bfloat16
i32 = jnp.int32

D_MODEL = 1024
HEAD_DIM = 64
N_HEADS = 8
N_GROUPS = 2
HEADS_PER_GROUP = 4
ATTN_WIDTH = 512
CONV_WIDTH = 512
N_BRANCH = 3
CMP_BLOCK = 32
CMP_STRIDE = 16
CMP_HIDDEN = 256
SEL_BLOCK = 64
SEL_TOPK = 16
WINDOW = 512
Q_BLOCK = 128
D_FF = 4096
EPS = 1e-6
NEG = -1e30
LOG2E = 1.4426950408889634
SLOPE_TERMS = 3
PICKED = -3e38
LANES = 128
CMP_CHUNK = 256
SEL_PER_CHUNK = CMP_CHUNK * CMP_STRIDE // SEL_BLOCK
IMP_ROWS = SEL_PER_CHUNK + 8
VMEM_LIMIT = 56 * 1024 * 1024

PROJ_CHUNK = 512
SUP_BLOCKS = 32
HEAD_BLOCKS = 20
GROUP_BLOCKS = 4
QB_PER_STEP = 4
CMP_QB = 2


def _dot(a, b):
    return jnp.dot(a, b, preferred_element_type=f32)


def _dot_nt(a, b):
    return lax.dot_general(a, b, (((1,), (1,)), ((), ())), preferred_element_type=f32)


def _dot_tn(a, b):
    return lax.dot_general(a, b, (((0,), (0,)), ((), ())), preferred_element_type=f32)


def _head_norm(z, e, g):
    sq = (z * z).astype(bf16)
    outs = []
    for c in range(z.shape[1] // LANES):
        sl = slice(c * LANES, (c + 1) * LANES)
        outs.append(z[:, sl] * lax.rsqrt(_dot(sq[:, sl], e) * (1.0 / HEAD_DIM) + EPS))
    y = outs[0] if len(outs) == 1 else jnp.concatenate(outs, axis=1)
    return y * g


def _pos_cols(pos, width):
    rows = pos.shape[0]
    lane = lax.broadcasted_iota(i32, (rows, LANES), 1) - HEAD_DIM
    tile = jnp.where(lane < 0, 0.0, jnp.where(lane < SLOPE_TERMS, (pos >> 7).astype(f32),
                                              jnp.where(lane < 2 * SLOPE_TERMS, (pos & 127).astype(f32), 0.0)))
    return tile if width == LANES else jnp.concatenate([tile] * (width // LANES), axis=1)


def _ones_col(rows, width):
    lane = lax.broadcasted_iota(i32, (rows, LANES), 1)
    tile = jnp.where(lane == HEAD_DIM, 1.0, 0.0)
    return tile if width == LANES else jnp.concatenate([tile] * (width // LANES), axis=1)


def _inproj_body(layer_ref, x_ref, gm_ref, w_ref, gq_ref, gks_ref, gkw_ref, e_ref, tail_ref,
                 qt_ref, kc_ref, vc_ref, ks_ref, kw_ref, vs_ref, vw_ref, u_ref, bg_ref, gate_ref,
                 *, tm, seq):
    x = x_ref[...]
    ms = jnp.mean(x * x, axis=-1, keepdims=True)
    h = ((x * lax.rsqrt(ms + EPS)) * gm_ref[...]).astype(bf16)
    e = e_ref[...]
    two = 2 * LANES

    def proj(k):
        return _dot(h, w_ref[:, k * PROJ_CHUNK:(k + 1) * PROJ_CHUNK])

    def emit_queries(z):
        qn = _head_norm(z, e, gq_ref[...]) * (HEAD_DIM ** -0.5 * LOG2E)
        gw = HEADS_PER_GROUP * HEAD_DIM
        for cb in range(tm // Q_BLOCK):
            for g in range(N_GROUPS):
                blk = qn[cb * Q_BLOCK:(cb + 1) * Q_BLOCK, g * gw:(g + 1) * gw].T
                top = jnp.concatenate([blk[r * HEAD_DIM:(r + 1) * HEAD_DIM] for r in range(HEADS_PER_GROUP)],
                                      axis=1)
                qt_ref[cb, g] = jnp.concatenate([top.astype(bf16), tail_ref[g]], axis=0)

    def emit_compress_inputs(z):
        kc_ref[...] = z[:, 0:LANES].astype(bf16)
        vc_ref[...] = z[:, LANES:two].astype(bf16)
        gate_ref[...] = 1.0 / (1.0 + jnp.exp(-z[:, two:two + LANES]))

    def emit_keys(z):
        t0 = (pl.program_id(0) * tm) % seq
        pc = _pos_cols(lax.broadcasted_iota(i32, (tm, LANES), 0) + t0, two)
        ks_ref[...] = (_head_norm(z[:, 0:two], e, gks_ref[...]) + pc).astype(bf16)
        kw_ref[...] = (_head_norm(z[:, two:], e, gkw_ref[...]) + pc).astype(bf16)

    def emit_values(z):
        ones = _ones_col(tm, two)
        vs_ref[...] = (z[:, 0:two] + ones).astype(bf16)
        vw_ref[...] = (z[:, two:] + ones).astype(bf16)

    z0 = proj(0)
    z1 = proj(1)
    emit_queries(z0)
    z2 = proj(2)
    emit_compress_inputs(z1)
    z3 = proj(3)
    emit_keys(z2)
    z4 = proj(4)
    emit_values(z3)
    z5 = proj(5)
    z6 = proj(6)
    u_ref[...] = (z4 * z5).astype(bf16)
    bg_ref[...] = z6.astype(bf16)


def _layer_weight(a):
    return pl.BlockSpec((None,) + a.shape[1:], lambda i, layer: (layer[0],) + (0,) * (a.ndim - 1))


def _inproj(layer, x2, gm, w_all, gq, gks, gkw, e, tail, *, seq, tm=512):
    n = x2.shape[0]
    width = HEADS_PER_GROUP * Q_BLOCK
    row = lambda w_: pl.BlockSpec((tm, w_), lambda i, layer: (i, 0))
    full = lambda a: pl.BlockSpec(a.shape, lambda i, layer: (0,) * a.ndim)
    outs = [(LANES, bf16), (LANES, bf16), (2 * LANES, bf16), (2 * LANES, bf16),
            (2 * LANES, bf16), (2 * LANES, bf16), (CONV_WIDTH, bf16), (CONV_WIDTH, bf16), (LANES, f32)]
    qt_spec = pl.BlockSpec((tm // Q_BLOCK, N_GROUPS, LANES, width), lambda i, layer: (i, 0, 0, 0))
    qt_shape = jax.ShapeDtypeStruct((n // Q_BLOCK, N_GROUPS, LANES, width), bf16)
    return pl.pallas_call(
        functools.partial(_inproj_body, tm=tm, seq=seq),
        grid_spec=pltpu.PrefetchScalarGridSpec(
            num_scalar_prefetch=1, grid=(n // tm,),
            in_specs=[row(D_MODEL), full(gm), _layer_weight(w_all), full(gq), full(gks), full(gkw), full(e),
                      full(tail)],
            out_specs=[qt_spec] + [row(w_) for w_, _ in outs]),
        out_shape=[qt_shape] + [jax.ShapeDtypeStruct((n, w_), dt) for w_, dt in outs],
        compiler_params=pltpu.CompilerParams(dimension_semantics=("arbitrary",),
                                             vmem_limit_bytes=VMEM_LIMIT),
        name="inproj",
    )(layer, x2, gm, w_all, gq, gks, gkw, e, tail)


def _gelu_tanh(x):
    return 0.5 * x * (1.0 + jnp.tanh(0.7978845608028654 * (x + 0.044715 * (x * x * x))))


def _compress_body(layer_ref, zk_ref, zv_ref, w1_ref, pe_ref, b1_ref, w2k_ref, w2v_ref, b2k_ref, b2v_ref,
                   gk_ref, e_ref, kc_ref, vc_ref, *, tc):
    last = lax.broadcasted_iota(i32, (tc, 1), 0) == tc - 1

    def hidden(z_ref, kind):
        z = z_ref[0]
        a = _dot(z, w1_ref[kind, 0])
        b = _dot(z, w1_ref[kind, 1])
        b = jnp.concatenate([b[1:], jnp.zeros((1, b.shape[1]), f32)], axis=0)
        bias = (_dot(pe_ref[kind, 0], w1_ref[kind, 0]) + _dot(pe_ref[kind, 1], w1_ref[kind, 1]))[0:1]
        return _gelu_tanh(a + b + bias + b1_ref[kind]).astype(bf16)

    k = _dot(hidden(zk_ref, 0), w2k_ref[...]) + b2k_ref[...]
    k = _head_norm(k, e_ref[...], gk_ref[...])
    pos = lax.broadcasted_iota(i32, (tc, LANES), 0) * CMP_STRIDE + (CMP_BLOCK - 1)
    k = k + _pos_cols(pos, 2 * LANES)
    kc_ref[0] = jnp.where(last, 0.0, k).astype(bf16)
    v = _dot(hidden(zv_ref, 1), w2v_ref[...]) + b2v_ref[...]
    vc_ref[0] = jnp.where(last, 0.0, v + _ones_col(tc, 2 * LANES)).astype(bf16)


def _compress(layer, zk, zv, w1_all, pe, b1, w2k, w2v, b2k, b2v, gk, e):
    b, tc, _ = zk.shape
    blk = lambda a: pl.BlockSpec((1,) + a.shape[1:], lambda i, layer: (i,) + (0,) * (a.ndim - 1))
    full = lambda a: pl.BlockSpec(a.shape, lambda i, layer: (0,) * a.ndim)
    return pl.pallas_call(
        functools.partial(_compress_body, tc=tc),
        grid_spec=pltpu.PrefetchScalarGridSpec(
            num_scalar_prefetch=1, grid=(b,),
            in_specs=[blk(zk), blk(zv), _layer_weight(w1_all)]
            + [full(a) for a in (pe, b1, w2k, w2v, b2k, b2v, gk, e)],
            out_specs=[pl.BlockSpec((1, tc, 2 * LANES), lambda i, layer: (i, 0, 0))] * 2),
        out_shape=[jax.ShapeDtypeStruct((b, tc, 2 * LANES), bf16)] * 2,
        compiler_params=pltpu.CompilerParams(dimension_semantics=("arbitrary",),
                                             vmem_limit_bytes=VMEM_LIMIT),
        name="compress",
    )(layer, zk, zv, w1_all, pe, b1, w2k, w2v, b2k, b2v, gk, e)


def _query_pos(c):
    lane = lax.broadcasted_iota(i32, (1, HEADS_PER_GROUP * Q_BLOCK), 1)
    return c * Q_BLOCK + (lane & (Q_BLOCK - 1))


def _pick_top(vs, jf, n_pick):
    vs = list(vs)
    for _ in range(n_pick):
        for g, v in enumerate(vs):
            mx = jnp.max(v, axis=0, keepdims=True)
            idx = jnp.min(jnp.where(v == mx, jf, float(jf.shape[0])), axis=0, keepdims=True)
            vs[g] = jnp.where(jf == idx, PICKED, v)
    return tuple(vs)


def _cmp_variant(nchunk, c0, qt_ref, kc_ref, vc_ref, mct_ref, tri_ref, o_ref, madd_ref, lst_ref, cnt_ref,
                 *, nsel):
    units = [(qb, g) for qb in range(CMP_QB) for g in range(N_GROUPS)]
    nrow = nchunk * CMP_CHUNK
    npre = nchunk * SEL_PER_CHUNK
    tqs = [_query_pos(c0 + qb) for qb in range(CMP_QB)]
    sub = lax.broadcasted_iota(i32, (CMP_CHUNK, 1), 0)
    lanes = lambda g: slice(g * LANES, (g + 1) * LANES)
    chunk = lambda a, k: a[k * CMP_CHUNK:(k + 1) * CMP_CHUNK]

    tiles = []
    for qb, g in units:
        s = _dot(kc_ref[0, 0:nrow, lanes(g)], qt_ref[qb, g])
        row = []
        for k in range(nchunk):
            t = chunk(s, k)
            if k >= nchunk - 2:
                seen = (k * CMP_CHUNK + sub) * CMP_STRIDE + (CMP_BLOCK - 1) <= tqs[qb]
                t = jnp.where(seen, t, NEG)
            row.append(t)
        tiles.append(row)

    accs, imps = [], []
    for u, (qb, g) in enumerate(units):
        m = tiles[u][0].max(axis=0, keepdims=True)
        for t in tiles[u][1:]:
            m = jnp.maximum(m, t.max(axis=0, keepdims=True))
        parts = [jnp.exp2(t - m).astype(bf16) for t in tiles[u]]
        accs.append(_dot_tn(vc_ref[0, 0:nrow, lanes(g)], jnp.concatenate(parts, axis=0)))
        rows, carry = [], None
        for part in parts:
            piece = _dot(mct_ref[...], part)
            body = piece[0:SEL_PER_CHUNK]
            if carry is not None:
                body = jnp.concatenate([body[0:8] + carry, body[8:]], axis=0)
            rows.append(body)
            carry = piece[SEL_PER_CHUNK:IMP_ROWS]
        imps.append(rows[0] if nchunk == 1 else jnp.concatenate(rows, axis=0))

    j = lax.broadcasted_iota(i32, (npre, Q_BLOCK), 0)
    jf = j.astype(f32)
    t1s = [tq[:, 0:Q_BLOCK] for tq in tqs]
    valids = [j * SEL_BLOCK <= t1 for t1 in t1s]
    vs = []
    for u, (qb, g) in enumerate(units):
        any_key = tqs[qb] >= CMP_BLOCK - 1
        inv = jnp.where(any_key, 1.0 / jnp.maximum(accs[u][HEAD_DIM:HEAD_DIM + 1], 1e-30), 0.0)
        o_ref[0, g, qb] = accs[u][0:HEAD_DIM] * inv
        imp4 = imps[u] * inv
        imp = imp4[:, 0:Q_BLOCK]
        for r in range(1, HEADS_PER_GROUP):
            imp = imp + imp4[:, r * Q_BLOCK:(r + 1) * Q_BLOCK]
        jt = t1s[qb] >> 6
        v = jnp.where(valids[qb], imp, NEG)
        vs.append(jnp.where(j == 0, PICKED, jnp.where(j == jt, PICKED, jnp.where(j == jt - 1, PICKED, v))))
    vs = _pick_top(vs, jf, SEL_TOPK - 3)

    rr = lax.broadcasted_iota(i32, (nsel, nsel), 0).astype(f32)
    ones8 = jnp.ones((8, Q_BLOCK), bf16)
    jrow = jnp.broadcast_to(lax.broadcasted_iota(i32, (1, nsel), 1).astype(f32), (8, nsel)).astype(bf16)
    for u, (qb, g) in enumerate(units):
        picked = vs[u] == PICKED
        madd = jnp.where(valids[qb], jnp.where(picked, 0.0, NEG), NEG)
        sel = jnp.where(valids[qb], jnp.where(picked, 1.0, 0.0), 0.0).astype(bf16)
        if npre < nsel:
            madd = jnp.concatenate([madd, jnp.full((nsel - npre, Q_BLOCK), NEG, f32)], axis=0)
            sel = jnp.concatenate([sel, jnp.zeros((nsel - npre, Q_BLOCK), bf16)], axis=0)
        madd_ref[0, g, qb] = madd
        flag = jnp.where(_dot_nt(ones8, sel)[0:1] > 0.0, 1.0, 0.0)
        flag8 = jnp.broadcast_to(flag, (8, nsel)).astype(bf16)
        prefix = _dot(flag8, tri_ref[...])[0:1]
        place = jnp.where(prefix == rr, flag, 0.0).astype(bf16)
        lst_ref[0, g, qb] = _dot_nt(jrow, place).astype(i32)
        cnt_ref[0, g, qb] = _dot(flag8, jnp.ones((nsel, LANES), bf16)).astype(i32)


def _cmp_body(qt_ref, kc_ref, vc_ref, mct_ref, tri_ref, o_ref, madd_ref, lst_ref, cnt_ref, *, nsel, nvar):
    c0 = pl.program_id(1) * CMP_QB
    c_last = c0 + CMP_QB - 1
    nch = (c_last * (Q_BLOCK // CMP_STRIDE) + (Q_BLOCK - CMP_BLOCK) // CMP_STRIDE) // CMP_CHUNK + 1
    for n in range(1, nvar + 1):
        pl.when(nch == n)(functools.partial(
            _cmp_variant, n, c0, qt_ref, kc_ref, vc_ref, mct_ref, tri_ref, o_ref, madd_ref, lst_ref, cnt_ref,
            nsel=nsel))


def _cmp_attention(qt, kc, vc, mct, tri, *, batch):
    g = N_GROUPS
    nqb = qt.shape[0] // batch
    tc = kc.shape[1]
    nsel = tc * CMP_STRIDE // SEL_BLOCK
    width = HEADS_PER_GROUP * Q_BLOCK
    nstep = nqb // CMP_QB
    per_q = lambda r_, c_: pl.BlockSpec((1, g, CMP_QB, r_, c_), lambda bi, ci: (bi, 0, ci, 0, 0))
    shape = lambda r_, c_, dt: jax.ShapeDtypeStruct((batch, g, nqb, r_, c_), dt)
    return pl.pallas_call(
        functools.partial(_cmp_body, nsel=nsel, nvar=tc // CMP_CHUNK),
        grid=(batch, nstep),
        in_specs=[pl.BlockSpec((CMP_QB, g, LANES, width), lambda bi, ci: (bi * nstep + ci, 0, 0, 0)),
                  pl.BlockSpec((1, tc, 2 * LANES), lambda bi, ci: (bi, 0, 0)),
                  pl.BlockSpec((1, tc, 2 * LANES), lambda bi, ci: (bi, 0, 0)),
                  pl.BlockSpec(mct.shape, lambda bi, ci: (0, 0)),
                  pl.BlockSpec(tri.shape, lambda bi, ci: (0, 0))],
        out_specs=[per_q(HEAD_DIM, width), per_q(nsel, Q_BLOCK), per_q(8, nsel), per_q(8, LANES)],
        out_shape=[shape(HEAD_DIM, width, f32), shape(nsel, Q_BLOCK, f32),
                   shape(8, nsel, i32), shape(8, LANES, i32)],
        compiler_params=pltpu.CompilerParams(dimension_semantics=("arbitrary",) * 2,
                                             vmem_limit_bytes=VMEM_LIMIT),
        name="cmp_attention",
    )(qt, kc, vc, mct, tri)


def _normalize(o_aug):
    return o_aug[0:HEAD_DIM] * (1.0 / jnp.maximum(o_aug[HEAD_DIM:HEAD_DIM + 1], 1e-30))


def _slc_win_body(lst_ref, cnt_ref, qt_ref, ks_ref, vs_ref, kw_ref, vw_ref,
                  madd_ref, ocmp_ref, gate_ref, wtab_ref, out_ref, s_scr, *, nsel):
    width = HEADS_PER_GROUP * Q_BLOCK
    nwin = (WINDOW + Q_BLOCK) // LANES
    wq = WINDOW // Q_BLOCK
    grp_rows = GROUP_BLOCKS * SEL_BLOCK

    class Block:
        def __init__(self, qb):
            self.qb = qb
            self.c = pl.program_id(2) * QB_PER_STEP + qb
            self.q0 = pl.multiple_of(self.c * Q_BLOCK, Q_BLOCK)
            self.qta = qt_ref[qb, 0]
            self.n_off = cnt_ref[qb, 0, 0] - 2

        def block_ids(self, first, nblk):
            ids = []
            for u in range(nblk):
                i = first + u
                j = lst_ref[self.qb, 0, jnp.minimum(i, nsel - 1)]
                ids.append((i < self.n_off, j, pl.multiple_of(j * SEL_BLOCK, SEL_BLOCK)))
            return ids

        def scores(self, ids):
            kcat = jnp.concatenate([ks_ref[0, pl.ds(r0, SEL_BLOCK), :] for _, _, r0 in ids], axis=0)
            sg = _dot(kcat, self.qta)
            tiles = []
            for u, (live, j, _) in enumerate(ids):
                mrow = jnp.where(live, madd_ref[0, 0, self.qb, pl.ds(j, 1), :], NEG)
                mrow = jnp.concatenate([mrow] * HEADS_PER_GROUP, axis=1)
                tiles.append(sg[u * SEL_BLOCK:(u + 1) * SEL_BLOCK] + mrow)
            return tiles

        def values(self, ids):
            return jnp.concatenate([vs_ref[0, pl.ds(r0, SEL_BLOCK), :] for _, _, r0 in ids], axis=0)

    def front(blk):
        c, q0, qta = blk.c, blk.q0, blk.qta
        ws = pl.multiple_of(jnp.maximum(c - wq, 0) * Q_BLOCK, Q_BLOCK)
        s = _dot(kw_ref[0, pl.ds(ws, WINDOW + Q_BLOCK), :], qta)
        chunks = []
        for k in range(nwin):
            steady = 1 if k == 0 else (2 if k == nwin - 1 else 0)
            tab = jnp.where(c >= wq, steady, jnp.where(k < c, 0, jnp.where(k == c, 2, 3)))
            chunks.append(s[k * LANES:(k + 1) * LANES] + wtab_ref[tab])
        head = blk.block_ids(0, HEAD_BLOCKS)
        tiles = [_dot(ks_ref[0, pl.ds(q0, Q_BLOCK), :], qta) + wtab_ref[2]] + blk.scores(head)
        return ws, chunks, head, tiles

    def softmax_pv(tiles, v):
        m = tiles[0].max(axis=0, keepdims=True)
        for su in tiles[1:]:
            m = jnp.maximum(m, su.max(axis=0, keepdims=True))
        p = jnp.concatenate([jnp.exp2(su - m).astype(bf16) for su in tiles], axis=0)
        return m, _dot_tn(v, p)

    def middle(blk, ws, chunks, head, tiles):
        _, o_win = softmax_pv(chunks, vw_ref[0, pl.ds(ws, WINDOW + Q_BLOCK), :])
        m_run, o_run = softmax_pv(
            tiles, jnp.concatenate([vs_ref[0, pl.ds(blk.q0, Q_BLOCK), :], blk.values(head)], axis=0))
        return _normalize(o_win), m_run, o_run

    def rest(blk, m_run, o_run):
        def segment(si, carry):
            m_run, o_run = carry
            base = HEAD_BLOCKS + si * SUP_BLOCKS
            ngrp = (jnp.minimum(SUP_BLOCKS, blk.n_off - base) + GROUP_BLOCKS - 1) // GROUP_BLOCKS
            rows = lambda gi: pl.ds(pl.multiple_of(gi * grp_rows, grp_rows), grp_rows)

            def score(gi, mx):
                sg = jnp.concatenate(blk.scores(blk.block_ids(base + gi * GROUP_BLOCKS, GROUP_BLOCKS)), axis=0)
                s_scr[rows(gi), :] = sg
                return jnp.maximum(mx, sg.max(axis=0, keepdims=True))

            m_new = lax.fori_loop(0, ngrp, score, m_run)

            def weigh(gi, acc):
                pg = jnp.exp2(s_scr[rows(gi), :] - m_new).astype(bf16)
                return acc + _dot_tn(blk.values(blk.block_ids(base + gi * GROUP_BLOCKS, GROUP_BLOCKS)), pg)

            o_seg = lax.fori_loop(0, ngrp, weigh, jnp.zeros((LANES, width), f32))
            return m_new, jnp.exp2(m_run - m_new) * o_run + o_seg

        nseg = (jnp.maximum(blk.n_off - HEAD_BLOCKS, 0) + SUP_BLOCKS - 1) // SUP_BLOCKS
        return lax.fori_loop(0, nseg, segment, (m_run, o_run))[1]

    blocks = [Block(qb) for qb in range(QB_PER_STEP)]
    fronts = [front(blk) for blk in blocks]
    fronts = [middle(blk, *f) for blk, f in zip(blocks, fronts)]
    tails = [rest(blk, m_run, o_run) for blk, (_, m_run, o_run) in zip(blocks, fronts)]
    for blk, (o_win, _, _), o_run in zip(blocks, fronts, tails):
        qb = blk.qb
        gate = gate_ref[0, 0, qb]
        mix = gate[0:1] * ocmp_ref[0, 0, qb] + gate[1:2] * _normalize(o_run) + gate[2:3] * o_win
        rows = jnp.concatenate([mix[:, r * Q_BLOCK:(r + 1) * Q_BLOCK] for r in range(HEADS_PER_GROUP)],
                               axis=0)
        out_ref[0, qb * Q_BLOCK:(qb + 1) * Q_BLOCK, :] = rows.T.astype(bf16)


def _slc_win_attention(lst, cnt, qt, ks, vs, kw, vw, madd, ocmp, gate_t, wtab):
    b, t, _ = ks.shape
    g = N_GROUPS
    nqb, nsel = t // Q_BLOCK, t // SEL_BLOCK
    width = HEADS_PER_GROUP * Q_BLOCK
    nstep = nqb // QB_PER_STEP
    flat = lambda bi, gi, ci: (bi * g + gi) * nstep + ci
    smem = lambda w_: pl.BlockSpec((QB_PER_STEP, 1, w_), lambda bi, gi, ci: (flat(bi, gi, ci), 0, 0),
                                   memory_space=pltpu.SMEM)
    per_q = lambda r_, c_: pl.BlockSpec((1, 1, QB_PER_STEP, r_, c_), lambda bi, gi, ci: (bi, gi, ci, 0, 0))
    keys = pl.BlockSpec((1, t, LANES), lambda bi, gi, ci: (bi, 0, gi))
    return pl.pallas_call(
        functools.partial(_slc_win_body, nsel=nsel),
        grid=(b, g, nstep),
        in_specs=[smem(nsel), smem(LANES),
                  pl.BlockSpec((QB_PER_STEP, 1, LANES, width), lambda bi, gi, ci: (bi * nstep + ci, gi, 0, 0)),
                  keys, keys, keys, keys,
                  per_q(nsel, Q_BLOCK), per_q(HEAD_DIM, width), per_q(8, width),
                  pl.BlockSpec(wtab.shape, lambda bi, gi, ci: (0, 0, 0))],
        out_specs=pl.BlockSpec((1, QB_PER_STEP * Q_BLOCK, 2 * LANES), lambda bi, gi, ci: (bi, ci, gi)),
        out_shape=jax.ShapeDtypeStruct((b, t, ATTN_WIDTH), bf16),
        scratch_shapes=[pltpu.VMEM((SUP_BLOCKS * SEL_BLOCK, width), f32)],
        compiler_params=pltpu.CompilerParams(dimension_semantics=("arbitrary",) * 3,
                                             vmem_limit_bytes=VMEM_LIMIT),
        name="slc_win_attention",
    )(lst, cnt, qt, ks, vs, kw, vw, madd, ocmp, gate_t, wtab)


PREV_ROWS = 16


def _rms(v, g):
    return (v * lax.rsqrt(jnp.mean(v * v, axis=-1, keepdims=True) + EPS)) * g


def _mix_ffn_body(layer_ref, x_ref, attn_ref, u_ref, uprev_ref, bg_ref, cw_ref, go_ref, wo_ref, gf_ref, wu_ref,
                  wd_ref,
                  out_ref, *, tm, seq, chunk):
    first = (pl.program_id(0) * tm) % seq == 0
    u = u_ref[...].astype(f32)
    prev = jnp.where(first, 0.0, uprev_ref[0].astype(f32))
    ext = jnp.concatenate([prev, u], axis=0)
    cw = cw_ref[...]
    conv = (cw[0:1] * ext[PREV_ROWS - 2:PREV_ROWS - 2 + tm] + cw[1:2] * ext[PREV_ROWS - 1:PREV_ROWS - 1 + tm]
            + cw[2:3] * u)
    conv = bg_ref[...].astype(f32) * conv
    go = go_ref[...]
    mixed = jnp.concatenate([_rms(attn_ref[...].astype(f32), go[:, :ATTN_WIDTH]),
                             _rms(conv, go[:, ATTN_WIDTH:])], axis=1).astype(bf16)
    x = x_ref[...] + _dot(mixed, wo_ref[...])
    h = _rms(x, gf_ref[...]).astype(bf16)
    acc = x
    for c in range(D_FF // chunk):
        a = jnp.maximum(_dot(h, wu_ref[:, c * chunk:(c + 1) * chunk]), 0.0)
        acc = acc + _dot((a * a).astype(bf16), wd_ref[c * chunk:(c + 1) * chunk, :])
    out_ref[...] = acc


def _mix_ffn(layer, x2, attn2, u, bgate, cw, go, wo_all, gf, wu_all, wd_all, *, seq, tm=512, chunk=1024):
    n = x2.shape[0]
    uprev = u.reshape(n // PREV_ROWS, PREV_ROWS, CONV_WIDTH)
    row = lambda w_: pl.BlockSpec((tm, w_), lambda i, layer: (i, 0))
    full = lambda a: pl.BlockSpec(a.shape, lambda i, layer: (0,) * a.ndim)
    once = lambda a: pl.BlockSpec((None,) + a.shape[1:], lambda i, layer: (layer[0],) + (0,) * (a.ndim - 1),
                                  pipeline_mode=pl.Buffered(1))
    return pl.pallas_call(
        functools.partial(_mix_ffn_body, tm=tm, seq=seq, chunk=chunk),
        grid_spec=pltpu.PrefetchScalarGridSpec(
            num_scalar_prefetch=1, grid=(n // tm,),
            in_specs=[row(D_MODEL), row(ATTN_WIDTH), row(CONV_WIDTH),
                      pl.BlockSpec((1, PREV_ROWS, CONV_WIDTH),
                                   lambda i, layer: (jnp.maximum(i * (tm // PREV_ROWS) - 1, 0), 0, 0)),
                      row(CONV_WIDTH), full(cw), full(go), once(wo_all), full(gf), once(wu_all), once(wd_all)],
            out_specs=row(D_MODEL)),
        out_shape=jax.ShapeDtypeStruct((n, D_MODEL), f32),
        input_output_aliases={1: 0},
        compiler_params=pltpu.CompilerParams(dimension_semantics=("arbitrary",),
                                             vmem_limit_bytes=VMEM_LIMIT),
        name="mix_ffn",
    )(layer, x2, attn2, u, uprev, bgate, cw, go, wo_all, gf, wu_all, wd_all)


def _interleave_zero(w):
    z = jnp.zeros(w.shape[:-1] + (HEAD_DIM,), w.dtype)
    return jnp.concatenate([w[..., :HEAD_DIM], z, w[..., HEAD_DIM:], z], axis=-1)


def _prep_params(g_mix_norm, w_in, g_q, g_k, pe_cmp, w_cmp1, b_cmp1, w_cmp2, b_cmp2,
                 conv_w, g_out, w_o, g_ffn_norm, w_up, w_down):
    depth = w_in.shape[0]
    o = np.cumsum([0, ATTN_WIDTH] + [LANES] * 6 + [N_BRANCH * N_HEADS] + [CONV_WIDTH] * 3)
    part = lambda i: w_in[..., int(o[i]):int(o[i + 1])]
    q, kc, vc, ks, vs, kw, vw, gl, hc, cg, bg = [part(i) for i in range(11)]
    gl = jnp.pad(gl, ((0, 0), (0, 0), (0, LANES - gl.shape[-1])))
    w = jnp.concatenate([q, kc, vc, gl, jnp.zeros_like(gl), _interleave_zero(ks), _interleave_zero(kw),
                         _interleave_zero(vs), _interleave_zero(vw), hc, cg, bg], axis=-1).astype(bf16)
    tile2 = lambda gk: _interleave_zero(jnp.concatenate([gk, gk], axis=-1))[:, None, :]
    w1 = w_cmp1.astype(bf16).reshape(depth, 2, 2, CMP_STRIDE, HEAD_DIM, CMP_HIDDEN)
    z1 = jnp.zeros_like(w1)
    w1g = jnp.stack([jnp.concatenate([w1, z1], axis=-1), jnp.concatenate([z1, w1], axis=-1)], axis=4)
    w1g = w1g.reshape(depth, 2, 2, CMP_STRIDE * LANES, N_GROUPS * CMP_HIDDEN)
    pe = pe_cmp.reshape(depth, 2, 2, CMP_STRIDE, 1, HEAD_DIM)
    pe = jnp.broadcast_to(pe, (depth, 2, 2, CMP_STRIDE, N_GROUPS, HEAD_DIM)).reshape(depth, 2, 2, 1, -1)
    pe = jnp.pad(pe, ((0, 0), (0, 0), (0, 0), (0, 7), (0, 0))).astype(bf16)
    b1 = jnp.concatenate([b_cmp1, b_cmp1], axis=-1)[:, :, None, :]
    z2 = jnp.zeros_like(w_cmp2)
    w2 = jnp.concatenate([jnp.concatenate([w_cmp2, z2], axis=-1), jnp.concatenate([z2, w_cmp2], axis=-1)],
                         axis=2)
    w2 = _interleave_zero(w2).astype(bf16)
    b2 = _interleave_zero(jnp.concatenate([b_cmp2, b_cmp2], axis=-1))[:, :, None, :]
    return dict(
        gm=g_mix_norm[:, None, :], w=w,
        gq=jnp.tile(g_q, (1, N_HEADS))[:, None, :],
        gks=tile2(g_k[:, 1]), gkw=tile2(g_k[:, 2]), gkc=tile2(g_k[:, 0]),
        w1=w1g, pe=pe, b1=b1,
        w2k=w2[:, 0], w2v=w2[:, 1], b2k=b2[:, 0], b2v=b2[:, 1],
        cw=jnp.pad(conv_w, ((0, 0), (0, 8 - conv_w.shape[1]), (0, 0))),
        go=g_out[:, None, :], wo=w_o.astype(bf16),
        gf=g_ffn_norm[:, None, :], wu=w_up.astype(bf16), wd=w_down.astype(bf16),
    )


def _constants(nsel):
    lane = np.arange(LANES)
    e = (lane[:, None] // HEAD_DIM == lane[None, :] // HEAD_DIM).astype(np.float32)
    tail = np.zeros((N_GROUPS, HEAD_DIM, HEADS_PER_GROUP * Q_BLOCK), np.float32)
    for gi in range(N_GROUPS):
        for r in range(HEADS_PER_GROUP):
            rest = np.float64(2.0 ** -(gi * HEADS_PER_GROUP + r + 1)) * LOG2E
            for k in range(SLOPE_TERMS):
                term = np.float64(np.asarray(rest, np.float32).astype(jnp.bfloat16).astype(np.float32))
                tail[gi, k, r * Q_BLOCK:(r + 1) * Q_BLOCK] = term * LANES
                tail[gi, SLOPE_TERMS + k, r * Q_BLOCK:(r + 1) * Q_BLOCK] = term
                rest = rest - term
    mct = np.zeros((IMP_ROWS, CMP_CHUNK), np.float32)
    for i in range(CMP_CHUNK):
        lo, hi = i * CMP_STRIDE, i * CMP_STRIDE + CMP_BLOCK
        for jj in range(SEL_PER_CHUNK + 1):
            ov = min(hi, (jj + 1) * SEL_BLOCK) - max(lo, jj * SEL_BLOCK)
            if ov > 0:
                mct[jj, i] = ov / CMP_BLOCK
    kk = np.arange(LANES)[:, None]
    ql = np.tile(np.arange(Q_BLOCK), HEADS_PER_GROUP)[None, :]
    wtab = np.zeros((4, LANES, HEADS_PER_GROUP * Q_BLOCK), np.float32)
    wtab[1] = np.where(kk > ql, 0.0, NEG)
    wtab[2] = np.where(kk <= ql, 0.0, NEG)
    wtab[3] = NEG
    tri = np.arange(nsel)[:, None] < np.arange(nsel)[None, :]
    return (jnp.asarray(e, bf16), jnp.asarray(tail, bf16), jnp.asarray(mct, bf16), jnp.asarray(wtab),
            jnp.asarray(tri, bf16))


def _layer(x2, layer, p, big, consts, *, batch, seq):
    e, tail, mct, wtab, tri = consts
    g = N_GROUPS
    nqb, nsel, tc = seq // Q_BLOCK, seq // SEL_BLOCK, seq // CMP_STRIDE
    qt, kc, vc, ks, kw, vs, vw, u, bgate, gates = _inproj(
        layer, x2, p["gm"], big["w"], p["gq"], p["gks"], p["gkw"], e, tail, seq=seq)
    kcmp, vcmp = _compress(layer, kc.reshape(batch, tc, CMP_STRIDE * LANES),
                           vc.reshape(batch, tc, CMP_STRIDE * LANES),
                           big["w1"], p["pe"], p["b1"], p["w2k"], p["w2v"], p["b2k"], p["b2v"], p["gkc"], e)
    gate_t = gates[:, :N_HEADS * N_BRANCH].reshape(batch, nqb, Q_BLOCK, g, HEADS_PER_GROUP, N_BRANCH)
    gate_t = gate_t.transpose(0, 3, 1, 5, 4, 2).reshape(batch, g, nqb, N_BRANCH, HEADS_PER_GROUP * Q_BLOCK)
    gate_t = jnp.pad(gate_t, ((0, 0), (0, 0), (0, 0), (0, 8 - N_BRANCH), (0, 0)))
    ocmp, madd, lst, cnt = _cmp_attention(qt, kcmp, vcmp, mct, tri, batch=batch)
    rows3 = lambda a: a.reshape(batch, seq, 2 * LANES)
    smem = lambda a: a[:, :, :, 0, :].reshape(batch * g * nqb, 1, a.shape[-1])
    attn = _slc_win_attention(smem(lst), smem(cnt), qt, rows3(ks), rows3(vs),
                              rows3(kw), rows3(vw), madd, ocmp, gate_t, wtab)
    return _mix_ffn(layer, x2, attn.reshape(batch * seq, ATTN_WIDTH), u, bgate, p["cw"], p["go"], big["wo"],
                    p["gf"], big["wu"], big["wd"], seq=seq)


def kernel(x, g_mix_norm, w_in, g_q, g_k, pe_cmp, w_cmp1, b_cmp1, w_cmp2, b_cmp2, conv_w, g_out, w_o,
           g_ffn_norm, w_up, w_down):
    batch, seq, d = x.shape
    assert d == D_MODEL and seq % (CMP_CHUNK * CMP_STRIDE) == 0 and seq >= WINDOW + Q_BLOCK
    params = _prep_params(g_mix_norm, w_in, g_q, g_k, pe_cmp, w_cmp1, b_cmp1, w_cmp2, b_cmp2,
                          conv_w, g_out, w_o, g_ffn_norm, w_up, w_down)
    consts = _constants(seq // SEL_BLOCK)

    big = {k: params.pop(k) for k in ("w", "w1", "wo", "wu", "wd")}
    layers = jnp.arange(w_in.shape[0], dtype=i32)[:, None]

    def step(x2, xs):
        layer, p = xs
        return _layer(x2, layer, p, big, consts, batch=batch, seq=seq), None

    x2, _ = lax.scan(step, x.reshape(batch * seq, d), (layers, params))
    return x2.reshape(batch, seq, d)
```

```python
import functools

import numpy as np
import jax
import jax.numpy as jnp
from jax import lax
from jax.experimental import pallas as pl
from jax.experimental.pallas import tpu as pltpu

f32 = jnp.float32
bf16 = jnp.bfloat16
i32 = jnp.int32

D_MODEL = 1024
HEAD_DIM = 64
N_HEADS = 8
N_GROUPS = 2
HEADS_PER_GROUP = 4
ATTN_WIDTH = 512
CONV_WIDTH = 512
N_BRANCH = 3
CMP_BLOCK = 32
CMP_STRIDE = 16
CMP_HIDDEN = 256
SEL_BLOCK = 64
SEL_TOPK = 16
WINDOW = 512
Q_BLOCK = 128
D_FF = 4096
EPS = 1e-6
NEG = -1e30
LOG2E = 1.4426950408889634
SLOPE_TERMS = 3
PICKED = -3e38
LANES = 128
CMP_CHUNK = 256
SEL_PER_CHUNK = CMP_CHUNK * CMP_STRIDE // SEL_BLOCK
IMP_ROWS = SEL_PER_CHUNK + 8
VMEM_LIMIT = 56 * 1024 * 1024

V_ROWS = 72
PROJ_CHUNK = 512
SUP_BLOCKS = 32
HEAD_BLOCKS = 20
GROUP_BLOCKS = 4
QB_PER_STEP = 4
CMP_QB = 2


def _dot(a, b):
    return jnp.dot(a, b, preferred_element_type=f32)


def _dot_nt(a, b):
    return lax.dot_general(a, b, (((1,), (1,)), ((), ())), preferred_element_type=f32)


def _dot_tn(a, b):
    return lax.dot_general(a, b, (((0,), (0,)), ((), ())), preferred_element_type=f32)


def _values_dot(v, p):
    return _dot_tn(v[:, 0:V_ROWS], p)


def _head_norm(z, e, g):
    sq = (z * z).astype(bf16)
    outs = []
    for c in range(z.shape[1] // LANES):
        sl = slice(c * LANES, (c + 1) * LANES)
        outs.append(z[:, sl] * lax.rsqrt(_dot(sq[:, sl], e) * (1.0 / HEAD_DIM) + EPS))
    y = outs[0] if len(outs) == 1 else jnp.concatenate(outs, axis=1)
    return y * g


def _pos_cols(pos, width):
    rows = pos.shape[0]
    lane = lax.broadcasted_iota(i32, (rows, LANES), 1) - HEAD_DIM
    tile = jnp.where(lane < 0, 0.0, jnp.where(lane < SLOPE_TERMS, (pos >> 7).astype(f32),
                                              jnp.where(lane < 2 * SLOPE_TERMS, (pos & 127).astype(f32), 0.0)))
    return tile if width == LANES else jnp.concatenate([tile] * (width // LANES), axis=1)


def _ones_col(rows, width):
    lane = lax.broadcasted_iota(i32, (rows, LANES), 1)
    tile = jnp.where(lane == HEAD_DIM, 1.0, 0.0)
    return tile if width == LANES else jnp.concatenate([tile] * (width // LANES), axis=1)


def _inproj_body(layer_ref, x_ref, gm_ref, w_ref, gq_ref, gks_ref, gkw_ref, e_ref, tail_ref,
                 qt_ref, kc_ref, vc_ref, ks_ref, kw_ref, vs_ref, vw_ref, u_ref, bg_ref, gate_ref,
                 *, tm, seq):
    x = x_ref[...]
    ms = jnp.mean(x * x, axis=-1, keepdims=True)
    h = ((x * lax.rsqrt(ms + EPS)) * gm_ref[...]).astype(bf16)
    e = e_ref[...]
    two = 2 * LANES

    def proj(k):
        return _dot(h, w_ref[:, k * PROJ_CHUNK:(k + 1) * PROJ_CHUNK])

    def emit_queries(z):
        qn = _head_norm(z, e, gq_ref[...]) * (HEAD_DIM ** -0.5 * LOG2E)
        gw = HEADS_PER_GROUP * HEAD_DIM
        for cb in range(tm // Q_BLOCK):
            for g in range(N_GROUPS):
                blk = qn[cb * Q_BLOCK:(cb + 1) * Q_BLOCK, g * gw:(g + 1) * gw].T
                top = jnp.concatenate([blk[r * HEAD_DIM:(r + 1) * HEAD_DIM] for r in range(HEADS_PER_GROUP)],
                                      axis=1)
                qt_ref[cb, g] = jnp.concatenate([top.astype(bf16), tail_ref[g]], axis=0)

    def emit_compress_inputs(z):
        kc_ref[...] = z[:, 0:LANES].astype(bf16)
        vc_ref[...] = z[:, LANES:two].astype(bf16)
        gate_ref[...] = 1.0 / (1.0 + jnp.exp(-z[:, two:two + LANES]))

    def emit_keys(z):
        t0 = (pl.program_id(0) * tm) % seq
        pc = _pos_cols(lax.broadcasted_iota(i32, (tm, LANES), 0) + t0, two)
        ks_ref[...] = (_head_norm(z[:, 0:two], e, gks_ref[...]) + pc).astype(bf16)
        kw_ref[...] = (_head_norm(z[:, two:], e, gkw_ref[...]) + pc).astype(bf16)

    def emit_values(z):
        ones = _ones_col(tm, two)
        vs_ref[...] = (z[:, 0:two] + ones).astype(bf16)
        vw_ref[...] = (z[:, two:] + ones).astype(bf16)

    z0 = proj(0)
    z1 = proj(1)
    emit_queries(z0)
    z2 = proj(2)
    emit_compress_inputs(z1)
    z3 = proj(3)
    emit_keys(z2)
    z4 = proj(4)
    emit_values(z3)
    z5 = proj(5)
    z6 = proj(6)
    u_ref[...] = (z4 * z5).astype(bf16)
    bg_ref[...] = z6.astype(bf16)


def _layer_weight(a):
    return pl.BlockSpec((None,) + a.shape[1:], lambda i, layer: (layer[0],) + (0,) * (a.ndim - 1))


def _inproj(layer, x2, gm, w_all, gq, gks, gkw, e, tail, *, seq, tm=512):
    n = x2.shape[0]
    width = HEADS_PER_GROUP * Q_BLOCK
    row = lambda w_: pl.BlockSpec((tm, w_), lambda i, layer: (i, 0))
    full = lambda a: pl.BlockSpec(a.shape, lambda i, layer: (0,) * a.ndim)
    outs = [(LANES, bf16), (LANES, bf16), (2 * LANES, bf16), (2 * LANES, bf16),
            (2 * LANES, bf16), (2 * LANES, bf16), (CONV_WIDTH, bf16), (CONV_WIDTH, bf16), (LANES, f32)]
    qt_spec = pl.BlockSpec((tm // Q_BLOCK, N_GROUPS, LANES, width), lambda i, layer: (i, 0, 0, 0))
    qt_shape = jax.ShapeDtypeStruct((n // Q_BLOCK, N_GROUPS, LANES, width), bf16)
    return pl.pallas_call(
        functools.partial(_inproj_body, tm=tm, seq=seq),
        grid_spec=pltpu.PrefetchScalarGridSpec(
            num_scalar_prefetch=1, grid=(n // tm,),
            in_specs=[row(D_MODEL), full(gm), _layer_weight(w_all), full(gq), full(gks), full(gkw), full(e),
                      full(tail)],
            out_specs=[qt_spec] + [row(w_) for w_, _ in outs]),
        out_shape=[qt_shape] + [jax.ShapeDtypeStruct((n, w_), dt) for w_, dt in outs],
        compiler_params=pltpu.CompilerParams(dimension_semantics=("arbitrary",),
                                             vmem_limit_bytes=VMEM_LIMIT),
        name="inproj",
    )(layer, x2, gm, w_all, gq, gks, gkw, e, tail)


def _gelu_tanh(x):
    return 0.5 * x * (1.0 + jnp.tanh(0.7978845608028654 * (x + 0.044715 * (x * x * x))))


def _compress_body(layer_ref, zk_ref, zv_ref, w1_ref, pe_ref, b1_ref, w2k_ref, w2v_ref, b2k_ref, b2v_ref,
                   gk_ref, e_ref, kc_ref, vc_ref, *, tc):
    last = lax.broadcasted_iota(i32, (tc, 1), 0) == tc - 1

    def hidden(z_ref, kind):
        z = z_ref[0]
        a = _dot(z, w1_ref[kind, 0])
        b = _dot(z, w1_ref[kind, 1])
        b = jnp.concatenate([b[1:], jnp.zeros((1, b.shape[1]), f32)], axis=0)
        bias = (_dot(pe_ref[kind, 0], w1_ref[kind, 0]) + _dot(pe_ref[kind, 1], w1_ref[kind, 1]))[0:1]
        return _gelu_tanh(a + b + bias + b1_ref[kind]).astype(bf16)

    k = _dot(hidden(zk_ref, 0), w2k_ref[...]) + b2k_ref[...]
    k = _head_norm(k, e_ref[...], gk_ref[...])
    pos = lax.broadcasted_iota(i32, (tc, LANES), 0) * CMP_STRIDE + (CMP_BLOCK - 1)
    k = k + _pos_cols(pos, 2 * LANES)
    kc_ref[0] = jnp.where(last, 0.0, k).astype(bf16)
    v = _dot(hidden(zv_ref, 1), w2v_ref[...]) + b2v_ref[...]
    vc_ref[0] = jnp.where(last, 0.0, v + _ones_col(tc, 2 * LANES)).astype(bf16)


def _compress(layer, zk, zv, w1_all, pe, b1, w2k, w2v, b2k, b2v, gk, e):
    b, tc, _ = zk.shape
    blk = lambda a: pl.BlockSpec((1,) + a.shape[1:], lambda i, layer: (i,) + (0,) * (a.ndim - 1))
    full = lambda a: pl.BlockSpec(a.shape, lambda i, layer: (0,) * a.ndim)
    return pl.pallas_call(
        functools.partial(_compress_body, tc=tc),
        grid_spec=pltpu.PrefetchScalarGridSpec(
            num_scalar_prefetch=1, grid=(b,),
            in_specs=[blk(zk), blk(zv), _layer_weight(w1_all)]
            + [full(a) for a in (pe, b1, w2k, w2v, b2k, b2v, gk, e)],
            out_specs=[pl.BlockSpec((1, tc, 2 * LANES), lambda i, layer: (i, 0, 0))] * 2),
        out_shape=[jax.ShapeDtypeStruct((b, tc, 2 * LANES), bf16)] * 2,
        compiler_params=pltpu.CompilerParams(dimension_semantics=("arbitrary",),
                                             vmem_limit_bytes=VMEM_LIMIT),
        name="compress",
    )(layer, zk, zv, w1_all, pe, b1, w2k, w2v, b2k, b2v, gk, e)


def _query_pos(c):
    lane = lax.broadcasted_iota(i32, (1, HEADS_PER_GROUP * Q_BLOCK), 1)
    return c * Q_BLOCK + (lane & (Q_BLOCK - 1))


def _pick_top(vs, jf, n_pick):
    vs = list(vs)
    for _ in range(n_pick):
        for g, v in enumerate(vs):
            mx = jnp.max(v, axis=0, keepdims=True)
            idx = jnp.min(jnp.where(v == mx, jf, float(jf.shape[0])), axis=0, keepdims=True)
            vs[g] = jnp.where(jf == idx, PICKED, v)
    return tuple(vs)


def _cmp_variant(nchunk, c0, qt_ref, kc_ref, vc_ref, mct_ref, tri_ref, o_ref, madd_ref, lst_ref, cnt_ref,
                 *, nsel):
    units = [(qb, g) for qb in range(CMP_QB) for g in range(N_GROUPS)]
    nrow = nchunk * CMP_CHUNK
    npre = nchunk * SEL_PER_CHUNK
    tqs = [_query_pos(c0 + qb) for qb in range(CMP_QB)]
    sub = lax.broadcasted_iota(i32, (CMP_CHUNK, 1), 0)
    lanes = lambda g: slice(g * LANES, (g + 1) * LANES)
    chunk = lambda a, k: a[k * CMP_CHUNK:(k + 1) * CMP_CHUNK]

    tiles = []
    for qb, g in units:
        s = _dot(kc_ref[0, 0:nrow, lanes(g)], qt_ref[qb, g])
        row = []
        for k in range(nchunk):
            t = chunk(s, k)
            if k >= nchunk - 2:
                seen = (k * CMP_CHUNK + sub) * CMP_STRIDE + (CMP_BLOCK - 1) <= tqs[qb]
                t = jnp.where(seen, t, NEG)
            row.append(t)
        tiles.append(row)

    accs, imps = [], []
    for u, (qb, g) in enumerate(units):
        m = tiles[u][0].max(axis=0, keepdims=True)
        for t in tiles[u][1:]:
            m = jnp.maximum(m, t.max(axis=0, keepdims=True))
        parts = [jnp.exp2(t - m).astype(bf16) for t in tiles[u]]
        accs.append(_values_dot(vc_ref[0, 0:nrow, lanes(g)], jnp.concatenate(parts, axis=0)))
        rows, carry = [], None
        for part in parts:
            piece = _dot(mct_ref[...], part)
            body = piece[0:SEL_PER_CHUNK]
            if carry is not None:
                body = jnp.concatenate([body[0:8] + carry, body[8:]], axis=0)
            rows.append(body)
            carry = piece[SEL_PER_CHUNK:IMP_ROWS]
        imps.append(rows[0] if nchunk == 1 else jnp.concatenate(rows, axis=0))

    j = lax.broadcasted_iota(i32, (npre, Q_BLOCK), 0)
    jf = j.astype(f32)
    t1s = [tq[:, 0:Q_BLOCK] for tq in tqs]
    valids = [j * SEL_BLOCK <= t1 for t1 in t1s]
    vs = []
    for u, (qb, g) in enumerate(units):
        any_key = tqs[qb] >= CMP_BLOCK - 1
        inv = jnp.where(any_key, 1.0 / jnp.maximum(accs[u][HEAD_DIM:HEAD_DIM + 1], 1e-30), 0.0)
        o_ref[0, g, qb] = accs[u][0:HEAD_DIM] * inv
        imp4 = imps[u] * inv
        imp = imp4[:, 0:Q_BLOCK]
        for r in range(1, HEADS_PER_GROUP):
            imp = imp + imp4[:, r * Q_BLOCK:(r + 1) * Q_BLOCK]
        jt = t1s[qb] >> 6
        v = jnp.where(valids[qb], imp, NEG)
        vs.append(jnp.where(j == 0, PICKED, jnp.where(j == jt, PICKED, jnp.where(j == jt - 1, PICKED, v))))
    vs = _pick_top(vs, jf, SEL_TOPK - 3)

    rr = lax.broadcasted_iota(i32, (nsel, nsel), 0).astype(f32)
    ones8 = jnp.ones((8, Q_BLOCK), bf16)
    jrow = jnp.broadcast_to(lax.broadcasted_iota(i32, (1, nsel), 1).astype(f32), (8, nsel)).astype(bf16)
    for u, (qb, g) in enumerate(units):
        picked = vs[u] == PICKED
        madd = jnp.where(valids[qb], jnp.where(picked, 0.0, NEG), NEG)
        sel = jnp.where(valids[qb], jnp.where(picked, 1.0, 0.0), 0.0).astype(bf16)
        if npre < nsel:
            madd = jnp.concatenate([madd, jnp.full((nsel - npre, Q_BLOCK), NEG, f32)], axis=0)
            sel = jnp.concatenate([sel, jnp.zeros((nsel - npre, Q_BLOCK), bf16)], axis=0)
        madd_ref[0, g, qb] = madd
        flag = jnp.where(_dot_nt(ones8, sel)[0:1] > 0.0, 1.0, 0.0)
        flag8 = jnp.broadcast_to(flag, (8, nsel)).astype(bf16)
        prefix = _dot(flag8, tri_ref[...])[0:1]
        place = jnp.where(prefix == rr, flag, 0.0).astype(bf16)
        lst_ref[0, g, qb] = _dot_nt(jrow, place).astype(i32)
        cnt_ref[0, g, qb] = _dot(flag8, jnp.ones((nsel, LANES), bf16)).astype(i32)


def _cmp_body(qt_ref, kc_ref, vc_ref, mct_ref, tri_ref, o_ref, madd_ref, lst_ref, cnt_ref, *, nsel, nvar):
    c0 = pl.program_id(1) * CMP_QB
    c_last = c0 + CMP_QB - 1
    nch = (c_last * (Q_BLOCK // CMP_STRIDE) + (Q_BLOCK - CMP_BLOCK) // CMP_STRIDE) // CMP_CHUNK + 1
    for n in range(1, nvar + 1):
        pl.when(nch == n)(functools.partial(
            _cmp_variant, n, c0, qt_ref, kc_ref, vc_ref, mct_ref, tri_ref, o_ref, madd_ref, lst_ref, cnt_ref,
            nsel=nsel))


def _cmp_attention(qt, kc, vc, mct, tri, *, batch):
    g = N_GROUPS
    nqb = qt.shape[0] // batch
    tc = kc.shape[1]
    nsel = tc * CMP_STRIDE // SEL_BLOCK
    width = HEADS_PER_GROUP * Q_BLOCK
    nstep = nqb // CMP_QB
    per_q = lambda r_, c_: pl.BlockSpec((1, g, CMP_QB, r_, c_), lambda bi, ci: (bi, 0, ci, 0, 0))
    shape = lambda r_, c_, dt: jax.ShapeDtypeStruct((batch, g, nqb, r_, c_), dt)
    return pl.pallas_call(
        functools.partial(_cmp_body, nsel=nsel, nvar=tc // CMP_CHUNK),
        grid=(batch, nstep),
        in_specs=[pl.BlockSpec((CMP_QB, g, LANES, width), lambda bi, ci: (bi * nstep + ci, 0, 0, 0)),
                  pl.BlockSpec((1, tc, 2 * LANES), lambda bi, ci: (bi, 0, 0)),
                  pl.BlockSpec((1, tc, 2 * LANES), lambda bi, ci: (bi, 0, 0)),
                  pl.BlockSpec(mct.shape, lambda bi, ci: (0, 0)),
                  pl.BlockSpec(tri.shape, lambda bi, ci: (0, 0))],
        out_specs=[per_q(HEAD_DIM, width), per_q(nsel, Q_BLOCK), per_q(8, nsel), per_q(8, LANES)],
        out_shape=[shape(HEAD_DIM, width, f32), shape(nsel, Q_BLOCK, f32),
                   shape(8, nsel, i32), shape(8, LANES, i32)],
        compiler_params=pltpu.CompilerParams(dimension_semantics=("arbitrary",) * 2,
                                             vmem_limit_bytes=VMEM_LIMIT),
        name="cmp_attention",
    )(qt, kc, vc, mct, tri)


def _normalize(o_aug):
    return o_aug[0:HEAD_DIM] * (1.0 / jnp.maximum(o_aug[HEAD_DIM:HEAD_DIM + 1], 1e-30))


def _slc_win_body(lst_ref, cnt_ref, qt_ref, ks_ref, vs_ref, kw_ref, vw_ref,
                  madd_ref, ocmp_ref, gate_ref, wtab_ref, out_ref, s_scr, *, nsel):
    width = HEADS_PER_GROUP * Q_BLOCK
    nwin = (WINDOW + Q_BLOCK) // LANES
    wq = WINDOW // Q_BLOCK
    grp_rows = GROUP_BLOCKS * SEL_BLOCK

    class Block:
        def __init__(self, qb):
            self.qb = qb
            self.c = pl.program_id(2) * QB_PER_STEP + qb
            self.q0 = pl.multiple_of(self.c * Q_BLOCK, Q_BLOCK)
            self.qta = qt_ref[qb, 0]
            self.n_off = cnt_ref[qb, 0, 0] - 2

        def block_ids(self, first, nblk):
            ids = []
            for u in range(nblk):
                i = first + u
                j = lst_ref[self.qb, 0, jnp.minimum(i, nsel - 1)]
                ids.append((i < self.n_off, j, pl.multiple_of(j * SEL_BLOCK, SEL_BLOCK)))
            return ids

        def scores(self, ids):
            kcat = jnp.concatenate([ks_ref[0, pl.ds(r0, SEL_BLOCK), :] for _, _, r0 in ids], axis=0)
            sg = _dot(kcat, self.qta)
            tiles = []
            for u, (live, j, _) in enumerate(ids):
                mrow = jnp.where(live, madd_ref[0, 0, self.qb, pl.ds(j, 1), :], NEG)
                mrow = jnp.concatenate([mrow] * HEADS_PER_GROUP, axis=1)
                tiles.append(sg[u * SEL_BLOCK:(u + 1) * SEL_BLOCK] + mrow)
            return tiles

        def values(self, ids):
            return jnp.concatenate([vs_ref[0, pl.ds(r0, SEL_BLOCK), :] for _, _, r0 in ids], axis=0)

    def front(blk):
        c, q0, qta = blk.c, blk.q0, blk.qta
        ws = pl.multiple_of(jnp.maximum(c - wq, 0) * Q_BLOCK, Q_BLOCK)
        s = _dot(kw_ref[0, pl.ds(ws, WINDOW + Q_BLOCK), :], qta)
        chunks = []
        for k in range(nwin):
            steady = 1 if k == 0 else (2 if k == nwin - 1 else 0)
            tab = jnp.where(c >= wq, steady, jnp.where(k < c, 0, jnp.where(k == c, 2, 3)))
            chunks.append(s[k * LANES:(k + 1) * LANES] + wtab_ref[tab])
        head = blk.block_ids(0, HEAD_BLOCKS)
        tiles = [_dot(ks_ref[0, pl.ds(q0, Q_BLOCK), :], qta) + wtab_ref[2]] + blk.scores(head)
        return ws, chunks, head, tiles

    def softmax_pv(tiles, v):
        m = tiles[0].max(axis=0, keepdims=True)
        for su in tiles[1:]:
            m = jnp.maximum(m, su.max(axis=0, keepdims=True))
        p = jnp.concatenate([jnp.exp2(su - m).astype(bf16) for su in tiles], axis=0)
        return m, _values_dot(v, p)

    def middle(blk, ws, chunks, head, tiles):
        _, o_win = softmax_pv(chunks, vw_ref[0, pl.ds(ws, WINDOW + Q_BLOCK), :])
        m_run, o_run = softmax_pv(
            tiles, jnp.concatenate([vs_ref[0, pl.ds(blk.q0, Q_BLOCK), :], blk.values(head)], axis=0))
        return _normalize(o_win), m_run, o_run

    def rest(blk, m_run, o_run):
        def segment(si, carry):
            m_run, o_run = carry
            base = HEAD_BLOCKS + si * SUP_BLOCKS
            ngrp = (jnp.minimum(SUP_BLOCKS, blk.n_off - base) + GROUP_BLOCKS - 1) // GROUP_BLOCKS
            rows = lambda gi: pl.ds(pl.multiple_of(gi * grp_rows, grp_rows), grp_rows)

            def score(gi, mx):
                sg = jnp.concatenate(blk.scores(blk.block_ids(base + gi * GROUP_BLOCKS, GROUP_BLOCKS)), axis=0)
                s_scr[rows(gi), :] = sg
                return jnp.maximum(mx, sg.max(axis=0, keepdims=True))

            m_new = lax.fori_loop(0, ngrp, score, m_run)

            def weigh(gi, acc):
                pg = jnp.exp2(s_scr[rows(gi), :] - m_new).astype(bf16)
                return acc + _values_dot(blk.values(blk.block_ids(base + gi * GROUP_BLOCKS, GROUP_BLOCKS)), pg)

            o_seg = lax.fori_loop(0, ngrp, weigh, jnp.zeros((V_ROWS, width), f32))
            return m_new, jnp.exp2(m_run - m_new) * o_run + o_seg

        nseg = (jnp.maximum(blk.n_off - HEAD_BLOCKS, 0) + SUP_BLOCKS - 1) // SUP_BLOCKS
        return lax.fori_loop(0, nseg, segment, (m_run, o_run))[1]

    blocks = [Block(qb) for qb in range(QB_PER_STEP)]
    fronts = [front(blk) for blk in blocks]
    fronts = [middle(blk, *f) for blk, f in zip(blocks, fronts)]
    tails = [rest(blk, m_run, o_run) for blk, (_, m_run, o_run) in zip(blocks, fronts)]
    for blk, (o_win, _, _), o_run in zip(blocks, fronts, tails):
        qb = blk.qb
        gate = gate_ref[0, 0, qb]
        mix = gate[0:1] * ocmp_ref[0, 0, qb] + gate[1:2] * _normalize(o_run) + gate[2:3] * o_win
        rows = jnp.concatenate([mix[:, r * Q_BLOCK:(r + 1) * Q_BLOCK] for r in range(HEADS_PER_GROUP)],
                               axis=0)
        out_ref[0, qb * Q_BLOCK:(qb + 1) * Q_BLOCK, :] = rows.T.astype(bf16)


def _slc_win_attention(lst, cnt, qt, ks, vs, kw, vw, madd, ocmp, gate_t, wtab):
    b, t, _ = ks.shape
    g = N_GROUPS
    nqb, nsel = t // Q_BLOCK, t // SEL_BLOCK
    width = HEADS_PER_GROUP * Q_BLOCK
    nstep = nqb // QB_PER_STEP
    flat = lambda bi, gi, ci: (bi * g + gi) * nstep + ci
    smem = lambda w_: pl.BlockSpec((QB_PER_STEP, 1, w_), lambda bi, gi, ci: (flat(bi, gi, ci), 0, 0),
                                   memory_space=pltpu.SMEM)
    per_q = lambda r_, c_: pl.BlockSpec((1, 1, QB_PER_STEP, r_, c_), lambda bi, gi, ci: (bi, gi, ci, 0, 0))
    keys = pl.BlockSpec((1, t, LANES), lambda bi, gi, ci: (bi, 0, gi))
    return pl.pallas_call(
        functools.partial(_slc_win_body, nsel=nsel),
        grid=(b, g, nstep),
        in_specs=[smem(nsel), smem(LANES),
                  pl.BlockSpec((QB_PER_STEP, 1, LANES, width), lambda bi, gi, ci: (bi * nstep + ci, gi, 0, 0)),
                  keys, keys, keys, keys,
                  per_q(nsel, Q_BLOCK), per_q(HEAD_DIM, width), per_q(8, width),
                  pl.BlockSpec(wtab.shape, lambda bi, gi, ci: (0, 0, 0))],
        out_specs=pl.BlockSpec((1, QB_PER_STEP * Q_BLOCK, 2 * LANES), lambda bi, gi, ci: (bi, ci, gi)),
        out_shape=jax.ShapeDtypeStruct((b, t, ATTN_WIDTH), bf16),
        scratch_shapes=[pltpu.VMEM((SUP_BLOCKS * SEL_BLOCK, width), f32)],
        compiler_params=pltpu.CompilerParams(dimension_semantics=("arbitrary",) * 3,
                                             vmem_limit_bytes=VMEM_LIMIT),
        name="slc_win_attention",
    )(lst, cnt, qt, ks, vs, kw, vw, madd, ocmp, gate_t, wtab)


PREV_ROWS = 16


def _rms(v, g):
    return (v * lax.rsqrt(jnp.mean(v * v, axis=-1, keepdims=True) + EPS)) * g


def _mix_ffn_body(layer_ref, x_ref, attn_ref, u_ref, uprev_ref, bg_ref, cw_ref, go_ref, wo_ref, gf_ref, wu_ref,
                  wd_ref,
                  out_ref, *, tm, seq, chunk):
    first = (pl.program_id(0) * tm) % seq == 0
    u = u_ref[...].astype(f32)
    prev = jnp.where(first, 0.0, uprev_ref[0].astype(f32))
    ext = jnp.concatenate([prev, u], axis=0)
    cw = cw_ref[...]
    conv = (cw[0:1] * ext[PREV_ROWS - 2:PREV_ROWS - 2 + tm] + cw[1:2] * ext[PREV_ROWS - 1:PREV_ROWS - 1 + tm]
            + cw[2:3] * u)
    conv = bg_ref[...].astype(f32) * conv
    go = go_ref[...]
    mixed = jnp.concatenate([_rms(attn_ref[...].astype(f32), go[:, :ATTN_WIDTH]),
                             _rms(conv, go[:, ATTN_WIDTH:])], axis=1).astype(bf16)
    x = x_ref[...] + _dot(mixed, wo_ref[...])
    h = _rms(x, gf_ref[...]).astype(bf16)
    acc = x
    for c in range(D_FF // chunk):
        a = jnp.maximum(_dot(h, wu_ref[:, c * chunk:(c + 1) * chunk]), 0.0)
        acc = acc + _dot((a * a).astype(bf16), wd_ref[c * chunk:(c + 1) * chunk, :])
    out_ref[...] = acc


def _mix_ffn(layer, x2, attn2, u, bgate, cw, go, wo_all, gf, wu_all, wd_all, *, seq, tm=512, chunk=1024):
    n = x2.shape[0]
    uprev = u.reshape(n // PREV_ROWS, PREV_ROWS, CONV_WIDTH)
    row = lambda w_: pl.BlockSpec((tm, w_), lambda i, layer: (i, 0))
    full = lambda a: pl.BlockSpec(a.shape, lambda i, layer: (0,) * a.ndim)
    once = lambda a: pl.BlockSpec((None,) + a.shape[1:], lambda i, layer: (layer[0],) + (0,) * (a.ndim - 1),
                                  pipeline_mode=pl.Buffered(1))
    return pl.pallas_call(
        functools.partial(_mix_ffn_body, tm=tm, seq=seq, chunk=chunk),
        grid_spec=pltpu.PrefetchScalarGridSpec(
            num_scalar_prefetch=1, grid=(n // tm,),
            in_specs=[row(D_MODEL), row(ATTN_WIDTH), row(CONV_WIDTH),
                      pl.BlockSpec((1, PREV_ROWS, CONV_WIDTH),
                                   lambda i, layer: (jnp.maximum(i * (tm // PREV_ROWS) - 1, 0), 0, 0)),
                      row(CONV_WIDTH), full(cw), full(go), once(wo_all), full(gf), once(wu_all), once(wd_all)],
            out_specs=row(D_MODEL)),
        out_shape=jax.ShapeDtypeStruct((n, D_MODEL), f32),
        input_output_aliases={1: 0},
        compiler_params=pltpu.CompilerParams(dimension_semantics=("arbitrary",),
                                             vmem_limit_bytes=VMEM_LIMIT),
        name="mix_ffn",
    )(layer, x2, attn2, u, uprev, bgate, cw, go, wo_all, gf, wu_all, wd_all)


def _interleave_zero(w):
    z = jnp.zeros(w.shape[:-1] + (HEAD_DIM,), w.dtype)
    return jnp.concatenate([w[..., :HEAD_DIM], z, w[..., HEAD_DIM:], z], axis=-1)


def _prep_params(g_mix_norm, w_in, g_q, g_k, pe_cmp, w_cmp1, b_cmp1, w_cmp2, b_cmp2,
                 conv_w, g_out, w_o, g_ffn_norm, w_up, w_down):
    depth = w_in.shape[0]
    o = np.cumsum([0, ATTN_WIDTH] + [LANES] * 6 + [N_BRANCH * N_HEADS] + [CONV_WIDTH] * 3)
    part = lambda i: w_in[..., int(o[i]):int(o[i + 1])]
    q, kc, vc, ks, vs, kw, vw, gl, hc, cg, bg = [part(i) for i in range(11)]
    gl = jnp.pad(gl, ((0, 0), (0, 0), (0, LANES - gl.shape[-1])))
    w = jnp.concatenate([q, kc, vc, gl, jnp.zeros_like(gl), _interleave_zero(ks), _interleave_zero(kw),
                         _interleave_zero(vs), _interleave_zero(vw), hc, cg, bg], axis=-1).astype(bf16)
    tile2 = lambda gk: _interleave_zero(jnp.concatenate([gk, gk], axis=-1))[:, None, :]
    w1 = w_cmp1.astype(bf16).reshape(depth, 2, 2, CMP_STRIDE, HEAD_DIM, CMP_HIDDEN)
    z1 = jnp.zeros_like(w1)
    w1g = jnp.stack([jnp.concatenate([w1, z1], axis=-1), jnp.concatenate([z1, w1], axis=-1)], axis=4)
    w1g = w1g.reshape(depth, 2, 2, CMP_STRIDE * LANES, N_GROUPS * CMP_HIDDEN)
    pe = pe_cmp.reshape(depth, 2, 2, CMP_STRIDE, 1, HEAD_DIM)
    pe = jnp.broadcast_to(pe, (depth, 2, 2, CMP_STRIDE, N_GROUPS, HEAD_DIM)).reshape(depth, 2, 2, 1, -1)
    pe = jnp.pad(pe, ((0, 0), (0, 0), (0, 0), (0, 7), (0, 0))).astype(bf16)
    b1 = jnp.concatenate([b_cmp1, b_cmp1], axis=-1)[:, :, None, :]
    z2 = jnp.zeros_like(w_cmp2)
    w2 = jnp.concatenate([jnp.concatenate([w_cmp2, z2], axis=-1), jnp.concatenate([z2, w_cmp2], axis=-1)],
                         axis=2)
    w2 = _interleave_zero(w2).astype(bf16)
    b2 = _interleave_zero(jnp.concatenate([b_cmp2, b_cmp2], axis=-1))[:, :, None, :]
    return dict(
        gm=g_mix_norm[:, None, :], w=w,
        gq=jnp.tile(g_q, (1, N_HEADS))[:, None, :],
        gks=tile2(g_k[:, 1]), gkw=tile2(g_k[:, 2]), gkc=tile2(g_k[:, 0]),
        w1=w1g, pe=pe, b1=b1,
        w2k=w2[:, 0], w2v=w2[:, 1], b2k=b2[:, 0], b2v=b2[:, 1],
        cw=jnp.pad(conv_w, ((0, 0), (0, 8 - conv_w.shape[1]), (0, 0))),
        go=g_out[:, None, :], wo=w_o.astype(bf16),
        gf=g_ffn_norm[:, None, :], wu=w_up.astype(bf16), wd=w_down.astype(bf16),
    )


def _constants(nsel):
    lane = np.arange(LANES)
    e = (lane[:, None] // HEAD_DIM == lane[None, :] // HEAD_DIM).astype(np.float32)
    tail = np.zeros((N_GROUPS, HEAD_DIM, HEADS_PER_GROUP * Q_BLOCK), np.float32)
    for gi in range(N_GROUPS):
        for r in range(HEADS_PER_GROUP):
            rest = np.float64(2.0 ** -(gi * HEADS_PER_GROUP + r + 1)) * LOG2E
            for k in range(SLOPE_TERMS):
                term = np.float64(np.asarray(rest, np.float32).astype(jnp.bfloat16).astype(np.float32))
                tail[gi, k, r * Q_BLOCK:(r + 1) * Q_BLOCK] = term * LANES
                tail[gi, SLOPE_TERMS + k, r * Q_BLOCK:(r + 1) * Q_BLOCK] = term
                rest = rest - term
    mct = np.zeros((IMP_ROWS, CMP_CHUNK), np.float32)
    for i in range(CMP_CHUNK):
        lo, hi = i * CMP_STRIDE, i * CMP_STRIDE + CMP_BLOCK
        for jj in range(SEL_PER_CHUNK + 1):
            ov = min(hi, (jj + 1) * SEL_BLOCK) - max(lo, jj * SEL_BLOCK)
            if ov > 0:
                mct[jj, i] = ov / CMP_BLOCK
    kk = np.arange(LANES)[:, None]
    ql = np.tile(np.arange(Q_BLOCK), HEADS_PER_GROUP)[None, :]
    wtab = np.zeros((4, LANES, HEADS_PER_GROUP * Q_BLOCK), np.float32)
    wtab[1] = np.where(kk > ql, 0.0, NEG)
    wtab[2] = np.where(kk <= ql, 0.0, NEG)
    wtab[3] = NEG
    tri = np.arange(nsel)[:, None] < np.arange(nsel)[None, :]
    return (jnp.asarray(e, bf16), jnp.asarray(tail, bf16), jnp.asarray(mct, bf16), jnp.asarray(wtab),
            jnp.asarray(tri, bf16))


def _layer(x2, layer, p, big, consts, *, batch, seq):
    e, tail, mct, wtab, tri = consts
    g = N_GROUPS
    nqb, nsel, tc = seq // Q_BLOCK, seq // SEL_BLOCK, seq // CMP_STRIDE
    qt, kc, vc, ks, kw, vs, vw, u, bgate, gates = _inproj(
        layer, x2, p["gm"], big["w"], p["gq"], p["gks"], p["gkw"], e, tail, seq=seq)
    kcmp, vcmp = _compress(layer, kc.reshape(batch, tc, CMP_STRIDE * LANES),
                           vc.reshape(batch, tc, CMP_STRIDE * LANES),
                           big["w1"], p["pe"], p["b1"], p["w2k"], p["w2v"], p["b2k"], p["b2v"], p["gkc"], e)
    gate_t = gates[:, :N_HEADS * N_BRANCH].reshape(batch, nqb, Q_BLOCK, g, HEADS_PER_GROUP, N_BRANCH)
    gate_t = gate_t.transpose(0, 3, 1, 5, 4, 2).reshape(batch, g, nqb, N_BRANCH, HEADS_PER_GROUP * Q_BLOCK)
    gate_t = jnp.pad(gate_t, ((0, 0), (0, 0), (0, 0), (0, 8 - N_BRANCH), (0, 0)))
    ocmp, madd, lst, cnt = _cmp_attention(qt, kcmp, vcmp, mct, tri, batch=batch)
    rows3 = lambda a: a.reshape(batch, seq, 2 * LANES)
    smem = lambda a: a[:, :, :, 0, :].reshape(batch * g * nqb, 1, a.shape[-1])
    attn = _slc_win_attention(smem(lst), smem(cnt), qt, rows3(ks), rows3(vs),
                              rows3(kw), rows3(vw), madd, ocmp, gate_t, wtab)
    return _mix_ffn(layer, x2, attn.reshape(batch * seq, ATTN_WIDTH), u, bgate, p["cw"], p["go"], big["wo"],
                    p["gf"], big["wu"], big["wd"], seq=seq)


def kernel(x, g_mix_norm, w_in, g_q, g_k, pe_cmp, w_cmp1, b_cmp1, w_cmp2, b_cmp2, conv_w, g_out, w_o,
           g_ffn_norm, w_up, w_down):
    batch, seq, d = x.shape
    assert d == D_MODEL and seq % (CMP_CHUNK * CMP_STRIDE) == 0 and seq >= WINDOW + Q_BLOCK
    params = _prep_params(g_mix_norm, w_in, g_q, g_k, pe_cmp, w_cmp1, b_cmp1, w_cmp2, b_cmp2,
                          conv_w, g_out, w_o, g_ffn_norm, w_up, w_down)
    consts = _constants(seq // SEL_BLOCK)

    big = {k: params.pop(k) for k in ("w", "w1", "wo", "wu", "wd")}
    layers = jnp.arange(w_in.shape[0], dtype=i32)[:, None]

    def step(x2, xs):
        layer, p = xs
        return _layer(x2, layer, p, big, consts, batch=batch, seq=seq), None

    x2, _ = lax.scan(step, x.reshape(batch * seq, d), (layers, params))
    return x2.reshape(batch, seq, d)
```

```python
import functools

import numpy as np
import jax
import jax.numpy as jnp
from jax import lax
from jax.experimental import pallas as pl
from jax.experimental.pallas import tpu as pltpu

f32 = jnp.float32
bf16 = jnp.bfloat16
i32 = jnp.int32

D_MODEL = 1024
HEAD_DIM = 64
N_HEADS = 8
N_GROUPS = 2
HEADS_PER_GROUP = 4
ATTN_WIDTH = 512
CONV_WIDTH = 512
N_BRANCH = 3
CMP_BLOCK = 32
CMP_STRIDE = 16
CMP_HIDDEN = 256
SEL_BLOCK = 64
SEL_TOPK = 16
WINDOW = 512
Q_BLOCK = 128
D_FF = 4096
EPS = 1e-6
NEG = -1e30
LOG2E = 1.4426950408889634
SLOPE_TERMS = 3
PICKED = -3e38
LANES = 128
CMP_CHUNK = 256
SEL_PER_CHUNK = CMP_CHUNK * CMP_STRIDE // SEL_BLOCK
IMP_ROWS = SEL_PER_CHUNK + 8
VMEM_LIMIT = 56 * 1024 * 1024

PROJ_CHUNK = 512
SUP_BLOCKS = 32
HEAD_BLOCKS = 20
GROUP_BLOCKS = 4
QB_PER_STEP = 4
CMP_QB = 2


def _dot(a, b):
    return jnp.dot(a, b, preferred_element_type=f32)


def _dot_nt(a, b):
    return lax.dot_general(a, b, (((1,), (1,)), ((), ())), preferred_element_type=f32)


def _dot_tn(a, b):
    return lax.dot_general(a, b, (((0,), (0,)), ((), ())), preferred_element_type=f32)


def _head_norm(z, e, g):
    sq = (z * z).astype(bf16)
    outs = []
    for c in range(z.shape[1] // LANES):
        sl = slice(c * LANES, (c + 1) * LANES)
        outs.append(z[:, sl] * lax.rsqrt(_dot(sq[:, sl], e) * (1.0 / HEAD_DIM) + EPS))
    y = outs[0] if len(outs) == 1 else jnp.concatenate(outs, axis=1)
    return y * g


def _pos_cols(pos, width):
    rows = pos.shape[0]
    lane = lax.broadcasted_iota(i32, (rows, LANES), 1) - HEAD_DIM
    tile = jnp.where(lane < 0, 0.0, jnp.where(lane < SLOPE_TERMS, (pos >> 7).astype(f32),
                                              jnp.where(lane < 2 * SLOPE_TERMS, (pos & 127).astype(f32), 0.0)))
    return tile if width == LANES else jnp.concatenate([tile] * (width // LANES), axis=1)


def _ones_col(rows, width):
    lane = lax.broadcasted_iota(i32, (rows, LANES), 1)
    tile = jnp.where(lane == HEAD_DIM, 1.0, 0.0)
    return tile if width == LANES else jnp.concatenate([tile] * (width // LANES), axis=1)


def _inproj_body(layer_ref, x_ref, gm_ref, w_ref, gq_ref, gks_ref, gkw_ref, e_ref, tail_ref,
                 qt_ref, kc_ref, vc_ref, ks_ref, kw_ref, vs_ref, vw_ref, u_ref, bg_ref, gate_ref,
                 *, tm, seq):
    x = x_ref[...]
    ms = jnp.mean(x * x, axis=-1, keepdims=True)
    h = ((x * lax.rsqrt(ms + EPS)) * gm_ref[...]).astype(bf16)
    e = e_ref[...]
    two = 2 * LANES

    def proj(k):
        return _dot(h, w_ref[:, k * PROJ_CHUNK:(k + 1) * PROJ_CHUNK])

    def emit_queries(z):
        qn = _head_norm(z, e, gq_ref[...]) * (HEAD_DIM ** -0.5 * LOG2E)
        gw = HEADS_PER_GROUP * HEAD_DIM
        for cb in range(tm // Q_BLOCK):
            for g in range(N_GROUPS):
                blk = qn[cb * Q_BLOCK:(cb + 1) * Q_BLOCK, g * gw:(g + 1) * gw].T
                top = jnp.concatenate([blk[r * HEAD_DIM:(r + 1) * HEAD_DIM] for r in range(HEADS_PER_GROUP)],
                                      axis=1)
                qt_ref[cb, g] = jnp.concatenate([top.astype(bf16), tail_ref[g]], axis=0)

    def emit_compress_inputs(z):
        kc_ref[...] = z[:, 0:LANES].astype(bf16)
        vc_ref[...] = z[:, LANES:two].astype(bf16)
        gates = 1.0 / (1.0 + jnp.exp(-z[:, two:two + LANES]))
        pad = jnp.zeros((8 - N_BRANCH, HEADS_PER_GROUP * Q_BLOCK), f32)
        for cb in range(tm // Q_BLOCK):
            gt = gates[cb * Q_BLOCK:(cb + 1) * Q_BLOCK].T
            for g in range(N_GROUPS):
                col = lambda r, br: (g * HEADS_PER_GROUP + r) * N_BRANCH + br
                rows = [jnp.concatenate([gt[col(r, br):col(r, br) + 1] for r in range(HEADS_PER_GROUP)], axis=1)
                        for br in range(N_BRANCH)]
                gate_ref[cb, g] = jnp.concatenate(rows + [pad], axis=0)

    def emit_keys(z):
        t0 = (pl.program_id(0) * tm) % seq
        pc = _pos_cols(lax.broadcasted_iota(i32, (tm, LANES), 0) + t0, two)
        ks_ref[...] = (_head_norm(z[:, 0:two], e, gks_ref[...]) + pc).astype(bf16)
        kw_ref[...] = (_head_norm(z[:, two:], e, gkw_ref[...]) + pc).astype(bf16)

    def emit_values(z):
        ones = _ones_col(tm, two)
        vs_ref[...] = (z[:, 0:two] + ones).astype(bf16)
        vw_ref[...] = (z[:, two:] + ones).astype(bf16)

    z0 = proj(0)
    z1 = proj(1)
    emit_queries(z0)
    z2 = proj(2)
    emit_compress_inputs(z1)
    z3 = proj(3)
    emit_keys(z2)
    z4 = proj(4)
    emit_values(z3)
    z5 = proj(5)
    z6 = proj(6)
    u_ref[...] = (z4 * z5).astype(bf16)
    bg_ref[...] = z6.astype(bf16)


def _layer_weight(a):
    return pl.BlockSpec((None,) + a.shape[1:], lambda i, layer: (layer[0],) + (0,) * (a.ndim - 1))


def _inproj(layer, x2, gm, w_all, gq, gks, gkw, e, tail, *, seq, tm=512):
    n = x2.shape[0]
    width = HEADS_PER_GROUP * Q_BLOCK
    row = lambda w_: pl.BlockSpec((tm, w_), lambda i, layer: (i, 0))
    full = lambda a: pl.BlockSpec(a.shape, lambda i, layer: (0,) * a.ndim)
    outs = [(LANES, bf16), (LANES, bf16), (2 * LANES, bf16), (2 * LANES, bf16),
            (2 * LANES, bf16), (2 * LANES, bf16), (CONV_WIDTH, bf16), (CONV_WIDTH, bf16)]
    per_block = lambda rows: pl.BlockSpec((tm // Q_BLOCK, N_GROUPS, rows, width), lambda i, layer: (i, 0, 0, 0))
    qt_spec = per_block(LANES)
    qt_shape = jax.ShapeDtypeStruct((n // Q_BLOCK, N_GROUPS, LANES, width), bf16)
    gate_shape = jax.ShapeDtypeStruct((n // Q_BLOCK, N_GROUPS, 8, width), f32)
    return pl.pallas_call(
        functools.partial(_inproj_body, tm=tm, seq=seq),
        grid_spec=pltpu.PrefetchScalarGridSpec(
            num_scalar_prefetch=1, grid=(n // tm,),
            in_specs=[row(D_MODEL), full(gm), _layer_weight(w_all), full(gq), full(gks), full(gkw), full(e),
                      full(tail)],
            out_specs=[qt_spec] + [row(w_) for w_, _ in outs] + [per_block(8)]),
        out_shape=[qt_shape] + [jax.ShapeDtypeStruct((n, w_), dt) for w_, dt in outs] + [gate_shape],
        compiler_params=pltpu.CompilerParams(dimension_semantics=("arbitrary",),
                                             vmem_limit_bytes=VMEM_LIMIT),
        name="inproj",
    )(layer, x2, gm, w_all, gq, gks, gkw, e, tail)


def _gelu_tanh(x):
    return 0.5 * x * (1.0 + jnp.tanh(0.7978845608028654 * (x + 0.044715 * (x * x * x))))


def _compress_body(layer_ref, zk_ref, zv_ref, w1_ref, pe_ref, b1_ref, w2k_ref, w2v_ref, b2k_ref, b2v_ref,
                   gk_ref, e_ref, kc_ref, vc_ref, *, tc):
    last = lax.broadcasted_iota(i32, (tc, 1), 0) == tc - 1

    def hidden(z_ref, kind):
        z = z_ref[0]
        a = _dot(z, w1_ref[kind, 0])
        b = _dot(z, w1_ref[kind, 1])
        b = jnp.concatenate([b[1:], jnp.zeros((1, b.shape[1]), f32)], axis=0)
        bias = (_dot(pe_ref[kind, 0], w1_ref[kind, 0]) + _dot(pe_ref[kind, 1], w1_ref[kind, 1]))[0:1]
        return _gelu_tanh(a + b + bias + b1_ref[kind]).astype(bf16)

    k = _dot(hidden(zk_ref, 0), w2k_ref[...]) + b2k_ref[...]
    k = _head_norm(k, e_ref[...], gk_ref[...])
    pos = lax.broadcasted_iota(i32, (tc, LANES), 0) * CMP_STRIDE + (CMP_BLOCK - 1)
    k = k + _pos_cols(pos, 2 * LANES)
    kc_ref[0] = jnp.where(last, 0.0, k).astype(bf16)
    v = _dot(hidden(zv_ref, 1), w2v_ref[...]) + b2v_ref[...]
    vc_ref[0] = jnp.where(last, 0.0, v + _ones_col(tc, 2 * LANES)).astype(bf16)


def _compress(layer, zk, zv, w1_all, pe, b1, w2k, w2v, b2k, b2v, gk, e):
    b, tc, _ = zk.shape
    blk = lambda a: pl.BlockSpec((1,) + a.shape[1:], lambda i, layer: (i,) + (0,) * (a.ndim - 1))
    full = lambda a: pl.BlockSpec(a.shape, lambda i, layer: (0,) * a.ndim)
    return pl.pallas_call(
        functools.partial(_compress_body, tc=tc),
        grid_spec=pltpu.PrefetchScalarGridSpec(
            num_scalar_prefetch=1, grid=(b,),
            in_specs=[blk(zk), blk(zv), _layer_weight(w1_all)]
            + [full(a) for a in (pe, b1, w2k, w2v, b2k, b2v, gk, e)],
            out_specs=[pl.BlockSpec((1, tc, 2 * LANES), lambda i, layer: (i, 0, 0))] * 2),
        out_shape=[jax.ShapeDtypeStruct((b, tc, 2 * LANES), bf16)] * 2,
        compiler_params=pltpu.CompilerParams(dimension_semantics=("arbitrary",),
                                             vmem_limit_bytes=VMEM_LIMIT),
        name="compress",
    )(layer, zk, zv, w1_all, pe, b1, w2k, w2v, b2k, b2v, gk, e)


def _query_pos(c):
    lane = lax.broadcasted_iota(i32, (1, HEADS_PER_GROUP * Q_BLOCK), 1)
    return c * Q_BLOCK + (lane & (Q_BLOCK - 1))


def _pick_top(vs, jf, n_pick):
    vs = list(vs)
    for _ in range(n_pick):
        for g, v in enumerate(vs):
            mx = jnp.max(v, axis=0, keepdims=True)
            idx = jnp.min(jnp.where(v == mx, jf, float(jf.shape[0])), axis=0, keepdims=True)
            vs[g] = jnp.where(jf == idx, PICKED, v)
    return tuple(vs)


def _cmp_variant(nchunk, c0, qt_ref, kc_ref, vc_ref, mct_ref, tri_ref, o_ref, madd_ref, lst_ref, cnt_ref,
                 *, nsel):
    units = [(qb, g) for qb in range(CMP_QB) for g in range(N_GROUPS)]
    nrow = nchunk * CMP_CHUNK
    npre = nchunk * SEL_PER_CHUNK
    tqs = [_query_pos(c0 + qb) for qb in range(CMP_QB)]
    sub = lax.broadcasted_iota(i32, (CMP_CHUNK, 1), 0)
    lanes = lambda g: slice(g * LANES, (g + 1) * LANES)
    chunk = lambda a, k: a[k * CMP_CHUNK:(k + 1) * CMP_CHUNK]

    tiles = []
    for qb, g in units:
        s = _dot(kc_ref[0, 0:nrow, lanes(g)], qt_ref[qb, g])
        row = []
        for k in range(nchunk):
            t = chunk(s, k)
            if k >= nchunk - 2:
                seen = (k * CMP_CHUNK + sub) * CMP_STRIDE + (CMP_BLOCK - 1) <= tqs[qb]
                t = jnp.where(seen, t, NEG)
            row.append(t)
        tiles.append(row)

    accs, imps = [], []
    for u, (qb, g) in enumerate(units):
        m = tiles[u][0].max(axis=0, keepdims=True)
        for t in tiles[u][1:]:
            m = jnp.maximum(m, t.max(axis=0, keepdims=True))
        parts = [jnp.exp2(t - m).astype(bf16) for t in tiles[u]]
        accs.append(_dot_tn(vc_ref[0, 0:nrow, lanes(g)], jnp.concatenate(parts, axis=0)))
        rows, carry = [], None
        for part in parts:
            piece = _dot(mct_ref[...], part)
            body = piece[0:SEL_PER_CHUNK]
            if carry is not None:
                body = jnp.concatenate([body[0:8] + carry, body[8:]], axis=0)
            rows.append(body)
            carry = piece[SEL_PER_CHUNK:IMP_ROWS]
        imps.append(rows[0] if nchunk == 1 else jnp.concatenate(rows, axis=0))

    j = lax.broadcasted_iota(i32, (npre, Q_BLOCK), 0)
    jf = j.astype(f32)
    t1s = [tq[:, 0:Q_BLOCK] for tq in tqs]
    valids = [j * SEL_BLOCK <= t1 for t1 in t1s]
    vs = []
    for u, (qb, g) in enumerate(units):
        any_key = tqs[qb] >= CMP_BLOCK - 1
        inv = jnp.where(any_key, 1.0 / jnp.maximum(accs[u][HEAD_DIM:HEAD_DIM + 1], 1e-30), 0.0)
        o_ref[0, g, qb] = accs[u][0:HEAD_DIM] * inv
        imp4 = imps[u] * inv
        imp = imp4[:, 0:Q_BLOCK]
        for r in range(1, HEADS_PER_GROUP):
            imp = imp + imp4[:, r * Q_BLOCK:(r + 1) * Q_BLOCK]
        jt = t1s[qb] >> 6
        v = jnp.where(valids[qb], imp, NEG)
        vs.append(jnp.where(j == 0, PICKED, jnp.where(j == jt, PICKED, jnp.where(j == jt - 1, PICKED, v))))
    vs = _pick_top(vs, jf, SEL_TOPK - 3)

    rr = lax.broadcasted_iota(i32, (nsel, nsel), 0).astype(f32)
    ones8 = jnp.ones((8, Q_BLOCK), bf16)
    jrow = jnp.broadcast_to(lax.broadcasted_iota(i32, (1, nsel), 1).astype(f32), (8, nsel)).astype(bf16)
    for u, (qb, g) in enumerate(units):
        picked = vs[u] == PICKED
        madd = jnp.where(valids[qb], jnp.where(picked, 0.0, NEG), NEG)
        sel = jnp.where(valids[qb], jnp.where(picked, 1.0, 0.0), 0.0).astype(bf16)
        if npre < nsel:
            madd = jnp.concatenate([madd, jnp.full((nsel - npre, Q_BLOCK), NEG, f32)], axis=0)
            sel = jnp.concatenate([sel, jnp.zeros((nsel - npre, Q_BLOCK), bf16)], axis=0)
        madd_ref[0, g, qb] = madd
        flag = jnp.where(_dot_nt(ones8, sel)[0:1] > 0.0, 1.0, 0.0)
        flag8 = jnp.broadcast_to(flag, (8, nsel)).astype(bf16)
        prefix = _dot(flag8, tri_ref[...])[0:1]
        place = jnp.where(prefix == rr, flag, 0.0).astype(bf16)
        lst_ref[0, g, qb] = _dot_nt(jrow, place).astype(i32)
        cnt_ref[0, g, qb] = _dot(flag8, jnp.ones((nsel, LANES), bf16)).astype(i32)


def _cmp_body(qt_ref, kc_ref, vc_ref, mct_ref, tri_ref, o_ref, madd_ref, lst_ref, cnt_ref, *, nsel, nvar):
    c0 = pl.program_id(1) * CMP_QB
    c_last = c0 + CMP_QB - 1
    nch = (c_last * (Q_BLOCK // CMP_STRIDE) + (Q_BLOCK - CMP_BLOCK) // CMP_STRIDE) // CMP_CHUNK + 1
    for n in range(1, nvar + 1):
        pl.when(nch == n)(functools.partial(
            _cmp_variant, n, c0, qt_ref, kc_ref, vc_ref, mct_ref, tri_ref, o_ref, madd_ref, lst_ref, cnt_ref,
            nsel=nsel))


def _cmp_attention(qt, kc, vc, mct, tri, *, batch):
    g = N_GROUPS
    nqb = qt.shape[0] // batch
    tc = kc.shape[1]
    nsel = tc * CMP_STRIDE // SEL_BLOCK
    width = HEADS_PER_GROUP * Q_BLOCK
    nstep = nqb // CMP_QB
    per_q = lambda r_, c_: pl.BlockSpec((1, g, CMP_QB, r_, c_), lambda bi, ci: (bi, 0, ci, 0, 0))
    shape = lambda r_, c_, dt: jax.ShapeDtypeStruct((batch, g, nqb, r_, c_), dt)
    return pl.pallas_call(
        functools.partial(_cmp_body, nsel=nsel, nvar=tc // CMP_CHUNK),
        grid=(batch, nstep),
        in_specs=[pl.BlockSpec((CMP_QB, g, LANES, width), lambda bi, ci: (bi * nstep + ci, 0, 0, 0)),
                  pl.BlockSpec((1, tc, 2 * LANES), lambda bi, ci: (bi, 0, 0)),
                  pl.BlockSpec((1, tc, 2 * LANES), lambda bi, ci: (bi, 0, 0)),
                  pl.BlockSpec(mct.shape, lambda bi, ci: (0, 0)),
                  pl.BlockSpec(tri.shape, lambda bi, ci: (0, 0))],
        out_specs=[per_q(HEAD_DIM, width), per_q(nsel, Q_BLOCK), per_q(8, nsel), per_q(8, LANES)],
        out_shape=[shape(HEAD_DIM, width, f32), shape(nsel, Q_BLOCK, f32),
                   shape(8, nsel, i32), shape(8, LANES, i32)],
        compiler_params=pltpu.CompilerParams(dimension_semantics=("arbitrary",) * 2,
                                             vmem_limit_bytes=VMEM_LIMIT),
        name="cmp_attention",
    )(qt, kc, vc, mct, tri)


def _normalize(o_aug):
    return o_aug[0:HEAD_DIM] * (1.0 / jnp.maximum(o_aug[HEAD_DIM:HEAD_DIM + 1], 1e-30))


def _slc_win_body(lst_ref, cnt_ref, qt_ref, ks_ref, vs_ref, kw_ref, vw_ref,
                  madd_ref, ocmp_ref, gate_ref, wtab_ref, out_ref, s_scr, *, nsel):
    width = HEADS_PER_GROUP * Q_BLOCK
    nwin = (WINDOW + Q_BLOCK) // LANES
    wq = WINDOW // Q_BLOCK
    grp_rows = GROUP_BLOCKS * SEL_BLOCK

    class Block:
        def __init__(self, qb):
            self.qb = qb
            self.c = pl.program_id(2) * QB_PER_STEP + qb
            self.q0 = pl.multiple_of(self.c * Q_BLOCK, Q_BLOCK)
            self.qta = qt_ref[qb, 0]
            self.n_off = cnt_ref[qb, 0, 0] - 2

        def block_ids(self, first, nblk):
            ids = []
            for u in range(nblk):
                i = first + u
                j = lst_ref[self.qb, 0, jnp.minimum(i, nsel - 1)]
                ids.append((i < self.n_off, j, pl.multiple_of(j * SEL_BLOCK, SEL_BLOCK)))
            return ids

        def scores(self, ids):
            kcat = jnp.concatenate([ks_ref[0, pl.ds(r0, SEL_BLOCK), :] for _, _, r0 in ids], axis=0)
            sg = _dot(kcat, self.qta)
            tiles = []
            for u, (live, j, _) in enumerate(ids):
                mrow = jnp.where(live, madd_ref[0, 0, self.qb, pl.ds(j, 1), :], NEG)
                mrow = jnp.concatenate([mrow] * HEADS_PER_GROUP, axis=1)
                tiles.append(sg[u * SEL_BLOCK:(u + 1) * SEL_BLOCK] + mrow)
            return tiles

        def values(self, ids):
            return jnp.concatenate([vs_ref[0, pl.ds(r0, SEL_BLOCK), :] for _, _, r0 in ids], axis=0)

    def front(blk):
        c, q0, qta = blk.c, blk.q0, blk.qta
        ws = pl.multiple_of(jnp.maximum(c - wq, 0) * Q_BLOCK, Q_BLOCK)
        s = _dot(kw_ref[0, pl.ds(ws, WINDOW + Q_BLOCK), :], qta)
        chunks = []
        for k in range(nwin):
            steady = 1 if k == 0 else (2 if k == nwin - 1 else 0)
            tab = jnp.where(c >= wq, steady, jnp.where(k < c, 0, jnp.where(k == c, 2, 3)))
            chunks.append(s[k * LANES:(k + 1) * LANES] + wtab_ref[tab])
        head = blk.block_ids(0, HEAD_BLOCKS)
        tiles = [_dot(ks_ref[0, pl.ds(q0, Q_BLOCK), :], qta) + wtab_ref[2]] + blk.scores(head)
        return ws, chunks, head, tiles

    def softmax_pv(tiles, v):
        m = tiles[0].max(axis=0, keepdims=True)
        for su in tiles[1:]:
            m = jnp.maximum(m, su.max(axis=0, keepdims=True))
        p = jnp.concatenate([jnp.exp2(su - m).astype(bf16) for su in tiles], axis=0)
        return m, _dot_tn(v, p)

    def middle(blk, ws, chunks, head, tiles):
        _, o_win = softmax_pv(chunks, vw_ref[0, pl.ds(ws, WINDOW + Q_BLOCK), :])
        m_run, o_run = softmax_pv(
            tiles, jnp.concatenate([vs_ref[0, pl.ds(blk.q0, Q_BLOCK), :], blk.values(head)], axis=0))
        return _normalize(o_win), m_run, o_run

    def rest(blk, m_run, o_run):
        def segment(si, carry):
            m_run, o_run = carry
            base = HEAD_BLOCKS + si * SUP_BLOCKS
            ngrp = (jnp.minimum(SUP_BLOCKS, blk.n_off - base) + GROUP_BLOCKS - 1) // GROUP_BLOCKS
            rows = lambda gi: pl.ds(pl.multiple_of(gi * grp_rows, grp_rows), grp_rows)

            def score(gi, mx):
                sg = jnp.concatenate(blk.scores(blk.block_ids(base + gi * GROUP_BLOCKS, GROUP_BLOCKS)), axis=0)
                s_scr[rows(gi), :] = sg
                return jnp.maximum(mx, sg.max(axis=0, keepdims=True))

            m_new = lax.fori_loop(0, ngrp, score, m_run)

            def weigh(gi, acc):
                pg = jnp.exp2(s_scr[rows(gi), :] - m_new).astype(bf16)
                return acc + _dot_tn(blk.values(blk.block_ids(base + gi * GROUP_BLOCKS, GROUP_BLOCKS)), pg)

            o_seg = lax.fori_loop(0, ngrp, weigh, jnp.zeros((LANES, width), f32))
            return m_new, jnp.exp2(m_run - m_new) * o_run + o_seg

        nseg = (jnp.maximum(blk.n_off - HEAD_BLOCKS, 0) + SUP_BLOCKS - 1) // SUP_BLOCKS
        return lax.fori_loop(0, nseg, segment, (m_run, o_run))[1]

    blocks = [Block(qb) for qb in range(QB_PER_STEP)]
    fronts = [front(blk) for blk in blocks]
    fronts = [middle(blk, *f) for blk, f in zip(blocks, fronts)]
    tails = [rest(blk, m_run, o_run) for blk, (_, m_run, o_run) in zip(blocks, fronts)]
    for blk, (o_win, _, _), o_run in zip(blocks, fronts, tails):
        qb = blk.qb
        gate = gate_ref[qb, 0]
        mix = gate[0:1] * ocmp_ref[0, 0, qb] + gate[1:2] * _normalize(o_run) + gate[2:3] * o_win
        rows = jnp.concatenate([mix[:, r * Q_BLOCK:(r + 1) * Q_BLOCK] for r in range(HEADS_PER_GROUP)],
                               axis=0)
        out_ref[0, qb * Q_BLOCK:(qb + 1) * Q_BLOCK, :] = rows.T.astype(bf16)


def _slc_win_attention(lst, cnt, qt, ks, vs, kw, vw, madd, ocmp, gate_t, wtab):
    b, t, _ = ks.shape
    g = N_GROUPS
    nqb, nsel = t // Q_BLOCK, t // SEL_BLOCK
    width = HEADS_PER_GROUP * Q_BLOCK
    nstep = nqb // QB_PER_STEP
    flat = lambda bi, gi, ci: (bi * g + gi) * nstep + ci
    smem = lambda w_: pl.BlockSpec((QB_PER_STEP, 1, w_), lambda bi, gi, ci: (flat(bi, gi, ci), 0, 0),
                                   memory_space=pltpu.SMEM)
    per_q = lambda r_, c_: pl.BlockSpec((1, 1, QB_PER_STEP, r_, c_), lambda bi, gi, ci: (bi, gi, ci, 0, 0))
    keys = pl.BlockSpec((1, t, LANES), lambda bi, gi, ci: (bi, 0, gi), pipeline_mode=pl.Buffered(1))
    return pl.pallas_call(
        functools.partial(_slc_win_body, nsel=nsel),
        grid=(b, g, nstep),
        in_specs=[smem(nsel), smem(LANES),
                  pl.BlockSpec((QB_PER_STEP, 1, LANES, width), lambda bi, gi, ci: (bi * nstep + ci, gi, 0, 0)),
                  keys, keys, keys, keys,
                  per_q(nsel, Q_BLOCK), per_q(HEAD_DIM, width),
                  pl.BlockSpec((QB_PER_STEP, 1, 8, width), lambda bi, gi, ci: (bi * nstep + ci, gi, 0, 0)),
                  pl.BlockSpec(wtab.shape, lambda bi, gi, ci: (0, 0, 0))],
        out_specs=pl.BlockSpec((1, QB_PER_STEP * Q_BLOCK, 2 * LANES), lambda bi, gi, ci: (bi, ci, gi)),
        out_shape=jax.ShapeDtypeStruct((b, t, ATTN_WIDTH), bf16),
        scratch_shapes=[pltpu.VMEM((SUP_BLOCKS * SEL_BLOCK, width), f32)],
        compiler_params=pltpu.CompilerParams(dimension_semantics=("arbitrary",) * 3,
                                             vmem_limit_bytes=VMEM_LIMIT),
        name="slc_win_attention",
    )(lst, cnt, qt, ks, vs, kw, vw, madd, ocmp, gate_t, wtab)


PREV_ROWS = 16


def _rms(v, g):
    return (v * lax.rsqrt(jnp.mean(v * v, axis=-1, keepdims=True) + EPS)) * g


def _mix_ffn_body(layer_ref, x_ref, attn_ref, u_ref, uprev_ref, bg_ref, cw_ref, go_ref, wo_ref, gf_ref, wu_ref,
                  wd_ref,
                  out_ref, *, tm, seq, chunk):
    first = (pl.program_id(0) * tm) % seq == 0
    u = u_ref[...].astype(f32)
    prev = jnp.where(first, 0.0, uprev_ref[0].astype(f32))
    ext = jnp.concatenate([prev, u], axis=0)
    cw = cw_ref[...]
    conv = (cw[0:1] * ext[PREV_ROWS - 2:PREV_ROWS - 2 + tm] + cw[1:2] * ext[PREV_ROWS - 1:PREV_ROWS - 1 + tm]
            + cw[2:3] * u)
    conv = bg_ref[...].astype(f32) * conv
    go = go_ref[...]
    mixed = jnp.concatenate([_rms(attn_ref[...].astype(f32), go[:, :ATTN_WIDTH]),
                             _rms(conv, go[:, ATTN_WIDTH:])], axis=1).astype(bf16)
    x = x_ref[...] + _dot(mixed, wo_ref[...])
    h = _rms(x, gf_ref[...]).astype(bf16)
    acc = x
    for c in range(D_FF // chunk):
        a = jnp.maximum(_dot(h, wu_ref[:, c * chunk:(c + 1) * chunk]), 0.0)
        acc = acc + _dot((a * a).astype(bf16), wd_ref[c * chunk:(c + 1) * chunk, :])
    out_ref[...] = acc


def _mix_ffn(layer, x2, attn2, u, bgate, cw, go, wo_all, gf, wu_all, wd_all, *, seq, tm=512, chunk=1024):
    n = x2.shape[0]
    uprev = u.reshape(n // PREV_ROWS, PREV_ROWS, CONV_WIDTH)
    row = lambda w_: pl.BlockSpec((tm, w_), lambda i, layer: (i, 0))
    full = lambda a: pl.BlockSpec(a.shape, lambda i, layer: (0,) * a.ndim)
    once = lambda a: pl.BlockSpec((None,) + a.shape[1:], lambda i, layer: (layer[0],) + (0,) * (a.ndim - 1),
                                  pipeline_mode=pl.Buffered(1))
    return pl.pallas_call(
        functools.partial(_mix_ffn_body, tm=tm, seq=seq, chunk=chunk),
        grid_spec=pltpu.PrefetchScalarGridSpec(
            num_scalar_prefetch=1, grid=(n // tm,),
            in_specs=[row(D_MODEL), row(ATTN_WIDTH), row(CONV_WIDTH),
                      pl.BlockSpec((1, PREV_ROWS, CONV_WIDTH),
                                   lambda i, layer: (jnp.maximum(i * (tm // PREV_ROWS) - 1, 0), 0, 0)),
                      row(CONV_WIDTH), full(cw), full(go), once(wo_all), full(gf), once(wu_all), once(wd_all)],
            out_specs=row(D_MODEL)),
        out_shape=jax.ShapeDtypeStruct((n, D_MODEL), f32),
        input_output_aliases={1: 0},
        compiler_params=pltpu.CompilerParams(dimension_semantics=("arbitrary",),
                                             vmem_limit_bytes=VMEM_LIMIT),
        name="mix_ffn",
    )(layer, x2, attn2, u, uprev, bgate, cw, go, wo_all, gf, wu_all, wd_all)


def _interleave_zero(w):
    z = jnp.zeros(w.shape[:-1] + (HEAD_DIM,), w.dtype)
    return jnp.concatenate([w[..., :HEAD_DIM], z, w[..., HEAD_DIM:], z], axis=-1)


def _prep_params(g_mix_norm, w_in, g_q, g_k, pe_cmp, w_cmp1, b_cmp1, w_cmp2, b_cmp2,
                 conv_w, g_out, w_o, g_ffn_norm, w_up, w_down):
    depth = w_in.shape[0]
    o = np.cumsum([0, ATTN_WIDTH] + [LANES] * 6 + [N_BRANCH * N_HEADS] + [CONV_WIDTH] * 3)
    part = lambda i: w_in[..., int(o[i]):int(o[i + 1])]
    q, kc, vc, ks, vs, kw, vw, gl, hc, cg, bg = [part(i) for i in range(11)]
    gl = jnp.pad(gl, ((0, 0), (0, 0), (0, LANES - gl.shape[-1])))
    w = jnp.concatenate([q, kc, vc, gl, jnp.zeros_like(gl), _interleave_zero(ks), _interleave_zero(kw),
                         _interleave_zero(vs), _interleave_zero(vw), hc, cg, bg], axis=-1).astype(bf16)
    tile2 = lambda gk: _interleave_zero(jnp.concatenate([gk, gk], axis=-1))[:, None, :]
    w1 = w_cmp1.astype(bf16).reshape(depth, 2, 2, CMP_STRIDE, HEAD_DIM, CMP_HIDDEN)
    z1 = jnp.zeros_like(w1)
    w1g = jnp.stack([jnp.concatenate([w1, z1], axis=-1), jnp.concatenate([z1, w1], axis=-1)], axis=4)
    w1g = w1g.reshape(depth, 2, 2, CMP_STRIDE * LANES, N_GROUPS * CMP_HIDDEN)
    pe = pe_cmp.reshape(depth, 2, 2, CMP_STRIDE, 1, HEAD_DIM)
    pe = jnp.broadcast_to(pe, (depth, 2, 2, CMP_STRIDE, N_GROUPS, HEAD_DIM)).reshape(depth, 2, 2, 1, -1)
    pe = jnp.pad(pe, ((0, 0), (0, 0), (0, 0), (0, 7), (0, 0))).astype(bf16)
    b1 = jnp.concatenate([b_cmp1, b_cmp1], axis=-1)[:, :, None, :]
    z2 = jnp.zeros_like(w_cmp2)
    w2 = jnp.concatenate([jnp.concatenate([w_cmp2, z2], axis=-1), jnp.concatenate([z2, w_cmp2], axis=-1)],
                         axis=2)
    w2 = _interleave_zero(w2).astype(bf16)
    b2 = _interleave_zero(jnp.concatenate([b_cmp2, b_cmp2], axis=-1))[:, :, None, :]
    return dict(
        gm=g_mix_norm[:, None, :], w=w,
        gq=jnp.tile(g_q, (1, N_HEADS))[:, None, :],
        gks=tile2(g_k[:, 1]), gkw=tile2(g_k[:, 2]), gkc=tile2(g_k[:, 0]),
        w1=w1g, pe=pe, b1=b1,
        w2k=w2[:, 0], w2v=w2[:, 1], b2k=b2[:, 0], b2v=b2[:, 1],
        cw=jnp.pad(conv_w, ((0, 0), (0, 8 - conv_w.shape[1]), (0, 0))),
        go=g_out[:, None, :], wo=w_o.astype(bf16),
        gf=g_ffn_norm[:, None, :], wu=w_up.astype(bf16), wd=w_down.astype(bf16),
    )


def _constants(nsel):
    lane = np.arange(LANES)
    e = (lane[:, None] // HEAD_DIM == lane[None, :] // HEAD_DIM).astype(np.float32)
    tail = np.zeros((N_GROUPS, HEAD_DIM, HEADS_PER_GROUP * Q_BLOCK), np.float32)
    for gi in range(N_GROUPS):
        for r in range(HEADS_PER_GROUP):
            rest = np.float64(2.0 ** -(gi * HEADS_PER_GROUP + r + 1)) * LOG2E
            for k in range(SLOPE_TERMS):
                term = np.float64(np.asarray(rest, np.float32).astype(jnp.bfloat16).astype(np.float32))
                tail[gi, k, r * Q_BLOCK:(r + 1) * Q_BLOCK] = term * LANES
                tail[gi, SLOPE_TERMS + k, r * Q_BLOCK:(r + 1) * Q_BLOCK] = term
                rest = rest - term
    mct = np.zeros((IMP_ROWS, CMP_CHUNK), np.float32)
    for i in range(CMP_CHUNK):
        lo, hi = i * CMP_STRIDE, i * CMP_STRIDE + CMP_BLOCK
        for jj in range(SEL_PER_CHUNK + 1):
            ov = min(hi, (jj + 1) * SEL_BLOCK) - max(lo, jj * SEL_BLOCK)
            if ov > 0:
                mct[jj, i] = ov / CMP_BLOCK
    kk = np.arange(LANES)[:, None]
    ql = np.tile(np.arange(Q_BLOCK), HEADS_PER_GROUP)[None, :]
    wtab = np.zeros((4, LANES, HEADS_PER_GROUP * Q_BLOCK), np.float32)
    wtab[1] = np.where(kk > ql, 0.0, NEG)
    wtab[2] = np.where(kk <= ql, 0.0, NEG)
    wtab[3] = NEG
    tri = np.arange(nsel)[:, None] < np.arange(nsel)[None, :]
    return (jnp.asarray(e, bf16), jnp.asarray(tail, bf16), jnp.asarray(mct, bf16), jnp.asarray(wtab),
            jnp.asarray(tri, bf16))


def _layer(x2, layer, p, big, consts, *, batch, seq):
    e, tail, mct, wtab, tri = consts
    g = N_GROUPS
    nqb, nsel, tc = seq // Q_BLOCK, seq // SEL_BLOCK, seq // CMP_STRIDE
    qt, kc, vc, ks, kw, vs, vw, u, bgate, gate_t = _inproj(
        layer, x2, p["gm"], big["w"], p["gq"], p["gks"], p["gkw"], e, tail, seq=seq)
    kcmp, vcmp = _compress(layer, kc.reshape(batch, tc, CMP_STRIDE * LANES),
                           vc.reshape(batch, tc, CMP_STRIDE * LANES),
                           big["w1"], p["pe"], p["b1"], p["w2k"], p["w2v"], p["b2k"], p["b2v"], p["gkc"], e)
    ocmp, madd, lst, cnt = _cmp_attention(qt, kcmp, vcmp, mct, tri, batch=batch)
    rows3 = lambda a: a.reshape(batch, seq, 2 * LANES)
    smem = lambda a: a[:, :, :, 0, :].reshape(batch * g * nqb, 1, a.shape[-1])
    attn = _slc_win_attention(smem(lst), smem(cnt), qt, rows3(ks), rows3(vs),
                              rows3(kw), rows3(vw), madd, ocmp, gate_t, wtab)
    return _mix_ffn(layer, x2, attn.reshape(batch * seq, ATTN_WIDTH), u, bgate, p["cw"], p["go"], big["wo"],
                    p["gf"], big["wu"], big["wd"], seq=seq)


def kernel(x, g_mix_norm, w_in, g_q, g_k, pe_cmp, w_cmp1, b_cmp1, w_cmp2, b_cmp2, conv_w, g_out, w_o,
           g_ffn_norm, w_up, w_down):
    batch, seq, d = x.shape
    assert d == D_MODEL and seq % (CMP_CHUNK * CMP_STRIDE) == 0 and seq >= WINDOW + Q_BLOCK
    params = _prep_params(g_mix_norm, w_in, g_q, g_k, pe_cmp, w_cmp1, b_cmp1, w_cmp2, b_cmp2,
                          conv_w, g_out, w_o, g_ffn_norm, w_up, w_down)
    consts = _constants(seq // SEL_BLOCK)

    big = {k: params.pop(k) for k in ("w", "w1", "wo", "wu", "wd")}
    layers = jnp.arange(w_in.shape[0], dtype=i32)[:, None]

    def step(x2, xs):
        layer, p = xs
        return _layer(x2, layer, p, big, consts, batch=batch, seq=seq), None

    x2, _ = lax.scan(step, x.reshape(batch * seq, d), (layers, params))
    return x2.reshape(batch, seq, d)
```

```python
import functools

import numpy as np
import jax
import jax.numpy as jnp
from jax import lax
from jax.experimental import pallas as pl
from jax.experimental.pallas import tpu as pltpu

f32 = jnp.float32
bf16 = jnp.bfloat16
i32 = jnp.int32

D_MODEL = 1024
HEAD_DIM = 64
N_HEADS = 8
N_GROUPS = 2
HEADS_PER_GROUP = 4
ATTN_WIDTH = 512
CONV_WIDTH = 512
N_BRANCH = 3
CMP_BLOCK = 32
CMP_STRIDE = 16
CMP_HIDDEN = 256
SEL_BLOCK = 64
SEL_TOPK = 16
WINDOW = 512
Q_BLOCK = 128
D_FF = 4096
EPS = 1e-6
NEG = -1e30
LOG2E = 1.4426950408889634
SLOPE_TERMS = 3
PICKED = -3e38
LANES = 128
CMP_CHUNK = 256
SEL_PER_CHUNK = CMP_CHUNK * CMP_STRIDE // SEL_BLOCK
IMP_ROWS = SEL_PER_CHUNK + 8
VMEM_LIMIT = 56 * 1024 * 1024

PROJ_CHUNK = 512
SUP_BLOCKS = 32
HEAD_BLOCKS = 20
GROUP_BLOCKS = 4
QB_PER_STEP = 4
CMP_QB = 2


def _dot(a, b):
    return jnp.dot(a, b, preferred_element_type=f32)


def _dot_nt(a, b):
    return lax.dot_general(a, b, (((1,), (1,)), ((), ())), preferred_element_type=f32)


def _dot_tn(a, b):
    return lax.dot_general(a, b, (((0,), (0,)), ((), ())), preferred_element_type=f32)


def _head_norm(z, e, g):
    sq = (z * z).astype(bf16)
    outs = []
    for c in range(z.shape[1] // LANES):
        sl = slice(c * LANES, (c + 1) * LANES)
        outs.append(z[:, sl] * lax.rsqrt(_dot(sq[:, sl], e) * (1.0 / HEAD_DIM) + EPS))
    y = outs[0] if len(outs) == 1 else jnp.concatenate(outs, axis=1)
    return y * g


def _pos_cols(pos, width):
    rows = pos.shape[0]
    lane = lax.broadcasted_iota(i32, (rows, LANES), 1) - HEAD_DIM
    tile = jnp.where(lane < 0, 0.0, jnp.where(lane < SLOPE_TERMS, (pos >> 7).astype(f32),
                                              jnp.where(lane < 2 * SLOPE_TERMS, (pos & 127).astype(f32), 0.0)))
    return tile if width == LANES else jnp.concatenate([tile] * (width // LANES), axis=1)


def _ones_col(rows, width):
    lane = lax.broadcasted_iota(i32, (rows, LANES), 1)
    tile = jnp.where(lane == HEAD_DIM, 1.0, 0.0)
    return tile if width == LANES else jnp.concatenate([tile] * (width // LANES), axis=1)


def _inproj_body(layer_ref, x_ref, gm_ref, w_ref, gq_ref, gks_ref, gkw_ref, e_ref, tail_ref,
                 qt_ref, kc_ref, vc_ref, ks_ref, kw_ref, vs_ref, vw_ref, u_ref, bg_ref, gate_ref, kv_scr,
                 *, tm, seq):
    x = x_ref[...]
    ms = jnp.mean(x * x, axis=-1, keepdims=True)
    h = ((x * lax.rsqrt(ms + EPS)) * gm_ref[...]).astype(bf16)
    e = e_ref[...]
    two = 2 * LANES

    def proj(k):
        return _dot(h, w_ref[:, k * PROJ_CHUNK:(k + 1) * PROJ_CHUNK])

    def emit_queries(z):
        qn = _head_norm(z, e, gq_ref[...]) * (HEAD_DIM ** -0.5 * LOG2E)
        gw = HEADS_PER_GROUP * HEAD_DIM
        for cb in range(tm // Q_BLOCK):
            for g in range(N_GROUPS):
                blk = qn[cb * Q_BLOCK:(cb + 1) * Q_BLOCK, g * gw:(g + 1) * gw].T
                top = jnp.concatenate([blk[r * HEAD_DIM:(r + 1) * HEAD_DIM] for r in range(HEADS_PER_GROUP)],
                                      axis=1)
                qt_ref[cb, g] = jnp.concatenate([top.astype(bf16), tail_ref[g]], axis=0)

    def emit_compress_inputs(z):
        kv_scr[0] = z[:, 0:LANES]
        kv_scr[1] = z[:, LANES:two]
        for p in range(CMP_STRIDE):
            rows = pl.ds(p, tm // CMP_STRIDE, stride=CMP_STRIDE)
            kc_ref[:, p * LANES:(p + 1) * LANES] = kv_scr[0, rows, :].astype(bf16)
            vc_ref[:, p * LANES:(p + 1) * LANES] = kv_scr[1, rows, :].astype(bf16)
        gates = 1.0 / (1.0 + jnp.exp(-z[:, two:two + LANES]))
        pad = jnp.zeros((8 - N_BRANCH, HEADS_PER_GROUP * Q_BLOCK), f32)
        for cb in range(tm // Q_BLOCK):
            gt = gates[cb * Q_BLOCK:(cb + 1) * Q_BLOCK].T
            for g in range(N_GROUPS):
                col = lambda r, br: (g * HEADS_PER_GROUP + r) * N_BRANCH + br
                rows = [jnp.concatenate([gt[col(r, br):col(r, br) + 1] for r in range(HEADS_PER_GROUP)], axis=1)
                        for br in range(N_BRANCH)]
                gate_ref[cb, g] = jnp.concatenate(rows + [pad], axis=0)

    def emit_keys(z):
        t0 = (pl.program_id(0) * tm) % seq
        pc = _pos_cols(lax.broadcasted_iota(i32, (tm, LANES), 0) + t0, two)
        ks_ref[...] = (_head_norm(z[:, 0:two], e, gks_ref[...]) + pc).astype(bf16)
        kw_ref[...] = (_head_norm(z[:, two:], e, gkw_ref[...]) + pc).astype(bf16)

    def emit_values(z):
        ones = _ones_col(tm, two)
        vs_ref[...] = (z[:, 0:two] + ones).astype(bf16)
        vw_ref[...] = (z[:, two:] + ones).astype(bf16)

    z0 = proj(0)
    z1 = proj(1)
    emit_queries(z0)
    z2 = proj(2)
    emit_compress_inputs(z1)
    z3 = proj(3)
    emit_keys(z2)
    z4 = proj(4)
    emit_values(z3)
    z5 = proj(5)
    z6 = proj(6)
    u_ref[...] = (z4 * z5).astype(bf16)
    bg_ref[...] = z6.astype(bf16)


def _layer_weight(a):
    return pl.BlockSpec((None,) + a.shape[1:], lambda i, layer: (layer[0],) + (0,) * (a.ndim - 1))


def _inproj(layer, x2, gm, w_all, gq, gks, gkw, e, tail, *, seq, tm=512):
    n = x2.shape[0]
    width = HEADS_PER_GROUP * Q_BLOCK
    row = lambda w_: pl.BlockSpec((tm, w_), lambda i, layer: (i, 0))
    full = lambda a: pl.BlockSpec(a.shape, lambda i, layer: (0,) * a.ndim)
    outs = [(2 * LANES, bf16), (2 * LANES, bf16),
            (2 * LANES, bf16), (2 * LANES, bf16), (CONV_WIDTH, bf16), (CONV_WIDTH, bf16)]
    per_block = lambda rows: pl.BlockSpec((tm // Q_BLOCK, N_GROUPS, rows, width), lambda i, layer: (i, 0, 0, 0))
    qt_spec = per_block(LANES)
    qt_shape = jax.ShapeDtypeStruct((n // Q_BLOCK, N_GROUPS, LANES, width), bf16)
    gate_shape = jax.ShapeDtypeStruct((n // Q_BLOCK, N_GROUPS, 8, width), f32)
    chunk_spec = pl.BlockSpec((tm // CMP_STRIDE, CMP_STRIDE * LANES), lambda i, layer: (i, 0))
    chunk_shape = jax.ShapeDtypeStruct((n // CMP_STRIDE, CMP_STRIDE * LANES), bf16)
    return pl.pallas_call(
        functools.partial(_inproj_body, tm=tm, seq=seq),
        grid_spec=pltpu.PrefetchScalarGridSpec(
            num_scalar_prefetch=1, grid=(n // tm,),
            in_specs=[row(D_MODEL), full(gm), _layer_weight(w_all), full(gq), full(gks), full(gkw), full(e),
                      full(tail)],
            out_specs=[qt_spec, chunk_spec, chunk_spec] + [row(w_) for w_, _ in outs] + [per_block(8)],
            scratch_shapes=[pltpu.VMEM((2, tm, LANES), f32)]),
        out_shape=[qt_shape, chunk_shape, chunk_shape]
        + [jax.ShapeDtypeStruct((n, w_), dt) for w_, dt in outs] + [gate_shape],
        compiler_params=pltpu.CompilerParams(dimension_semantics=("arbitrary",),
                                             vmem_limit_bytes=VMEM_LIMIT),
        name="inproj",
    )(layer, x2, gm, w_all, gq, gks, gkw, e, tail)


def _gelu_tanh(x):
    return 0.5 * x * (1.0 + jnp.tanh(0.7978845608028654 * (x + 0.044715 * (x * x * x))))


def _compress_body(layer_ref, zk_ref, zv_ref, w1_ref, pe_ref, b1_ref, w2k_ref, w2v_ref, b2k_ref, b2v_ref,
                   gk_ref, e_ref, kc_ref, vc_ref, *, tc):
    last = lax.broadcasted_iota(i32, (tc, 1), 0) == tc - 1

    def hidden(z_ref, kind):
        z = z_ref[0]
        a = _dot(z, w1_ref[kind, 0])
        b = _dot(z, w1_ref[kind, 1])
        b = jnp.concatenate([b[1:], jnp.zeros((1, b.shape[1]), f32)], axis=0)
        bias = (_dot(pe_ref[kind, 0], w1_ref[kind, 0]) + _dot(pe_ref[kind, 1], w1_ref[kind, 1]))[0:1]
        return _gelu_tanh(a + b + bias + b1_ref[kind]).astype(bf16)

    k = _dot(hidden(zk_ref, 0), w2k_ref[...]) + b2k_ref[...]
    k = _head_norm(k, e_ref[...], gk_ref[...])
    pos = lax.broadcasted_iota(i32, (tc, LANES), 0) * CMP_STRIDE + (CMP_BLOCK - 1)
    k = k + _pos_cols(pos, 2 * LANES)
    kc_ref[0] = jnp.where(last, 0.0, k).astype(bf16)
    v = _dot(hidden(zv_ref, 1), w2v_ref[...]) + b2v_ref[...]
    vc_ref[0] = jnp.where(last, 0.0, v + _ones_col(tc, 2 * LANES)).astype(bf16)


def _compress(layer, zk, zv, w1_all, pe, b1, w2k, w2v, b2k, b2v, gk, e):
    b, tc, _ = zk.shape
    blk = lambda a: pl.BlockSpec((1,) + a.shape[1:], lambda i, layer: (i,) + (0,) * (a.ndim - 1))
    full = lambda a: pl.BlockSpec(a.shape, lambda i, layer: (0,) * a.ndim)
    return pl.pallas_call(
        functools.partial(_compress_body, tc=tc),
        grid_spec=pltpu.PrefetchScalarGridSpec(
            num_scalar_prefetch=1, grid=(b,),
            in_specs=[blk(zk), blk(zv), _layer_weight(w1_all)]
            + [full(a) for a in (pe, b1, w2k, w2v, b2k, b2v, gk, e)],
            out_specs=[pl.BlockSpec((1, tc, 2 * LANES), lambda i, layer: (i, 0, 0))] * 2),
        out_shape=[jax.ShapeDtypeStruct((b, tc, 2 * LANES), bf16)] * 2,
        compiler_params=pltpu.CompilerParams(dimension_semantics=("arbitrary",),
                                             vmem_limit_bytes=VMEM_LIMIT),
        name="compress",
    )(layer, zk, zv, w1_all, pe, b1, w2k, w2v, b2k, b2v, gk, e)


def _query_pos(c):
    lane = lax.broadcasted_iota(i32, (1, HEADS_PER_GROUP * Q_BLOCK), 1)
    return c * Q_BLOCK + (lane & (Q_BLOCK - 1))


def _pick_top(vs, jf, n_pick):
    vs = list(vs)
    for _ in range(n_pick):
        for g, v in enumerate(vs):
            mx = jnp.max(v, axis=0, keepdims=True)
            idx = jnp.min(jnp.where(v == mx, jf, float(jf.shape[0])), axis=0, keepdims=True)
            vs[g] = jnp.where(jf == idx, PICKED, v)
    return tuple(vs)


def _cmp_variant(nchunk, c0, qt_ref, kc_ref, vc_ref, mct_ref, tri_ref, o_ref, madd_ref, lst_ref, cnt_ref,
                 *, nsel):
    units = [(qb, g) for qb in range(CMP_QB) for g in range(N_GROUPS)]
    nrow = nchunk * CMP_CHUNK
    npre = nchunk * SEL_PER_CHUNK
    tqs = [_query_pos(c0 + qb) for qb in range(CMP_QB)]
    sub = lax.broadcasted_iota(i32, (CMP_CHUNK, 1), 0)
    lanes = lambda g: slice(g * LANES, (g + 1) * LANES)
    chunk = lambda a, k: a[k * CMP_CHUNK:(k + 1) * CMP_CHUNK]

    tiles = []
    for qb, g in units:
        s = _dot(kc_ref[0, 0:nrow, lanes(g)], qt_ref[qb, g])
        row = []
        for k in range(nchunk):
            t = chunk(s, k)
            if k >= nchunk - 2:
                seen = (k * CMP_CHUNK + sub) * CMP_STRIDE + (CMP_BLOCK - 1) <= tqs[qb]
                t = jnp.where(seen, t, NEG)
            row.append(t)
        tiles.append(row)

    accs, imps = [], []
    for u, (qb, g) in enumerate(units):
        m = tiles[u][0].max(axis=0, keepdims=True)
        for t in tiles[u][1:]:
            m = jnp.maximum(m, t.max(axis=0, keepdims=True))
        parts = [jnp.exp2(t - m).astype(bf16) for t in tiles[u]]
        accs.append(_dot_tn(vc_ref[0, 0:nrow, lanes(g)], jnp.concatenate(parts, axis=0)))
        rows, carry = [], None
        for part in parts:
            piece = _dot(mct_ref[...], part)
            body = piece[0:SEL_PER_CHUNK]
            if carry is not None:
                body = jnp.concatenate([body[0:8] + carry, body[8:]], axis=0)
            rows.append(body)
            carry = piece[SEL_PER_CHUNK:IMP_ROWS]
        imps.append(rows[0] if nchunk == 1 else jnp.concatenate(rows, axis=0))

    j = lax.broadcasted_iota(i32, (npre, Q_BLOCK), 0)
    jf = j.astype(f32)
    t1s = [tq[:, 0:Q_BLOCK] for tq in tqs]
    valids = [j * SEL_BLOCK <= t1 for t1 in t1s]
    vs = []
    for u, (qb, g) in enumerate(units):
        any_key = tqs[qb] >= CMP_BLOCK - 1
        inv = jnp.where(any_key, 1.0 / jnp.maximum(accs[u][HEAD_DIM:HEAD_DIM + 1], 1e-30), 0.0)
        o_ref[0, g, qb] = accs[u][0:HEAD_DIM] * inv
        imp4 = imps[u] * inv
        imp = imp4[:, 0:Q_BLOCK]
        for r in range(1, HEADS_PER_GROUP):
            imp = imp + imp4[:, r * Q_BLOCK:(r + 1) * Q_BLOCK]
        jt = t1s[qb] >> 6
        v = jnp.where(valids[qb], imp, NEG)
        vs.append(jnp.where(j == 0, PICKED, jnp.where(j == jt, PICKED, jnp.where(j == jt - 1, PICKED, v))))
    vs = _pick_top(vs, jf, SEL_TOPK - 3)

    rr = lax.broadcasted_iota(i32, (nsel, nsel), 0).astype(f32)
    ones8 = jnp.ones((8, Q_BLOCK), bf16)
    jrow = jnp.broadcast_to(lax.broadcasted_iota(i32, (1, nsel), 1).astype(f32), (8, nsel)).astype(bf16)
    for u, (qb, g) in enumerate(units):
        picked = vs[u] == PICKED
        madd = jnp.where(valids[qb], jnp.where(picked, 0.0, NEG), NEG)
        sel = jnp.where(valids[qb], jnp.where(picked, 1.0, 0.0), 0.0).astype(bf16)
        if npre < nsel:
            madd = jnp.concatenate([madd, jnp.full((nsel - npre, Q_BLOCK), NEG, f32)], axis=0)
            sel = jnp.concatenate([sel, jnp.zeros((nsel - npre, Q_BLOCK), bf16)], axis=0)
        madd_ref[0, g, qb] = madd
        flag = jnp.where(_dot_nt(ones8, sel)[0:1] > 0.0, 1.0, 0.0)
        flag8 = jnp.broadcast_to(flag, (8, nsel)).astype(bf16)
        prefix = _dot(flag8, tri_ref[...])[0:1]
        place = jnp.where(prefix == rr, flag, 0.0).astype(bf16)
        lst_ref[0, g, qb] = _dot_nt(jrow, place).astype(i32)
        cnt_ref[0, g, qb] = _dot(flag8, jnp.ones((nsel, LANES), bf16)).astype(i32)


def _cmp_body(qt_ref, kc_ref, vc_ref, mct_ref, tri_ref, o_ref, madd_ref, lst_ref, cnt_ref, *, nsel, nvar):
    c0 = pl.program_id(1) * CMP_QB
    c_last = c0 + CMP_QB - 1
    nch = (c_last * (Q_BLOCK // CMP_STRIDE) + (Q_BLOCK - CMP_BLOCK) // CMP_STRIDE) // CMP_CHUNK + 1
    for n in range(1, nvar + 1):
        pl.when(nch == n)(functools.partial(
            _cmp_variant, n, c0, qt_ref, kc_ref, vc_ref, mct_ref, tri_ref, o_ref, madd_ref, lst_ref, cnt_ref,
            nsel=nsel))


def _cmp_attention(qt, kc, vc, mct, tri, *, batch):
    g = N_GROUPS
    nqb = qt.shape[0] // batch
    tc = kc.shape[1]
    nsel = tc * CMP_STRIDE // SEL_BLOCK
    width = HEADS_PER_GROUP * Q_BLOCK
    nstep = nqb // CMP_QB
    per_q = lambda r_, c_: pl.BlockSpec((1, g, CMP_QB, r_, c_), lambda bi, ci: (bi, 0, ci, 0, 0))
    shape = lambda r_, c_, dt: jax.ShapeDtypeStruct((batch, g, nqb, r_, c_), dt)
    return pl.pallas_call(
        functools.partial(_cmp_body, nsel=nsel, nvar=tc // CMP_CHUNK),
        grid=(batch, nstep),
        in_specs=[pl.BlockSpec((CMP_QB, g, LANES, width), lambda bi, ci: (bi * nstep + ci, 0, 0, 0)),
                  pl.BlockSpec((1, tc, 2 * LANES), lambda bi, ci: (bi, 0, 0)),
                  pl.BlockSpec((1, tc, 2 * LANES), lambda bi, ci: (bi, 0, 0)),
                  pl.BlockSpec(mct.shape, lambda bi, ci: (0, 0)),
                  pl.BlockSpec(tri.shape, lambda bi, ci: (0, 0))],
        out_specs=[per_q(HEAD_DIM, width), per_q(nsel, Q_BLOCK), per_q(8, nsel), per_q(8, LANES)],
        out_shape=[shape(HEAD_DIM, width, f32), shape(nsel, Q_BLOCK, f32),
                   shape(8, nsel, i32), shape(8, LANES, i32)],
        compiler_params=pltpu.CompilerParams(dimension_semantics=("arbitrary",) * 2,
                                             vmem_limit_bytes=VMEM_LIMIT),
        name="cmp_attention",
    )(qt, kc, vc, mct, tri)


def _normalize(o_aug):
    return o_aug[0:HEAD_DIM] * (1.0 / jnp.maximum(o_aug[HEAD_DIM:HEAD_DIM + 1], 1e-30))


def _slc_win_body(lst_ref, cnt_ref, qt_ref, ks_ref, vs_ref, kw_ref, vw_ref,
                  madd_ref, ocmp_ref, gate_ref, wtab_ref, out_ref, s_scr, *, nsel):
    width = HEADS_PER_GROUP * Q_BLOCK
    nwin = (WINDOW + Q_BLOCK) // LANES
    wq = WINDOW // Q_BLOCK
    grp_rows = GROUP_BLOCKS * SEL_BLOCK

    class Block:
        def __init__(self, qb):
            self.qb = qb
            self.c = pl.program_id(2) * QB_PER_STEP + qb
            self.q0 = pl.multiple_of(self.c * Q_BLOCK, Q_BLOCK)
            self.qta = qt_ref[qb, 0]
            self.n_off = cnt_ref[qb, 0, 0] - 2

        def block_ids(self, first, nblk):
            ids = []
            for u in range(nblk):
                i = first + u
                j = lst_ref[self.qb, 0, jnp.minimum(i, nsel - 1)]
                ids.append((i < self.n_off, j, pl.multiple_of(j * SEL_BLOCK, SEL_BLOCK)))
            return ids

        def scores(self, ids):
            kcat = jnp.concatenate([ks_ref[0, pl.ds(r0, SEL_BLOCK), :] for _, _, r0 in ids], axis=0)
            sg = _dot(kcat, self.qta)
            tiles = []
            for u, (live, j, _) in enumerate(ids):
                mrow = jnp.where(live, madd_ref[0, 0, self.qb, pl.ds(j, 1), :], NEG)
                mrow = jnp.concatenate([mrow] * HEADS_PER_GROUP, axis=1)
                tiles.append(sg[u * SEL_BLOCK:(u + 1) * SEL_BLOCK] + mrow)
            return tiles

        def values(self, ids):
            return jnp.concatenate([vs_ref[0, pl.ds(r0, SEL_BLOCK), :] for _, _, r0 in ids], axis=0)

    def front(blk):
        c, q0, qta = blk.c, blk.q0, blk.qta
        ws = pl.multiple_of(jnp.maximum(c - wq, 0) * Q_BLOCK, Q_BLOCK)
        s = _dot(kw_ref[0, pl.ds(ws, WINDOW + Q_BLOCK), :], qta)
        chunks = []
        for k in range(nwin):
            steady = 1 if k == 0 else (2 if k == nwin - 1 else 0)
            tab = jnp.where(c >= wq, steady, jnp.where(k < c, 0, jnp.where(k == c, 2, 3)))
            chunks.append(s[k * LANES:(k + 1) * LANES] + wtab_ref[tab])
        head = blk.block_ids(0, HEAD_BLOCKS)
        tiles = [_dot(ks_ref[0, pl.ds(q0, Q_BLOCK), :], qta) + wtab_ref[2]] + blk.scores(head)
        return ws, chunks, head, tiles

    def softmax_pv(tiles, v):
        m = tiles[0].max(axis=0, keepdims=True)
        for su in tiles[1:]:
            m = jnp.maximum(m, su.max(axis=0, keepdims=True))
        p = jnp.concatenate([jnp.exp2(su - m).astype(bf16) for su in tiles], axis=0)
        return m, _dot_tn(v, p)

    def middle(blk, ws, chunks, head, tiles):
        _, o_win = softmax_pv(chunks, vw_ref[0, pl.ds(ws, WINDOW + Q_BLOCK), :])
        m_run, o_run = softmax_pv(
            tiles, jnp.concatenate([vs_ref[0, pl.ds(blk.q0, Q_BLOCK), :], blk.values(head)], axis=0))
        return _normalize(o_win), m_run, o_run

    def rest(blk, m_run, o_run):
        def segment(si, carry):
            m_run, o_run = carry
            base = HEAD_BLOCKS + si * SUP_BLOCKS
            ngrp = (jnp.minimum(SUP_BLOCKS, blk.n_off - base) + GROUP_BLOCKS - 1) // GROUP_BLOCKS
            rows = lambda gi: pl.ds(pl.multiple_of(gi * grp_rows, grp_rows), grp_rows)

            def score(gi, mx):
                sg = jnp.concatenate(blk.scores(blk.block_ids(base + gi * GROUP_BLOCKS, GROUP_BLOCKS)), axis=0)
                s_scr[rows(gi), :] = sg
                return jnp.maximum(mx, sg.max(axis=0, keepdims=True))

            m_new = lax.fori_loop(0, ngrp, score, m_run)

            def weigh(gi, acc):
                pg = jnp.exp2(s_scr[rows(gi), :] - m_new).astype(bf16)
                return acc + _dot_tn(blk.values(blk.block_ids(base + gi * GROUP_BLOCKS, GROUP_BLOCKS)), pg)

            o_seg = lax.fori_loop(0, ngrp, weigh, jnp.zeros((LANES, width), f32))
            return m_new, jnp.exp2(m_run - m_new) * o_run + o_seg

        nseg = (jnp.maximum(blk.n_off - HEAD_BLOCKS, 0) + SUP_BLOCKS - 1) // SUP_BLOCKS
        return lax.fori_loop(0, nseg, segment, (m_run, o_run))[1]

    blocks = [Block(qb) for qb in range(QB_PER_STEP)]
    fronts = [front(blk) for blk in blocks]
    fronts = [middle(blk, *f) for blk, f in zip(blocks, fronts)]
    tails = [rest(blk, m_run, o_run) for blk, (_, m_run, o_run) in zip(blocks, fronts)]
    for blk, (o_win, _, _), o_run in zip(blocks, fronts, tails):
        qb = blk.qb
        gate = gate_ref[qb, 0]
        mix = gate[0:1] * ocmp_ref[0, 0, qb] + gate[1:2] * _normalize(o_run) + gate[2:3] * o_win
        rows = jnp.concatenate([mix[:, r * Q_BLOCK:(r + 1) * Q_BLOCK] for r in range(HEADS_PER_GROUP)],
                               axis=0)
        out_ref[0, qb * Q_BLOCK:(qb + 1) * Q_BLOCK, :] = rows.T.astype(bf16)


def _slc_win_attention(lst, cnt, qt, ks, vs, kw, vw, madd, ocmp, gate_t, wtab):
    b, t, _ = ks.shape
    g = N_GROUPS
    nqb, nsel = t // Q_BLOCK, t // SEL_BLOCK
    width = HEADS_PER_GROUP * Q_BLOCK
    nstep = nqb // QB_PER_STEP
    flat = lambda bi, gi, ci: (bi * g + gi) * nstep + ci
    smem = lambda w_: pl.BlockSpec((QB_PER_STEP, 1, w_), lambda bi, gi, ci: (flat(bi, gi, ci), 0, 0),
                                   memory_space=pltpu.SMEM)
    per_q = lambda r_, c_: pl.BlockSpec((1, 1, QB_PER_STEP, r_, c_), lambda bi, gi, ci: (bi, gi, ci, 0, 0))
    keys = pl.BlockSpec((1, t, LANES), lambda bi, gi, ci: (bi, 0, gi), pipeline_mode=pl.Buffered(1))
    return pl.pallas_call(
        functools.partial(_slc_win_body, nsel=nsel),
        grid=(b, g, nstep),
        in_specs=[smem(nsel), smem(LANES),
                  pl.BlockSpec((QB_PER_STEP, 1, LANES, width), lambda bi, gi, ci: (bi * nstep + ci, gi, 0, 0)),
                  keys, keys, keys, keys,
                  per_q(nsel, Q_BLOCK), per_q(HEAD_DIM, width),
                  pl.BlockSpec((QB_PER_STEP, 1, 8, width), lambda bi, gi, ci: (bi * nstep + ci, gi, 0, 0)),
                  pl.BlockSpec(wtab.shape, lambda bi, gi, ci: (0, 0, 0))],
        out_specs=pl.BlockSpec((1, QB_PER_STEP * Q_BLOCK, 2 * LANES), lambda bi, gi, ci: (bi, ci, gi)),
        out_shape=jax.ShapeDtypeStruct((b, t, ATTN_WIDTH), bf16),
        scratch_shapes=[pltpu.VMEM((SUP_BLOCKS * SEL_BLOCK, width), f32)],
        compiler_params=pltpu.CompilerParams(dimension_semantics=("arbitrary",) * 3,
                                             vmem_limit_bytes=VMEM_LIMIT),
        name="slc_win_attention",
    )(lst, cnt, qt, ks, vs, kw, vw, madd, ocmp, gate_t, wtab)


PREV_ROWS = 16


def _rms(v, g):
    return (v * lax.rsqrt(jnp.mean(v * v, axis=-1, keepdims=True) + EPS)) * g


def _mix_ffn_body(layer_ref, x_ref, attn_ref, u_ref, uprev_ref, bg_ref, cw_ref, go_ref, wo_ref, gf_ref, wu_ref,
                  wd_ref,
                  out_ref, *, tm, seq, chunk):
    first = (pl.program_id(0) * tm) % seq == 0
    u = u_ref[...].astype(f32)
    prev = jnp.where(first, 0.0, uprev_ref[0].astype(f32))
    ext = jnp.concatenate([prev, u], axis=0)
    cw = cw_ref[...]
    conv = (cw[0:1] * ext[PREV_ROWS - 2:PREV_ROWS - 2 + tm] + cw[1:2] * ext[PREV_ROWS - 1:PREV_ROWS - 1 + tm]
            + cw[2:3] * u)
    conv = bg_ref[...].astype(f32) * conv
    go = go_ref[...]
    mixed = jnp.concatenate([_rms(attn_ref[...].astype(f32), go[:, :ATTN_WIDTH]),
                             _rms(conv, go[:, ATTN_WIDTH:])], axis=1).astype(bf16)
    x = x_ref[...] + _dot(mixed, wo_ref[...])
    h = _rms(x, gf_ref[...]).astype(bf16)
    acc = x
    for c in range(D_FF // chunk):
        a = jnp.maximum(_dot(h, wu_ref[:, c * chunk:(c + 1) * chunk]), 0.0)
        acc = acc + _dot((a * a).astype(bf16), wd_ref[c * chunk:(c + 1) * chunk, :])
    out_ref[...] = acc


def _mix_ffn(layer, x2, attn2, u, bgate, cw, go, wo_all, gf, wu_all, wd_all, *, seq, tm=512, chunk=1024):
    n = x2.shape[0]
    uprev = u.reshape(n // PREV_ROWS, PREV_ROWS, CONV_WIDTH)
    row = lambda w_: pl.BlockSpec((tm, w_), lambda i, layer: (i, 0))
    full = lambda a: pl.BlockSpec(a.shape, lambda i, layer: (0,) * a.ndim)
    once = lambda a: pl.BlockSpec((None,) + a.shape[1:], lambda i, layer: (layer[0],) + (0,) * (a.ndim - 1),
                                  pipeline_mode=pl.Buffered(1))
    return pl.pallas_call(
        functools.partial(_mix_ffn_body, tm=tm, seq=seq, chunk=chunk),
        grid_spec=pltpu.PrefetchScalarGridSpec(
            num_scalar_prefetch=1, grid=(n // tm,),
            in_specs=[row(D_MODEL), row(ATTN_WIDTH), row(CONV_WIDTH),
                      pl.BlockSpec((1, PREV_ROWS, CONV_WIDTH),
                                   lambda i, layer: (jnp.maximum(i * (tm // PREV_ROWS) - 1, 0), 0, 0)),
                      row(CONV_WIDTH), full(cw), full(go), once(wo_all), full(gf), once(wu_all), once(wd_all)],
            out_specs=row(D_MODEL)),
        out_shape=jax.ShapeDtypeStruct((n, D_MODEL), f32),
        input_output_aliases={1: 0},
        compiler_params=pltpu.CompilerParams(dimension_semantics=("arbitrary",),
                                             vmem_limit_bytes=VMEM_LIMIT),
        name="mix_ffn",
    )(layer, x2, attn2, u, uprev, bgate, cw, go, wo_all, gf, wu_all, wd_all)


def _interleave_zero(w):
    z = jnp.zeros(w.shape[:-1] + (HEAD_DIM,), w.dtype)
    return jnp.concatenate([w[..., :HEAD_DIM], z, w[..., HEAD_DIM:], z], axis=-1)


def _prep_params(g_mix_norm, w_in, g_q, g_k, pe_cmp, w_cmp1, b_cmp1, w_cmp2, b_cmp2,
                 conv_w, g_out, w_o, g_ffn_norm, w_up, w_down):
    depth = w_in.shape[0]
    o = np.cumsum([0, ATTN_WIDTH] + [LANES] * 6 + [N_BRANCH * N_HEADS] + [CONV_WIDTH] * 3)
    part = lambda i: w_in[..., int(o[i]):int(o[i + 1])]
    q, kc, vc, ks, vs, kw, vw, gl, hc, cg, bg = [part(i) for i in range(11)]
    gl = jnp.pad(gl, ((0, 0), (0, 0), (0, LANES - gl.shape[-1])))
    w = jnp.concatenate([q, kc, vc, gl, jnp.zeros_like(gl), _interleave_zero(ks), _interleave_zero(kw),
                         _interleave_zero(vs), _interleave_zero(vw), hc, cg, bg], axis=-1).astype(bf16)
    tile2 = lambda gk: _interleave_zero(jnp.concatenate([gk, gk], axis=-1))[:, None, :]
    w1 = w_cmp1.astype(bf16).reshape(depth, 2, 2, CMP_STRIDE, HEAD_DIM, CMP_HIDDEN)
    z1 = jnp.zeros_like(w1)
    w1g = jnp.stack([jnp.concatenate([w1, z1], axis=-1), jnp.concatenate([z1, w1], axis=-1)], axis=4)
    w1g = w1g.reshape(depth, 2, 2, CMP_STRIDE * LANES, N_GROUPS * CMP_HIDDEN)
    pe = pe_cmp.reshape(depth, 2, 2, CMP_STRIDE, 1, HEAD_DIM)
    pe = jnp.broadcast_to(pe, (depth, 2, 2, CMP_STRIDE, N_GROUPS, HEAD_DIM)).reshape(depth, 2, 2, 1, -1)
    pe = jnp.pad(pe, ((0, 0), (0, 0), (0, 0), (0, 7), (0, 0))).astype(bf16)
    b1 = jnp.concatenate([b_cmp1, b_cmp1], axis=-1)[:, :, None, :]
    z2 = jnp.zeros_like(w_cmp2)
    w2 = jnp.concatenate([jnp.concatenate([w_cmp2, z2], axis=-1), jnp.concatenate([z2, w_cmp2], axis=-1)],
                         axis=2)
    w2 = _interleave_zero(w2).astype(bf16)
    b2 = _interleave_zero(jnp.concatenate([b_cmp2, b_cmp2], axis=-1))[:, :, None, :]
    return dict(
        gm=g_mix_norm[:, None, :], w=w,
        gq=jnp.tile(g_q, (1, N_HEADS))[:, None, :],
        gks=tile2(g_k[:, 1]), gkw=tile2(g_k[:, 2]), gkc=tile2(g_k[:, 0]),
        w1=w1g, pe=pe, b1=b1,
        w2k=w2[:, 0], w2v=w2[:, 1], b2k=b2[:, 0], b2v=b2[:, 1],
        cw=jnp.pad(conv_w, ((0, 0), (0, 8 - conv_w.shape[1]), (0, 0))),
        go=g_out[:, None, :], wo=w_o.astype(bf16),
        gf=g_ffn_norm[:, None, :], wu=w_up.astype(bf16), wd=w_down.astype(bf16),
    )


def _constants(nsel):
    lane = np.arange(LANES)
    e = (lane[:, None] // HEAD_DIM == lane[None, :] // HEAD_DIM).astype(np.float32)
    tail = np.zeros((N_GROUPS, HEAD_DIM, HEADS_PER_GROUP * Q_BLOCK), np.float32)
    for gi in range(N_GROUPS):
        for r in range(HEADS_PER_GROUP):
            rest = np.float64(2.0 ** -(gi * HEADS_PER_GROUP + r + 1)) * LOG2E
            for k in range(SLOPE_TERMS):
                term = np.float64(np.asarray(rest, np.float32).astype(jnp.bfloat16).astype(np.float32))
                tail[gi, k, r * Q_BLOCK:(r + 1) * Q_BLOCK] = term * LANES
                tail[gi, SLOPE_TERMS + k, r * Q_BLOCK:(r + 1) * Q_BLOCK] = term
                rest = rest - term
    mct = np.zeros((IMP_ROWS, CMP_CHUNK), np.float32)
    for i in range(CMP_CHUNK):
        lo, hi = i * CMP_STRIDE, i * CMP_STRIDE + CMP_BLOCK
        for jj in range(SEL_PER_CHUNK + 1):
            ov = min(hi, (jj + 1) * SEL_BLOCK) - max(lo, jj * SEL_BLOCK)
            if ov > 0:
                mct[jj, i] = ov / CMP_BLOCK
    kk = np.arange(LANES)[:, None]
    ql = np.tile(np.arange(Q_BLOCK), HEADS_PER_GROUP)[None, :]
    wtab = np.zeros((4, LANES, HEADS_PER_GROUP * Q_BLOCK), np.float32)
    wtab[1] = np.where(kk > ql, 0.0, NEG)
    wtab[2] = np.where(kk <= ql, 0.0, NEG)
    wtab[3] = NEG
    tri = np.arange(nsel)[:, None] < np.arange(nsel)[None, :]
    return (jnp.asarray(e, bf16), jnp.asarray(tail, bf16), jnp.asarray(mct, bf16), jnp.asarray(wtab),
            jnp.asarray(tri, bf16))


def _layer(x2, layer, p, big, consts, *, batch, seq):
    e, tail, mct, wtab, tri = consts
    g = N_GROUPS
    nqb, nsel, tc = seq // Q_BLOCK, seq // SEL_BLOCK, seq // CMP_STRIDE
    qt, kc, vc, ks, kw, vs, vw, u, bgate, gate_t = _inproj(
        layer, x2, p["gm"], big["w"], p["gq"], p["gks"], p["gkw"], e, tail, seq=seq)
    kcmp, vcmp = _compress(layer, kc.reshape(batch, tc, CMP_STRIDE * LANES),
                           vc.reshape(batch, tc, CMP_STRIDE * LANES),
                           big["w1"], p["pe"], p["b1"], p["w2k"], p["w2v"], p["b2k"], p["b2v"], p["gkc"], e)
    ocmp, madd, lst, cnt = _cmp_attention(qt, kcmp, vcmp, mct, tri, batch=batch)
    rows3 = lambda a: a.reshape(batch, seq, 2 * LANES)
    smem = lambda a: a[:, :, :, 0, :].reshape(batch * g * nqb, 1, a.shape[-1])
    attn = _slc_win_attention(smem(lst), smem(cnt), qt, rows3(ks), rows3(vs),
                              rows3(kw), rows3(vw), madd, ocmp, gate_t, wtab)
    return _mix_ffn(layer, x2, attn.reshape(batch * seq, ATTN_WIDTH), u, bgate, p["cw"], p["go"], big["wo"],
                    p["gf"], big["wu"], big["wd"], seq=seq)


def kernel(x, g_mix_norm, w_in, g_q, g_k, pe_cmp, w_cmp1, b_cmp1, w_cmp2, b_cmp2, conv_w, g_out, w_o,
           g_ffn_norm, w_up, w_down):
    batch, seq, d = x.shape
    assert d == D_MODEL and seq % (CMP_CHUNK * CMP_STRIDE) == 0 and seq >= WINDOW + Q_BLOCK
    params = _prep_params(g_mix_norm, w_in, g_q, g_k, pe_cmp, w_cmp1, b_cmp1, w_cmp2, b_cmp2,
                          conv_w, g_out, w_o, g_ffn_norm, w_up, w_down)
    consts = _constants(seq // SEL_BLOCK)

    big = {k: params.pop(k) for k in ("w", "w1", "wo", "wu", "wd")}
    layers = jnp.arange(w_in.shape[0], dtype=i32)[:, None]

    def step(x2, xs):
        layer, p = xs
        return _layer(x2, layer, p, big, consts, batch=batch, seq=seq), None

    x2, _ = lax.scan(step, x.reshape(batch * seq, d), (layers, params))
    return x2.reshape(batch, seq, d)
```

```python
import functools

import numpy as np
import jax
import jax.numpy as jnp
from jax import lax
from jax.experimental import pallas as pl
from jax.experimental.pallas import tpu as pltpu

f32 = jnp.float32
bf16 = jnp.bfloat16
i32 = jnp.int32

D_MODEL = 1024
HEAD_DIM = 64
N_HEADS = 8
N_GROUPS = 2
HEADS_PER_GROUP = 4
ATTN_WIDTH = 512
CONV_WIDTH = 512
N_BRANCH = 3
CMP_BLOCK = 32
CMP_STRIDE = 16
CMP_HIDDEN = 256
SEL_BLOCK = 64
SEL_TOPK = 16
WINDOW = 512
Q_BLOCK = 128
D_FF = 4096
EPS = 1e-6
NEG = -1e30
LOG2E = 1.4426950408889634
SLOPE_TERMS = 3
PICKED = -3e38
LANES = 128
CMP_CHUNK = 256
SEL_PER_CHUNK = CMP_CHUNK * CMP_STRIDE // SEL_BLOCK
IMP_ROWS = SEL_PER_CHUNK + 8
VMEM_LIMIT = 56 * 1024 * 1024

PROJ_CHUNK = 512
SUP_BLOCKS = 32
HEAD_BLOCKS = 20
GROUP_BLOCKS = 4
QB_PER_STEP = 4
CMP_QB = 2


def _dot(a, b):
    return jnp.dot(a, b, preferred_element_type=f32)


def _dot_nt(a, b):
    return lax.dot_general(a, b, (((1,), (1,)), ((), ())), preferred_element_type=f32)


def _dot_tn(a, b):
    return lax.dot_general(a, b, (((0,), (0,)), ((), ())), preferred_element_type=f32)


def _head_norm(z, e, g):
    sq = (z * z).astype(bf16)
    outs = []
    for c in range(z.shape[1] // LANES):
        sl = slice(c * LANES, (c + 1) * LANES)
        outs.append(z[:, sl] * lax.rsqrt(_dot(sq[:, sl], e) * (1.0 / HEAD_DIM) + EPS))
    y = outs[0] if len(outs) == 1 else jnp.concatenate(outs, axis=1)
    return y * g


def _pos_cols(pos, width):
    rows = pos.shape[0]
    lane = lax.broadcasted_iota(i32, (rows, LANES), 1) - HEAD_DIM
    tile = jnp.where(lane < 0, 0.0, jnp.where(lane < SLOPE_TERMS, (pos >> 7).astype(f32),
                                              jnp.where(lane < 2 * SLOPE_TERMS, (pos & 127).astype(f32), 0.0)))
    return tile if width == LANES else jnp.concatenate([tile] * (width // LANES), axis=1)


def _ones_col(rows, width):
    lane = lax.broadcasted_iota(i32, (rows, LANES), 1)
    tile = jnp.where(lane == HEAD_DIM, 1.0, 0.0)
    return tile if width == LANES else jnp.concatenate([tile] * (width // LANES), axis=1)


def _inproj_body(layer_ref, x_ref, gm_ref, w_ref, gq_ref, gks_ref, gkw_ref, e_ref, tail_ref,
                 qt_ref, kc_ref, vc_ref, ks_ref, kw_ref, vs_ref, vw_ref, u_ref, bg_ref, gate_ref, kv_scr,
                 *, tm, seq):
    x = x_ref[...]
    ms = jnp.mean(x * x, axis=-1, keepdims=True)
    h = ((x * lax.rsqrt(ms + EPS)) * gm_ref[...]).astype(bf16)
    e = e_ref[...]
    two = 2 * LANES

    def proj(k):
        return _dot(h, w_ref[:, k * PROJ_CHUNK:(k + 1) * PROJ_CHUNK])

    def emit_queries(z):
        qn = _head_norm(z, e, gq_ref[...]) * (HEAD_DIM ** -0.5 * LOG2E)
        gw = HEADS_PER_GROUP * HEAD_DIM
        for cb in range(tm // Q_BLOCK):
            for g in range(N_GROUPS):
                blk = qn[cb * Q_BLOCK:(cb + 1) * Q_BLOCK, g * gw:(g + 1) * gw].T
                top = jnp.concatenate([blk[r * HEAD_DIM:(r + 1) * HEAD_DIM] for r in range(HEADS_PER_GROUP)],
                                      axis=1)
                qt_ref[cb, g] = jnp.concatenate([top.astype(bf16), tail_ref[g]], axis=0)

    def emit_compress_inputs(z):
        kv_scr[0] = z[:, 0:LANES]
        kv_scr[1] = z[:, LANES:two]
        for p in range(CMP_STRIDE):
            rows = pl.ds(p, tm // CMP_STRIDE, stride=CMP_STRIDE)
            kc_ref[:, p * LANES:(p + 1) * LANES] = kv_scr[0, rows, :].astype(bf16)
            vc_ref[:, p * LANES:(p + 1) * LANES] = kv_scr[1, rows, :].astype(bf16)
        gates = 1.0 / (1.0 + jnp.exp(-z[:, two:two + LANES]))
        pad = jnp.zeros((8 - N_BRANCH, HEADS_PER_GROUP * Q_BLOCK), f32)
        for cb in range(tm // Q_BLOCK):
            gt = gates[cb * Q_BLOCK:(cb + 1) * Q_BLOCK].T
            for g in range(N_GROUPS):
                col = lambda r, br: (g * HEADS_PER_GROUP + r) * N_BRANCH + br
                rows = [jnp.concatenate([gt[col(r, br):col(r, br) + 1] for r in range(HEADS_PER_GROUP)], axis=1)
                        for br in range(N_BRANCH)]
                gate_ref[cb, g] = jnp.concatenate(rows + [pad], axis=0)

    def emit_keys(z):
        t0 = (pl.program_id(0) * tm) % seq
        pc = _pos_cols(lax.broadcasted_iota(i32, (tm, LANES), 0) + t0, two)
        ks_ref[...] = (_head_norm(z[:, 0:two], e, gks_ref[...]) + pc).astype(bf16)
        kw_ref[...] = (_head_norm(z[:, two:], e, gkw_ref[...]) + pc).astype(bf16)

    def emit_values(z):
        ones = _ones_col(tm, two)
        vs_ref[...] = (z[:, 0:two] + ones).astype(bf16)
        vw_ref[...] = (z[:, two:] + ones).astype(bf16)

    z0 = proj(0)
    z1 = proj(1)
    emit_queries(z0)
    z2 = proj(2)
    emit_compress_inputs(z1)
    z3 = proj(3)
    emit_keys(z2)
    z4 = proj(4)
    emit_values(z3)
    z5 = proj(5)
    z6 = proj(6)
    u_ref[...] = (z4 * z5).astype(bf16)
    bg_ref[...] = z6.astype(bf16)


def _layer_weight(a):
    return pl.BlockSpec((None,) + a.shape[1:], lambda i, layer: (layer[0],) + (0,) * (a.ndim - 1))


def _inproj(layer, x2, gm, w_all, gq, gks, gkw, e, tail, *, seq, tm=512):
    n = x2.shape[0]
    width = HEADS_PER_GROUP * Q_BLOCK
    row = lambda w_: pl.BlockSpec((tm, w_), lambda i, layer: (i, 0))
    full = lambda a: pl.BlockSpec(a.shape, lambda i, layer: (0,) * a.ndim)
    outs = [(2 * LANES, bf16), (2 * LANES, bf16),
            (2 * LANES, bf16), (2 * LANES, bf16), (CONV_WIDTH, bf16), (CONV_WIDTH, bf16)]
    per_block = lambda rows: pl.BlockSpec((tm // Q_BLOCK, N_GROUPS, rows, width), lambda i, layer: (i, 0, 0, 0))
    qt_spec = per_block(LANES)
    qt_shape = jax.ShapeDtypeStruct((n // Q_BLOCK, N_GROUPS, LANES, width), bf16)
    gate_shape = jax.ShapeDtypeStruct((n // Q_BLOCK, N_GROUPS, 8, width), f32)
    chunk_spec = pl.BlockSpec((tm // CMP_STRIDE, CMP_STRIDE * LANES), lambda i, layer: (i, 0))
    chunk_shape = jax.ShapeDtypeStruct((n // CMP_STRIDE, CMP_STRIDE * LANES), bf16)
    return pl.pallas_call(
        functools.partial(_inproj_body, tm=tm, seq=seq),
        grid_spec=pltpu.PrefetchScalarGridSpec(
            num_scalar_prefetch=1, grid=(n // tm,),
            in_specs=[row(D_MODEL), full(gm), _layer_weight(w_all), full(gq), full(gks), full(gkw), full(e),
                      full(tail)],
            out_specs=[qt_spec, chunk_spec, chunk_spec] + [row(w_) for w_, _ in outs] + [per_block(8)],
            scratch_shapes=[pltpu.VMEM((2, tm, LANES), f32)]),
        out_shape=[qt_shape, chunk_shape, chunk_shape]
        + [jax.ShapeDtypeStruct((n, w_), dt) for w_, dt in outs] + [gate_shape],
        compiler_params=pltpu.CompilerParams(dimension_semantics=("arbitrary",),
                                             vmem_limit_bytes=VMEM_LIMIT),
        name="inproj",
    )(layer, x2, gm, w_all, gq, gks, gkw, e, tail)


def _gelu_tanh(x):
    return 0.5 * x * (1.0 + jnp.tanh(0.7978845608028654 * (x + 0.044715 * (x * x * x))))


def _compress_body(layer_ref, zk_ref, zv_ref, w1_ref, pe_ref, b1_ref, w2k_ref, w2v_ref, b2k_ref, b2v_ref,
                   gk_ref, e_ref, kc_ref, vc_ref, *, tc):
    last = lax.broadcasted_iota(i32, (tc, 1), 0) == tc - 1

    def hidden(z_ref, kind):
        z = z_ref[0]
        a = _dot(z, w1_ref[kind, 0])
        b = _dot(z, w1_ref[kind, 1])
        b = jnp.concatenate([b[1:], jnp.zeros((1, b.shape[1]), f32)], axis=0)
        bias = (_dot(pe_ref[kind, 0], w1_ref[kind, 0]) + _dot(pe_ref[kind, 1], w1_ref[kind, 1]))[0:1]
        return _gelu_tanh(a + b + bias + b1_ref[kind]).astype(bf16)

    k = _dot(hidden(zk_ref, 0), w2k_ref[...]) + b2k_ref[...]
    k = _head_norm(k, e_ref[...], gk_ref[...])
    pos = lax.broadcasted_iota(i32, (tc, LANES), 0) * CMP_STRIDE + (CMP_BLOCK - 1)
    k = k + _pos_cols(pos, 2 * LANES)
    kc_ref[0] = jnp.where(last, 0.0, k).astype(bf16)
    v = _dot(hidden(zv_ref, 1), w2v_ref[...]) + b2v_ref[...]
    vc_ref[0] = jnp.where(last, 0.0, v + _ones_col(tc, 2 * LANES)).astype(bf16)


def _compress(layer, zk, zv, w1_all, pe, b1, w2k, w2v, b2k, b2v, gk, e):
    b, tc, _ = zk.shape
    blk = lambda a: pl.BlockSpec((1,) + a.shape[1:], lambda i, layer: (i,) + (0,) * (a.ndim - 1))
    full = lambda a: pl.BlockSpec(a.shape, lambda i, layer: (0,) * a.ndim)
    return pl.pallas_call(
        functools.partial(_compress_body, tc=tc),
        grid_spec=pltpu.PrefetchScalarGridSpec(
            num_scalar_prefetch=1, grid=(b,),
            in_specs=[blk(zk), blk(zv), _layer_weight(w1_all)]
            + [full(a) for a in (pe, b1, w2k, w2v, b2k, b2v, gk, e)],
            out_specs=[pl.BlockSpec((1, tc, 2 * LANES), lambda i, layer: (i, 0, 0))] * 2),
        out_shape=[jax.ShapeDtypeStruct((b, tc, 2 * LANES), bf16)] * 2,
        compiler_params=pltpu.CompilerParams(dimension_semantics=("arbitrary",),
                                             vmem_limit_bytes=VMEM_LIMIT),
        name="compress",
    )(layer, zk, zv, w1_all, pe, b1, w2k, w2v, b2k, b2v, gk, e)


def _query_pos(c):
    lane = lax.broadcasted_iota(i32, (1, HEADS_PER_GROUP * Q_BLOCK), 1)
    return c * Q_BLOCK + (lane & (Q_BLOCK - 1))


def _pick_top(vs, jf, n_pick):
    vs = list(vs)
    for _ in range(n_pick):
        for g, v in enumerate(vs):
            mx = jnp.max(v, axis=0, keepdims=True)
            idx = jnp.min(jnp.where(v == mx, jf, float(jf.shape[0])), axis=0, keepdims=True)
            vs[g] = jnp.where(jf == idx, PICKED, v)
    return tuple(vs)


def _cmp_variant(nchunk, c0, qt_ref, kc_ref, vc_ref, mct_ref, tri_ref, o_ref, madd_ref, lst_ref, cnt_ref,
                 *, nsel):
    units = [(qb, g) for qb in range(CMP_QB) for g in range(N_GROUPS)]
    nrow = nchunk * CMP_CHUNK
    npre = nchunk * SEL_PER_CHUNK
    tqs = [_query_pos(c0 + qb) for qb in range(CMP_QB)]
    sub = lax.broadcasted_iota(i32, (CMP_CHUNK, 1), 0)
    lanes = lambda g: slice(g * LANES, (g + 1) * LANES)
    chunk = lambda a, k: a[k * CMP_CHUNK:(k + 1) * CMP_CHUNK]

    tiles = []
    for qb, g in units:
        s = _dot(kc_ref[0, 0:nrow, lanes(g)], qt_ref[qb, g])
        row = []
        for k in range(nchunk):
            t = chunk(s, k)
            if k >= nchunk - 2:
                seen = (k * CMP_CHUNK + sub) * CMP_STRIDE + (CMP_BLOCK - 1) <= tqs[qb]
                t = jnp.where(seen, t, NEG)
            row.append(t)
        tiles.append(row)

    accs, imps = [], []
    for u, (qb, g) in enumerate(units):
        m = tiles[u][0].max(axis=0, keepdims=True)
        for t in tiles[u][1:]:
            m = jnp.maximum(m, t.max(axis=0, keepdims=True))
        parts = [jnp.exp2(t - m).astype(bf16) for t in tiles[u]]
        accs.append(_dot_tn(vc_ref[0, 0:nrow, lanes(g)], jnp.concatenate(parts, axis=0)))
        rows, carry = [], None
        for part in parts:
            piece = _dot(mct_ref[...], part)
            body = piece[0:SEL_PER_CHUNK]
            if carry is not None:
                body = jnp.concatenate([body[0:8] + carry, body[8:]], axis=0)
            rows.append(body)
            carry = piece[SEL_PER_CHUNK:IMP_ROWS]
        imps.append(rows[0] if nchunk == 1 else jnp.concatenate(rows, axis=0))

    j = lax.broadcasted_iota(i32, (npre, Q_BLOCK), 0)
    jf = j.astype(f32)
    t1s = [tq[:, 0:Q_BLOCK] for tq in tqs]
    valids = [j * SEL_BLOCK <= t1 for t1 in t1s]
    vs = []
    for u, (qb, g) in enumerate(units):
        any_key = tqs[qb] >= CMP_BLOCK - 1
        inv = jnp.where(any_key, 1.0 / jnp.maximum(accs[u][HEAD_DIM:HEAD_DIM + 1], 1e-30), 0.0)
        o_ref[0, g, qb] = accs[u][0:HEAD_DIM] * inv
        imp4 = imps[u] * inv
        imp = imp4[:, 0:Q_BLOCK]
        for r in range(1, HEADS_PER_GROUP):
            imp = imp + imp4[:, r * Q_BLOCK:(r + 1) * Q_BLOCK]
        jt = t1s[qb] >> 6
        v = jnp.where(valids[qb], imp, NEG)
        vs.append(jnp.where(j == 0, PICKED, jnp.where(j == jt, PICKED, jnp.where(j == jt - 1, PICKED, v))))
    vs = _pick_top(vs, jf, SEL_TOPK - 3)

    rr = lax.broadcasted_iota(i32, (nsel, nsel), 0).astype(f32)
    ones8 = jnp.ones((8, Q_BLOCK), bf16)
    jrow = jnp.broadcast_to(lax.broadcasted_iota(i32, (1, nsel), 1).astype(f32), (8, nsel)).astype(bf16)
    for u, (qb, g) in enumerate(units):
        picked = vs[u] == PICKED
        madd = jnp.where(valids[qb], jnp.where(picked, 0.0, NEG), NEG)
        sel = jnp.where(valids[qb], jnp.where(picked, 1.0, 0.0), 0.0).astype(bf16)
        if npre < nsel:
            madd = jnp.concatenate([madd, jnp.full((nsel - npre, Q_BLOCK), NEG, f32)], axis=0)
            sel = jnp.concatenate([sel, jnp.zeros((nsel - npre, Q_BLOCK), bf16)], axis=0)
        madd_ref[0, g, qb] = madd
        flag = jnp.where(_dot_nt(ones8, sel)[0:1] > 0.0, 1.0, 0.0)
        flag8 = jnp.broadcast_to(flag, (8, nsel)).astype(bf16)
        prefix = _dot(flag8, tri_ref[...])[0:1]
        place = jnp.where(prefix == rr, flag, 0.0).astype(bf16)
        lst_ref[0, g, qb] = _dot_nt(jrow, place).astype(i32)
        cnt_ref[0, g, qb] = _dot(flag8, jnp.ones((nsel, LANES), bf16)).astype(i32)


def _cmp_body(qt_ref, kc_ref, vc_ref, mct_ref, tri_ref, o_ref, madd_ref, lst_ref, cnt_ref, *, nsel, nvar):
    c0 = pl.program_id(1) * CMP_QB
    c_last = c0 + CMP_QB - 1
    nch = (c_last * (Q_BLOCK // CMP_STRIDE) + (Q_BLOCK - CMP_BLOCK) // CMP_STRIDE) // CMP_CHUNK + 1
    for n in range(1, nvar + 1):
        pl.when(nch == n)(functools.partial(
            _cmp_variant, n, c0, qt_ref, kc_ref, vc_ref, mct_ref, tri_ref, o_ref, madd_ref, lst_ref, cnt_ref,
            nsel=nsel))


def _cmp_attention(qt, kc, vc, mct, tri, *, batch):
    g = N_GROUPS
    nqb = qt.shape[0] // batch
    tc = kc.shape[1]
    nsel = tc * CMP_STRIDE // SEL_BLOCK
    width = HEADS_PER_GROUP * Q_BLOCK
    nstep = nqb // CMP_QB
    per_q = lambda r_, c_: pl.BlockSpec((1, g, CMP_QB, r_, c_), lambda bi, ci: (bi, 0, ci, 0, 0))
    shape = lambda r_, c_, dt: jax.ShapeDtypeStruct((batch, g, nqb, r_, c_), dt)
    return pl.pallas_call(
        functools.partial(_cmp_body, nsel=nsel, nvar=tc // CMP_CHUNK),
        grid=(batch, nstep),
        in_specs=[pl.BlockSpec((CMP_QB, g, LANES, width), lambda bi, ci: (bi * nstep + ci, 0, 0, 0)),
                  pl.BlockSpec((1, tc, 2 * LANES), lambda bi, ci: (bi, 0, 0)),
                  pl.BlockSpec((1, tc, 2 * LANES), lambda bi, ci: (bi, 0, 0)),
                  pl.BlockSpec(mct.shape, lambda bi, ci: (0, 0)),
                  pl.BlockSpec(tri.shape, lambda bi, ci: (0, 0))],
        out_specs=[per_q(HEAD_DIM, width), per_q(nsel, Q_BLOCK), per_q(8, nsel), per_q(8, LANES)],
        out_shape=[shape(HEAD_DIM, width, f32), shape(nsel, Q_BLOCK, f32),
                   shape(8, nsel, i32), shape(8, LANES, i32)],
        compiler_params=pltpu.CompilerParams(dimension_semantics=("arbitrary",) * 2,
                                             vmem_limit_bytes=VMEM_LIMIT),
        name="cmp_attention",
    )(qt, kc, vc, mct, tri)


def _normalize(o_aug):
    return o_aug[0:HEAD_DIM] * (1.0 / jnp.maximum(o_aug[HEAD_DIM:HEAD_DIM + 1], 1e-30))


def _slc_win_body(lst_ref, cnt_ref, qt_ref, ks_ref, vs_ref, kw_ref, vw_ref,
                  madd_ref, ocmp_ref, gate_ref, wtab_ref, out_ref, s_scr, *, nsel):
    width = HEADS_PER_GROUP * Q_BLOCK
    nwin = (WINDOW + Q_BLOCK) // LANES
    wq = WINDOW // Q_BLOCK
    grp_rows = GROUP_BLOCKS * SEL_BLOCK

    class Block:
        def __init__(self, qb):
            self.qb = qb
            self.c = pl.program_id(2) * QB_PER_STEP + qb
            self.q0 = pl.multiple_of(self.c * Q_BLOCK, Q_BLOCK)
            self.qta = qt_ref[qb, 0]
            self.n_off = cnt_ref[qb, 0, 0] - 2

        def block_ids(self, first, nblk):
            ids = []
            for u in range(nblk):
                i = first + u
                j = lst_ref[self.qb, 0, jnp.minimum(i, nsel - 1)]
                ids.append((i < self.n_off, j, pl.multiple_of(j * SEL_BLOCK, SEL_BLOCK)))
            return ids

        def scores(self, ids):
            kcat = jnp.concatenate([ks_ref[0, pl.ds(r0, SEL_BLOCK), :] for _, _, r0 in ids], axis=0)
            sg = _dot(kcat, self.qta)
            tiles = []
            for u, (live, j, _) in enumerate(ids):
                mrow = jnp.where(live, madd_ref[0, 0, self.qb, pl.ds(j, 1), :], NEG)
                mrow = jnp.concatenate([mrow] * HEADS_PER_GROUP, axis=1)
                tiles.append(sg[u * SEL_BLOCK:(u + 1) * SEL_BLOCK] + mrow)
            return tiles

        def values(self, ids):
            return jnp.concatenate([vs_ref[0, pl.ds(r0, SEL_BLOCK), :] for _, _, r0 in ids], axis=0)

    def front(blk):
        c, q0, qta = blk.c, blk.q0, blk.qta
        ws = pl.multiple_of(jnp.maximum(c - wq, 0) * Q_BLOCK, Q_BLOCK)
        s = _dot(kw_ref[0, pl.ds(ws, WINDOW + Q_BLOCK), :], qta)
        chunks = []
        for k in range(nwin):
            steady = 1 if k == 0 else (2 if k == nwin - 1 else 0)
            tab = jnp.where(c >= wq, steady, jnp.where(k < c, 0, jnp.where(k == c, 2, 3)))
            chunks.append(s[k * LANES:(k + 1) * LANES] + wtab_ref[tab])
        head = blk.block_ids(0, HEAD_BLOCKS)
        tiles = [_dot(ks_ref[0, pl.ds(q0, Q_BLOCK), :], qta) + wtab_ref[2]] + blk.scores(head)
        return ws, chunks, head, tiles

    def softmax_pv(tiles, v):
        m = tiles[0].max(axis=0, keepdims=True)
        for su in tiles[1:]:
            m = jnp.maximum(m, su.max(axis=0, keepdims=True))
        p = jnp.concatenate([jnp.exp2(su - m).astype(bf16) for su in tiles], axis=0)
        return m, _dot_tn(v, p)

    def middle(blk, ws, chunks, head, tiles):
        _, o_win = softmax_pv(chunks, vw_ref[0, pl.ds(ws, WINDOW + Q_BLOCK), :])
        m_run, o_run = softmax_pv(
            tiles, jnp.concatenate([vs_ref[0, pl.ds(blk.q0, Q_BLOCK), :], blk.values(head)], axis=0))
        return _normalize(o_win), m_run, o_run

    def rest(blk, m_run, o_run):
        def segment(si, carry):
            m_run, o_run = carry
            base = HEAD_BLOCKS + si * SUP_BLOCKS
            ngrp = (jnp.minimum(SUP_BLOCKS, blk.n_off - base) + GROUP_BLOCKS - 1) // GROUP_BLOCKS
            rows = lambda gi: pl.ds(pl.multiple_of(gi * grp_rows, grp_rows), grp_rows)

            def score(gi, mx):
                sg = jnp.concatenate(blk.scores(blk.block_ids(base + gi * GROUP_BLOCKS, GROUP_BLOCKS)), axis=0)
                s_scr[rows(gi), :] = sg
                return jnp.maximum(mx, sg.max(axis=0, keepdims=True))

            m_new = lax.fori_loop(0, ngrp, score, m_run)

            def weigh(gi, acc):
                pg = jnp.exp2(s_scr[rows(gi), :] - m_new).astype(bf16)
                return acc + _dot_tn(blk.values(blk.block_ids(base + gi * GROUP_BLOCKS, GROUP_BLOCKS)), pg)

            o_seg = lax.fori_loop(0, ngrp, weigh, jnp.zeros((LANES, width), f32))
            return m_new, jnp.exp2(m_run - m_new) * o_run + o_seg

        nseg = (jnp.maximum(blk.n_off - HEAD_BLOCKS, 0) + SUP_BLOCKS - 1) // SUP_BLOCKS
        return lax.fori_loop(0, nseg, segment, (m_run, o_run))[1]

    blocks = [Block(qb) for qb in range(QB_PER_STEP)]
    fronts = [front(blk) for blk in blocks]
    fronts = [middle(blk, *f) for blk, f in zip(blocks, fronts)]
    tails = [rest(blk, m_run, o_run) for blk, (_, m_run, o_run) in zip(blocks, fronts)]
    for blk, (o_win, _, _), o_run in zip(blocks, fronts, tails):
        qb = blk.qb
        gate = gate_ref[qb, 0]
        mix = gate[0:1] * ocmp_ref[0, 0, qb] + gate[1:2] * _normalize(o_run) + gate[2:3] * o_win
        rows = jnp.concatenate([mix[:, r * Q_BLOCK:(r + 1) * Q_BLOCK] for r in range(HEADS_PER_GROUP)],
                               axis=0)
        out_ref[0, qb * Q_BLOCK:(qb + 1) * Q_BLOCK, :] = rows.T.astype(bf16)


def _slc_win_attention(lst, cnt, qt, ks, vs, kw, vw, madd, ocmp, gate_t, wtab):
    b, t, _ = ks.shape
    g = N_GROUPS
    nqb, nsel = t // Q_BLOCK, t // SEL_BLOCK
    width = HEADS_PER_GROUP * Q_BLOCK
    nstep = nqb // QB_PER_STEP
    flat = lambda bi, gi, ci: (bi * g + gi) * nstep + ci
    smem = lambda w_: pl.BlockSpec((QB_PER_STEP, 1, w_), lambda bi, gi, ci: (flat(bi, gi, ci), 0, 0),
                                   memory_space=pltpu.SMEM)
    per_q = lambda r_, c_: pl.BlockSpec((1, 1, QB_PER_STEP, r_, c_), lambda bi, gi, ci: (bi, gi, ci, 0, 0))
    keys = pl.BlockSpec((1, t, LANES), lambda bi, gi, ci: (bi, 0, gi), pipeline_mode=pl.Buffered(1))
    return pl.pallas_call(
        functools.partial(_slc_win_body, nsel=nsel),
        grid=(b, g, nstep),
        in_specs=[smem(nsel), smem(LANES),
                  pl.BlockSpec((QB_PER_STEP, 1, LANES, width), lambda bi, gi, ci: (bi * nstep + ci, gi, 0, 0)),
                  keys, keys, keys, keys,
                  per_q(nsel, Q_BLOCK), per_q(HEAD_DIM, width),
                  pl.BlockSpec((QB_PER_STEP, 1, 8, width), lambda bi, gi, ci: (bi * nstep + ci, gi, 0, 0)),
                  pl.BlockSpec(wtab.shape, lambda bi, gi, ci: (0, 0, 0))],
        out_specs=pl.BlockSpec((1, QB_PER_STEP * Q_BLOCK, 2 * LANES), lambda bi, gi, ci: (bi, ci, gi)),
        out_shape=jax.ShapeDtypeStruct((b, t, ATTN_WIDTH), bf16),
        scratch_shapes=[pltpu.VMEM((SUP_BLOCKS * SEL_BLOCK, width), f32)],
        compiler_params=pltpu.CompilerParams(dimension_semantics=("arbitrary",) * 3,
                                             vmem_limit_bytes=VMEM_LIMIT),
        name="slc_win_attention",
    )(lst, cnt, qt, ks, vs, kw, vw, madd, ocmp, gate_t, wtab)


PREV_ROWS = 16


def _rms(v, g):
    return (v * lax.rsqrt(jnp.mean(v * v, axis=-1, keepdims=True) + EPS)) * g


def _mix_ffn_body(layer_ref, x_ref, attn_ref, u_ref, uprev_ref, bg_ref, cw_ref, go_ref, wo_ref, gf_ref, wu_ref,
                  wd_ref,
                  out_ref, *, tm, seq, chunk):
    first = (pl.program_id(0) * tm) % seq == 0
    u = u_ref[...].astype(f32)
    prev = jnp.where(first, 0.0, uprev_ref[0].astype(f32))
    ext = jnp.concatenate([prev, u], axis=0)
    cw = cw_ref[...]
    conv = (cw[0:1] * ext[PREV_ROWS - 2:PREV_ROWS - 2 + tm] + cw[1:2] * ext[PREV_ROWS - 1:PREV_ROWS - 1 + tm]
            + cw[2:3] * u)
    conv = bg_ref[...].astype(f32) * conv
    go = go_ref[...]
    mixed = jnp.concatenate([_rms(attn_ref[...].astype(f32), go[:, :ATTN_WIDTH]),
                             _rms(conv, go[:, ATTN_WIDTH:])], axis=1).astype(bf16)
    x = x_ref[...] + _dot(mixed, wo_ref[...])
    h = _rms(x, gf_ref[...]).astype(bf16)
    acc = x
    for c in range(D_FF // chunk):
        a = jnp.maximum(_dot(h, wu_ref[:, c * chunk:(c + 1) * chunk]), 0.0)
        acc = acc + _dot((a * a).astype(bf16), wd_ref[c * chunk:(c + 1) * chunk, :])
    out_ref[...] = acc


def _mix_ffn(layer, x2, attn2, u, bgate, cw, go, wo_all, gf, wu_all, wd_all, *, seq, in_place, tm=512, chunk=1024):
    n = x2.shape[0]
    uprev = u.reshape(n // PREV_ROWS, PREV_ROWS, CONV_WIDTH)
    row = lambda w_: pl.BlockSpec((tm, w_), lambda i, layer: (i, 0))
    full = lambda a: pl.BlockSpec(a.shape, lambda i, layer: (0,) * a.ndim)
    once = lambda a: pl.BlockSpec((None,) + a.shape[1:], lambda i, layer: (layer[0],) + (0,) * (a.ndim - 1),
                                  pipeline_mode=pl.Buffered(1))
    return pl.pallas_call(
        functools.partial(_mix_ffn_body, tm=tm, seq=seq, chunk=chunk),
        grid_spec=pltpu.PrefetchScalarGridSpec(
            num_scalar_prefetch=1, grid=(n // tm,),
            in_specs=[row(D_MODEL), row(ATTN_WIDTH), row(CONV_WIDTH),
                      pl.BlockSpec((1, PREV_ROWS, CONV_WIDTH),
                                   lambda i, layer: (jnp.maximum(i * (tm // PREV_ROWS) - 1, 0), 0, 0)),
                      row(CONV_WIDTH), full(cw), full(go), once(wo_all), full(gf), once(wu_all), once(wd_all)],
            out_specs=row(D_MODEL)),
        out_shape=jax.ShapeDtypeStruct((n, D_MODEL), f32),
        input_output_aliases={1: 0} if in_place else {},
        compiler_params=pltpu.CompilerParams(dimension_semantics=("arbitrary",),
                                             vmem_limit_bytes=VMEM_LIMIT),
        name="mix_ffn",
    )(layer, x2, attn2, u, uprev, bgate, cw, go, wo_all, gf, wu_all, wd_all)


def _interleave_zero(w):
    z = jnp.zeros(w.shape[:-1] + (HEAD_DIM,), w.dtype)
    return jnp.concatenate([w[..., :HEAD_DIM], z, w[..., HEAD_DIM:], z], axis=-1)


def _prep_params(g_mix_norm, w_in, g_q, g_k, pe_cmp, w_cmp1, b_cmp1, w_cmp2, b_cmp2,
                 conv_w, g_out, w_o, g_ffn_norm, w_up, w_down):
    depth = w_in.shape[0]
    o = np.cumsum([0, ATTN_WIDTH] + [LANES] * 6 + [N_BRANCH * N_HEADS] + [CONV_WIDTH] * 3)
    part = lambda i: w_in[..., int(o[i]):int(o[i + 1])]
    q, kc, vc, ks, vs, kw, vw, gl, hc, cg, bg = [part(i) for i in range(11)]
    gl = jnp.pad(gl, ((0, 0), (0, 0), (0, LANES - gl.shape[-1])))
    w = jnp.concatenate([q, kc, vc, gl, jnp.zeros_like(gl), _interleave_zero(ks), _interleave_zero(kw),
                         _interleave_zero(vs), _interleave_zero(vw), hc, cg, bg], axis=-1).astype(bf16)
    tile2 = lambda gk: _interleave_zero(jnp.concatenate([gk, gk], axis=-1))[:, None, :]
    w1 = w_cmp1.astype(bf16).reshape(depth, 2, 2, CMP_STRIDE, HEAD_DIM, CMP_HIDDEN)
    z1 = jnp.zeros_like(w1)
    w1g = jnp.stack([jnp.concatenate([w1, z1], axis=-1), jnp.concatenate([z1, w1], axis=-1)], axis=4)
    w1g = w1g.reshape(depth, 2, 2, CMP_STRIDE * LANES, N_GROUPS * CMP_HIDDEN)
    pe = pe_cmp.reshape(depth, 2, 2, CMP_STRIDE, 1, HEAD_DIM)
    pe = jnp.broadcast_to(pe, (depth, 2, 2, CMP_STRIDE, N_GROUPS, HEAD_DIM)).reshape(depth, 2, 2, 1, -1)
    pe = jnp.pad(pe, ((0, 0), (0, 0), (0, 0), (0, 7), (0, 0))).astype(bf16)
    b1 = jnp.concatenate([b_cmp1, b_cmp1], axis=-1)[:, :, None, :]
    z2 = jnp.zeros_like(w_cmp2)
    w2 = jnp.concatenate([jnp.concatenate([w_cmp2, z2], axis=-1), jnp.concatenate([z2, w_cmp2], axis=-1)],
                         axis=2)
    w2 = _interleave_zero(w2).astype(bf16)
    b2 = _interleave_zero(jnp.concatenate([b_cmp2, b_cmp2], axis=-1))[:, :, None, :]
    return dict(
        gm=g_mix_norm[:, None, :], w=w,
        gq=jnp.tile(g_q, (1, N_HEADS))[:, None, :],
        gks=tile2(g_k[:, 1]), gkw=tile2(g_k[:, 2]), gkc=tile2(g_k[:, 0]),
        w1=w1g, pe=pe, b1=b1,
        w2k=w2[:, 0], w2v=w2[:, 1], b2k=b2[:, 0], b2v=b2[:, 1],
        cw=jnp.pad(conv_w, ((0, 0), (0, 8 - conv_w.shape[1]), (0, 0))),
        go=g_out[:, None, :], wo=w_o.astype(bf16),
        gf=g_ffn_norm[:, None, :], wu=w_up.astype(bf16), wd=w_down.astype(bf16),
    )


def _constants(nsel):
    lane = np.arange(LANES)
    e = (lane[:, None] // HEAD_DIM == lane[None, :] // HEAD_DIM).astype(np.float32)
    tail = np.zeros((N_GROUPS, HEAD_DIM, HEADS_PER_GROUP * Q_BLOCK), np.float32)
    for gi in range(N_GROUPS):
        for r in range(HEADS_PER_GROUP):
            rest = np.float64(2.0 ** -(gi * HEADS_PER_GROUP + r + 1)) * LOG2E
            for k in range(SLOPE_TERMS):
                term = np.float64(np.asarray(rest, np.float32).astype(jnp.bfloat16).astype(np.float32))
                tail[gi, k, r * Q_BLOCK:(r + 1) * Q_BLOCK] = term * LANES
                tail[gi, SLOPE_TERMS + k, r * Q_BLOCK:(r + 1) * Q_BLOCK] = term
                rest = rest - term
    mct = np.zeros((IMP_ROWS, CMP_CHUNK), np.float32)
    for i in range(CMP_CHUNK):
        lo, hi = i * CMP_STRIDE, i * CMP_STRIDE + CMP_BLOCK
        for jj in range(SEL_PER_CHUNK + 1):
            ov = min(hi, (jj + 1) * SEL_BLOCK) - max(lo, jj * SEL_BLOCK)
            if ov > 0:
                mct[jj, i] = ov / CMP_BLOCK
    kk = np.arange(LANES)[:, None]
    ql = np.tile(np.arange(Q_BLOCK), HEADS_PER_GROUP)[None, :]
    wtab = np.zeros((4, LANES, HEADS_PER_GROUP * Q_BLOCK), np.float32)
    wtab[1] = np.where(kk > ql, 0.0, NEG)
    wtab[2] = np.where(kk <= ql, 0.0, NEG)
    wtab[3] = NEG
    tri = np.arange(nsel)[:, None] < np.arange(nsel)[None, :]
    return (jnp.asarray(e, bf16), jnp.asarray(tail, bf16), jnp.asarray(mct, bf16), jnp.asarray(wtab),
            jnp.asarray(tri, bf16))


def _layer(x2, layer, p, big, consts, *, batch, seq, in_place):
    e, tail, mct, wtab, tri = consts
    g = N_GROUPS
    nqb, nsel, tc = seq // Q_BLOCK, seq // SEL_BLOCK, seq // CMP_STRIDE
    qt, kc, vc, ks, kw, vs, vw, u, bgate, gate_t = _inproj(
        layer, x2, p["gm"], big["w"], p["gq"], p["gks"], p["gkw"], e, tail, seq=seq)
    kcmp, vcmp = _compress(layer, kc.reshape(batch, tc, CMP_STRIDE * LANES),
                           vc.reshape(batch, tc, CMP_STRIDE * LANES),
                           big["w1"], p["pe"], p["b1"], p["w2k"], p["w2v"], p["b2k"], p["b2v"], p["gkc"], e)
    ocmp, madd, lst, cnt = _cmp_attention(qt, kcmp, vcmp, mct, tri, batch=batch)
    rows3 = lambda a: a.reshape(batch, seq, 2 * LANES)
    smem = lambda a: a[:, :, :, 0, :].reshape(batch * g * nqb, 1, a.shape[-1])
    attn = _slc_win_attention(smem(lst), smem(cnt), qt, rows3(ks), rows3(vs),
                              rows3(kw), rows3(vw), madd, ocmp, gate_t, wtab)
    return _mix_ffn(layer, x2, attn.reshape(batch * seq, ATTN_WIDTH), u, bgate, p["cw"], p["go"], big["wo"],
                    p["gf"], big["wu"], big["wd"], seq=seq, in_place=in_place)


def kernel(x, g_mix_norm, w_in, g_q, g_k, pe_cmp, w_cmp1, b_cmp1, w_cmp2, b_cmp2, conv_w, g_out, w_o,
           g_ffn_norm, w_up, w_down):
    batch, seq, d = x.shape
    assert d == D_MODEL and seq % (CMP_CHUNK * CMP_STRIDE) == 0 and seq >= WINDOW + Q_BLOCK
    params = _prep_params(g_mix_norm, w_in, g_q, g_k, pe_cmp, w_cmp1, b_cmp1, w_cmp2, b_cmp2,
                          conv_w, g_out, w_o, g_ffn_norm, w_up, w_down)
    consts = _constants(seq // SEL_BLOCK)

    big = {k: params.pop(k) for k in ("w", "w1", "wo", "wu", "wd")}
    layers = jnp.arange(w_in.shape[0], dtype=i32)[:, None]

    def step(x2, xs):
        layer, p = xs
        return _layer(x2, layer, p, big, consts, batch=batch, seq=seq, in_place=True), None

    first = jax.tree.map(lambda a: a[0], (layers, params))
    rest = jax.tree.map(lambda a: a[1:], (layers, params))
    x2 = _layer(x.reshape(batch * seq, d), *first, big, consts, batch=batch, seq=seq, in_place=False)
    x2, _ = lax.scan(step, x2, rest)
    return x2.reshape(batch, seq, d)
```

```python
import functools

import numpy as np
import jax
import jax.numpy as jnp
from jax import lax
from jax.experimental import pallas as pl
from jax.experimental.pallas import tpu as pltpu

f32 = jnp.float32
bf16 = jnp.bfloat16
i32 = jnp.int32

D_MODEL = 1024
HEAD_DIM = 64
N_HEADS = 8
N_GROUPS = 2
HEADS_PER_GROUP = 4
ATTN_WIDTH = 512
CONV_WIDTH = 512
N_BRANCH = 3
CMP_BLOCK = 32
CMP_STRIDE = 16
CMP_HIDDEN = 256
SEL_BLOCK = 64
SEL_TOPK = 16
WINDOW = 512
Q_BLOCK = 128
D_FF = 4096
EPS = 1e-6
NEG = -1e30
LOG2E = 1.4426950408889634
SLOPE_TERMS = 3
PICKED = -3e38
LANES = 128
CMP_CHUNK = 256
SEL_PER_CHUNK = CMP_CHUNK * CMP_STRIDE // SEL_BLOCK
IMP_ROWS = SEL_PER_CHUNK + 8
VMEM_LIMIT = 56 * 1024 * 1024

PROJ_CHUNK = 512
SUP_BLOCKS = 32
HEAD_BLOCKS = 20
GROUP_BLOCKS = 4
QB_PER_STEP = 4
CMP_QB = 2


def _dot(a, b):
    return jnp.dot(a, b, preferred_element_type=f32)


def _dot_nt(a, b):
    return lax.dot_general(a, b, (((1,), (1,)), ((), ())), preferred_element_type=f32)


def _dot_tn(a, b):
    return lax.dot_general(a, b, (((0,), (0,)), ((), ())), preferred_element_type=f32)


def _head_norm(z, e, g):
    sq = (z * z).astype(bf16)
    outs = []
    step = e.shape[0]
    for c in range(z.shape[1] // step):
        sl = slice(c * step, (c + 1) * step)
        outs.append(z[:, sl] * lax.rsqrt(_dot(sq[:, sl], e) * (1.0 / HEAD_DIM) + EPS))
    y = outs[0] if len(outs) == 1 else jnp.concatenate(outs, axis=1)
    return y * g


def _pos_cols(pos, width):
    rows = pos.shape[0]
    lane = lax.broadcasted_iota(i32, (rows, LANES), 1) - HEAD_DIM
    tile = jnp.where(lane < 0, 0.0, jnp.where(lane < SLOPE_TERMS, (pos >> 7).astype(f32),
                                              jnp.where(lane < 2 * SLOPE_TERMS, (pos & 127).astype(f32), 0.0)))
    return tile if width == LANES else jnp.concatenate([tile] * (width // LANES), axis=1)


def _ones_col(rows, width):
    lane = lax.broadcasted_iota(i32, (rows, LANES), 1)
    tile = jnp.where(lane == HEAD_DIM, 1.0, 0.0)
    return tile if width == LANES else jnp.concatenate([tile] * (width // LANES), axis=1)


def _inproj_body(layer_ref, x_ref, gm_ref, w_ref, gq_ref, gks_ref, gkw_ref, e_ref, tail_ref,
                 qt_ref, kc_ref, vc_ref, ks_ref, kw_ref, vs_ref, vw_ref, u_ref, bg_ref, gate_ref, kv_scr,
                 *, tm, seq):
    x = x_ref[...]
    ms = jnp.mean(x * x, axis=-1, keepdims=True)
    h = ((x * lax.rsqrt(ms + EPS)) * gm_ref[...]).astype(bf16)
    e = e_ref[...]
    two = 2 * LANES

    def proj(k):
        return _dot(h, w_ref[:, k * PROJ_CHUNK:(k + 1) * PROJ_CHUNK])

    def emit_queries(z):
        qn = _head_norm(z, e, gq_ref[...]) * (HEAD_DIM ** -0.5 * LOG2E)
        gw = HEADS_PER_GROUP * HEAD_DIM
        for cb in range(tm // Q_BLOCK):
            for g in range(N_GROUPS):
                blk = qn[cb * Q_BLOCK:(cb + 1) * Q_BLOCK, g * gw:(g + 1) * gw].T
                top = jnp.concatenate([blk[r * HEAD_DIM:(r + 1) * HEAD_DIM] for r in range(HEADS_PER_GROUP)],
                                      axis=1)
                qt_ref[cb, g] = jnp.concatenate([top.astype(bf16), tail_ref[g]], axis=0)

    def emit_compress_inputs(z):
        kv_scr[0] = z[:, 0:LANES]
        kv_scr[1] = z[:, LANES:two]
        for p in range(CMP_STRIDE):
            rows = pl.ds(p, tm // CMP_STRIDE, stride=CMP_STRIDE)
            kc_ref[:, p * LANES:(p + 1) * LANES] = kv_scr[0, rows, :].astype(bf16)
            vc_ref[:, p * LANES:(p + 1) * LANES] = kv_scr[1, rows, :].astype(bf16)
        gates = 1.0 / (1.0 + jnp.exp(-z[:, two:two + LANES]))
        pad = jnp.zeros((8 - N_BRANCH, HEADS_PER_GROUP * Q_BLOCK), f32)
        for cb in range(tm // Q_BLOCK):
            gt = gates[cb * Q_BLOCK:(cb + 1) * Q_BLOCK].T
            for g in range(N_GROUPS):
                col = lambda r, br: (g * HEADS_PER_GROUP + r) * N_BRANCH + br
                rows = [jnp.concatenate([gt[col(r, br):col(r, br) + 1] for r in range(HEADS_PER_GROUP)], axis=1)
                        for br in range(N_BRANCH)]
                gate_ref[cb, g] = jnp.concatenate(rows + [pad], axis=0)

    def emit_keys_values(z):
        t0 = (pl.program_id(0) * tm) % seq
        pc = _pos_cols(lax.broadcasted_iota(i32, (tm, LANES), 0) + t0, LANES)
        ones = _ones_col(tm, LANES)
        low = lax.broadcasted_iota(i32, (tm, LANES), 1) < HEAD_DIM

        def spread(t, extra):
            other = pltpu.roll(t, HEAD_DIM, axis=1)
            return jnp.concatenate([jnp.where(low, t, extra), jnp.where(low, other, extra)], axis=1).astype(bf16)

        ks_ref[...] = spread(_head_norm(z[:, 0:LANES], e, gks_ref[...]), pc)
        kw_ref[...] = spread(_head_norm(z[:, LANES:two], e, gkw_ref[...]), pc)
        vs_ref[...] = spread(z[:, two:two + LANES], ones)
        vw_ref[...] = spread(z[:, two + LANES:], ones)

    z0 = proj(0)
    z1 = proj(1)
    emit_queries(z0)
    z2 = proj(2)
    emit_compress_inputs(z1)
    z3 = proj(3)
    emit_keys_values(z2)
    z4 = proj(4)
    z5 = proj(5)
    u_ref[...] = (z3 * z4).astype(bf16)
    bg_ref[...] = z5.astype(bf16)


def _layer_weight(a):
    return pl.BlockSpec((None,) + a.shape[1:], lambda i, layer: (layer[0],) + (0,) * (a.ndim - 1))


def _inproj(layer, x2, gm, w_all, gq, gks, gkw, e, tail, *, seq, tm=512):
    n = x2.shape[0]
    width = HEADS_PER_GROUP * Q_BLOCK
    row = lambda w_: pl.BlockSpec((tm, w_), lambda i, layer: (i, 0))
    full = lambda a: pl.BlockSpec(a.shape, lambda i, layer: (0,) * a.ndim)
    outs = [(2 * LANES, bf16), (2 * LANES, bf16),
            (2 * LANES, bf16), (2 * LANES, bf16), (CONV_WIDTH, bf16), (CONV_WIDTH, bf16)]
    per_block = lambda rows: pl.BlockSpec((tm // Q_BLOCK, N_GROUPS, rows, width), lambda i, layer: (i, 0, 0, 0))
    qt_spec = per_block(LANES)
    qt_shape = jax.ShapeDtypeStruct((n // Q_BLOCK, N_GROUPS, LANES, width), bf16)
    gate_shape = jax.ShapeDtypeStruct((n // Q_BLOCK, N_GROUPS, 8, width), f32)
    chunk_spec = pl.BlockSpec((tm // CMP_STRIDE, CMP_STRIDE * LANES), lambda i, layer: (i, 0))
    chunk_shape = jax.ShapeDtypeStruct((n // CMP_STRIDE, CMP_STRIDE * LANES), bf16)
    return pl.pallas_call(
        functools.partial(_inproj_body, tm=tm, seq=seq),
        grid_spec=pltpu.PrefetchScalarGridSpec(
            num_scalar_prefetch=1, grid=(n // tm,),
            in_specs=[row(D_MODEL), full(gm), _layer_weight(w_all), full(gq), full(gks), full(gkw), full(e),
                      full(tail)],
            out_specs=[qt_spec, chunk_spec, chunk_spec] + [row(w_) for w_, _ in outs] + [per_block(8)],
            scratch_shapes=[pltpu.VMEM((2, tm, LANES), f32)]),
        out_shape=[qt_shape, chunk_shape, chunk_shape]
        + [jax.ShapeDtypeStruct((n, w_), dt) for w_, dt in outs] + [gate_shape],
        compiler_params=pltpu.CompilerParams(dimension_semantics=("arbitrary",),
                                             vmem_limit_bytes=VMEM_LIMIT),
        name="inproj",
    )(layer, x2, gm, w_all, gq, gks, gkw, e, tail)


def _gelu_tanh(x):
    return 0.5 * x * (1.0 + jnp.tanh(0.7978845608028654 * (x + 0.044715 * (x * x * x))))


def _compress_body(layer_ref, zk_ref, zv_ref, w1_ref, pe_ref, b1_ref, w2k_ref, w2v_ref, b2k_ref, b2v_ref,
                   gk_ref, e_ref, kc_ref, vc_ref, *, tc):
    last = lax.broadcasted_iota(i32, (tc, 1), 0) == tc - 1

    def hidden(z_ref, kind):
        z = z_ref[0]
        a = _dot(z, w1_ref[kind, 0])
        b = _dot(z, w1_ref[kind, 1])
        b = jnp.concatenate([b[1:], jnp.zeros((1, b.shape[1]), f32)], axis=0)
        bias = (_dot(pe_ref[kind, 0], w1_ref[kind, 0]) + _dot(pe_ref[kind, 1], w1_ref[kind, 1]))[0:1]
        return _gelu_tanh(a + b + bias + b1_ref[kind]).astype(bf16)

    k = _dot(hidden(zk_ref, 0), w2k_ref[...]) + b2k_ref[...]
    k = _head_norm(k, e_ref[...], gk_ref[...])
    pos = lax.broadcasted_iota(i32, (tc, LANES), 0) * CMP_STRIDE + (CMP_BLOCK - 1)
    k = k + _pos_cols(pos, 2 * LANES)
    kc_ref[0] = jnp.where(last, 0.0, k).astype(bf16)
    v = _dot(hidden(zv_ref, 1), w2v_ref[...]) + b2v_ref[...]
    vc_ref[0] = jnp.where(last, 0.0, v + _ones_col(tc, 2 * LANES)).astype(bf16)


def _compress(layer, zk, zv, w1_all, pe, b1, w2k, w2v, b2k, b2v, gk, e):
    b, tc, _ = zk.shape
    blk = lambda a: pl.BlockSpec((1,) + a.shape[1:], lambda i, layer: (i,) + (0,) * (a.ndim - 1))
    full = lambda a: pl.BlockSpec(a.shape, lambda i, layer: (0,) * a.ndim)
    return pl.pallas_call(
        functools.partial(_compress_body, tc=tc),
        grid_spec=pltpu.PrefetchScalarGridSpec(
            num_scalar_prefetch=1, grid=(b,),
            in_specs=[blk(zk), blk(zv), _layer_weight(w1_all)]
            + [full(a) for a in (pe, b1, w2k, w2v, b2k, b2v, gk, e)],
            out_specs=[pl.BlockSpec((1, tc, 2 * LANES), lambda i, layer: (i, 0, 0))] * 2),
        out_shape=[jax.ShapeDtypeStruct((b, tc, 2 * LANES), bf16)] * 2,
        compiler_params=pltpu.CompilerParams(dimension_semantics=("arbitrary",),
                                             vmem_limit_bytes=VMEM_LIMIT),
        name="compress",
    )(layer, zk, zv, w1_all, pe, b1, w2k, w2v, b2k, b2v, gk, e)


def _query_pos(c):
    lane = lax.broadcasted_iota(i32, (1, HEADS_PER_GROUP * Q_BLOCK), 1)
    return c * Q_BLOCK + (lane & (Q_BLOCK - 1))


def _pick_top(vs, jf, n_pick):
    vs = list(vs)
    for _ in range(n_pick):
        for g, v in enumerate(vs):
            mx = jnp.max(v, axis=0, keepdims=True)
            idx = jnp.min(jnp.where(v == mx, jf, float(jf.shape[0])), axis=0, keepdims=True)
            vs[g] = jnp.where(jf == idx, PICKED, v)
    return tuple(vs)


def _cmp_variant(nchunk, c0, qt_ref, kc_ref, vc_ref, mct_ref, tri_ref, o_ref, madd_ref, lst_ref, cnt_ref,
                 *, nsel):
    units = [(qb, g) for qb in range(CMP_QB) for g in range(N_GROUPS)]
    nrow = nchunk * CMP_CHUNK
    npre = nchunk * SEL_PER_CHUNK
    tqs = [_query_pos(c0 + qb) for qb in range(CMP_QB)]
    sub = lax.broadcasted_iota(i32, (CMP_CHUNK, 1), 0)
    lanes = lambda g: slice(g * LANES, (g + 1) * LANES)
    chunk = lambda a, k: a[k * CMP_CHUNK:(k + 1) * CMP_CHUNK]

    tiles = []
    for qb, g in units:
        s = _dot(kc_ref[0, 0:nrow, lanes(g)], qt_ref[qb, g])
        row = []
        for k in range(nchunk):
            t = chunk(s, k)
            if k >= nchunk - 2:
                seen = (k * CMP_CHUNK + sub) * CMP_STRIDE + (CMP_BLOCK - 1) <= tqs[qb]
                t = jnp.where(seen, t, NEG)
            row.append(t)
        tiles.append(row)

    accs, imps = [], []
    for u, (qb, g) in enumerate(units):
        m = tiles[u][0].max(axis=0, keepdims=True)
        for t in tiles[u][1:]:
            m = jnp.maximum(m, t.max(axis=0, keepdims=True))
        parts = [jnp.exp2(t - m).astype(bf16) for t in tiles[u]]
        accs.append(_dot_tn(vc_ref[0, 0:nrow, lanes(g)], jnp.concatenate(parts, axis=0)))
        rows, carry = [], None
        for part in parts:
            piece = _dot(mct_ref[...], part)
            body = piece[0:SEL_PER_CHUNK]
            if carry is not None:
                body = jnp.concatenate([body[0:8] + carry, body[8:]], axis=0)
            rows.append(body)
            carry = piece[SEL_PER_CHUNK:IMP_ROWS]
        imps.append(rows[0] if nchunk == 1 else jnp.concatenate(rows, axis=0))

    j = lax.broadcasted_iota(i32, (npre, Q_BLOCK), 0)
    jf = j.astype(f32)
    t1s = [tq[:, 0:Q_BLOCK] for tq in tqs]
    valids = [j * SEL_BLOCK <= t1 for t1 in t1s]
    vs = []
    for u, (qb, g) in enumerate(units):
        any_key = tqs[qb] >= CMP_BLOCK - 1
        inv = jnp.where(any_key, 1.0 / jnp.maximum(accs[u][HEAD_DIM:HEAD_DIM + 1], 1e-30), 0.0)
        o_ref[0, g, qb] = accs[u][0:HEAD_DIM] * inv
        imp4 = imps[u] * inv
        imp = imp4[:, 0:Q_BLOCK]
        for r in range(1, HEADS_PER_GROUP):
            imp = imp + imp4[:, r * Q_BLOCK:(r + 1) * Q_BLOCK]
        jt = t1s[qb] >> 6
        v = jnp.where(valids[qb], imp, NEG)
        vs.append(jnp.where(j == 0, PICKED, jnp.where(j == jt, PICKED, jnp.where(j == jt - 1, PICKED, v))))
    vs = _pick_top(vs, jf, SEL_TOPK - 3)

    rr = lax.broadcasted_iota(i32, (nsel, nsel), 0).astype(f32)
    ones8 = jnp.ones((8, Q_BLOCK), bf16)
    jrow = jnp.broadcast_to(lax.broadcasted_iota(i32, (1, nsel), 1).astype(f32), (8, nsel)).astype(bf16)
    for u, (qb, g) in enumerate(units):
        picked = vs[u] == PICKED
        madd = jnp.where(valids[qb], jnp.where(picked, 0.0, NEG), NEG)
        sel = jnp.where(valids[qb], jnp.where(picked, 1.0, 0.0), 0.0).astype(bf16)
        if npre < nsel:
            madd = jnp.concatenate([madd, jnp.full((nsel - npre, Q_BLOCK), NEG, f32)], axis=0)
            sel = jnp.concatenate([sel, jnp.zeros((nsel - npre, Q_BLOCK), bf16)], axis=0)
        madd_ref[0, g, qb] = madd
        flag = jnp.where(_dot_nt(ones8, sel)[0:1] > 0.0, 1.0, 0.0)
        flag8 = jnp.broadcast_to(flag, (8, nsel)).astype(bf16)
        prefix = _dot(flag8, tri_ref[...])[0:1]
        place = jnp.where(prefix == rr, flag, 0.0).astype(bf16)
        lst_ref[0, g, qb] = _dot_nt(jrow, place).astype(i32)
        cnt_ref[0, g, qb] = _dot(flag8, jnp.ones((nsel, LANES), bf16)).astype(i32)


def _cmp_body(qt_ref, kc_ref, vc_ref, mct_ref, tri_ref, o_ref, madd_ref, lst_ref, cnt_ref, *, nsel, nvar):
    c0 = pl.program_id(1) * CMP_QB
    c_last = c0 + CMP_QB - 1
    nch = (c_last * (Q_BLOCK // CMP_STRIDE) + (Q_BLOCK - CMP_BLOCK) // CMP_STRIDE) // CMP_CHUNK + 1
    for n in range(1, nvar + 1):
        pl.when(nch == n)(functools.partial(
            _cmp_variant, n, c0, qt_ref, kc_ref, vc_ref, mct_ref, tri_ref, o_ref, madd_ref, lst_ref, cnt_ref,
            nsel=nsel))


def _cmp_attention(qt, kc, vc, mct, tri, *, batch):
    g = N_GROUPS
    nqb = qt.shape[0] // batch
    tc = kc.shape[1]
    nsel = tc * CMP_STRIDE // SEL_BLOCK
    width = HEADS_PER_GROUP * Q_BLOCK
    nstep = nqb // CMP_QB
    per_q = lambda r_, c_: pl.BlockSpec((1, g, CMP_QB, r_, c_), lambda bi, ci: (bi, 0, ci, 0, 0))
    shape = lambda r_, c_, dt: jax.ShapeDtypeStruct((batch, g, nqb, r_, c_), dt)
    return pl.pallas_call(
        functools.partial(_cmp_body, nsel=nsel, nvar=tc // CMP_CHUNK),
        grid=(batch, nstep),
        in_specs=[pl.BlockSpec((CMP_QB, g, LANES, width), lambda bi, ci: (bi * nstep + ci, 0, 0, 0)),
                  pl.BlockSpec((1, tc, 2 * LANES), lambda bi, ci: (bi, 0, 0)),
                  pl.BlockSpec((1, tc, 2 * LANES), lambda bi, ci: (bi, 0, 0)),
                  pl.BlockSpec(mct.shape, lambda bi, ci: (0, 0)),
                  pl.BlockSpec(tri.shape, lambda bi, ci: (0, 0))],
        out_specs=[per_q(HEAD_DIM, width), per_q(nsel, Q_BLOCK), per_q(8, nsel), per_q(8, LANES)],
        out_shape=[shape(HEAD_DIM, width, f32), shape(nsel, Q_BLOCK, f32),
                   shape(8, nsel, i32), shape(8, LANES, i32)],
        compiler_params=pltpu.CompilerParams(dimension_semantics=("arbitrary",) * 2,
                                             vmem_limit_bytes=VMEM_LIMIT),
        name="cmp_attention",
    )(qt, kc, vc, mct, tri)


def _normalize(o_aug):
    return o_aug[0:HEAD_DIM] * (1.0 / jnp.maximum(o_aug[HEAD_DIM:HEAD_DIM + 1], 1e-30))


def _slc_win_body(lst_ref, cnt_ref, qt_ref, ks_ref, vs_ref, kw_ref, vw_ref,
                  madd_ref, ocmp_ref, gate_ref, wtab_ref, out_ref, s_scr, *, nsel):
    width = HEADS_PER_GROUP * Q_BLOCK
    nwin = (WINDOW + Q_BLOCK) // LANES
    wq = WINDOW // Q_BLOCK
    grp_rows = GROUP_BLOCKS * SEL_BLOCK

    class Block:
        def __init__(self, qb):
            self.qb = qb
            self.c = pl.program_id(2) * QB_PER_STEP + qb
            self.q0 = pl.multiple_of(self.c * Q_BLOCK, Q_BLOCK)
            self.qta = qt_ref[qb, 0]
            self.n_off = cnt_ref[qb, 0, 0] - 2

        def block_ids(self, first, nblk):
            ids = []
            for u in range(nblk):
                i = first + u
                j = lst_ref[self.qb, 0, jnp.minimum(i, nsel - 1)]
                ids.append((i < self.n_off, j, pl.multiple_of(j * SEL_BLOCK, SEL_BLOCK)))
            return ids

        def scores(self, ids):
            kcat = jnp.concatenate([ks_ref[0, pl.ds(r0, SEL_BLOCK), :] for _, _, r0 in ids], axis=0)
            sg = _dot(kcat, self.qta)
            tiles = []
            for u, (live, j, _) in enumerate(ids):
                mrow = jnp.where(live, madd_ref[0, 0, self.qb, pl.ds(j, 1), :], NEG)
                mrow = jnp.concatenate([mrow] * HEADS_PER_GROUP, axis=1)
                tiles.append(sg[u * SEL_BLOCK:(u + 1) * SEL_BLOCK] + mrow)
            return tiles

        def values(self, ids):
            return jnp.concatenate([vs_ref[0, pl.ds(r0, SEL_BLOCK), :] for _, _, r0 in ids], axis=0)

    def front(blk):
        c, q0, qta = blk.c, blk.q0, blk.qta
        ws = pl.multiple_of(jnp.maximum(c - wq, 0) * Q_BLOCK, Q_BLOCK)
        s = _dot(kw_ref[0, pl.ds(ws, WINDOW + Q_BLOCK), :], qta)
        chunks = []
        for k in range(nwin):
            steady = 1 if k == 0 else (2 if k == nwin - 1 else 0)
            tab = jnp.where(c >= wq, steady, jnp.where(k < c, 0, jnp.where(k == c, 2, 3)))
            chunks.append(s[k * LANES:(k + 1) * LANES] + wtab_ref[tab])
        head = blk.block_ids(0, HEAD_BLOCKS)
        tiles = [_dot(ks_ref[0, pl.ds(q0, Q_BLOCK), :], qta) + wtab_ref[2]] + blk.scores(head)
        return ws, chunks, head, tiles

    def softmax_pv(tiles, v):
        m = tiles[0].max(axis=0, keepdims=True)
        for su in tiles[1:]:
            m = jnp.maximum(m, su.max(axis=0, keepdims=True))
        p = jnp.concatenate([jnp.exp2(su - m).astype(bf16) for su in tiles], axis=0)
        return m, _dot_tn(v, p)

    def middle(blk, ws, chunks, head, tiles):
        _, o_win = softmax_pv(chunks, vw_ref[0, pl.ds(ws, WINDOW + Q_BLOCK), :])
        m_run, o_run = softmax_pv(
            tiles, jnp.concatenate([vs_ref[0, pl.ds(blk.q0, Q_BLOCK), :], blk.values(head)], axis=0))
        return _normalize(o_win), m_run, o_run

    def rest(blk, m_run, o_run):
        def segment(si, carry):
            m_run, o_run = carry
            base = HEAD_BLOCKS + si * SUP_BLOCKS
            ngrp = (jnp.minimum(SUP_BLOCKS, blk.n_off - base) + GROUP_BLOCKS - 1) // GROUP_BLOCKS
            rows = lambda gi: pl.ds(pl.multiple_of(gi * grp_rows, grp_rows), grp_rows)

            def score(gi, mx):
                sg = jnp.concatenate(blk.scores(blk.block_ids(base + gi * GROUP_BLOCKS, GROUP_BLOCKS)), axis=0)
                s_scr[rows(gi), :] = sg
                return jnp.maximum(mx, sg.max(axis=0, keepdims=True))

            m_new = lax.fori_loop(0, ngrp, score, m_run)

            def weigh(gi, acc):
                pg = jnp.exp2(s_scr[rows(gi), :] - m_new).astype(bf16)
                return acc + _dot_tn(blk.values(blk.block_ids(base + gi * GROUP_BLOCKS, GROUP_BLOCKS)), pg)

            o_seg = lax.fori_loop(0, ngrp, weigh, jnp.zeros((LANES, width), f32))
            return m_new, jnp.exp2(m_run - m_new) * o_run + o_seg

        nseg = (jnp.maximum(blk.n_off - HEAD_BLOCKS, 0) + SUP_BLOCKS - 1) // SUP_BLOCKS
        return lax.fori_loop(0, nseg, segment, (m_run, o_run))[1]

    blocks = [Block(qb) for qb in range(QB_PER_STEP)]
    fronts = [front(blk) for blk in blocks]
    fronts = [middle(blk, *f) for blk, f in zip(blocks, fronts)]
    tails = [rest(blk, m_run, o_run) for blk, (_, m_run, o_run) in zip(blocks, fronts)]
    for blk, (o_win, _, _), o_run in zip(blocks, fronts, tails):
        qb = blk.qb
        gate = gate_ref[qb, 0]
        mix = gate[0:1] * ocmp_ref[0, 0, qb] + gate[1:2] * _normalize(o_run) + gate[2:3] * o_win
        rows = jnp.concatenate([mix[:, r * Q_BLOCK:(r + 1) * Q_BLOCK] for r in range(HEADS_PER_GROUP)],
                               axis=0)
        out_ref[0, qb * Q_BLOCK:(qb + 1) * Q_BLOCK, :] = rows.T.astype(bf16)


def _slc_win_attention(lst, cnt, qt, ks, vs, kw, vw, madd, ocmp, gate_t, wtab):
    b, t, _ = ks.shape
    g = N_GROUPS
    nqb, nsel = t // Q_BLOCK, t // SEL_BLOCK
    width = HEADS_PER_GROUP * Q_BLOCK
    nstep = nqb // QB_PER_STEP
    flat = lambda bi, gi, ci: (bi * g + gi) * nstep + ci
    smem = lambda w_: pl.BlockSpec((QB_PER_STEP, 1, w_), lambda bi, gi, ci: (flat(bi, gi, ci), 0, 0),
                                   memory_space=pltpu.SMEM)
    per_q = lambda r_, c_: pl.BlockSpec((1, 1, QB_PER_STEP, r_, c_), lambda bi, gi, ci: (bi, gi, ci, 0, 0))
    keys = pl.BlockSpec((1, t, LANES), lambda bi, gi, ci: (bi, 0, gi), pipeline_mode=pl.Buffered(1))
    return pl.pallas_call(
        functools.partial(_slc_win_body, nsel=nsel),
        grid=(b, g, nstep),
        in_specs=[smem(nsel), smem(LANES),
                  pl.BlockSpec((QB_PER_STEP, 1, LANES, width), lambda bi, gi, ci: (bi * nstep + ci, gi, 0, 0)),
                  keys, keys, keys, keys,
                  per_q(nsel, Q_BLOCK), per_q(HEAD_DIM, width),
                  pl.BlockSpec((QB_PER_STEP, 1, 8, width), lambda bi, gi, ci: (bi * nstep + ci, gi, 0, 0)),
                  pl.BlockSpec(wtab.shape, lambda bi, gi, ci: (0, 0, 0))],
        out_specs=pl.BlockSpec((1, QB_PER_STEP * Q_BLOCK, 2 * LANES), lambda bi, gi, ci: (bi, ci, gi)),
        out_shape=jax.ShapeDtypeStruct((b, t, ATTN_WIDTH), bf16),
        scratch_shapes=[pltpu.VMEM((SUP_BLOCKS * SEL_BLOCK, width), f32)],
        compiler_params=pltpu.CompilerParams(dimension_semantics=("arbitrary",) * 3,
                                             vmem_limit_bytes=VMEM_LIMIT),
        name="slc_win_attention",
    )(lst, cnt, qt, ks, vs, kw, vw, madd, ocmp, gate_t, wtab)


PREV_ROWS = 16


def _rms(v, g):
    return (v * lax.rsqrt(jnp.mean(v * v, axis=-1, keepdims=True) + EPS)) * g


def _mix_ffn_body(layer_ref, x_ref, attn_ref, u_ref, uprev_ref, bg_ref, cw_ref, go_ref, wo_ref, gf_ref, wu_ref,
                  wd_ref,
                  out_ref, *, tm, seq, chunk):
    first = (pl.program_id(0) * tm) % seq == 0
    u = u_ref[...].astype(f32)
    prev = jnp.where(first, 0.0, uprev_ref[0].astype(f32))
    ext = jnp.concatenate([prev, u], axis=0)
    cw = cw_ref[...]
    conv = (cw[0:1] * ext[PREV_ROWS - 2:PREV_ROWS - 2 + tm] + cw[1:2] * ext[PREV_ROWS - 1:PREV_ROWS - 1 + tm]
            + cw[2:3] * u)
    conv = bg_ref[...].astype(f32) * conv
    go = go_ref[...]
    mixed = jnp.concatenate([_rms(attn_ref[...].astype(f32), go[:, :ATTN_WIDTH]),
                             _rms(conv, go[:, ATTN_WIDTH:])], axis=1).astype(bf16)
    x = x_ref[...] + _dot(mixed, wo_ref[...])
    h = _rms(x, gf_ref[...]).astype(bf16)
    acc = x
    for c in range(D_FF // chunk):
        a = jnp.maximum(_dot(h, wu_ref[:, c * chunk:(c + 1) * chunk]), 0.0)
        acc = acc + _dot((a * a).astype(bf16), wd_ref[c * chunk:(c + 1) * chunk, :])
    out_ref[...] = acc


def _mix_ffn(layer, x2, attn2, u, bgate, cw, go, wo_all, gf, wu_all, wd_all, *, seq, in_place, tm=512, chunk=1024):
    n = x2.shape[0]
    uprev = u.reshape(n // PREV_ROWS, PREV_ROWS, CONV_WIDTH)
    row = lambda w_: pl.BlockSpec((tm, w_), lambda i, layer: (i, 0))
    full = lambda a: pl.BlockSpec(a.shape, lambda i, layer: (0,) * a.ndim)
    once = lambda a: pl.BlockSpec((None,) + a.shape[1:], lambda i, layer: (layer[0],) + (0,) * (a.ndim - 1),
                                  pipeline_mode=pl.Buffered(1))
    return pl.pallas_call(
        functools.partial(_mix_ffn_body, tm=tm, seq=seq, chunk=chunk),
        grid_spec=pltpu.PrefetchScalarGridSpec(
            num_scalar_prefetch=1, grid=(n // tm,),
            in_specs=[row(D_MODEL), row(ATTN_WIDTH), row(CONV_WIDTH),
                      pl.BlockSpec((1, PREV_ROWS, CONV_WIDTH),
                                   lambda i, layer: (jnp.maximum(i * (tm // PREV_ROWS) - 1, 0), 0, 0)),
                      row(CONV_WIDTH), full(cw), full(go), once(wo_all), full(gf), once(wu_all), once(wd_all)],
            out_specs=row(D_MODEL)),
        out_shape=jax.ShapeDtypeStruct((n, D_MODEL), f32),
        input_output_aliases={1: 0} if in_place else {},
        compiler_params=pltpu.CompilerParams(dimension_semantics=("arbitrary",),
                                             vmem_limit_bytes=VMEM_LIMIT),
        name="mix_ffn",
    )(layer, x2, attn2, u, uprev, bgate, cw, go, wo_all, gf, wu_all, wd_all)


def _interleave_zero(w):
    z = jnp.zeros(w.shape[:-1] + (HEAD_DIM,), w.dtype)
    return jnp.concatenate([w[..., :HEAD_DIM], z, w[..., HEAD_DIM:], z], axis=-1)


def _prep_params(g_mix_norm, w_in, g_q, g_k, pe_cmp, w_cmp1, b_cmp1, w_cmp2, b_cmp2,
                 conv_w, g_out, w_o, g_ffn_norm, w_up, w_down):
    depth = w_in.shape[0]
    o = np.cumsum([0, ATTN_WIDTH] + [LANES] * 6 + [N_BRANCH * N_HEADS] + [CONV_WIDTH] * 3)
    part = lambda i: w_in[..., int(o[i]):int(o[i + 1])]
    q, kc, vc, ks, vs, kw, vw, gl, hc, cg, bg = [part(i) for i in range(11)]
    gl = jnp.pad(gl, ((0, 0), (0, 0), (0, LANES - gl.shape[-1])))
    w = jnp.concatenate([q, kc, vc, gl, jnp.zeros_like(gl), ks, kw, vs, vw, hc, cg, bg], axis=-1).astype(bf16)
    pair = lambda gk: jnp.concatenate([gk, gk], axis=-1)[:, None, :]
    tile2 = lambda gk: _interleave_zero(jnp.concatenate([gk, gk], axis=-1))[:, None, :]
    w1 = w_cmp1.astype(bf16).reshape(depth, 2, 2, CMP_STRIDE, HEAD_DIM, CMP_HIDDEN)
    z1 = jnp.zeros_like(w1)
    w1g = jnp.stack([jnp.concatenate([w1, z1], axis=-1), jnp.concatenate([z1, w1], axis=-1)], axis=4)
    w1g = w1g.reshape(depth, 2, 2, CMP_STRIDE * LANES, N_GROUPS * CMP_HIDDEN)
    pe = pe_cmp.reshape(depth, 2, 2, CMP_STRIDE, 1, HEAD_DIM)
    pe = jnp.broadcast_to(pe, (depth, 2, 2, CMP_STRIDE, N_GROUPS, HEAD_DIM)).reshape(depth, 2, 2, 1, -1)
    pe = jnp.pad(pe, ((0, 0), (0, 0), (0, 0), (0, 7), (0, 0))).astype(bf16)
    b1 = jnp.concatenate([b_cmp1, b_cmp1], axis=-1)[:, :, None, :]
    z2 = jnp.zeros_like(w_cmp2)
    w2 = jnp.concatenate([jnp.concatenate([w_cmp2, z2], axis=-1), jnp.concatenate([z2, w_cmp2], axis=-1)],
                         axis=2)
    w2 = _interleave_zero(w2).astype(bf16)
    b2 = _interleave_zero(jnp.concatenate([b_cmp2, b_cmp2], axis=-1))[:, :, None, :]
    return dict(
        gm=g_mix_norm[:, None, :], w=w,
        gq=jnp.tile(g_q, (1, N_HEADS))[:, None, :],
        gks=pair(g_k[:, 1]), gkw=pair(g_k[:, 2]), gkc=tile2(g_k[:, 0]),
        w1=w1g, pe=pe, b1=b1,
        w2k=w2[:, 0], w2v=w2[:, 1], b2k=b2[:, 0], b2v=b2[:, 1],
        cw=jnp.pad(conv_w, ((0, 0), (0, 8 - conv_w.shape[1]), (0, 0))),
        go=g_out[:, None, :], wo=w_o.astype(bf16),
        gf=g_ffn_norm[:, None, :], wu=w_up.astype(bf16), wd=w_down.astype(bf16),
    )


def _constants(nsel):
    lane = np.arange(LANES)
    e = (lane[:, None] // HEAD_DIM == lane[None, :] // HEAD_DIM).astype(np.float32)
    tail = np.zeros((N_GROUPS, HEAD_DIM, HEADS_PER_GROUP * Q_BLOCK), np.float32)
    for gi in range(N_GROUPS):
        for r in range(HEADS_PER_GROUP):
            rest = np.float64(2.0 ** -(gi * HEADS_PER_GROUP + r + 1)) * LOG2E
            for k in range(SLOPE_TERMS):
                term = np.float64(np.asarray(rest, np.float32).astype(jnp.bfloat16).astype(np.float32))
                tail[gi, k, r * Q_BLOCK:(r + 1) * Q_BLOCK] = term * LANES
                tail[gi, SLOPE_TERMS + k, r * Q_BLOCK:(r + 1) * Q_BLOCK] = term
                rest = rest - term
    mct = np.zeros((IMP_ROWS, CMP_CHUNK), np.float32)
    for i in range(CMP_CHUNK):
        lo, hi = i * CMP_STRIDE, i * CMP_STRIDE + CMP_BLOCK
        for jj in range(SEL_PER_CHUNK + 1):
            ov = min(hi, (jj + 1) * SEL_BLOCK) - max(lo, jj * SEL_BLOCK)
            if ov > 0:
                mct[jj, i] = ov / CMP_BLOCK
    kk = np.arange(LANES)[:, None]
    ql = np.tile(np.arange(Q_BLOCK), HEADS_PER_GROUP)[None, :]
    wtab = np.zeros((4, LANES, HEADS_PER_GROUP * Q_BLOCK), np.float32)
    wtab[1] = np.where(kk > ql, 0.0, NEG)
    wtab[2] = np.where(kk <= ql, 0.0, NEG)
    wtab[3] = NEG
    tri = np.arange(nsel)[:, None] < np.arange(nsel)[None, :]
    return (jnp.asarray(e, bf16), jnp.asarray(tail, bf16), jnp.asarray(mct, bf16), jnp.asarray(wtab),
            jnp.asarray(tri, bf16))


def _layer(x2, layer, p, big, consts, *, batch, seq, in_place):
    e, tail, mct, wtab, tri = consts
    g = N_GROUPS
    nqb, nsel, tc = seq // Q_BLOCK, seq // SEL_BLOCK, seq // CMP_STRIDE
    qt, kc, vc, ks, kw, vs, vw, u, bgate, gate_t = _inproj(
        layer, x2, p["gm"], big["w"], p["gq"], p["gks"], p["gkw"], e, tail, seq=seq)
    kcmp, vcmp = _compress(layer, kc.reshape(batch, tc, CMP_STRIDE * LANES),
                           vc.reshape(batch, tc, CMP_STRIDE * LANES),
                           big["w1"], p["pe"], p["b1"], p["w2k"], p["w2v"], p["b2k"], p["b2v"], p["gkc"], e)
    ocmp, madd, lst, cnt = _cmp_attention(qt, kcmp, vcmp, mct, tri, batch=batch)
    rows3 = lambda a: a.reshape(batch, seq, 2 * LANES)
    smem = lambda a: a[:, :, :, 0, :].reshape(batch * g * nqb, 1, a.shape[-1])
    attn = _slc_win_attention(smem(lst), smem(cnt), qt, rows3(ks), rows3(vs),
                              rows3(kw), rows3(vw), madd, ocmp, gate_t, wtab)
    return _mix_ffn(layer, x2, attn.reshape(batch * seq, ATTN_WIDTH), u, bgate, p["cw"], p["go"], big["wo"],
                    p["gf"], big["wu"], big["wd"], seq=seq, in_place=in_place)


def kernel(x, g_mix_norm, w_in, g_q, g_k, pe_cmp, w_cmp1, b_cmp1, w_cmp2, b_cmp2, conv_w, g_out, w_o,
           g_ffn_norm, w_up, w_down):
    batch, seq, d = x.shape
    assert d == D_MODEL and seq % (CMP_CHUNK * CMP_STRIDE) == 0 and seq >= WINDOW + Q_BLOCK
    params = _prep_params(g_mix_norm, w_in, g_q, g_k, pe_cmp, w_cmp1, b_cmp1, w_cmp2, b_cmp2,
                          conv_w, g_out, w_o, g_ffn_norm, w_up, w_down)
    consts = _constants(seq // SEL_BLOCK)

    big = {k: params.pop(k) for k in ("w", "w1", "wo", "wu", "wd")}
    layers = jnp.arange(w_in.shape[0], dtype=i32)[:, None]

    def step(x2, xs):
        layer, p = xs
        return _layer(x2, layer, p, big, consts, batch=batch, seq=seq, in_place=True), None

    first = jax.tree.map(lambda a: a[0], (layers, params))
    rest = jax.tree.map(lambda a: a[1:], (layers, params))
    x2 = _layer(x.reshape(batch * seq, d), *first, big, consts, batch=batch, seq=seq, in_place=False)
    x2, _ = lax.scan(step, x2, rest)
    return x2.reshape(batch, seq, d)
```

```python
import functools

import numpy as np
import jax
import jax.numpy as jnp
from jax import lax
from jax.experimental import pallas as pl
from jax.experimental.pallas import tpu as pltpu

f32 = jnp.float32
bf16 = jnp.bfloat16
i32 = jnp.int32

D_MODEL = 1024
HEAD_DIM = 64
N_HEADS = 8
N_GROUPS = 2
HEADS_PER_GROUP = 4
ATTN_WIDTH = 512
CONV_WIDTH = 512
N_BRANCH = 3
CMP_BLOCK = 32
CMP_STRIDE = 16
CMP_HIDDEN = 256
SEL_BLOCK = 64
SEL_TOPK = 16
WINDOW = 512
Q_BLOCK = 128
D_FF = 4096
EPS = 1e-6
NEG = -1e30
LOG2E = 1.4426950408889634
SLOPE_TERMS = 3
PICKED = -3e38
LANES = 128
CMP_CHUNK = 256
SEL_PER_CHUNK = CMP_CHUNK * CMP_STRIDE // SEL_BLOCK
IMP_ROWS = SEL_PER_CHUNK + 8
VMEM_LIMIT = 56 * 1024 * 1024

PROJ_CHUNK = 512
SUP_BLOCKS = 32
HEAD_BLOCKS = 20
GROUP_BLOCKS = 4
QB_PER_STEP = 4
CMP_QB = 4


def _dot(a, b):
    return jnp.dot(a, b, preferred_element_type=f32)


def _dot_nt(a, b):
    return lax.dot_general(a, b, (((1,), (1,)), ((), ())), preferred_element_type=f32)


def _dot_tn(a, b):
    return lax.dot_general(a, b, (((0,), (0,)), ((), ())), preferred_element_type=f32)


def _head_norm(z, e, g):
    sq = (z * z).astype(bf16)
    outs = []
    step = e.shape[0]
    for c in range(z.shape[1] // step):
        sl = slice(c * step, (c + 1) * step)
        outs.append(z[:, sl] * lax.rsqrt(_dot(sq[:, sl], e) * (1.0 / HEAD_DIM) + EPS))
    y = outs[0] if len(outs) == 1 else jnp.concatenate(outs, axis=1)
    return y * g


def _pos_cols(pos, width):
    rows = pos.shape[0]
    lane = lax.broadcasted_iota(i32, (rows, LANES), 1) - HEAD_DIM
    tile = jnp.where(lane < 0, 0.0, jnp.where(lane < SLOPE_TERMS, (pos >> 7).astype(f32),
                                              jnp.where(lane < 2 * SLOPE_TERMS, (pos & 127).astype(f32), 0.0)))
    return tile if width == LANES else jnp.concatenate([tile] * (width // LANES), axis=1)


def _ones_col(rows, width):
    lane = lax.broadcasted_iota(i32, (rows, LANES), 1)
    tile = jnp.where(lane == HEAD_DIM, 1.0, 0.0)
    return tile if width == LANES else jnp.concatenate([tile] * (width // LANES), axis=1)


def _inproj_body(layer_ref, x_ref, gm_ref, w_ref, gq_ref, gks_ref, gkw_ref, e_ref, tail_ref,
                 qt_ref, kc_ref, vc_ref, ks_ref, kw_ref, vs_ref, vw_ref, u_ref, bg_ref, gate_ref, kv_scr,
                 *, tm, seq):
    x = x_ref[...]
    ms = jnp.mean(x * x, axis=-1, keepdims=True)
    h = ((x * lax.rsqrt(ms + EPS)) * gm_ref[...]).astype(bf16)
    e = e_ref[...]
    two = 2 * LANES

    def proj(k):
        return _dot(h, w_ref[:, k * PROJ_CHUNK:(k + 1) * PROJ_CHUNK])

    def emit_queries(z):
        qn = _head_norm(z, e, gq_ref[...]) * (HEAD_DIM ** -0.5 * LOG2E)
        gw = HEADS_PER_GROUP * HEAD_DIM
        for cb in range(tm // Q_BLOCK):
            for g in range(N_GROUPS):
                blk = qn[cb * Q_BLOCK:(cb + 1) * Q_BLOCK, g * gw:(g + 1) * gw].T
                top = jnp.concatenate([blk[r * HEAD_DIM:(r + 1) * HEAD_DIM] for r in range(HEADS_PER_GROUP)],
                                      axis=1)
                qt_ref[cb, g] = jnp.concatenate([top.astype(bf16), tail_ref[g]], axis=0)

    def emit_compress_inputs(z):
        kv_scr[0] = z[:, 0:LANES]
        kv_scr[1] = z[:, LANES:two]
        for p in range(CMP_STRIDE):
            rows = pl.ds(p, tm // CMP_STRIDE, stride=CMP_STRIDE)
            kc_ref[:, p * LANES:(p + 1) * LANES] = kv_scr[0, rows, :].astype(bf16)
            vc_ref[:, p * LANES:(p + 1) * LANES] = kv_scr[1, rows, :].astype(bf16)
        gates = 1.0 / (1.0 + jnp.exp(-z[:, two:two + LANES]))
        pad = jnp.zeros((8 - N_BRANCH, HEADS_PER_GROUP * Q_BLOCK), f32)
        for cb in range(tm // Q_BLOCK):
            gt = gates[cb * Q_BLOCK:(cb + 1) * Q_BLOCK].T
            for g in range(N_GROUPS):
                col = lambda r, br: (g * HEADS_PER_GROUP + r) * N_BRANCH + br
                rows = [jnp.concatenate([gt[col(r, br):col(r, br) + 1] for r in range(HEADS_PER_GROUP)], axis=1)
                        for br in range(N_BRANCH)]
                gate_ref[cb, g] = jnp.concatenate(rows + [pad], axis=0)

    def emit_keys_values(z):
        t0 = (pl.program_id(0) * tm) % seq
        pc = _pos_cols(lax.broadcasted_iota(i32, (tm, LANES), 0) + t0, LANES)
        ones = _ones_col(tm, LANES)
        low = lax.broadcasted_iota(i32, (tm, LANES), 1) < HEAD_DIM

        def spread(t, extra):
            other = pltpu.roll(t, HEAD_DIM, axis=1)
            return jnp.concatenate([jnp.where(low, t, extra), jnp.where(low, other, extra)], axis=1).astype(bf16)

        ks_ref[...] = spread(_head_norm(z[:, 0:LANES], e, gks_ref[...]), pc)
        kw_ref[...] = spread(_head_norm(z[:, LANES:two], e, gkw_ref[...]), pc)
        vs_ref[...] = spread(z[:, two:two + LANES], ones)
        vw_ref[...] = spread(z[:, two + LANES:], ones)

    z0 = proj(0)
    z1 = proj(1)
    emit_queries(z0)
    z2 = proj(2)
    emit_compress_inputs(z1)
    z3 = proj(3)
    emit_keys_values(z2)
    z4 = proj(4)
    z5 = proj(5)
    u_ref[...] = (z3 * z4).astype(bf16)
    bg_ref[...] = z5.astype(bf16)


def _layer_weight(a):
    return pl.BlockSpec((None,) + a.shape[1:], lambda i, layer: (layer[0],) + (0,) * (a.ndim - 1))


def _inproj(layer, x2, gm, w_all, gq, gks, gkw, e, tail, *, seq, tm=512):
    n = x2.shape[0]
    width = HEADS_PER_GROUP * Q_BLOCK
    row = lambda w_: pl.BlockSpec((tm, w_), lambda i, layer: (i, 0))
    full = lambda a: pl.BlockSpec(a.shape, lambda i, layer: (0,) * a.ndim)
    outs = [(2 * LANES, bf16), (2 * LANES, bf16),
            (2 * LANES, bf16), (2 * LANES, bf16), (CONV_WIDTH, bf16), (CONV_WIDTH, bf16)]
    per_block = lambda rows: pl.BlockSpec((tm // Q_BLOCK, N_GROUPS, rows, width), lambda i, layer: (i, 0, 0, 0))
    qt_spec = per_block(LANES)
    qt_shape = jax.ShapeDtypeStruct((n // Q_BLOCK, N_GROUPS, LANES, width), bf16)
    gate_shape = jax.ShapeDtypeStruct((n // Q_BLOCK, N_GROUPS, 8, width), f32)
    chunk_spec = pl.BlockSpec((tm // CMP_STRIDE, CMP_STRIDE * LANES), lambda i, layer: (i, 0))
    chunk_shape = jax.ShapeDtypeStruct((n // CMP_STRIDE, CMP_STRIDE * LANES), bf16)
    return pl.pallas_call(
        functools.partial(_inproj_body, tm=tm, seq=seq),
        grid_spec=pltpu.PrefetchScalarGridSpec(
            num_scalar_prefetch=1, grid=(n // tm,),
            in_specs=[row(D_MODEL), full(gm), _layer_weight(w_all), full(gq), full(gks), full(gkw), full(e),
                      full(tail)],
            out_specs=[qt_spec, chunk_spec, chunk_spec] + [row(w_) for w_, _ in outs] + [per_block(8)],
            scratch_shapes=[pltpu.VMEM((2, tm, LANES), f32)]),
        out_shape=[qt_shape, chunk_shape, chunk_shape]
        + [jax.ShapeDtypeStruct((n, w_), dt) for w_, dt in outs] + [gate_shape],
        compiler_params=pltpu.CompilerParams(dimension_semantics=("arbitrary",),
                                             vmem_limit_bytes=VMEM_LIMIT),
        name="inproj",
    )(layer, x2, gm, w_all, gq, gks, gkw, e, tail)


def _gelu_tanh(x):
    return 0.5 * x * (1.0 + jnp.tanh(0.7978845608028654 * (x + 0.044715 * (x * x * x))))


def _compress_body(layer_ref, zk_ref, zv_ref, w1_ref, pe_ref, b1_ref, w2k_ref, w2v_ref, b2k_ref, b2v_ref,
                   gk_ref, e_ref, kc_ref, vc_ref, *, tc):
    last = lax.broadcasted_iota(i32, (tc, 1), 0) == tc - 1

    def hidden(z_ref, kind):
        z = z_ref[0]
        a = _dot(z, w1_ref[kind, 0])
        b = _dot(z, w1_ref[kind, 1])
        b = jnp.concatenate([b[1:], jnp.zeros((1, b.shape[1]), f32)], axis=0)
        bias = (_dot(pe_ref[kind, 0], w1_ref[kind, 0]) + _dot(pe_ref[kind, 1], w1_ref[kind, 1]))[0:1]
        return _gelu_tanh(a + b + bias + b1_ref[kind]).astype(bf16)

    k = _dot(hidden(zk_ref, 0), w2k_ref[...]) + b2k_ref[...]
    k = _head_norm(k, e_ref[...], gk_ref[...])
    pos = lax.broadcasted_iota(i32, (tc, LANES), 0) * CMP_STRIDE + (CMP_BLOCK - 1)
    k = k + _pos_cols(pos, 2 * LANES)
    kc_ref[0] = jnp.where(last, 0.0, k).astype(bf16)
    v = _dot(hidden(zv_ref, 1), w2v_ref[...]) + b2v_ref[...]
    vc_ref[0] = jnp.where(last, 0.0, v + _ones_col(tc, 2 * LANES)).astype(bf16)


def _compress(layer, zk, zv, w1_all, pe, b1, w2k, w2v, b2k, b2v, gk, e):
    b, tc, _ = zk.shape
    blk = lambda a: pl.BlockSpec((1,) + a.shape[1:], lambda i, layer: (i,) + (0,) * (a.ndim - 1))
    full = lambda a: pl.BlockSpec(a.shape, lambda i, layer: (0,) * a.ndim)
    return pl.pallas_call(
        functools.partial(_compress_body, tc=tc),
        grid_spec=pltpu.PrefetchScalarGridSpec(
            num_scalar_prefetch=1, grid=(b,),
            in_specs=[blk(zk), blk(zv), _layer_weight(w1_all)]
            + [full(a) for a in (pe, b1, w2k, w2v, b2k, b2v, gk, e)],
            out_specs=[pl.BlockSpec((1, tc, 2 * LANES), lambda i, layer: (i, 0, 0))] * 2),
        out_shape=[jax.ShapeDtypeStruct((b, tc, 2 * LANES), bf16)] * 2,
        compiler_params=pltpu.CompilerParams(dimension_semantics=("arbitrary",),
                                             vmem_limit_bytes=VMEM_LIMIT),
        name="compress",
    )(layer, zk, zv, w1_all, pe, b1, w2k, w2v, b2k, b2v, gk, e)


def _query_pos(c):
    lane = lax.broadcasted_iota(i32, (1, HEADS_PER_GROUP * Q_BLOCK), 1)
    return c * Q_BLOCK + (lane & (Q_BLOCK - 1))


def _pick_top(vs, jf, n_pick):
    vs = list(vs)
    for _ in range(n_pick):
        for g, v in enumerate(vs):
            mx = jnp.max(v, axis=0, keepdims=True)
            idx = jnp.min(jnp.where(v == mx, jf, float(jf.shape[0])), axis=0, keepdims=True)
            vs[g] = jnp.where(jf == idx, PICKED, v)
    return tuple(vs)


def _cmp_variant(nchunk, c0, qt_ref, kc_ref, vc_ref, mct_ref, tri_ref, o_ref, madd_ref, lst_ref, cnt_ref,
                 *, nsel):
    units = [(qb, g) for qb in range(CMP_QB) for g in range(N_GROUPS)]
    nrow = nchunk * CMP_CHUNK
    npre = nchunk * SEL_PER_CHUNK
    tqs = [_query_pos(c0 + qb) for qb in range(CMP_QB)]
    sub = lax.broadcasted_iota(i32, (CMP_CHUNK, 1), 0)
    lanes = lambda g: slice(g * LANES, (g + 1) * LANES)
    chunk = lambda a, k: a[k * CMP_CHUNK:(k + 1) * CMP_CHUNK]

    tiles = []
    for qb, g in units:
        s = _dot(kc_ref[0, 0:nrow, lanes(g)], qt_ref[qb, g])
        row = []
        for k in range(nchunk):
            t = chunk(s, k)
            if k >= nchunk - 2:
                seen = (k * CMP_CHUNK + sub) * CMP_STRIDE + (CMP_BLOCK - 1) <= tqs[qb]
                t = jnp.where(seen, t, NEG)
            row.append(t)
        tiles.append(row)

    accs, imps = [], []
    for u, (qb, g) in enumerate(units):
        m = tiles[u][0].max(axis=0, keepdims=True)
        for t in tiles[u][1:]:
            m = jnp.maximum(m, t.max(axis=0, keepdims=True))
        parts = [jnp.exp2(t - m).astype(bf16) for t in tiles[u]]
        accs.append(_dot_tn(vc_ref[0, 0:nrow, lanes(g)], jnp.concatenate(parts, axis=0)))
        rows, carry = [], None
        for part in parts:
            piece = _dot(mct_ref[...], part)
            body = piece[0:SEL_PER_CHUNK]
            if carry is not None:
                body = jnp.concatenate([body[0:8] + carry, body[8:]], axis=0)
            rows.append(body)
            carry = piece[SEL_PER_CHUNK:IMP_ROWS]
        imps.append(rows[0] if nchunk == 1 else jnp.concatenate(rows, axis=0))

    j = lax.broadcasted_iota(i32, (npre, Q_BLOCK), 0)
    jf = j.astype(f32)
    t1s = [tq[:, 0:Q_BLOCK] for tq in tqs]
    valids = [j * SEL_BLOCK <= t1 for t1 in t1s]
    vs = []
    for u, (qb, g) in enumerate(units):
        any_key = tqs[qb] >= CMP_BLOCK - 1
        inv = jnp.where(any_key, 1.0 / jnp.maximum(accs[u][HEAD_DIM:HEAD_DIM + 1], 1e-30), 0.0)
        o_ref[0, g, qb] = accs[u][0:HEAD_DIM] * inv
        imp4 = imps[u] * inv
        imp = imp4[:, 0:Q_BLOCK]
        for r in range(1, HEADS_PER_GROUP):
            imp = imp + imp4[:, r * Q_BLOCK:(r + 1) * Q_BLOCK]
        jt = t1s[qb] >> 6
        v = jnp.where(valids[qb], imp, NEG)
        vs.append(jnp.where(j == 0, PICKED, jnp.where(j == jt, PICKED, jnp.where(j == jt - 1, PICKED, v))))
    vs = _pick_top(vs, jf, SEL_TOPK - 3)

    rr = lax.broadcasted_iota(i32, (nsel, nsel), 0).astype(f32)
    ones8 = jnp.ones((8, Q_BLOCK), bf16)
    jrow = jnp.broadcast_to(lax.broadcasted_iota(i32, (1, nsel), 1).astype(f32), (8, nsel)).astype(bf16)
    for u, (qb, g) in enumerate(units):
        picked = vs[u] == PICKED
        madd = jnp.where(valids[qb], jnp.where(picked, 0.0, NEG), NEG)
        sel = jnp.where(valids[qb], jnp.where(picked, 1.0, 0.0), 0.0).astype(bf16)
        if npre < nsel:
            madd = jnp.concatenate([madd, jnp.full((nsel - npre, Q_BLOCK), NEG, f32)], axis=0)
            sel = jnp.concatenate([sel, jnp.zeros((nsel - npre, Q_BLOCK), bf16)], axis=0)
        madd_ref[0, g, qb] = madd
        flag = jnp.where(_dot_nt(ones8, sel)[0:1] > 0.0, 1.0, 0.0)
        flag8 = jnp.broadcast_to(flag, (8, nsel)).astype(bf16)
        prefix = _dot(flag8, tri_ref[...])[0:1]
        place = jnp.where(prefix == rr, flag, 0.0).astype(bf16)
        lst_ref[0, g, qb] = _dot_nt(jrow, place).astype(i32)
        cnt_ref[0, g, qb] = _dot(flag8, jnp.ones((nsel, LANES), bf16)).astype(i32)


def _cmp_body(qt_ref, kc_ref, vc_ref, mct_ref, tri_ref, o_ref, madd_ref, lst_ref, cnt_ref, *, nsel, nvar):
    c0 = pl.program_id(1) * CMP_QB
    c_last = c0 + CMP_QB - 1
    nch = (c_last * (Q_BLOCK // CMP_STRIDE) + (Q_BLOCK - CMP_BLOCK) // CMP_STRIDE) // CMP_CHUNK + 1
    for n in range(1, nvar + 1):
        pl.when(nch == n)(functools.partial(
            _cmp_variant, n, c0, qt_ref, kc_ref, vc_ref, mct_ref, tri_ref, o_ref, madd_ref, lst_ref, cnt_ref,
            nsel=nsel))


def _cmp_attention(qt, kc, vc, mct, tri, *, batch):
    g = N_GROUPS
    nqb = qt.shape[0] // batch
    tc = kc.shape[1]
    nsel = tc * CMP_STRIDE // SEL_BLOCK
    width = HEADS_PER_GROUP * Q_BLOCK
    nstep = nqb // CMP_QB
    per_q = lambda r_, c_: pl.BlockSpec((1, g, CMP_QB, r_, c_), lambda bi, ci: (bi, 0, ci, 0, 0))
    shape = lambda r_, c_, dt: jax.ShapeDtypeStruct((batch, g, nqb, r_, c_), dt)
    return pl.pallas_call(
        functools.partial(_cmp_body, nsel=nsel, nvar=tc // CMP_CHUNK),
        grid=(batch, nstep),
        in_specs=[pl.BlockSpec((CMP_QB, g, LANES, width), lambda bi, ci: (bi * nstep + ci, 0, 0, 0)),
                  pl.BlockSpec((1, tc, 2 * LANES), lambda bi, ci: (bi, 0, 0)),
                  pl.BlockSpec((1, tc, 2 * LANES), lambda bi, ci: (bi, 0, 0)),
                  pl.BlockSpec(mct.shape, lambda bi, ci: (0, 0)),
                  pl.BlockSpec(tri.shape, lambda bi, ci: (0, 0))],
        out_specs=[per_q(HEAD_DIM, width), per_q(nsel, Q_BLOCK), per_q(8, nsel), per_q(8, LANES)],
        out_shape=[shape(HEAD_DIM, width, f32), shape(nsel, Q_BLOCK, f32),
                   shape(8, nsel, i32), shape(8, LANES, i32)],
        compiler_params=pltpu.CompilerParams(dimension_semantics=("arbitrary",) * 2,
                                             vmem_limit_bytes=VMEM_LIMIT),
        name="cmp_attention",
    )(qt, kc, vc, mct, tri)


def _normalize(o_aug):
    return o_aug[0:HEAD_DIM] * (1.0 / jnp.maximum(o_aug[HEAD_DIM:HEAD_DIM + 1], 1e-30))


def _slc_win_body(lst_ref, cnt_ref, qt_ref, ks_ref, vs_ref, kw_ref, vw_ref,
                  madd_ref, ocmp_ref, gate_ref, wtab_ref, out_ref, s_scr, *, nsel):
    width = HEADS_PER_GROUP * Q_BLOCK
    nwin = (WINDOW + Q_BLOCK) // LANES
    wq = WINDOW // Q_BLOCK
    grp_rows = GROUP_BLOCKS * SEL_BLOCK

    class Block:
        def __init__(self, qb):
            self.qb = qb
            self.c = pl.program_id(2) * QB_PER_STEP + qb
            self.q0 = pl.multiple_of(self.c * Q_BLOCK, Q_BLOCK)
            self.qta = qt_ref[qb, 0]
            self.n_off = cnt_ref[qb, 0, 0] - 2

        def block_ids(self, first, nblk):
            ids = []
            for u in range(nblk):
                i = first + u
                j = lst_ref[self.qb, 0, jnp.minimum(i, nsel - 1)]
                ids.append((i < self.n_off, j, pl.multiple_of(j * SEL_BLOCK, SEL_BLOCK)))
            return ids

        def scores(self, ids):
            kcat = jnp.concatenate([ks_ref[0, pl.ds(r0, SEL_BLOCK), :] for _, _, r0 in ids], axis=0)
            sg = _dot(kcat, self.qta)
            tiles = []
            for u, (live, j, _) in enumerate(ids):
                mrow = jnp.where(live, madd_ref[0, 0, self.qb, pl.ds(j, 1), :], NEG)
                mrow = jnp.concatenate([mrow] * HEADS_PER_GROUP, axis=1)
                tiles.append(sg[u * SEL_BLOCK:(u + 1) * SEL_BLOCK] + mrow)
            return tiles

        def values(self, ids):
            return jnp.concatenate([vs_ref[0, pl.ds(r0, SEL_BLOCK), :] for _, _, r0 in ids], axis=0)

    def front(blk):
        c, q0, qta = blk.c, blk.q0, blk.qta
        ws = pl.multiple_of(jnp.maximum(c - wq, 0) * Q_BLOCK, Q_BLOCK)
        s = _dot(kw_ref[0, pl.ds(ws, WINDOW + Q_BLOCK), :], qta)
        chunks = []
        for k in range(nwin):
            steady = 1 if k == 0 else (2 if k == nwin - 1 else 0)
            tab = jnp.where(c >= wq, steady, jnp.where(k < c, 0, jnp.where(k == c, 2, 3)))
            chunks.append(s[k * LANES:(k + 1) * LANES] + wtab_ref[tab])
        head = blk.block_ids(0, HEAD_BLOCKS)
        tiles = [_dot(ks_ref[0, pl.ds(q0, Q_BLOCK), :], qta) + wtab_ref[2]] + blk.scores(head)
        return ws, chunks, head, tiles

    def softmax_pv(tiles, v):
        m = tiles[0].max(axis=0, keepdims=True)
        for su in tiles[1:]:
            m = jnp.maximum(m, su.max(axis=0, keepdims=True))
        p = jnp.concatenate([jnp.exp2(su - m).astype(bf16) for su in tiles], axis=0)
        return m, _dot_tn(v, p)

    def middle(blk, ws, chunks, head, tiles):
        _, o_win = softmax_pv(chunks, vw_ref[0, pl.ds(ws, WINDOW + Q_BLOCK), :])
        m_run, o_run = softmax_pv(
            tiles, jnp.concatenate([vs_ref[0, pl.ds(blk.q0, Q_BLOCK), :], blk.values(head)], axis=0))
        return _normalize(o_win), m_run, o_run

    def rest(blk, m_run, o_run):
        def segment(si, carry):
            m_run, o_run = carry
            base = HEAD_BLOCKS + si * SUP_BLOCKS
            ngrp = (jnp.minimum(SUP_BLOCKS, blk.n_off - base) + GROUP_BLOCKS - 1) // GROUP_BLOCKS
            rows = lambda gi: pl.ds(pl.multiple_of(gi * grp_rows, grp_rows), grp_rows)

            def score(gi, mx):
                sg = jnp.concatenate(blk.scores(blk.block_ids(base + gi * GROUP_BLOCKS, GROUP_BLOCKS)), axis=0)
                s_scr[rows(gi), :] = sg
                return jnp.maximum(mx, sg.max(axis=0, keepdims=True))

            m_new = lax.fori_loop(0, ngrp, score, m_run)

            def weigh(gi, acc):
                pg = jnp.exp2(s_scr[rows(gi), :] - m_new).astype(bf16)
                return acc + _dot_tn(blk.values(blk.block_ids(base + gi * GROUP_BLOCKS, GROUP_BLOCKS)), pg)

            o_seg = lax.fori_loop(0, ngrp, weigh, jnp.zeros((LANES, width), f32))
            return m_new, jnp.exp2(m_run - m_new) * o_run + o_seg

        nseg = (jnp.maximum(blk.n_off - HEAD_BLOCKS, 0) + SUP_BLOCKS - 1) // SUP_BLOCKS
        return lax.fori_loop(0, nseg, segment, (m_run, o_run))[1]

    blocks = [Block(qb) for qb in range(QB_PER_STEP)]
    fronts = [front(blk) for blk in blocks]
    fronts = [middle(blk, *f) for blk, f in zip(blocks, fronts)]
    tails = [rest(blk, m_run, o_run) for blk, (_, m_run, o_run) in zip(blocks, fronts)]
    for blk, (o_win, _, _), o_run in zip(blocks, fronts, tails):
        qb = blk.qb
        gate = gate_ref[qb, 0]
        mix = gate[0:1] * ocmp_ref[0, 0, qb] + gate[1:2] * _normalize(o_run) + gate[2:3] * o_win
        rows = jnp.concatenate([mix[:, r * Q_BLOCK:(r + 1) * Q_BLOCK] for r in range(HEADS_PER_GROUP)],
                               axis=0)
        out_ref[0, qb * Q_BLOCK:(qb + 1) * Q_BLOCK, :] = rows.T.astype(bf16)


def _slc_win_attention(lst, cnt, qt, ks, vs, kw, vw, madd, ocmp, gate_t, wtab):
    b, t, _ = ks.shape
    g = N_GROUPS
    nqb, nsel = t // Q_BLOCK, t // SEL_BLOCK
    width = HEADS_PER_GROUP * Q_BLOCK
    nstep = nqb // QB_PER_STEP
    flat = lambda bi, gi, ci: (bi * g + gi) * nstep + ci
    smem = lambda w_: pl.BlockSpec((QB_PER_STEP, 1, w_), lambda bi, gi, ci: (flat(bi, gi, ci), 0, 0),
                                   memory_space=pltpu.SMEM)
    per_q = lambda r_, c_: pl.BlockSpec((1, 1, QB_PER_STEP, r_, c_), lambda bi, gi, ci: (bi, gi, ci, 0, 0))
    keys = pl.BlockSpec((1, t, LANES), lambda bi, gi, ci: (bi, 0, gi), pipeline_mode=pl.Buffered(1))
    return pl.pallas_call(
        functools.partial(_slc_win_body, nsel=nsel),
        grid=(b, g, nstep),
        in_specs=[smem(nsel), smem(LANES),
                  pl.BlockSpec((QB_PER_STEP, 1, LANES, width), lambda bi, gi, ci: (bi * nstep + ci, gi, 0, 0)),
                  keys, keys, keys, keys,
                  per_q(nsel, Q_BLOCK), per_q(HEAD_DIM, width),
                  pl.BlockSpec((QB_PER_STEP, 1, 8, width), lambda bi, gi, ci: (bi * nstep + ci, gi, 0, 0)),
                  pl.BlockSpec(wtab.shape, lambda bi, gi, ci: (0, 0, 0))],
        out_specs=pl.BlockSpec((1, QB_PER_STEP * Q_BLOCK, 2 * LANES), lambda bi, gi, ci: (bi, ci, gi)),
        out_shape=jax.ShapeDtypeStruct((b, t, ATTN_WIDTH), bf16),
        scratch_shapes=[pltpu.VMEM((SUP_BLOCKS * SEL_BLOCK, width), f32)],
        compiler_params=pltpu.CompilerParams(dimension_semantics=("arbitrary",) * 3,
                                             vmem_limit_bytes=VMEM_LIMIT),
        name="slc_win_attention",
    )(lst, cnt, qt, ks, vs, kw, vw, madd, ocmp, gate_t, wtab)


PREV_ROWS = 16


def _rms(v, g):
    return (v * lax.rsqrt(jnp.mean(v * v, axis=-1, keepdims=True) + EPS)) * g


def _mix_ffn_body(layer_ref, x_ref, attn_ref, u_ref, uprev_ref, bg_ref, cw_ref, go_ref, wo_ref, gf_ref, wu_ref,
                  wd_ref,
                  out_ref, *, tm, seq, chunk):
    first = (pl.program_id(0) * tm) % seq == 0
    u = u_ref[...].astype(f32)
    prev = jnp.where(first, 0.0, uprev_ref[0].astype(f32))
    ext = jnp.concatenate([prev, u], axis=0)
    cw = cw_ref[...]
    conv = (cw[0:1] * ext[PREV_ROWS - 2:PREV_ROWS - 2 + tm] + cw[1:2] * ext[PREV_ROWS - 1:PREV_ROWS - 1 + tm]
            + cw[2:3] * u)
    conv = bg_ref[...].astype(f32) * conv
    go = go_ref[...]
    mixed = jnp.concatenate([_rms(attn_ref[...].astype(f32), go[:, :ATTN_WIDTH]),
                             _rms(conv, go[:, ATTN_WIDTH:])], axis=1).astype(bf16)
    x = x_ref[...] + _dot(mixed, wo_ref[...])
    h = _rms(x, gf_ref[...]).astype(bf16)
    acc = x
    for c in range(D_FF // chunk):
        a = jnp.maximum(_dot(h, wu_ref[:, c * chunk:(c + 1) * chunk]), 0.0)
        acc = acc + _dot((a * a).astype(bf16), wd_ref[c * chunk:(c + 1) * chunk, :])
    out_ref[...] = acc


def _mix_ffn(layer, x2, attn2, u, bgate, cw, go, wo_all, gf, wu_all, wd_all, *, seq, in_place, tm=512, chunk=1024):
    n = x2.shape[0]
    uprev = u.reshape(n // PREV_ROWS, PREV_ROWS, CONV_WIDTH)
    row = lambda w_: pl.BlockSpec((tm, w_), lambda i, layer: (i, 0))
    full = lambda a: pl.BlockSpec(a.shape, lambda i, layer: (0,) * a.ndim)
    once = lambda a: pl.BlockSpec((None,) + a.shape[1:], lambda i, layer: (layer[0],) + (0,) * (a.ndim - 1),
                                  pipeline_mode=pl.Buffered(1))
    return pl.pallas_call(
        functools.partial(_mix_ffn_body, tm=tm, seq=seq, chunk=chunk),
        grid_spec=pltpu.PrefetchScalarGridSpec(
            num_scalar_prefetch=1, grid=(n // tm,),
            in_specs=[row(D_MODEL), row(ATTN_WIDTH), row(CONV_WIDTH),
                      pl.BlockSpec((1, PREV_ROWS, CONV_WIDTH),
                                   lambda i, layer: (jnp.maximum(i * (tm // PREV_ROWS) - 1, 0), 0, 0)),
                      row(CONV_WIDTH), full(cw), full(go), once(wo_all), full(gf), once(wu_all), once(wd_all)],
            out_specs=row(D_MODEL)),
        out_shape=jax.ShapeDtypeStruct((n, D_MODEL), f32),
        input_output_aliases={1: 0} if in_place else {},
        compiler_params=pltpu.CompilerParams(dimension_semantics=("arbitrary",),
                                             vmem_limit_bytes=VMEM_LIMIT),
        name="mix_ffn",
    )(layer, x2, attn2, u, uprev, bgate, cw, go, wo_all, gf, wu_all, wd_all)


def _interleave_zero(w):
    z = jnp.zeros(w.shape[:-1] + (HEAD_DIM,), w.dtype)
    return jnp.concatenate([w[..., :HEAD_DIM], z, w[..., HEAD_DIM:], z], axis=-1)


def _prep_params(g_mix_norm, w_in, g_q, g_k, pe_cmp, w_cmp1, b_cmp1, w_cmp2, b_cmp2,
                 conv_w, g_out, w_o, g_ffn_norm, w_up, w_down):
    depth = w_in.shape[0]
    o = np.cumsum([0, ATTN_WIDTH] + [LANES] * 6 + [N_BRANCH * N_HEADS] + [CONV_WIDTH] * 3)
    part = lambda i: w_in[..., int(o[i]):int(o[i + 1])]
    q, kc, vc, ks, vs, kw, vw, gl, hc, cg, bg = [part(i) for i in range(11)]
    gl = jnp.pad(gl, ((0, 0), (0, 0), (0, LANES - gl.shape[-1])))
    w = jnp.concatenate([q, kc, vc, gl, jnp.zeros_like(gl), ks, kw, vs, vw, hc, cg, bg], axis=-1).astype(bf16)
    pair = lambda gk: jnp.concatenate([gk, gk], axis=-1)[:, None, :]
    tile2 = lambda gk: _interleave_zero(jnp.concatenate([gk, gk], axis=-1))[:, None, :]
    w1 = w_cmp1.astype(bf16).reshape(depth, 2, 2, CMP_STRIDE, HEAD_DIM, CMP_HIDDEN)
    z1 = jnp.zeros_like(w1)
    w1g = jnp.stack([jnp.concatenate([w1, z1], axis=-1), jnp.concatenate([z1, w1], axis=-1)], axis=4)
    w1g = w1g.reshape(depth, 2, 2, CMP_STRIDE * LANES, N_GROUPS * CMP_HIDDEN)
    pe = pe_cmp.reshape(depth, 2, 2, CMP_STRIDE, 1, HEAD_DIM)
    pe = jnp.broadcast_to(pe, (depth, 2, 2, CMP_STRIDE, N_GROUPS, HEAD_DIM)).reshape(depth, 2, 2, 1, -1)
    pe = jnp.pad(pe, ((0, 0), (0, 0), (0, 0), (0, 7), (0, 0))).astype(bf16)
    b1 = jnp.concatenate([b_cmp1, b_cmp1], axis=-1)[:, :, None, :]
    z2 = jnp.zeros_like(w_cmp2)
    w2 = jnp.concatenate([jnp.concatenate([w_cmp2, z2], axis=-1), jnp.concatenate([z2, w_cmp2], axis=-1)],
                         axis=2)
    w2 = _interleave_zero(w2).astype(bf16)
    b2 = _interleave_zero(jnp.concatenate([b_cmp2, b_cmp2], axis=-1))[:, :, None, :]
    return dict(
        gm=g_mix_norm[:, None, :], w=w,
        gq=jnp.tile(g_q, (1, N_HEADS))[:, None, :],
        gks=pair(g_k[:, 1]), gkw=pair(g_k[:, 2]), gkc=tile2(g_k[:, 0]),
        w1=w1g, pe=pe, b1=b1,
        w2k=w2[:, 0], w2v=w2[:, 1], b2k=b2[:, 0], b2v=b2[:, 1],
        cw=jnp.pad(conv_w, ((0, 0), (0, 8 - conv_w.shape[1]), (0, 0))),
        go=g_out[:, None, :], wo=w_o.astype(bf16),
        gf=g_ffn_norm[:, None, :], wu=w_up.astype(bf16), wd=w_down.astype(bf16),
    )


def _constants(nsel):
    lane = np.arange(LANES)
    e = (lane[:, None] // HEAD_DIM == lane[None, :] // HEAD_DIM).astype(np.float32)
    tail = np.zeros((N_GROUPS, HEAD_DIM, HEADS_PER_GROUP * Q_BLOCK), np.float32)
    for gi in range(N_GROUPS):
        for r in range(HEADS_PER_GROUP):
            rest = np.float64(2.0 ** -(gi * HEADS_PER_GROUP + r + 1)) * LOG2E
            for k in range(SLOPE_TERMS):
                term = np.float64(np.asarray(rest, np.float32).astype(jnp.bfloat16).astype(np.float32))
                tail[gi, k, r * Q_BLOCK:(r + 1) * Q_BLOCK] = term * LANES
                tail[gi, SLOPE_TERMS + k, r * Q_BLOCK:(r + 1) * Q_BLOCK] = term
                rest = rest - term
    mct = np.zeros((IMP_ROWS, CMP_CHUNK), np.float32)
    for i in range(CMP_CHUNK):
        lo, hi = i * CMP_STRIDE, i * CMP_STRIDE + CMP_BLOCK
        for jj in range(SEL_PER_CHUNK + 1):
            ov = min(hi, (jj + 1) * SEL_BLOCK) - max(lo, jj * SEL_BLOCK)
            if ov > 0:
                mct[jj, i] = ov / CMP_BLOCK
    kk = np.arange(LANES)[:, None]
    ql = np.tile(np.arange(Q_BLOCK), HEADS_PER_GROUP)[None, :]
    wtab = np.zeros((4, LANES, HEADS_PER_GROUP * Q_BLOCK), np.float32)
    wtab[1] = np.where(kk > ql, 0.0, NEG)
    wtab[2] = np.where(kk <= ql, 0.0, NEG)
    wtab[3] = NEG
    tri = np.arange(nsel)[:, None] < np.arange(nsel)[None, :]
    return (jnp.asarray(e, bf16), jnp.asarray(tail, bf16), jnp.asarray(mct, bf16), jnp.asarray(wtab),
            jnp.asarray(tri, bf16))


def _layer(x2, layer, p, big, consts, *, batch, seq, in_place):
    e, tail, mct, wtab, tri = consts
    g = N_GROUPS
    nqb, nsel, tc = seq // Q_BLOCK, seq // SEL_BLOCK, seq // CMP_STRIDE
    qt, kc, vc, ks, kw, vs, vw, u, bgate, gate_t = _inproj(
        layer, x2, p["gm"], big["w"], p["gq"], p["gks"], p["gkw"], e, tail, seq=seq)
    kcmp, vcmp = _compress(layer, kc.reshape(batch, tc, CMP_STRIDE * LANES),
                           vc.reshape(batch, tc, CMP_STRIDE * LANES),
                           big["w1"], p["pe"], p["b1"], p["w2k"], p["w2v"], p["b2k"], p["b2v"], p["gkc"], e)
    ocmp, madd, lst, cnt = _cmp_attention(qt, kcmp, vcmp, mct, tri, batch=batch)
    rows3 = lambda a: a.reshape(batch, seq, 2 * LANES)
    smem = lambda a: a[:, :, :, 0, :].reshape(batch * g * nqb, 1, a.shape[-1])
    attn = _slc_win_attention(smem(lst), smem(cnt), qt, rows3(ks), rows3(vs),
                              rows3(kw), rows3(vw), madd, ocmp, gate_t, wtab)
    return _mix_ffn(layer, x2, attn.reshape(batch * seq, ATTN_WIDTH), u, bgate, p["cw"], p["go"], big["wo"],
                    p["gf"], big["wu"], big["wd"], seq=seq, in_place=in_place)


def kernel(x, g_mix_norm, w_in, g_q, g_k, pe_cmp, w_cmp1, b_cmp1, w_cmp2, b_cmp2, conv_w, g_out, w_o,
           g_ffn_norm, w_up, w_down):
    batch, seq, d = x.shape
    assert d == D_MODEL and seq % (CMP_CHUNK * CMP_STRIDE) == 0 and seq >= WINDOW + Q_BLOCK
    params = _prep_params(g_mix_norm, w_in, g_q, g_k, pe_cmp, w_cmp1, b_cmp1, w_cmp2, b_cmp2,
                          conv_w, g_out, w_o, g_ffn_norm, w_up, w_down)
    consts = _constants(seq // SEL_BLOCK)

    big = {k: params.pop(k) for k in ("w", "w1", "wo", "wu", "wd")}
    layers = jnp.arange(w_in.shape[0], dtype=i32)[:, None]

    def step(x2, xs):
        layer, p = xs
        return _layer(x2, layer, p, big, consts, batch=batch, seq=seq, in_place=True), None

    first = jax.tree.map(lambda a: a[0], (layers, params))
    rest = jax.tree.map(lambda a: a[1:], (layers, params))
    x2 = _layer(x.reshape(batch * seq, d), *first, big, consts, batch=batch, seq=seq, in_place=False)
    x2, _ = lax.scan(step, x2, rest)
    return x2.reshape(batch, seq, d)
```

```python
import functools

import numpy as np
import jax
import jax.numpy as jnp
from jax import lax
from jax.experimental import pallas as pl
from jax.experimental.pallas import tpu as pltpu

f32 = jnp.float32
bf16 = jnp.bfloat16
i32 = jnp.int32

D_MODEL = 1024
HEAD_DIM = 64
N_HEADS = 8
N_GROUPS = 2
HEADS_PER_GROUP = 4
ATTN_WIDTH = 512
CONV_WIDTH = 512
N_BRANCH = 3
CMP_BLOCK = 32
CMP_STRIDE = 16
CMP_HIDDEN = 256
SEL_BLOCK = 64
SEL_TOPK = 16
WINDOW = 512
Q_BLOCK = 128
D_FF = 4096
EPS = 1e-6
NEG = -1e30
LOG2E = 1.4426950408889634
SLOPE_TERMS = 3
PICKED = -3e38
LANES = 128
CMP_CHUNK = 256
SEL_PER_CHUNK = CMP_CHUNK * CMP_STRIDE // SEL_BLOCK
IMP_ROWS = SEL_PER_CHUNK + 8
VMEM_LIMIT = 56 * 1024 * 1024

PROJ_CHUNK = 512
SUP_BLOCKS = 32
HEAD_BLOCKS = 20
GROUP_BLOCKS = 4
QB_PER_STEP = 8
PIPE_DEPTH = 2
CMP_QB = 4


def _dot(a, b):
    return jnp.dot(a, b, preferred_element_type=f32)


def _dot_nt(a, b):
    return lax.dot_general(a, b, (((1,), (1,)), ((), ())), preferred_element_type=f32)


def _dot_tn(a, b):
    return lax.dot_general(a, b, (((0,), (0,)), ((), ())), preferred_element_type=f32)


def _head_norm(z, e, g):
    sq = (z * z).astype(bf16)
    outs = []
    step = e.shape[0]
    for c in range(z.shape[1] // step):
        sl = slice(c * step, (c + 1) * step)
        outs.append(z[:, sl] * lax.rsqrt(_dot(sq[:, sl], e) * (1.0 / HEAD_DIM) + EPS))
    y = outs[0] if len(outs) == 1 else jnp.concatenate(outs, axis=1)
    return y * g


def _pos_cols(pos, width):
    rows = pos.shape[0]
    lane = lax.broadcasted_iota(i32, (rows, LANES), 1) - HEAD_DIM
    tile = jnp.where(lane < 0, 0.0, jnp.where(lane < SLOPE_TERMS, (pos >> 7).astype(f32),
                                              jnp.where(lane < 2 * SLOPE_TERMS, (pos & 127).astype(f32), 0.0)))
    return tile if width == LANES else jnp.concatenate([tile] * (width // LANES), axis=1)


def _ones_col(rows, width):
    lane = lax.broadcasted_iota(i32, (rows, LANES), 1)
    tile = jnp.where(lane == HEAD_DIM, 1.0, 0.0)
    return tile if width == LANES else jnp.concatenate([tile] * (width // LANES), axis=1)


def _inproj_body(layer_ref, x_ref, gm_ref, w_ref, gq_ref, gks_ref, gkw_ref, e_ref, tail_ref,
                 qt_ref, kc_ref, vc_ref, ks_ref, kw_ref, vs_ref, vw_ref, u_ref, bg_ref, gate_ref, kv_scr,
                 *, tm, seq):
    x = x_ref[...]
    ms = jnp.mean(x * x, axis=-1, keepdims=True)
    h = ((x * lax.rsqrt(ms + EPS)) * gm_ref[...]).astype(bf16)
    e = e_ref[...]
    two = 2 * LANES

    def proj(k):
        return _dot(h, w_ref[:, k * PROJ_CHUNK:(k + 1) * PROJ_CHUNK])

    def emit_queries(z):
        qn = _head_norm(z, e, gq_ref[...]) * (HEAD_DIM ** -0.5 * LOG2E)
        gw = HEADS_PER_GROUP * HEAD_DIM
        for cb in range(tm // Q_BLOCK):
            for g in range(N_GROUPS):
                blk = qn[cb * Q_BLOCK:(cb + 1) * Q_BLOCK, g * gw:(g + 1) * gw].T
                top = jnp.concatenate([blk[r * HEAD_DIM:(r + 1) * HEAD_DIM] for r in range(HEADS_PER_GROUP)],
                                      axis=1)
                qt_ref[cb, g] = jnp.concatenate([top.astype(bf16), tail_ref[g]], axis=0)

    def emit_compress_inputs(z):
        kv_scr[0] = z[:, 0:LANES]
        kv_scr[1] = z[:, LANES:two]
        for p in range(CMP_STRIDE):
            rows = pl.ds(p, tm // CMP_STRIDE, stride=CMP_STRIDE)
            kc_ref[:, p * LANES:(p + 1) * LANES] = kv_scr[0, rows, :].astype(bf16)
            vc_ref[:, p * LANES:(p + 1) * LANES] = kv_scr[1, rows, :].astype(bf16)
        gates = 1.0 / (1.0 + jnp.exp(-z[:, two:two + LANES]))
        pad = jnp.zeros((8 - N_BRANCH, HEADS_PER_GROUP * Q_BLOCK), f32)
        for cb in range(tm // Q_BLOCK):
            gt = gates[cb * Q_BLOCK:(cb + 1) * Q_BLOCK].T
            for g in range(N_GROUPS):
                col = lambda r, br: (g * HEADS_PER_GROUP + r) * N_BRANCH + br
                rows = [jnp.concatenate([gt[col(r, br):col(r, br) + 1] for r in range(HEADS_PER_GROUP)], axis=1)
                        for br in range(N_BRANCH)]
                gate_ref[cb, g] = jnp.concatenate(rows + [pad], axis=0)

    def emit_keys_values(z):
        t0 = (pl.program_id(0) * tm) % seq
        pc = _pos_cols(lax.broadcasted_iota(i32, (tm, LANES), 0) + t0, LANES)
        ones = _ones_col(tm, LANES)
        low = lax.broadcasted_iota(i32, (tm, LANES), 1) < HEAD_DIM

        def spread(t, extra):
            other = pltpu.roll(t, HEAD_DIM, axis=1)
            return jnp.concatenate([jnp.where(low, t, extra), jnp.where(low, other, extra)], axis=1).astype(bf16)

        ks_ref[...] = spread(_head_norm(z[:, 0:LANES], e, gks_ref[...]), pc)
        kw_ref[...] = spread(_head_norm(z[:, LANES:two], e, gkw_ref[...]), pc)
        vs_ref[...] = spread(z[:, two:two + LANES], ones)
        vw_ref[...] = spread(z[:, two + LANES:], ones)

    z0 = proj(0)
    z1 = proj(1)
    emit_queries(z0)
    z2 = proj(2)
    emit_compress_inputs(z1)
    z3 = proj(3)
    emit_keys_values(z2)
    z4 = proj(4)
    z5 = proj(5)
    u_ref[...] = (z3 * z4).astype(bf16)
    bg_ref[...] = z5.astype(bf16)


def _layer_weight(a):
    return pl.BlockSpec((None,) + a.shape[1:], lambda i, layer: (layer[0],) + (0,) * (a.ndim - 1))


def _inproj(layer, x2, gm, w_all, gq, gks, gkw, e, tail, *, seq, tm=512):
    n = x2.shape[0]
    width = HEADS_PER_GROUP * Q_BLOCK
    row = lambda w_: pl.BlockSpec((tm, w_), lambda i, layer: (i, 0))
    full = lambda a: pl.BlockSpec(a.shape, lambda i, layer: (0,) * a.ndim)
    outs = [(2 * LANES, bf16), (2 * LANES, bf16),
            (2 * LANES, bf16), (2 * LANES, bf16), (CONV_WIDTH, bf16), (CONV_WIDTH, bf16)]
    per_block = lambda rows: pl.BlockSpec((tm // Q_BLOCK, N_GROUPS, rows, width), lambda i, layer: (i, 0, 0, 0))
    qt_spec = per_block(LANES)
    qt_shape = jax.ShapeDtypeStruct((n // Q_BLOCK, N_GROUPS, LANES, width), bf16)
    gate_shape = jax.ShapeDtypeStruct((n // Q_BLOCK, N_GROUPS, 8, width), f32)
    chunk_spec = pl.BlockSpec((tm // CMP_STRIDE, CMP_STRIDE * LANES), lambda i, layer: (i, 0))
    chunk_shape = jax.ShapeDtypeStruct((n // CMP_STRIDE, CMP_STRIDE * LANES), bf16)
    return pl.pallas_call(
        functools.partial(_inproj_body, tm=tm, seq=seq),
        grid_spec=pltpu.PrefetchScalarGridSpec(
            num_scalar_prefetch=1, grid=(n // tm,),
            in_specs=[row(D_MODEL), full(gm), _layer_weight(w_all), full(gq), full(gks), full(gkw), full(e),
                      full(tail)],
            out_specs=[qt_spec, chunk_spec, chunk_spec] + [row(w_) for w_, _ in outs] + [per_block(8)],
            scratch_shapes=[pltpu.VMEM((2, tm, LANES), f32)]),
        out_shape=[qt_shape, chunk_shape, chunk_shape]
        + [jax.ShapeDtypeStruct((n, w_), dt) for w_, dt in outs] + [gate_shape],
        compiler_params=pltpu.CompilerParams(dimension_semantics=("arbitrary",),
                                             vmem_limit_bytes=VMEM_LIMIT),
        name="inproj",
    )(layer, x2, gm, w_all, gq, gks, gkw, e, tail)


def _gelu_tanh(x):
    return 0.5 * x * (1.0 + jnp.tanh(0.7978845608028654 * (x + 0.044715 * (x * x * x))))


def _compress_body(layer_ref, zk_ref, zv_ref, w1_ref, pe_ref, b1_ref, w2k_ref, w2v_ref, b2k_ref, b2v_ref,
                   gk_ref, e_ref, kc_ref, vc_ref, *, tc):
    last = lax.broadcasted_iota(i32, (tc, 1), 0) == tc - 1

    def hidden(z_ref, kind):
        z = z_ref[0]
        a = _dot(z, w1_ref[kind, 0])
        b = _dot(z, w1_ref[kind, 1])
        b = jnp.concatenate([b[1:], jnp.zeros((1, b.shape[1]), f32)], axis=0)
        bias = (_dot(pe_ref[kind, 0], w1_ref[kind, 0]) + _dot(pe_ref[kind, 1], w1_ref[kind, 1]))[0:1]
        return _gelu_tanh(a + b + bias + b1_ref[kind]).astype(bf16)

    k = _dot(hidden(zk_ref, 0), w2k_ref[...]) + b2k_ref[...]
    k = _head_norm(k, e_ref[...], gk_ref[...])
    pos = lax.broadcasted_iota(i32, (tc, LANES), 0) * CMP_STRIDE + (CMP_BLOCK - 1)
    k = k + _pos_cols(pos, 2 * LANES)
    kc_ref[0] = jnp.where(last, 0.0, k).astype(bf16)
    v = _dot(hidden(zv_ref, 1), w2v_ref[...]) + b2v_ref[...]
    vc_ref[0] = jnp.where(last, 0.0, v + _ones_col(tc, 2 * LANES)).astype(bf16)


def _compress(layer, zk, zv, w1_all, pe, b1, w2k, w2v, b2k, b2v, gk, e):
    b, tc, _ = zk.shape
    blk = lambda a: pl.BlockSpec((1,) + a.shape[1:], lambda i, layer: (i,) + (0,) * (a.ndim - 1))
    full = lambda a: pl.BlockSpec(a.shape, lambda i, layer: (0,) * a.ndim)
    return pl.pallas_call(
        functools.partial(_compress_body, tc=tc),
        grid_spec=pltpu.PrefetchScalarGridSpec(
            num_scalar_prefetch=1, grid=(b,),
            in_specs=[blk(zk), blk(zv), _layer_weight(w1_all)]
            + [full(a) for a in (pe, b1, w2k, w2v, b2k, b2v, gk, e)],
            out_specs=[pl.BlockSpec((1, tc, 2 * LANES), lambda i, layer: (i, 0, 0))] * 2),
        out_shape=[jax.ShapeDtypeStruct((b, tc, 2 * LANES), bf16)] * 2,
        compiler_params=pltpu.CompilerParams(dimension_semantics=("arbitrary",),
                                             vmem_limit_bytes=VMEM_LIMIT),
        name="compress",
    )(layer, zk, zv, w1_all, pe, b1, w2k, w2v, b2k, b2v, gk, e)


def _query_pos(c):
    lane = lax.broadcasted_iota(i32, (1, HEADS_PER_GROUP * Q_BLOCK), 1)
    return c * Q_BLOCK + (lane & (Q_BLOCK - 1))


def _pick_top(vs, jf, n_pick):
    vs = list(vs)
    for _ in range(n_pick):
        for g, v in enumerate(vs):
            mx = jnp.max(v, axis=0, keepdims=True)
            idx = jnp.min(jnp.where(v == mx, jf, float(jf.shape[0])), axis=0, keepdims=True)
            vs[g] = jnp.where(jf == idx, PICKED, v)
    return tuple(vs)


def _cmp_variant(nchunk, c0, qt_ref, kc_ref, vc_ref, mct_ref, tri_ref, o_ref, madd_ref, lst_ref, cnt_ref,
                 *, nsel):
    units = [(qb, g) for qb in range(CMP_QB) for g in range(N_GROUPS)]
    nrow = nchunk * CMP_CHUNK
    npre = nchunk * SEL_PER_CHUNK
    tqs = [_query_pos(c0 + qb) for qb in range(CMP_QB)]
    sub = lax.broadcasted_iota(i32, (CMP_CHUNK, 1), 0)
    lanes = lambda g: slice(g * LANES, (g + 1) * LANES)
    chunk = lambda a, k: a[k * CMP_CHUNK:(k + 1) * CMP_CHUNK]

    tiles = []
    for qb, g in units:
        s = _dot(kc_ref[0, 0:nrow, lanes(g)], qt_ref[qb, g])
        row = []
        for k in range(nchunk):
            t = chunk(s, k)
            if k >= nchunk - 2:
                seen = (k * CMP_CHUNK + sub) * CMP_STRIDE + (CMP_BLOCK - 1) <= tqs[qb]
                t = jnp.where(seen, t, NEG)
            row.append(t)
        tiles.append(row)

    accs, imps = [], []
    for u, (qb, g) in enumerate(units):
        m = tiles[u][0].max(axis=0, keepdims=True)
        for t in tiles[u][1:]:
            m = jnp.maximum(m, t.max(axis=0, keepdims=True))
        parts = [jnp.exp2(t - m).astype(bf16) for t in tiles[u]]
        accs.append(_dot_tn(vc_ref[0, 0:nrow, lanes(g)], jnp.concatenate(parts, axis=0)))
        rows, carry = [], None
        for part in parts:
            piece = _dot(mct_ref[...], part)
            body = piece[0:SEL_PER_CHUNK]
            if carry is not None:
                body = jnp.concatenate([body[0:8] + carry, body[8:]], axis=0)
            rows.append(body)
            carry = piece[SEL_PER_CHUNK:IMP_ROWS]
        imps.append(rows[0] if nchunk == 1 else jnp.concatenate(rows, axis=0))

    j = lax.broadcasted_iota(i32, (npre, Q_BLOCK), 0)
    jf = j.astype(f32)
    t1s = [tq[:, 0:Q_BLOCK] for tq in tqs]
    valids = [j * SEL_BLOCK <= t1 for t1 in t1s]
    vs = []
    for u, (qb, g) in enumerate(units):
        any_key = tqs[qb] >= CMP_BLOCK - 1
        inv = jnp.where(any_key, 1.0 / jnp.maximum(accs[u][HEAD_DIM:HEAD_DIM + 1], 1e-30), 0.0)
        o_ref[0, g, qb] = accs[u][0:HEAD_DIM] * inv
        imp4 = imps[u] * inv
        imp = imp4[:, 0:Q_BLOCK]
        for r in range(1, HEADS_PER_GROUP):
            imp = imp + imp4[:, r * Q_BLOCK:(r + 1) * Q_BLOCK]
        jt = t1s[qb] >> 6
        v = jnp.where(valids[qb], imp, NEG)
        vs.append(jnp.where(j == 0, PICKED, jnp.where(j == jt, PICKED, jnp.where(j == jt - 1, PICKED, v))))
    vs = _pick_top(vs, jf, SEL_TOPK - 3)

    rr = lax.broadcasted_iota(i32, (nsel, nsel), 0).astype(f32)
    ones8 = jnp.ones((8, Q_BLOCK), bf16)
    jrow = jnp.broadcast_to(lax.broadcasted_iota(i32, (1, nsel), 1).astype(f32), (8, nsel)).astype(bf16)
    for u, (qb, g) in enumerate(units):
        picked = vs[u] == PICKED
        madd = jnp.where(valids[qb], jnp.where(picked, 0.0, NEG), NEG)
        sel = jnp.where(valids[qb], jnp.where(picked, 1.0, 0.0), 0.0).astype(bf16)
        if npre < nsel:
            madd = jnp.concatenate([madd, jnp.full((nsel - npre, Q_BLOCK), NEG, f32)], axis=0)
            sel = jnp.concatenate([sel, jnp.zeros((nsel - npre, Q_BLOCK), bf16)], axis=0)
        madd_ref[0, g, qb] = madd
        flag = jnp.where(_dot_nt(ones8, sel)[0:1] > 0.0, 1.0, 0.0)
        flag8 = jnp.broadcast_to(flag, (8, nsel)).astype(bf16)
        prefix = _dot(flag8, tri_ref[...])[0:1]
        place = jnp.where(prefix == rr, flag, 0.0).astype(bf16)
        lst_ref[0, g, qb] = _dot_nt(jrow, place).astype(i32)
        cnt_ref[0, g, qb] = _dot(flag8, jnp.ones((nsel, LANES), bf16)).astype(i32)


def _cmp_body(qt_ref, kc_ref, vc_ref, mct_ref, tri_ref, o_ref, madd_ref, lst_ref, cnt_ref, *, nsel, nvar):
    c0 = pl.program_id(1) * CMP_QB
    c_last = c0 + CMP_QB - 1
    nch = (c_last * (Q_BLOCK // CMP_STRIDE) + (Q_BLOCK - CMP_BLOCK) // CMP_STRIDE) // CMP_CHUNK + 1
    for n in range(1, nvar + 1):
        pl.when(nch == n)(functools.partial(
            _cmp_variant, n, c0, qt_ref, kc_ref, vc_ref, mct_ref, tri_ref, o_ref, madd_ref, lst_ref, cnt_ref,
            nsel=nsel))


def _cmp_attention(qt, kc, vc, mct, tri, *, batch):
    g = N_GROUPS
    nqb = qt.shape[0] // batch
    tc = kc.shape[1]
    nsel = tc * CMP_STRIDE // SEL_BLOCK
    width = HEADS_PER_GROUP * Q_BLOCK
    nstep = nqb // CMP_QB
    per_q = lambda r_, c_: pl.BlockSpec((1, g, CMP_QB, r_, c_), lambda bi, ci: (bi, 0, ci, 0, 0))
    shape = lambda r_, c_, dt: jax.ShapeDtypeStruct((batch, g, nqb, r_, c_), dt)
    return pl.pallas_call(
        functools.partial(_cmp_body, nsel=nsel, nvar=tc // CMP_CHUNK),
        grid=(batch, nstep),
        in_specs=[pl.BlockSpec((CMP_QB, g, LANES, width), lambda bi, ci: (bi * nstep + ci, 0, 0, 0)),
                  pl.BlockSpec((1, tc, 2 * LANES), lambda bi, ci: (bi, 0, 0)),
                  pl.BlockSpec((1, tc, 2 * LANES), lambda bi, ci: (bi, 0, 0)),
                  pl.BlockSpec(mct.shape, lambda bi, ci: (0, 0)),
                  pl.BlockSpec(tri.shape, lambda bi, ci: (0, 0))],
        out_specs=[per_q(HEAD_DIM, width), per_q(nsel, Q_BLOCK), per_q(8, nsel), per_q(8, LANES)],
        out_shape=[shape(HEAD_DIM, width, f32), shape(nsel, Q_BLOCK, f32),
                   shape(8, nsel, i32), shape(8, LANES, i32)],
        compiler_params=pltpu.CompilerParams(dimension_semantics=("arbitrary",) * 2,
                                             vmem_limit_bytes=VMEM_LIMIT),
        name="cmp_attention",
    )(qt, kc, vc, mct, tri)


def _normalize(o_aug):
    return o_aug[0:HEAD_DIM] * (1.0 / jnp.maximum(o_aug[HEAD_DIM:HEAD_DIM + 1], 1e-30))


def _slc_win_body(lst_ref, cnt_ref, qt_ref, ks_ref, vs_ref, kw_ref, vw_ref,
                  madd_ref, ocmp_ref, gate_ref, wtab_ref, out_ref, s_scr, *, nsel):
    width = HEADS_PER_GROUP * Q_BLOCK
    nwin = (WINDOW + Q_BLOCK) // LANES
    wq = WINDOW // Q_BLOCK
    grp_rows = GROUP_BLOCKS * SEL_BLOCK

    class Block:
        def __init__(self, qb):
            self.qb = qb
            self.c = pl.program_id(2) * QB_PER_STEP + qb
            self.q0 = pl.multiple_of(self.c * Q_BLOCK, Q_BLOCK)
            self.qta = qt_ref[qb, 0]
            self.n_off = cnt_ref[qb, 0, 0] - 2

        def block_ids(self, first, nblk):
            ids = []
            for u in range(nblk):
                i = first + u
                j = lst_ref[self.qb, 0, jnp.minimum(i, nsel - 1)]
                ids.append((i < self.n_off, j, pl.multiple_of(j * SEL_BLOCK, SEL_BLOCK)))
            return ids

        def scores(self, ids):
            kcat = jnp.concatenate([ks_ref[0, pl.ds(r0, SEL_BLOCK), :] for _, _, r0 in ids], axis=0)
            sg = _dot(kcat, self.qta)
            tiles = []
            for u, (live, j, _) in enumerate(ids):
                mrow = jnp.where(live, madd_ref[0, 0, self.qb, pl.ds(j, 1), :], NEG)
                mrow = jnp.concatenate([mrow] * HEADS_PER_GROUP, axis=1)
                tiles.append(sg[u * SEL_BLOCK:(u + 1) * SEL_BLOCK] + mrow)
            return tiles

        def values(self, ids):
            return jnp.concatenate([vs_ref[0, pl.ds(r0, SEL_BLOCK), :] for _, _, r0 in ids], axis=0)

    def front(blk):
        c, q0, qta = blk.c, blk.q0, blk.qta
        ws = pl.multiple_of(jnp.maximum(c - wq, 0) * Q_BLOCK, Q_BLOCK)
        s = _dot(kw_ref[0, pl.ds(ws, WINDOW + Q_BLOCK), :], qta)
        chunks = []
        for k in range(nwin):
            steady = 1 if k == 0 else (2 if k == nwin - 1 else 0)
            tab = jnp.where(c >= wq, steady, jnp.where(k < c, 0, jnp.where(k == c, 2, 3)))
            chunks.append(s[k * LANES:(k + 1) * LANES] + wtab_ref[tab])
        head = blk.block_ids(0, HEAD_BLOCKS)
        tiles = [_dot(ks_ref[0, pl.ds(q0, Q_BLOCK), :], qta) + wtab_ref[2]] + blk.scores(head)
        return ws, chunks, head, tiles

    def softmax_pv(tiles, v):
        m = tiles[0].max(axis=0, keepdims=True)
        for su in tiles[1:]:
            m = jnp.maximum(m, su.max(axis=0, keepdims=True))
        p = jnp.concatenate([jnp.exp2(su - m).astype(bf16) for su in tiles], axis=0)
        return m, _dot_tn(v, p)

    def middle(blk, ws, chunks, head, tiles):
        _, o_win = softmax_pv(chunks, vw_ref[0, pl.ds(ws, WINDOW + Q_BLOCK), :])
        m_run, o_run = softmax_pv(
            tiles, jnp.concatenate([vs_ref[0, pl.ds(blk.q0, Q_BLOCK), :], blk.values(head)], axis=0))
        return _normalize(o_win), m_run, o_run

    def rest(blk, m_run, o_run):
        def segment(si, carry):
            m_run, o_run = carry
            base = HEAD_BLOCKS + si * SUP_BLOCKS
            ngrp = (jnp.minimum(SUP_BLOCKS, blk.n_off - base) + GROUP_BLOCKS - 1) // GROUP_BLOCKS
            rows = lambda gi: pl.ds(pl.multiple_of(gi * grp_rows, grp_rows), grp_rows)

            def score(gi, mx):
                sg = jnp.concatenate(blk.scores(blk.block_ids(base + gi * GROUP_BLOCKS, GROUP_BLOCKS)), axis=0)
                s_scr[rows(gi), :] = sg
                return jnp.maximum(mx, sg.max(axis=0, keepdims=True))

            m_new = lax.fori_loop(0, ngrp, score, m_run)

            def weigh(gi, acc):
                pg = jnp.exp2(s_scr[rows(gi), :] - m_new).astype(bf16)
                return acc + _dot_tn(blk.values(blk.block_ids(base + gi * GROUP_BLOCKS, GROUP_BLOCKS)), pg)

            o_seg = lax.fori_loop(0, ngrp, weigh, jnp.zeros((LANES, width), f32))
            return m_new, jnp.exp2(m_run - m_new) * o_run + o_seg

        nseg = (jnp.maximum(blk.n_off - HEAD_BLOCKS, 0) + SUP_BLOCKS - 1) // SUP_BLOCKS
        return lax.fori_loop(0, nseg, segment, (m_run, o_run))[1]

    blocks = [Block(qb) for qb in range(QB_PER_STEP)]
    scored = [front(blk) for blk in blocks[:PIPE_DEPTH]]
    fronts = []
    for i, blk in enumerate(blocks):
        if i + PIPE_DEPTH < len(blocks):
            scored.append(front(blocks[i + PIPE_DEPTH]))
        fronts.append(middle(blk, *scored[i]))
    tails = [rest(blk, m_run, o_run) for blk, (_, m_run, o_run) in zip(blocks, fronts)]
    for blk, (o_win, _, _), o_run in zip(blocks, fronts, tails):
        qb = blk.qb
        gate = gate_ref[qb, 0]
        mix = gate[0:1] * ocmp_ref[0, 0, qb] + gate[1:2] * _normalize(o_run) + gate[2:3] * o_win
        rows = jnp.concatenate([mix[:, r * Q_BLOCK:(r + 1) * Q_BLOCK] for r in range(HEADS_PER_GROUP)],
                               axis=0)
        out_ref[0, qb * Q_BLOCK:(qb + 1) * Q_BLOCK, :] = rows.T.astype(bf16)


def _slc_win_attention(lst, cnt, qt, ks, vs, kw, vw, madd, ocmp, gate_t, wtab):
    b, t, _ = ks.shape
    g = N_GROUPS
    nqb, nsel = t // Q_BLOCK, t // SEL_BLOCK
    width = HEADS_PER_GROUP * Q_BLOCK
    nstep = nqb // QB_PER_STEP
    flat = lambda bi, gi, ci: (bi * g + gi) * nstep + ci
    smem = lambda w_: pl.BlockSpec((QB_PER_STEP, 1, w_), lambda bi, gi, ci: (flat(bi, gi, ci), 0, 0),
                                   memory_space=pltpu.SMEM)
    per_q = lambda r_, c_: pl.BlockSpec((1, 1, QB_PER_STEP, r_, c_), lambda bi, gi, ci: (bi, gi, ci, 0, 0))
    keys = pl.BlockSpec((1, t, LANES), lambda bi, gi, ci: (bi, 0, gi), pipeline_mode=pl.Buffered(1))
    return pl.pallas_call(
        functools.partial(_slc_win_body, nsel=nsel),
        grid=(b, g, nstep),
        in_specs=[smem(nsel), smem(LANES),
                  pl.BlockSpec((QB_PER_STEP, 1, LANES, width), lambda bi, gi, ci: (bi * nstep + ci, gi, 0, 0)),
                  keys, keys, keys, keys,
                  per_q(nsel, Q_BLOCK), per_q(HEAD_DIM, width),
                  pl.BlockSpec((QB_PER_STEP, 1, 8, width), lambda bi, gi, ci: (bi * nstep + ci, gi, 0, 0)),
                  pl.BlockSpec(wtab.shape, lambda bi, gi, ci: (0, 0, 0))],
        out_specs=pl.BlockSpec((1, QB_PER_STEP * Q_BLOCK, 2 * LANES), lambda bi, gi, ci: (bi, ci, gi)),
        out_shape=jax.ShapeDtypeStruct((b, t, ATTN_WIDTH), bf16),
        scratch_shapes=[pltpu.VMEM((SUP_BLOCKS * SEL_BLOCK, width), f32)],
        compiler_params=pltpu.CompilerParams(dimension_semantics=("arbitrary",) * 3,
                                             vmem_limit_bytes=VMEM_LIMIT),
        name="slc_win_attention",
    )(lst, cnt, qt, ks, vs, kw, vw, madd, ocmp, gate_t, wtab)


PREV_ROWS = 16


def _rms(v, g):
    return (v * lax.rsqrt(jnp.mean(v * v, axis=-1, keepdims=True) + EPS)) * g


def _mix_ffn_body(layer_ref, x_ref, attn_ref, u_ref, uprev_ref, bg_ref, cw_ref, go_ref, wo_ref, gf_ref, wu_ref,
                  wd_ref,
                  out_ref, *, tm, seq, chunk):
    first = (pl.program_id(0) * tm) % seq == 0
    u = u_ref[...].astype(f32)
    prev = jnp.where(first, 0.0, uprev_ref[0].astype(f32))
    ext = jnp.concatenate([prev, u], axis=0)
    cw = cw_ref[...]
    conv = (cw[0:1] * ext[PREV_ROWS - 2:PREV_ROWS - 2 + tm] + cw[1:2] * ext[PREV_ROWS - 1:PREV_ROWS - 1 + tm]
            + cw[2:3] * u)
    conv = bg_ref[...].astype(f32) * conv
    go = go_ref[...]
    mixed = jnp.concatenate([_rms(attn_ref[...].astype(f32), go[:, :ATTN_WIDTH]),
                             _rms(conv, go[:, ATTN_WIDTH:])], axis=1).astype(bf16)
    x = x_ref[...] + _dot(mixed, wo_ref[...])
    h = _rms(x, gf_ref[...]).astype(bf16)
    acc = x
    for c in range(D_FF // chunk):
        a = jnp.maximum(_dot(h, wu_ref[:, c * chunk:(c + 1) * chunk]), 0.0)
        acc = acc + _dot((a * a).astype(bf16), wd_ref[c * chunk:(c + 1) * chunk, :])
    out_ref[...] = acc


def _mix_ffn(layer, x2, attn2, u, bgate, cw, go, wo_all, gf, wu_all, wd_all, *, seq, in_place, tm=512, chunk=1024):
    n = x2.shape[0]
    uprev = u.reshape(n // PREV_ROWS, PREV_ROWS, CONV_WIDTH)
    row = lambda w_: pl.BlockSpec((tm, w_), lambda i, layer: (i, 0))
    full = lambda a: pl.BlockSpec(a.shape, lambda i, layer: (0,) * a.ndim)
    once = lambda a: pl.BlockSpec((None,) + a.shape[1:], lambda i, layer: (layer[0],) + (0,) * (a.ndim - 1),
                                  pipeline_mode=pl.Buffered(1))
    return pl.pallas_call(
        functools.partial(_mix_ffn_body, tm=tm, seq=seq, chunk=chunk),
        grid_spec=pltpu.PrefetchScalarGridSpec(
            num_scalar_prefetch=1, grid=(n // tm,),
            in_specs=[row(D_MODEL), row(ATTN_WIDTH), row(CONV_WIDTH),
                      pl.BlockSpec((1, PREV_ROWS, CONV_WIDTH),
                                   lambda i, layer: (jnp.maximum(i * (tm // PREV_ROWS) - 1, 0), 0, 0)),
                      row(CONV_WIDTH), full(cw), full(go), once(wo_all), full(gf), once(wu_all), once(wd_all)],
            out_specs=row(D_MODEL)),
        out_shape=jax.ShapeDtypeStruct((n, D_MODEL), f32),
        input_output_aliases={1: 0} if in_place else {},
        compiler_params=pltpu.CompilerParams(dimension_semantics=("arbitrary",),
                                             vmem_limit_bytes=VMEM_LIMIT),
        name="mix_ffn",
    )(layer, x2, attn2, u, uprev, bgate, cw, go, wo_all, gf, wu_all, wd_all)


def _interleave_zero(w):
    z = jnp.zeros(w.shape[:-1] + (HEAD_DIM,), w.dtype)
    return jnp.concatenate([w[..., :HEAD_DIM], z, w[..., HEAD_DIM:], z], axis=-1)


def _prep_params(g_mix_norm, w_in, g_q, g_k, pe_cmp, w_cmp1, b_cmp1, w_cmp2, b_cmp2,
                 conv_w, g_out, w_o, g_ffn_norm, w_up, w_down):
    depth = w_in.shape[0]
    o = np.cumsum([0, ATTN_WIDTH] + [LANES] * 6 + [N_BRANCH * N_HEADS] + [CONV_WIDTH] * 3)
    part = lambda i: w_in[..., int(o[i]):int(o[i + 1])]
    q, kc, vc, ks, vs, kw, vw, gl, hc, cg, bg = [part(i) for i in range(11)]
    gl = jnp.pad(gl, ((0, 0), (0, 0), (0, LANES - gl.shape[-1])))
    w = jnp.concatenate([q, kc, vc, gl, jnp.zeros_like(gl), ks, kw, vs, vw, hc, cg, bg], axis=-1).astype(bf16)
    pair = lambda gk: jnp.concatenate([gk, gk], axis=-1)[:, None, :]
    tile2 = lambda gk: _interleave_zero(jnp.concatenate([gk, gk], axis=-1))[:, None, :]
    w1 = w_cmp1.astype(bf16).reshape(depth, 2, 2, CMP_STRIDE, HEAD_DIM, CMP_HIDDEN)
    z1 = jnp.zeros_like(w1)
    w1g = jnp.stack([jnp.concatenate([w1, z1], axis=-1), jnp.concatenate([z1, w1], axis=-1)], axis=4)
    w1g = w1g.reshape(depth, 2, 2, CMP_STRIDE * LANES, N_GROUPS * CMP_HIDDEN)
    pe = pe_cmp.reshape(depth, 2, 2, CMP_STRIDE, 1, HEAD_DIM)
    pe = jnp.broadcast_to(pe, (depth, 2, 2, CMP_STRIDE, N_GROUPS, HEAD_DIM)).reshape(depth, 2, 2, 1, -1)
    pe = jnp.pad(pe, ((0, 0), (0, 0), (0, 0), (0, 7), (0, 0))).astype(bf16)
    b1 = jnp.concatenate([b_cmp1, b_cmp1], axis=-1)[:, :, None, :]
    z2 = jnp.zeros_like(w_cmp2)
    w2 = jnp.concatenate([jnp.concatenate([w_cmp2, z2], axis=-1), jnp.concatenate([z2, w_cmp2], axis=-1)],
                         axis=2)
    w2 = _interleave_zero(w2).astype(bf16)
    b2 = _interleave_zero(jnp.concatenate([b_cmp2, b_cmp2], axis=-1))[:, :, None, :]
    return dict(
        gm=g_mix_norm[:, None, :], w=w,
        gq=jnp.tile(g_q, (1, N_HEADS))[:, None, :],
        gks=pair(g_k[:, 1]), gkw=pair(g_k[:, 2]), gkc=tile2(g_k[:, 0]),
        w1=w1g, pe=pe, b1=b1,
        w2k=w2[:, 0], w2v=w2[:, 1], b2k=b2[:, 0], b2v=b2[:, 1],
        cw=jnp.pad(conv_w, ((0, 0), (0, 8 - conv_w.shape[1]), (0, 0))),
        go=g_out[:, None, :], wo=w_o.astype(bf16),
        gf=g_ffn_norm[:, None, :], wu=w_up.astype(bf16), wd=w_down.astype(bf16),
    )


def _constants(nsel):
    lane = np.arange(LANES)
    e = (lane[:, None] // HEAD_DIM == lane[None, :] // HEAD_DIM).astype(np.float32)
    tail = np.zeros((N_GROUPS, HEAD_DIM, HEADS_PER_GROUP * Q_BLOCK), np.float32)
    for gi in range(N_GROUPS):
        for r in range(HEADS_PER_GROUP):
            rest = np.float64(2.0 ** -(gi * HEADS_PER_GROUP + r + 1)) * LOG2E
            for k in range(SLOPE_TERMS):
                term = np.float64(np.asarray(rest, np.float32).astype(jnp.bfloat16).astype(np.float32))
                tail[gi, k, r * Q_BLOCK:(r + 1) * Q_BLOCK] = term * LANES
                tail[gi, SLOPE_TERMS + k, r * Q_BLOCK:(r + 1) * Q_BLOCK] = term
                rest = rest - term
    mct = np.zeros((IMP_ROWS, CMP_CHUNK), np.float32)
    for i in range(CMP_CHUNK):
        lo, hi = i * CMP_STRIDE, i * CMP_STRIDE + CMP_BLOCK
        for jj in range(SEL_PER_CHUNK + 1):
            ov = min(hi, (jj + 1) * SEL_BLOCK) - max(lo, jj * SEL_BLOCK)
            if ov > 0:
                mct[jj, i] = ov / CMP_BLOCK
    kk = np.arange(LANES)[:, None]
    ql = np.tile(np.arange(Q_BLOCK), HEADS_PER_GROUP)[None, :]
    wtab = np.zeros((4, LANES, HEADS_PER_GROUP * Q_BLOCK), np.float32)
    wtab[1] = np.where(kk > ql, 0.0, NEG)
    wtab[2] = np.where(kk <= ql, 0.0, NEG)
    wtab[3] = NEG
    tri = np.arange(nsel)[:, None] < np.arange(nsel)[None, :]
    return (jnp.asarray(e, bf16), jnp.asarray(tail, bf16), jnp.asarray(mct, bf16), jnp.asarray(wtab),
            jnp.asarray(tri, bf16))


def _layer(x2, layer, p, big, consts, *, batch, seq, in_place):
    e, tail, mct, wtab, tri = consts
    g = N_GROUPS
    nqb, nsel, tc = seq // Q_BLOCK, seq // SEL_BLOCK, seq // CMP_STRIDE
    qt, kc, vc, ks, kw, vs, vw, u, bgate, gate_t = _inproj(
        layer, x2, p["gm"], big["w"], p["gq"], p["gks"], p["gkw"], e, tail, seq=seq)
    kcmp, vcmp = _compress(layer, kc.reshape(batch, tc, CMP_STRIDE * LANES),
                           vc.reshape(batch, tc, CMP_STRIDE * LANES),
                           big["w1"], p["pe"], p["b1"], p["w2k"], p["w2v"], p["b2k"], p["b2v"], p["gkc"], e)
    ocmp, madd, lst, cnt = _cmp_attention(qt, kcmp, vcmp, mct, tri, batch=batch)
    rows3 = lambda a: a.reshape(batch, seq, 2 * LANES)
    smem = lambda a: a[:, :, :, 0, :].reshape(batch * g * nqb, 1, a.shape[-1])
    attn = _slc_win_attention(smem(lst), smem(cnt), qt, rows3(ks), rows3(vs),
                              rows3(kw), rows3(vw), madd, ocmp, gate_t, wtab)
    return _mix_ffn(layer, x2, attn.reshape(batch * seq, ATTN_WIDTH), u, bgate, p["cw"], p["go"], big["wo"],
                    p["gf"], big["wu"], big["wd"], seq=seq, in_place=in_place)


def kernel(x, g_mix_norm, w_in, g_q, g_k, pe_cmp, w_cmp1, b_cmp1, w_cmp2, b_cmp2, conv_w, g_out, w_o,
           g_ffn_norm, w_up, w_down):
    batch, seq, d = x.shape
    assert d == D_MODEL and seq % (CMP_CHUNK * CMP_STRIDE) == 0 and seq >= WINDOW + Q_BLOCK
    params = _prep_params(g_mix_norm, w_in, g_q, g_k, pe_cmp, w_cmp1, b_cmp1, w_cmp2, b_cmp2,
                          conv_w, g_out, w_o, g_ffn_norm, w_up, w_down)
    consts = _constants(seq // SEL_BLOCK)

    big = {k: params.pop(k) for k in ("w", "w1", "wo", "wu", "wd")}
    layers = jnp.arange(w_in.shape[0], dtype=i32)[:, None]

    def step(x2, xs):
        layer, p = xs
        return _layer(x2, layer, p, big, consts, batch=batch, seq=seq, in_place=True), None

    first = jax.tree.map(lambda a: a[0], (layers, params))
    rest = jax.tree.map(lambda a: a[1:], (layers, params))
    x2 = _layer(x.reshape(batch * seq, d), *first, big, consts, batch=batch, seq=seq, in_place=False)
    x2, _ = lax.scan(step, x2, rest)
    return x2.reshape(batch, seq, d)
```

```python
import functools

import numpy as np
import jax
import jax.numpy as jnp
from jax import lax
from jax.experimental import pallas as pl
from jax.experimental.pallas import tpu as pltpu

f32 = jnp.float32
bf16 = jnp.bfloat16
i32 = jnp.int32

D_MODEL = 1024
HEAD_DIM = 64
N_HEADS = 8
N_GROUPS = 2
HEADS_PER_GROUP = 4
ATTN_WIDTH = 512
CONV_WIDTH = 512
N_BRANCH = 3
CMP_BLOCK = 32
CMP_STRIDE = 16
CMP_HIDDEN = 256
SEL_BLOCK = 64
SEL_TOPK = 16
WINDOW = 512
Q_BLOCK = 128
D_FF = 4096
EPS = 1e-6
NEG = -1e30
LOG2E = 1.4426950408889634
SLOPE_TERMS = 3
PICKED = -3e38
LANES = 128
CMP_CHUNK = 256
SEL_PER_CHUNK = CMP_CHUNK * CMP_STRIDE // SEL_BLOCK
IMP_ROWS = SEL_PER_CHUNK + 8
VMEM_LIMIT = 56 * 1024 * 1024

PROJ_CHUNK = 512
SUP_BLOCKS = 32
HEAD_BLOCKS = 20
GROUP_BLOCKS = 4
QB_PER_STEP = 16
PIPE_DEPTH = 2
CMP_QB = 4


def _dot(a, b):
    return jnp.dot(a, b, preferred_element_type=f32)


def _dot_nt(a, b):
    return lax.dot_general(a, b, (((1,), (1,)), ((), ())), preferred_element_type=f32)


def _dot_tn(a, b):
    return lax.dot_general(a, b, (((0,), (0,)), ((), ())), preferred_element_type=f32)


def _head_norm(z, e, g):
    sq = (z * z).astype(bf16)
    outs = []
    step = e.shape[0]
    for c in range(z.shape[1] // step):
        sl = slice(c * step, (c + 1) * step)
        outs.append(z[:, sl] * lax.rsqrt(_dot(sq[:, sl], e) * (1.0 / HEAD_DIM) + EPS))
    y = outs[0] if len(outs) == 1 else jnp.concatenate(outs, axis=1)
    return y * g


def _pos_cols(pos, width):
    rows = pos.shape[0]
    lane = lax.broadcasted_iota(i32, (rows, LANES), 1) - HEAD_DIM
    tile = jnp.where(lane < 0, 0.0, jnp.where(lane < SLOPE_TERMS, (pos >> 7).astype(f32),
                                              jnp.where(lane < 2 * SLOPE_TERMS, (pos & 127).astype(f32), 0.0)))
    return tile if width == LANES else jnp.concatenate([tile] * (width // LANES), axis=1)


def _ones_col(rows, width):
    lane = lax.broadcasted_iota(i32, (rows, LANES), 1)
    tile = jnp.where(lane == HEAD_DIM, 1.0, 0.0)
    return tile if width == LANES else jnp.concatenate([tile] * (width // LANES), axis=1)


def _inproj_body(layer_ref, x_ref, gm_ref, w_ref, gq_ref, gks_ref, gkw_ref, e_ref, tail_ref,
                 qt_ref, kc_ref, vc_ref, ks_ref, kw_ref, vs_ref, vw_ref, u_ref, bg_ref, gate_ref, kv_scr,
                 *, tm, seq):
    x = x_ref[...]
    ms = jnp.mean(x * x, axis=-1, keepdims=True)
    h = ((x * lax.rsqrt(ms + EPS)) * gm_ref[...]).astype(bf16)
    e = e_ref[...]
    two = 2 * LANES

    def proj(k):
        return _dot(h, w_ref[:, k * PROJ_CHUNK:(k + 1) * PROJ_CHUNK])

    def emit_queries(z):
        qn = _head_norm(z, e, gq_ref[...]) * (HEAD_DIM ** -0.5 * LOG2E)
        gw = HEADS_PER_GROUP * HEAD_DIM
        for cb in range(tm // Q_BLOCK):
            for g in range(N_GROUPS):
                blk = qn[cb * Q_BLOCK:(cb + 1) * Q_BLOCK, g * gw:(g + 1) * gw].T
                top = jnp.concatenate([blk[r * HEAD_DIM:(r + 1) * HEAD_DIM] for r in range(HEADS_PER_GROUP)],
                                      axis=1)
                qt_ref[cb, g] = jnp.concatenate([top.astype(bf16), tail_ref[g]], axis=0)

    def emit_compress_inputs(z):
        kv_scr[0] = z[:, 0:LANES]
        kv_scr[1] = z[:, LANES:two]
        for p in range(CMP_STRIDE):
            rows = pl.ds(p, tm // CMP_STRIDE, stride=CMP_STRIDE)
            kc_ref[:, p * LANES:(p + 1) * LANES] = kv_scr[0, rows, :].astype(bf16)
            vc_ref[:, p * LANES:(p + 1) * LANES] = kv_scr[1, rows, :].astype(bf16)
        gates = 1.0 / (1.0 + jnp.exp(-z[:, two:two + LANES]))
        pad = jnp.zeros((8 - N_BRANCH, HEADS_PER_GROUP * Q_BLOCK), f32)
        for cb in range(tm // Q_BLOCK):
            gt = gates[cb * Q_BLOCK:(cb + 1) * Q_BLOCK].T
            for g in range(N_GROUPS):
                col = lambda r, br: (g * HEADS_PER_GROUP + r) * N_BRANCH + br
                rows = [jnp.concatenate([gt[col(r, br):col(r, br) + 1] for r in range(HEADS_PER_GROUP)], axis=1)
                        for br in range(N_BRANCH)]
                gate_ref[cb, g] = jnp.concatenate(rows + [pad], axis=0)

    def emit_keys_values(z):
        t0 = (pl.program_id(0) * tm) % seq
        pc = _pos_cols(lax.broadcasted_iota(i32, (tm, LANES), 0) + t0, LANES)
        ones = _ones_col(tm, LANES)
        low = lax.broadcasted_iota(i32, (tm, LANES), 1) < HEAD_DIM

        def spread(t, extra):
            other = pltpu.roll(t, HEAD_DIM, axis=1)
            return jnp.concatenate([jnp.where(low, t, extra), jnp.where(low, other, extra)], axis=1).astype(bf16)

        ks_ref[...] = spread(_head_norm(z[:, 0:LANES], e, gks_ref[...]), pc)
        kw_ref[...] = spread(_head_norm(z[:, LANES:two], e, gkw_ref[...]), pc)
        vs_ref[...] = spread(z[:, two:two + LANES], ones)
        vw_ref[...] = spread(z[:, two + LANES:], ones)

    z0 = proj(0)
    z1 = proj(1)
    emit_queries(z0)
    z2 = proj(2)
    emit_compress_inputs(z1)
    z3 = proj(3)
    emit_keys_values(z2)
    z4 = proj(4)
    z5 = proj(5)
    u_ref[...] = (z3 * z4).astype(bf16)
    bg_ref[...] = z5.astype(bf16)


def _layer_weight(a):
    return pl.BlockSpec((None,) + a.shape[1:], lambda i, layer: (layer[0],) + (0,) * (a.ndim - 1))


def _inproj(layer, x2, gm, w_all, gq, gks, gkw, e, tail, *, seq, tm=512):
    n = x2.shape[0]
    width = HEADS_PER_GROUP * Q_BLOCK
    row = lambda w_: pl.BlockSpec((tm, w_), lambda i, layer: (i, 0))
    full = lambda a: pl.BlockSpec(a.shape, lambda i, layer: (0,) * a.ndim)
    outs = [(2 * LANES, bf16), (2 * LANES, bf16),
            (2 * LANES, bf16), (2 * LANES, bf16), (CONV_WIDTH, bf16), (CONV_WIDTH, bf16)]
    per_block = lambda rows: pl.BlockSpec((tm // Q_BLOCK, N_GROUPS, rows, width), lambda i, layer: (i, 0, 0, 0))
    qt_spec = per_block(LANES)
    qt_shape = jax.ShapeDtypeStruct((n // Q_BLOCK, N_GROUPS, LANES, width), bf16)
    gate_shape = jax.ShapeDtypeStruct((n // Q_BLOCK, N_GROUPS, 8, width), f32)
    chunk_spec = pl.BlockSpec((tm // CMP_STRIDE, CMP_STRIDE * LANES), lambda i, layer: (i, 0))
    chunk_shape = jax.ShapeDtypeStruct((n // CMP_STRIDE, CMP_STRIDE * LANES), bf16)
    return pl.pallas_call(
        functools.partial(_inproj_body, tm=tm, seq=seq),
        grid_spec=pltpu.PrefetchScalarGridSpec(
            num_scalar_prefetch=1, grid=(n // tm,),
            in_specs=[row(D_MODEL), full(gm), _layer_weight(w_all), full(gq), full(gks), full(gkw), full(e),
                      full(tail)],
            out_specs=[qt_spec, chunk_spec, chunk_spec] + [row(w_) for w_, _ in outs] + [per_block(8)],
            scratch_shapes=[pltpu.VMEM((2, tm, LANES), f32)]),
        out_shape=[qt_shape, chunk_shape, chunk_shape]
        + [jax.ShapeDtypeStruct((n, w_), dt) for w_, dt in outs] + [gate_shape],
        compiler_params=pltpu.CompilerParams(dimension_semantics=("arbitrary",),
                                             vmem_limit_bytes=VMEM_LIMIT),
        name="inproj",
    )(layer, x2, gm, w_all, gq, gks, gkw, e, tail)


def _gelu_tanh(x):
    return 0.5 * x * (1.0 + jnp.tanh(0.7978845608028654 * (x + 0.044715 * (x * x * x))))


def _compress_body(layer_ref, zk_ref, zv_ref, w1_ref, pe_ref, b1_ref, w2k_ref, w2v_ref, b2k_ref, b2v_ref,
                   gk_ref, e_ref, kc_ref, vc_ref, *, tc):
    last = lax.broadcasted_iota(i32, (tc, 1), 0) == tc - 1

    def hidden(z_ref, kind):
        z = z_ref[0]
        a = _dot(z, w1_ref[kind, 0])
        b = _dot(z, w1_ref[kind, 1])
        b = jnp.concatenate([b[1:], jnp.zeros((1, b.shape[1]), f32)], axis=0)
        bias = (_dot(pe_ref[kind, 0], w1_ref[kind, 0]) + _dot(pe_ref[kind, 1], w1_ref[kind, 1]))[0:1]
        return _gelu_tanh(a + b + bias + b1_ref[kind]).astype(bf16)

    k = _dot(hidden(zk_ref, 0), w2k_ref[...]) + b2k_ref[...]
    k = _head_norm(k, e_ref[...], gk_ref[...])
    pos = lax.broadcasted_iota(i32, (tc, LANES), 0) * CMP_STRIDE + (CMP_BLOCK - 1)
    k = k + _pos_cols(pos, 2 * LANES)
    kc_ref[0] = jnp.where(last, 0.0, k).astype(bf16)
    v = _dot(hidden(zv_ref, 1), w2v_ref[...]) + b2v_ref[...]
    vc_ref[0] = jnp.where(last, 0.0, v + _ones_col(tc, 2 * LANES)).astype(bf16)


def _compress(layer, zk, zv, w1_all, pe, b1, w2k, w2v, b2k, b2v, gk, e):
    b, tc, _ = zk.shape
    blk = lambda a: pl.BlockSpec((1,) + a.shape[1:], lambda i, layer: (i,) + (0,) * (a.ndim - 1))
    full = lambda a: pl.BlockSpec(a.shape, lambda i, layer: (0,) * a.ndim)
    return pl.pallas_call(
        functools.partial(_compress_body, tc=tc),
        grid_spec=pltpu.PrefetchScalarGridSpec(
            num_scalar_prefetch=1, grid=(b,),
            in_specs=[blk(zk), blk(zv), _layer_weight(w1_all)]
            + [full(a) for a in (pe, b1, w2k, w2v, b2k, b2v, gk, e)],
            out_specs=[pl.BlockSpec((1, tc, 2 * LANES), lambda i, layer: (i, 0, 0))] * 2),
        out_shape=[jax.ShapeDtypeStruct((b, tc, 2 * LANES), bf16)] * 2,
        compiler_params=pltpu.CompilerParams(dimension_semantics=("arbitrary",),
                                             vmem_limit_bytes=VMEM_LIMIT),
        name="compress",
    )(layer, zk, zv, w1_all, pe, b1, w2k, w2v, b2k, b2v, gk, e)


def _query_pos(c):
    lane = lax.broadcasted_iota(i32, (1, HEADS_PER_GROUP * Q_BLOCK), 1)
    return c * Q_BLOCK + (lane & (Q_BLOCK - 1))


def _pick_top(vs, jf, n_pick):
    vs = list(vs)
    for _ in range(n_pick):
        for g, v in enumerate(vs):
            mx = jnp.max(v, axis=0, keepdims=True)
            idx = jnp.min(jnp.where(v == mx, jf, float(jf.shape[0])), axis=0, keepdims=True)
            vs[g] = jnp.where(jf == idx, PICKED, v)
    return tuple(vs)


def _cmp_variant(nchunk, c0, qt_ref, kc_ref, vc_ref, mct_ref, tri_ref, o_ref, madd_ref, lst_ref, cnt_ref,
                 *, nsel):
    units = [(qb, g) for qb in range(CMP_QB) for g in range(N_GROUPS)]
    nrow = nchunk * CMP_CHUNK
    npre = nchunk * SEL_PER_CHUNK
    tqs = [_query_pos(c0 + qb) for qb in range(CMP_QB)]
    sub = lax.broadcasted_iota(i32, (CMP_CHUNK, 1), 0)
    lanes = lambda g: slice(g * LANES, (g + 1) * LANES)
    chunk = lambda a, k: a[k * CMP_CHUNK:(k + 1) * CMP_CHUNK]

    tiles = []
    for qb, g in units:
        s = _dot(kc_ref[0, 0:nrow, lanes(g)], qt_ref[qb, g])
        row = []
        for k in range(nchunk):
            t = chunk(s, k)
            if k >= nchunk - 2:
                seen = (k * CMP_CHUNK + sub) * CMP_STRIDE + (CMP_BLOCK - 1) <= tqs[qb]
                t = jnp.where(seen, t, NEG)
            row.append(t)
        tiles.append(row)

    accs, imps = [], []
    for u, (qb, g) in enumerate(units):
        m = tiles[u][0].max(axis=0, keepdims=True)
        for t in tiles[u][1:]:
            m = jnp.maximum(m, t.max(axis=0, keepdims=True))
        parts = [jnp.exp2(t - m).astype(bf16) for t in tiles[u]]
        accs.append(_dot_tn(vc_ref[0, 0:nrow, lanes(g)], jnp.concatenate(parts, axis=0)))
        rows, carry = [], None
        for part in parts:
            piece = _dot(mct_ref[...], part)
            body = piece[0:SEL_PER_CHUNK]
            if carry is not None:
                body = jnp.concatenate([body[0:8] + carry, body[8:]], axis=0)
            rows.append(body)
            carry = piece[SEL_PER_CHUNK:IMP_ROWS]
        imps.append(rows[0] if nchunk == 1 else jnp.concatenate(rows, axis=0))

    j = lax.broadcasted_iota(i32, (npre, Q_BLOCK), 0)
    jf = j.astype(f32)
    t1s = [tq[:, 0:Q_BLOCK] for tq in tqs]
    valids = [j * SEL_BLOCK <= t1 for t1 in t1s]
    vs = []
    for u, (qb, g) in enumerate(units):
        any_key = tqs[qb] >= CMP_BLOCK - 1
        inv = jnp.where(any_key, 1.0 / jnp.maximum(accs[u][HEAD_DIM:HEAD_DIM + 1], 1e-30), 0.0)
        o_ref[0, g, qb] = accs[u][0:HEAD_DIM] * inv
        imp4 = imps[u] * inv
        imp = imp4[:, 0:Q_BLOCK]
        for r in range(1, HEADS_PER_GROUP):
            imp = imp + imp4[:, r * Q_BLOCK:(r + 1) * Q_BLOCK]
        jt = t1s[qb] >> 6
        v = jnp.where(valids[qb], imp, NEG)
        vs.append(jnp.where(j == 0, PICKED, jnp.where(j == jt, PICKED, jnp.where(j == jt - 1, PICKED, v))))
    vs = _pick_top(vs, jf, SEL_TOPK - 3)

    rr = lax.broadcasted_iota(i32, (nsel, nsel), 0).astype(f32)
    ones8 = jnp.ones((8, Q_BLOCK), bf16)
    jrow = jnp.broadcast_to(lax.broadcasted_iota(i32, (1, nsel), 1).astype(f32), (8, nsel)).astype(bf16)
    for u, (qb, g) in enumerate(units):
        picked = vs[u] == PICKED
        madd = jnp.where(valids[qb], jnp.where(picked, 0.0, NEG), NEG)
        sel = jnp.where(valids[qb], jnp.where(picked, 1.0, 0.0), 0.0).astype(bf16)
        if npre < nsel:
            madd = jnp.concatenate([madd, jnp.full((nsel - npre, Q_BLOCK), NEG, f32)], axis=0)
            sel = jnp.concatenate([sel, jnp.zeros((nsel - npre, Q_BLOCK), bf16)], axis=0)
        madd_ref[0, g, qb] = madd
        flag = jnp.where(_dot_nt(ones8, sel)[0:1] > 0.0, 1.0, 0.0)
        flag8 = jnp.broadcast_to(flag, (8, nsel)).astype(bf16)
        prefix = _dot(flag8, tri_ref[...])[0:1]
        place = jnp.where(prefix == rr, flag, 0.0).astype(bf16)
        lst_ref[0, g, qb] = _dot_nt(jrow, place).astype(i32)
        cnt_ref[0, g, qb] = _dot(flag8, jnp.ones((nsel, LANES), bf16)).astype(i32)


def _cmp_body(qt_ref, kc_ref, vc_ref, mct_ref, tri_ref, o_ref, madd_ref, lst_ref, cnt_ref, *, nsel, nvar):
    c0 = pl.program_id(1) * CMP_QB
    c_last = c0 + CMP_QB - 1
    nch = (c_last * (Q_BLOCK // CMP_STRIDE) + (Q_BLOCK - CMP_BLOCK) // CMP_STRIDE) // CMP_CHUNK + 1
    for n in range(1, nvar + 1):
        pl.when(nch == n)(functools.partial(
            _cmp_variant, n, c0, qt_ref, kc_ref, vc_ref, mct_ref, tri_ref, o_ref, madd_ref, lst_ref, cnt_ref,
            nsel=nsel))


def _cmp_attention(qt, kc, vc, mct, tri, *, batch):
    g = N_GROUPS
    nqb = qt.shape[0] // batch
    tc = kc.shape[1]
    nsel = tc * CMP_STRIDE // SEL_BLOCK
    width = HEADS_PER_GROUP * Q_BLOCK
    nstep = nqb // CMP_QB
    per_q = lambda r_, c_: pl.BlockSpec((1, g, CMP_QB, r_, c_), lambda bi, ci: (bi, 0, ci, 0, 0))
    shape = lambda r_, c_, dt: jax.ShapeDtypeStruct((batch, g, nqb, r_, c_), dt)
    return pl.pallas_call(
        functools.partial(_cmp_body, nsel=nsel, nvar=tc // CMP_CHUNK),
        grid=(batch, nstep),
        in_specs=[pl.BlockSpec((CMP_QB, g, LANES, width), lambda bi, ci: (bi * nstep + ci, 0, 0, 0)),
                  pl.BlockSpec((1, tc, 2 * LANES), lambda bi, ci: (bi, 0, 0)),
                  pl.BlockSpec((1, tc, 2 * LANES), lambda bi, ci: (bi, 0, 0)),
                  pl.BlockSpec(mct.shape, lambda bi, ci: (0, 0)),
                  pl.BlockSpec(tri.shape, lambda bi, ci: (0, 0))],
        out_specs=[per_q(HEAD_DIM, width), per_q(nsel, Q_BLOCK), per_q(8, nsel), per_q(8, LANES)],
        out_shape=[shape(HEAD_DIM, width, f32), shape(nsel, Q_BLOCK, f32),
                   shape(8, nsel, i32), shape(8, LANES, i32)],
        compiler_params=pltpu.CompilerParams(dimension_semantics=("arbitrary",) * 2,
                                             vmem_limit_bytes=VMEM_LIMIT),
        name="cmp_attention",
    )(qt, kc, vc, mct, tri)


def _normalize(o_aug):
    return o_aug[0:HEAD_DIM] * (1.0 / jnp.maximum(o_aug[HEAD_DIM:HEAD_DIM + 1], 1e-30))


def _slc_win_body(lst_ref, cnt_ref, qt_ref, ks_ref, vs_ref, kw_ref, vw_ref,
                  madd_ref, ocmp_ref, gate_ref, wtab_ref, out_ref, s_scr, *, nsel):
    width = HEADS_PER_GROUP * Q_BLOCK
    nwin = (WINDOW + Q_BLOCK) // LANES
    wq = WINDOW // Q_BLOCK
    grp_rows = GROUP_BLOCKS * SEL_BLOCK

    class Block:
        def __init__(self, qb):
            self.qb = qb
            self.c = pl.program_id(2) * QB_PER_STEP + qb
            self.q0 = pl.multiple_of(self.c * Q_BLOCK, Q_BLOCK)
            self.qta = qt_ref[qb, 0]
            self.n_off = cnt_ref[qb, 0, 0] - 2

        def block_ids(self, first, nblk):
            ids = []
            for u in range(nblk):
                i = first + u
                j = lst_ref[self.qb, 0, jnp.minimum(i, nsel - 1)]
                ids.append((i < self.n_off, j, pl.multiple_of(j * SEL_BLOCK, SEL_BLOCK)))
            return ids

        def scores(self, ids):
            kcat = jnp.concatenate([ks_ref[0, pl.ds(r0, SEL_BLOCK), :] for _, _, r0 in ids], axis=0)
            sg = _dot(kcat, self.qta)
            tiles = []
            for u, (live, j, _) in enumerate(ids):
                mrow = jnp.where(live, madd_ref[0, 0, self.qb, pl.ds(j, 1), :], NEG)
                mrow = jnp.concatenate([mrow] * HEADS_PER_GROUP, axis=1)
                tiles.append(sg[u * SEL_BLOCK:(u + 1) * SEL_BLOCK] + mrow)
            return tiles

        def values(self, ids):
            return jnp.concatenate([vs_ref[0, pl.ds(r0, SEL_BLOCK), :] for _, _, r0 in ids], axis=0)

    def front(blk):
        c, q0, qta = blk.c, blk.q0, blk.qta
        ws = pl.multiple_of(jnp.maximum(c - wq, 0) * Q_BLOCK, Q_BLOCK)
        s = _dot(kw_ref[0, pl.ds(ws, WINDOW + Q_BLOCK), :], qta)
        chunks = []
        for k in range(nwin):
            steady = 1 if k == 0 else (2 if k == nwin - 1 else 0)
            tab = jnp.where(c >= wq, steady, jnp.where(k < c, 0, jnp.where(k == c, 2, 3)))
            chunks.append(s[k * LANES:(k + 1) * LANES] + wtab_ref[tab])
        head = blk.block_ids(0, HEAD_BLOCKS)
        tiles = [_dot(ks_ref[0, pl.ds(q0, Q_BLOCK), :], qta) + wtab_ref[2]] + blk.scores(head)
        return ws, chunks, head, tiles

    def softmax_pv(tiles, v):
        m = tiles[0].max(axis=0, keepdims=True)
        for su in tiles[1:]:
            m = jnp.maximum(m, su.max(axis=0, keepdims=True))
        p = jnp.concatenate([jnp.exp2(su - m).astype(bf16) for su in tiles], axis=0)
        return m, _dot_tn(v, p)

    def middle(blk, ws, chunks, head, tiles):
        _, o_win = softmax_pv(chunks, vw_ref[0, pl.ds(ws, WINDOW + Q_BLOCK), :])
        m_run, o_run = softmax_pv(
            tiles, jnp.concatenate([vs_ref[0, pl.ds(blk.q0, Q_BLOCK), :], blk.values(head)], axis=0))
        return _normalize(o_win), m_run, o_run

    def rest(blk, m_run, o_run):
        def segment(si, carry):
            m_run, o_run = carry
            base = HEAD_BLOCKS + si * SUP_BLOCKS
            ngrp = (jnp.minimum(SUP_BLOCKS, blk.n_off - base) + GROUP_BLOCKS - 1) // GROUP_BLOCKS
            rows = lambda gi: pl.ds(pl.multiple_of(gi * grp_rows, grp_rows), grp_rows)

            def score(gi, mx):
                sg = jnp.concatenate(blk.scores(blk.block_ids(base + gi * GROUP_BLOCKS, GROUP_BLOCKS)), axis=0)
                s_scr[rows(gi), :] = sg
                return jnp.maximum(mx, sg.max(axis=0, keepdims=True))

            m_new = lax.fori_loop(0, ngrp, score, m_run)

            def weigh(gi, acc):
                pg = jnp.exp2(s_scr[rows(gi), :] - m_new).astype(bf16)
                return acc + _dot_tn(blk.values(blk.block_ids(base + gi * GROUP_BLOCKS, GROUP_BLOCKS)), pg)

            o_seg = lax.fori_loop(0, ngrp, weigh, jnp.zeros((LANES, width), f32))
            return m_new, jnp.exp2(m_run - m_new) * o_run + o_seg

        nseg = (jnp.maximum(blk.n_off - HEAD_BLOCKS, 0) + SUP_BLOCKS - 1) // SUP_BLOCKS
        return lax.fori_loop(0, nseg, segment, (m_run, o_run))[1]

    blocks = [Block(qb) for qb in range(QB_PER_STEP)]
    scored = [front(blk) for blk in blocks[:PIPE_DEPTH]]
    fronts = []
    for i, blk in enumerate(blocks):
        if i + PIPE_DEPTH < len(blocks):
            scored.append(front(blocks[i + PIPE_DEPTH]))
        fronts.append(middle(blk, *scored[i]))
    tails = [rest(blk, m_run, o_run) for blk, (_, m_run, o_run) in zip(blocks, fronts)]
    for blk, (o_win, _, _), o_run in zip(blocks, fronts, tails):
        qb = blk.qb
        gate = gate_ref[qb, 0]
        mix = gate[0:1] * ocmp_ref[0, 0, qb] + gate[1:2] * _normalize(o_run) + gate[2:3] * o_win
        rows = jnp.concatenate([mix[:, r * Q_BLOCK:(r + 1) * Q_BLOCK] for r in range(HEADS_PER_GROUP)],
                               axis=0)
        out_ref[0, qb * Q_BLOCK:(qb + 1) * Q_BLOCK, :] = rows.T.astype(bf16)


def _slc_win_attention(lst, cnt, qt, ks, vs, kw, vw, madd, ocmp, gate_t, wtab):
    b, t, _ = ks.shape
    g = N_GROUPS
    nqb, nsel = t // Q_BLOCK, t // SEL_BLOCK
    width = HEADS_PER_GROUP * Q_BLOCK
    nstep = nqb // QB_PER_STEP
    flat = lambda bi, gi, ci: (bi * g + gi) * nstep + ci
    smem = lambda w_: pl.BlockSpec((QB_PER_STEP, 1, w_), lambda bi, gi, ci: (flat(bi, gi, ci), 0, 0),
                                   memory_space=pltpu.SMEM)
    per_q = lambda r_, c_: pl.BlockSpec((1, 1, QB_PER_STEP, r_, c_), lambda bi, gi, ci: (bi, gi, ci, 0, 0))
    keys = pl.BlockSpec((1, t, LANES), lambda bi, gi, ci: (bi, 0, gi), pipeline_mode=pl.Buffered(1))
    return pl.pallas_call(
        functools.partial(_slc_win_body, nsel=nsel),
        grid=(b, g, nstep),
        in_specs=[smem(nsel), smem(LANES),
                  pl.BlockSpec((QB_PER_STEP, 1, LANES, width), lambda bi, gi, ci: (bi * nstep + ci, gi, 0, 0)),
                  keys, keys, keys, keys,
                  per_q(nsel, Q_BLOCK), per_q(HEAD_DIM, width),
                  pl.BlockSpec((QB_PER_STEP, 1, 8, width), lambda bi, gi, ci: (bi * nstep + ci, gi, 0, 0)),
                  pl.BlockSpec(wtab.shape, lambda bi, gi, ci: (0, 0, 0))],
        out_specs=pl.BlockSpec((1, QB_PER_STEP * Q_BLOCK, 2 * LANES), lambda bi, gi, ci: (bi, ci, gi)),
        out_shape=jax.ShapeDtypeStruct((b, t, ATTN_WIDTH), bf16),
        scratch_shapes=[pltpu.VMEM((SUP_BLOCKS * SEL_BLOCK, width), f32)],
        compiler_params=pltpu.CompilerParams(dimension_semantics=("arbitrary",) * 3,
                                             vmem_limit_bytes=VMEM_LIMIT),
        name="slc_win_attention",
    )(lst, cnt, qt, ks, vs, kw, vw, madd, ocmp, gate_t, wtab)


PREV_ROWS = 16


def _rms(v, g):
    return (v * lax.rsqrt(jnp.mean(v * v, axis=-1, keepdims=True) + EPS)) * g


def _mix_ffn_body(layer_ref, x_ref, attn_ref, u_ref, uprev_ref, bg_ref, cw_ref, go_ref, wo_ref, gf_ref, wu_ref,
                  wd_ref,
                  out_ref, *, tm, seq, chunk):
    first = (pl.program_id(0) * tm) % seq == 0
    u = u_ref[...].astype(f32)
    prev = jnp.where(first, 0.0, uprev_ref[0].astype(f32))
    ext = jnp.concatenate([prev, u], axis=0)
    cw = cw_ref[...]
    conv = (cw[0:1] * ext[PREV_ROWS - 2:PREV_ROWS - 2 + tm] + cw[1:2] * ext[PREV_ROWS - 1:PREV_ROWS - 1 + tm]
            + cw[2:3] * u)
    conv = bg_ref[...].astype(f32) * conv
    go = go_ref[...]
    mixed = jnp.concatenate([_rms(attn_ref[...].astype(f32), go[:, :ATTN_WIDTH]),
                             _rms(conv, go[:, ATTN_WIDTH:])], axis=1).astype(bf16)
    x = x_ref[...] + _dot(mixed, wo_ref[...])
    h = _rms(x, gf_ref[...]).astype(bf16)
    acc = x
    for c in range(D_FF // chunk):
        a = jnp.maximum(_dot(h, wu_ref[:, c * chunk:(c + 1) * chunk]), 0.0)
        acc = acc + _dot((a * a).astype(bf16), wd_ref[c * chunk:(c + 1) * chunk, :])
    out_ref[...] = acc


def _mix_ffn(layer, x2, attn2, u, bgate, cw, go, wo_all, gf, wu_all, wd_all, *, seq, in_place, tm=512, chunk=1024):
    n = x2.shape[0]
    uprev = u.reshape(n // PREV_ROWS, PREV_ROWS, CONV_WIDTH)
    row = lambda w_: pl.BlockSpec((tm, w_), lambda i, layer: (i, 0))
    full = lambda a: pl.BlockSpec(a.shape, lambda i, layer: (0,) * a.ndim)
    once = lambda a: pl.BlockSpec((None,) + a.shape[1:], lambda i, layer: (layer[0],) + (0,) * (a.ndim - 1),
                                  pipeline_mode=pl.Buffered(1))
    return pl.pallas_call(
        functools.partial(_mix_ffn_body, tm=tm, seq=seq, chunk=chunk),
        grid_spec=pltpu.PrefetchScalarGridSpec(
            num_scalar_prefetch=1, grid=(n // tm,),
            in_specs=[row(D_MODEL), row(ATTN_WIDTH), row(CONV_WIDTH),
                      pl.BlockSpec((1, PREV_ROWS, CONV_WIDTH),
                                   lambda i, layer: (jnp.maximum(i * (tm // PREV_ROWS) - 1, 0), 0, 0)),
                      row(CONV_WIDTH), full(cw), full(go), once(wo_all), full(gf), once(wu_all), once(wd_all)],
            out_specs=row(D_MODEL)),
        out_shape=jax.ShapeDtypeStruct((n, D_MODEL), f32),
        input_output_aliases={1: 0} if in_place else {},
        compiler_params=pltpu.CompilerParams(dimension_semantics=("arbitrary",),
                                             vmem_limit_bytes=VMEM_LIMIT),
        name="mix_ffn",
    )(layer, x2, attn2, u, uprev, bgate, cw, go, wo_all, gf, wu_all, wd_all)


def _interleave_zero(w):
    z = jnp.zeros(w.shape[:-1] + (HEAD_DIM,), w.dtype)
    return jnp.concatenate([w[..., :HEAD_DIM], z, w[..., HEAD_DIM:], z], axis=-1)


def _prep_params(g_mix_norm, w_in, g_q, g_k, pe_cmp, w_cmp1, b_cmp1, w_cmp2, b_cmp2,
                 conv_w, g_out, w_o, g_ffn_norm, w_up, w_down):
    depth = w_in.shape[0]
    o = np.cumsum([0, ATTN_WIDTH] + [LANES] * 6 + [N_BRANCH * N_HEADS] + [CONV_WIDTH] * 3)
    part = lambda i: w_in[..., int(o[i]):int(o[i + 1])]
    q, kc, vc, ks, vs, kw, vw, gl, hc, cg, bg = [part(i) for i in range(11)]
    gl = jnp.pad(gl, ((0, 0), (0, 0), (0, LANES - gl.shape[-1])))
    w = jnp.concatenate([q, kc, vc, gl, jnp.zeros_like(gl), ks, kw, vs, vw, hc, cg, bg], axis=-1).astype(bf16)
    pair = lambda gk: jnp.concatenate([gk, gk], axis=-1)[:, None, :]
    tile2 = lambda gk: _interleave_zero(jnp.concatenate([gk, gk], axis=-1))[:, None, :]
    w1 = w_cmp1.astype(bf16).reshape(depth, 2, 2, CMP_STRIDE, HEAD_DIM, CMP_HIDDEN)
    z1 = jnp.zeros_like(w1)
    w1g = jnp.stack([jnp.concatenate([w1, z1], axis=-1), jnp.concatenate([z1, w1], axis=-1)], axis=4)
    w1g = w1g.reshape(depth, 2, 2, CMP_STRIDE * LANES, N_GROUPS * CMP_HIDDEN)
    pe = pe_cmp.reshape(depth, 2, 2, CMP_STRIDE, 1, HEAD_DIM)
    pe = jnp.broadcast_to(pe, (depth, 2, 2, CMP_STRIDE, N_GROUPS, HEAD_DIM)).reshape(depth, 2, 2, 1, -1)
    pe = jnp.pad(pe, ((0, 0), (0, 0), (0, 0), (0, 7), (0, 0))).astype(bf16)
    b1 = jnp.concatenate([b_cmp1, b_cmp1], axis=-1)[:, :, None, :]
    z2 = jnp.zeros_like(w_cmp2)
    w2 = jnp.concatenate([jnp.concatenate([w_cmp2, z2], axis=-1), jnp.concatenate([z2, w_cmp2], axis=-1)],
                         axis=2)
    w2 = _interleave_zero(w2).astype(bf16)
    b2 = _interleave_zero(jnp.concatenate([b_cmp2, b_cmp2], axis=-1))[:, :, None, :]
    return dict(
        gm=g_mix_norm[:, None, :], w=w,
        gq=jnp.tile(g_q, (1, N_HEADS))[:, None, :],
        gks=pair(g_k[:, 1]), gkw=pair(g_k[:, 2]), gkc=tile2(g_k[:, 0]),
        w1=w1g, pe=pe, b1=b1,
        w2k=w2[:, 0], w2v=w2[:, 1], b2k=b2[:, 0], b2v=b2[:, 1],
        cw=jnp.pad(conv_w, ((0, 0), (0, 8 - conv_w.shape[1]), (0, 0))),
        go=g_out[:, None, :], wo=w_o.astype(bf16),
        gf=g_ffn_norm[:, None, :], wu=w_up.astype(bf16), wd=w_down.astype(bf16),
    )


def _constants(nsel):
    lane = np.arange(LANES)
    e = (lane[:, None] // HEAD_DIM == lane[None, :] // HEAD_DIM).astype(np.float32)
    tail = np.zeros((N_GROUPS, HEAD_DIM, HEADS_PER_GROUP * Q_BLOCK), np.float32)
    for gi in range(N_GROUPS):
        for r in range(HEADS_PER_GROUP):
            rest = np.float64(2.0 ** -(gi * HEADS_PER_GROUP + r + 1)) * LOG2E
            for k in range(SLOPE_TERMS):
                term = np.float64(np.asarray(rest, np.float32).astype(jnp.bfloat16).astype(np.float32))
                tail[gi, k, r * Q_BLOCK:(r + 1) * Q_BLOCK] = term * LANES
                tail[gi, SLOPE_TERMS + k, r * Q_BLOCK:(r + 1) * Q_BLOCK] = term
                rest = rest - term
    mct = np.zeros((IMP_ROWS, CMP_CHUNK), np.float32)
    for i in range(CMP_CHUNK):
        lo, hi = i * CMP_STRIDE, i * CMP_STRIDE + CMP_BLOCK
        for jj in range(SEL_PER_CHUNK + 1):
            ov = min(hi, (jj + 1) * SEL_BLOCK) - max(lo, jj * SEL_BLOCK)
            if ov > 0:
                mct[jj, i] = ov / CMP_BLOCK
    kk = np.arange(LANES)[:, None]
    ql = np.tile(np.arange(Q_BLOCK), HEADS_PER_GROUP)[None, :]
    wtab = np.zeros((4, LANES, HEADS_PER_GROUP * Q_BLOCK), np.float32)
    wtab[1] = np.where(kk > ql, 0.0, NEG)
    wtab[2] = np.where(kk <= ql, 0.0, NEG)
    wtab[3] = NEG
    tri = np.arange(nsel)[:, None] < np.arange(nsel)[None, :]
    return (jnp.asarray(e, bf16), jnp.asarray(tail, bf16), jnp.asarray(mct, bf16), jnp.asarray(wtab),
            jnp.asarray(tri, bf16))


def _layer(x2, layer, p, big, consts, *, batch, seq, in_place):
    e, tail, mct, wtab, tri = consts
    g = N_GROUPS
    nqb, nsel, tc = seq // Q_BLOCK, seq // SEL_BLOCK, seq // CMP_STRIDE
    qt, kc, vc, ks, kw, vs, vw, u, bgate, gate_t = _inproj(
        layer, x2, p["gm"], big["w"], p["gq"], p["gks"], p["gkw"], e, tail, seq=seq)
    kcmp, vcmp = _compress(layer, kc.reshape(batch, tc, CMP_STRIDE * LANES),
                           vc.reshape(batch, tc, CMP_STRIDE * LANES),
                           big["w1"], p["pe"], p["b1"], p["w2k"], p["w2v"], p["b2k"], p["b2v"], p["gkc"], e)
    ocmp, madd, lst, cnt = _cmp_attention(qt, kcmp, vcmp, mct, tri, batch=batch)
    rows3 = lambda a: a.reshape(batch, seq, 2 * LANES)
    smem = lambda a: a[:, :, :, 0, :].reshape(batch * g * nqb, 1, a.shape[-1])
    attn = _slc_win_attention(smem(lst), smem(cnt), qt, rows3(ks), rows3(vs),
                              rows3(kw), rows3(vw), madd, ocmp, gate_t, wtab)
    return _mix_ffn(layer, x2, attn.reshape(batch * seq, ATTN_WIDTH), u, bgate, p["cw"], p["go"], big["wo"],
                    p["gf"], big["wu"], big["wd"], seq=seq, in_place=in_place)


def kernel(x, g_mix_norm, w_in, g_q, g_k, pe_cmp, w_cmp1, b_cmp1, w_cmp2, b_cmp2, conv_w, g_out, w_o,
           g_ffn_norm, w_up, w_down):
    batch, seq, d = x.shape
    assert d == D_MODEL and seq % (CMP_CHUNK * CMP_STRIDE) == 0 and seq >= WINDOW + Q_BLOCK
    params = _prep_params(g_mix_norm, w_in, g_q, g_k, pe_cmp, w_cmp1, b_cmp1, w_cmp2, b_cmp2,
                          conv_w, g_out, w_o, g_ffn_norm, w_up, w_down)
    consts = _constants(seq // SEL_BLOCK)

    big = {k: params.pop(k) for k in ("w", "w1", "wo", "wu", "wd")}
    layers = jnp.arange(w_in.shape[0], dtype=i32)[:, None]

    def step(x2, xs):
        layer, p = xs
        return _layer(x2, layer, p, big, consts, batch=batch, seq=seq, in_place=True), None

    first = jax.tree.map(lambda a: a[0], (layers, params))
    rest = jax.tree.map(lambda a: a[1:], (layers, params))
    x2 = _layer(x.reshape(batch * seq, d), *first, big, consts, batch=batch, seq=seq, in_place=False)
    x2, _ = lax.scan(step, x2, rest)
    return x2.reshape(batch, seq, d)
```
